```python
import math
import jax, jax.numpy as jnp
from jax import lax
import numpy as np

D_MODEL = 2048
BATCH = 1
SEQ = 8192
DEPTH = 4

HEAD_DIM = 128
D_MIX = D_MODEL
D_CONV = D_MIX // 4
D_FOX = (3 * D_MIX) // 8
D_NSA = D_MIX - D_CONV - D_FOX
H_FOX = D_FOX // HEAD_DIM
H_NSA = D_NSA // HEAD_DIM
NSA_KV_HEADS = 2
NSA_GROUP = H_NSA // NSA_KV_HEADS
NSA_KV_DIM = NSA_KV_HEADS * HEAD_DIM
CONV_WIDTH = 3
Q_BLOCK = 128
CMP_LEN = 32
CMP_STRIDE = 16
CMP_HIDDEN = 256
SEL_LEN = 64
SEL_TOPK = 16
WINDOW = 512
REL_BUCKETS = 32
REL_MAX_DIST = 128
NORM_EPS = 1e-6
NEG_BIG = -1e30
FORCED_SCORE = 1e4
N_IN = 4 * D_CONV + 4 * D_FOX + H_FOX + 2 * D_NSA + 6 * NSA_KV_DIM + 3 * H_NSA

kernel_name = "hybrid_conv_fox_nsa_trunk"


def _split_sizes():
    return ([D_CONV] * 4
            + [D_FOX] * 3 + [H_FOX, D_FOX]
            + [D_NSA] + [NSA_KV_DIM] * 6
            + [3 * H_NSA, D_NSA])


def _split_columns(proj):
    pts, acc = [], 0
    for w in _split_sizes()[:-1]:
        acc += w
        pts.append(acc)
    return jnp.split(proj, pts, axis=-1)


def _rms_norm(x, g):
    xf = x.astype(jnp.float32)
    y = xf * lax.rsqrt(jnp.mean(xf * xf, axis=-1, keepdims=True) + NORM_EPS)
    return (y * g.astype(jnp.float32)).astype(x.dtype)


def _t5_bucket(dist):
    max_exact = REL_BUCKETS // 2
    n = jnp.maximum(dist, 0)
    nf = jnp.maximum(n, max_exact).astype(jnp.float32)
    large = max_exact + (jnp.log(nf / max_exact) / math.log(REL_MAX_DIST / max_exact)
                         * (REL_BUCKETS - max_exact)).astype(jnp.int32)
    return jnp.where(n < max_exact, n, jnp.minimum(large, REL_BUCKETS - 1))


def _short_conv(u, gate_b, gate_c, conv_w):
    y = gate_c * u
    s = y.shape[1]
    yp = jnp.pad(y, ((0, 0), (CONV_WIDTH - 1, 0), (0, 0)))
    z = sum(conv_w[j] * yp[:, j:j + s] for j in range(CONV_WIDTH))
    return gate_b * z


def _forgetting_attention(q, k, v, f_logit, b_f):
    b, s, h, _ = q.shape
    scale = HEAD_DIM ** -0.5
    log_f = jax.nn.log_sigmoid(f_logit.astype(jnp.float32) + b_f.astype(jnp.float32))
    cum = lax.cumsum(log_f, axis=1).swapaxes(1, 2)
    nb = s // Q_BLOCK
    qb = q.reshape(b, nb, Q_BLOCK, h, HEAD_DIM).swapaxes(0, 1)
    cq = cum.reshape(b, h, nb, Q_BLOCK).transpose(2, 0, 1, 3)
    kpos = jnp.arange(s)

    def block(args):
        i, qi, ci = args
        qpos = i * Q_BLOCK + jnp.arange(Q_BLOCK)
        sc = jnp.einsum('bqhd,bkhd->bhqk', qi, k, preferred_element_type=jnp.float32) * scale
        sc = sc + ci[..., :, None] - cum[:, :, None, :]
        sc = jnp.where(kpos[None, :] <= qpos[:, None], sc, -jnp.inf)
        p = jax.nn.softmax(sc, axis=-1)
        return jnp.einsum('bhqk,bkhd->bqhd', p.astype(v.dtype), v)

    out = lax.map(block, (jnp.arange(nb), qb, cq))
    return out.swapaxes(0, 1).reshape(b, s, h * HEAD_DIM)


def _compress(kv, pe, w1, w2, blk_idx):
    b, _, g, _ = kv.shape
    n = blk_idx.shape[0]
    blocks = kv[:, blk_idx] + pe[:, None, :]
    flat = blocks.transpose(0, 1, 3, 2, 4).reshape(b, n, g, CMP_LEN * HEAD_DIM)
    return jax.nn.silu(flat @ w1) @ w2


def _native_sparse_attention(q, kc, vc, ks, vs, kw, vw, g_logit,
                             pe_k, w1_k, w2_k, pe_v, w1_v, w2_v, rel_bias):
    b, s = q.shape[:2]
    g_, r_ = NSA_KV_HEADS, NSA_GROUP
    scale = HEAD_DIM ** -0.5
    q = q.reshape(b, s, g_, r_, HEAD_DIM)
    kc, vc, ks, vs, kw, vw = [t.reshape(b, s, g_, HEAD_DIM) for t in (kc, vc, ks, vs, kw, vw)]

    n_cmp = (s - CMP_LEN) // CMP_STRIDE + 1
    blk_idx = np.arange(n_cmp)[:, None] * CMP_STRIDE + np.arange(CMP_LEN)[None, :]
    cmp_end = jnp.asarray(blk_idx[:, -1], dtype=jnp.int32)
    k_cmp = _compress(kc, pe_k, w1_k, w2_k, blk_idx)
    v_cmp = _compress(vc, pe_v, w1_v, w2_v, blk_idx)

    n_sel = s // SEL_LEN
    k_top = min(SEL_TOPK, n_sel)
    sel_start = np.arange(n_sel) * SEL_LEN
    c_lo = np.arange(n_cmp)[:, None] * CMP_STRIDE
    overlap = (c_lo < sel_start[None, :] + SEL_LEN) & (c_lo + CMP_LEN > sel_start[None, :])
    cmp_to_sel = jnp.asarray(overlap, dtype=jnp.float32)
    ks_blk = ks.reshape(b, n_sel, SEL_LEN, g_, HEAD_DIM).transpose(0, 3, 1, 2, 4)
    vs_blk = vs.reshape(b, n_sel, SEL_LEN, g_, HEAD_DIM).transpose(0, 3, 1, 2, 4)

    kw_pad = jnp.pad(kw, ((0, 0), (WINDOW, 0), (0, 0), (0, 0)))
    vw_pad = jnp.pad(vw, ((0, 0), (WINDOW, 0), (0, 0), (0, 0)))

    bias_tab = rel_bias.reshape(REL_BUCKETS, g_, r_).transpose(1, 2, 0)
    gates = jax.nn.sigmoid(g_logit.astype(jnp.float32)).reshape(b, s, g_, r_, 3)
    nb = s // Q_BLOCK
    qb = q.reshape(b, nb, Q_BLOCK, g_, r_, HEAD_DIM).swapaxes(0, 1)
    gb = gates.reshape(b, nb, Q_BLOCK, g_, r_, 3).swapaxes(0, 1)
    bidx = jnp.arange(b)[:, None, None, None]
    gidx = jnp.arange(g_)[None, :, None, None]
    g6 = jnp.arange(g_)[None, :, None, None, None, None]
    r6 = jnp.arange(r_)[None, None, :, None, None, None]
    sel_j = jnp.arange(n_sel)

    def block(args):
        i, qi, gi = args
        qpos = i * Q_BLOCK + jnp.arange(Q_BLOCK)

        s_c = jnp.einsum('bqgrd,bngd->bgrqn', qi, k_cmp, preferred_element_type=jnp.float32) * scale
        s_c = s_c + bias_tab[:, :, _t5_bucket(qpos[:, None] - cmp_end[None, :])]
        mask_c = cmp_end[None, :] <= qpos[:, None]
        p_c = jax.nn.softmax(jnp.where(mask_c, s_c, NEG_BIG), axis=-1)
        p_c = jnp.where(jnp.any(mask_c, axis=-1)[:, None], p_c, 0.0)
        o_c = jnp.einsum('bgrqn,bngd->bqgrd', p_c.astype(v_cmp.dtype), v_cmp)

        imp = jnp.einsum('bgrqn,nm->bgqm', p_c, cmp_to_sel)
        own = qpos // SEL_LEN
        eligible = sel_j[None, :] * SEL_LEN <= qpos[:, None]
        forced = ((sel_j[None, :] == 0) | (sel_j[None, :] == own[:, None])
                  | (sel_j[None, :] == own[:, None] - 1))
        score = jnp.where(eligible, jnp.where(forced, FORCED_SCORE, imp), -1.0)
        top_val, top_idx = lax.top_k(score, k_top)
        k_sel = ks_blk[bidx, gidx, top_idx]
        v_sel = vs_blk[bidx, gidx, top_idx]
        pos_sel = top_idx[..., None] * SEL_LEN + jnp.arange(SEL_LEN)
        dist_s = qpos[None, None, :, None, None] - pos_sel
        mask_s = (dist_s >= 0) & (top_val >= 0.0)[..., None]
        s_s = jnp.einsum('bqgrd,bgqkld->bgrqkl', qi, k_sel, preferred_element_type=jnp.float32) * scale
        s_s = s_s + bias_tab[g6, r6, _t5_bucket(dist_s)[:, :, None]]
        s_s = jnp.where(mask_s[:, :, None], s_s, NEG_BIG).reshape(b, g_, r_, Q_BLOCK, k_top * SEL_LEN)
        p_s = jax.nn.softmax(s_s, axis=-1).reshape(b, g_, r_, Q_BLOCK, k_top, SEL_LEN)
        o_s = jnp.einsum('bgrqkl,bgqkld->bqgrd', p_s.astype(v_sel.dtype), v_sel)

        k_win = lax.dynamic_slice_in_dim(kw_pad, i * Q_BLOCK, WINDOW + Q_BLOCK, axis=1)
        v_win = lax.dynamic_slice_in_dim(vw_pad, i * Q_BLOCK, WINDOW + Q_BLOCK, axis=1)
        kpos = i * Q_BLOCK - WINDOW + jnp.arange(WINDOW + Q_BLOCK)
        dist_w = qpos[:, None] - kpos[None, :]
        mask_w = (dist_w >= 0) & (dist_w < WINDOW) & (kpos[None, :] >= 0)
        s_w = jnp.einsum('bqgrd,bkgd->bgrqk', qi, k_win, preferred_element_type=jnp.float32) * scale
        s_w = s_w + bias_tab[:, :, _t5_bucket(dist_w)]
        p_w = jax.nn.softmax(jnp.where(mask_w, s_w, NEG_BIG), axis=-1)
        o_w = jnp.einsum('bgrqk,bkgd->bqgrd', p_w.astype(v_win.dtype), v_win)

        out = gi[..., 0:1] * o_c + gi[..., 1:2] * o_s + gi[..., 2:3] * o_w
        return out.astype(qi.dtype)

    out = lax.map(block, (jnp.arange(nb), qb, gb))
    return out.swapaxes(0, 1).reshape(b, s, H_NSA * HEAD_DIM)


def setup_inputs(seed: int = 0) -> dict:
    key = jax.random.key(seed)
    ks = jax.random.split(key, 18)
    f32 = jnp.float32

    def nrm(k, shape, sc):
        return jax.random.normal(k, shape, f32) * sc

    flat_cmp = CMP_LEN * HEAD_DIM
    return {
        'x': nrm(ks[0], (BATCH, SEQ, D_MODEL), 1.0),
        'c': nrm(ks[1], (BATCH, D_MODEL), 1.0),
        'w_ada': nrm(ks[2], (DEPTH, D_MODEL, 3 * D_MODEL), 0.5 * D_MODEL ** -0.5),
        'b_ada': nrm(ks[3], (DEPTH, 3 * D_MODEL), 0.02),
        'pre_norm': 1.0 + nrm(ks[4], (DEPTH, D_MODEL), 0.05),
        'post_norm': 1.0 + nrm(ks[5], (DEPTH, D_MODEL), 0.05),
        'w_in': nrm(ks[6], (DEPTH, D_MODEL, N_IN), D_MODEL ** -0.5),
        'b_forget': jax.random.uniform(ks[7], (DEPTH, H_FOX), f32, 1.0, 4.0),
        'conv_w': nrm(ks[8], (DEPTH, CONV_WIDTH, D_CONV), CONV_WIDTH ** -0.5),
        'cmp_pe_k': nrm(ks[9], (DEPTH, CMP_LEN, HEAD_DIM), 0.5),
        'cmp_w1_k': nrm(ks[10], (DEPTH, flat_cmp, CMP_HIDDEN), flat_cmp ** -0.5),
        'cmp_w2_k': nrm(ks[11], (DEPTH, CMP_HIDDEN, HEAD_DIM), 2.0 * CMP_HIDDEN ** -0.5),
        'cmp_pe_v': nrm(ks[12], (DEPTH, CMP_LEN, HEAD_DIM), 0.5),
        'cmp_w1_v': nrm(ks[13], (DEPTH, flat_cmp, CMP_HIDDEN), flat_cmp ** -0.5),
        'cmp_w2_v': nrm(ks[14], (DEPTH, CMP_HIDDEN, HEAD_DIM), 2.0 * CMP_HIDDEN ** -0.5),
        'w_out': nrm(ks[15], (DEPTH, D_MIX, D_MODEL), D_MIX ** -0.5),
        'rel_bias': nrm(ks[16], (REL_BUCKETS, H_NSA), 0.5),
    }


def reference(x, c, w_ada, b_ada, pre_norm, post_norm, w_in, b_forget, conv_w,
              cmp_pe_k, cmp_w1_k, cmp_w2_k, cmp_pe_v, cmp_w1_v, cmp_w2_v, w_out, rel_bias):
    b, s, _ = x.shape
    c_act = jax.nn.silu(c)
    for l in range(DEPTH):
        mod = c_act @ w_ada[l] + b_ada[l]
        shift, scale, gate = jnp.split(mod, 3, axis=-1)
        h = _rms_norm(x, pre_norm[l]) * (1.0 + scale[:, None]) + shift[:, None]

        proj = h @ w_in[l]
        (u_a, gb_a, gc_a, z_a,
         q_f, k_f, v_f, f_f, z_f,
         q_n, kc_n, vc_n, ks_n, vs_n, kw_n, vw_n, g_n, z_n) = _split_columns(proj)

        y_a = _short_conv(u_a, gb_a, gc_a, conv_w[l])
        y_f = _forgetting_attention(q_f.reshape(b, s, H_FOX, HEAD_DIM),
                                    k_f.reshape(b, s, H_FOX, HEAD_DIM),
                                    v_f.reshape(b, s, H_FOX, HEAD_DIM),
                                    f_f, b_forget[l])
        y_n = _native_sparse_attention(q_n, kc_n, vc_n, ks_n, vs_n, kw_n, vw_n, g_n,
                                       cmp_pe_k[l], cmp_w1_k[l], cmp_w2_k[l],
                                       cmp_pe_v[l], cmp_w1_v[l], cmp_w2_v[l], rel_bias)

        y = jnp.concatenate([y_a * jax.nn.silu(z_a),
                             y_f * jax.nn.silu(z_f),
                             y_n * jax.nn.silu(z_n)], axis=-1) @ w_out[l]
        x = x + gate[:, None] * _rms_norm(y, post_norm[l])
    return x
```

```python
import functools
import math

import numpy as np
import jax
import jax.numpy as jnp
from jax import lax
from jax.experimental import pallas as pl
from jax.experimental.pallas import tpu as pltpu

F32 = jnp.float32
BF16 = jnp.bfloat16

D_MODEL = 2048
DEPTH = 4
HD = 128
D_CONV = 512
D_FOX = 768
D_NSA = 768
H_FOX = 6
H_NSA = 6
NSA_G = 2
NSA_R = 3
CONV_WIDTH = 3
CMP_LEN = 32
CMP_STRIDE = 16
CMP_HIDDEN = 256
SEL_LEN = 64
SEL_TOPK = 16
WINDOW = 512
REL_BUCKETS = 32
REL_MAX_DIST = 128
NORM_EPS = 1e-6
NEG_BIG = -1e30
FORCED_SCORE = 1e4

TILE = 128
FOX_T = 256
N_A = 6912
N_T = 1280
VMEM_LIMIT = 56 * 1024 * 1024

NT_DIMS = (((1,), (1,)), ((), ()))

BLK_QF, BLK_KF, BLK_ZF = 16, 22, 28
BLK_QN, BLK_ZN, BLK_KS, BLK_KW = 34, 40, 46, 48
COL_KCVC = 6400


def _cparams(sem, vmem=VMEM_LIMIT):
    return pltpu.CompilerParams(dimension_semantics=sem, vmem_limit_bytes=vmem)


def _silu(v):
    return v * jax.nn.sigmoid(v)


def _mod_kernel(c_ref, w_ref, b_ref, o_ref):
    ca = _silu(c_ref[...])
    o_ref[0] = jnp.dot(ca, w_ref[0], precision=lax.Precision.HIGHEST,
                       preferred_element_type=F32) + b_ref[0]


def _ada_mod(c8, w_ada, b_ada):
    depth, d, n = w_ada.shape
    tn = 768
    return pl.pallas_call(
        _mod_kernel,
        grid=(depth, n // tn),
        in_specs=[pl.BlockSpec((8, d), lambda l, j: (0, 0)),
                  pl.BlockSpec((1, d, tn), lambda l, j: (l, 0, j)),
                  pl.BlockSpec((1, 1, tn), lambda l, j: (l, 0, j))],
        out_specs=pl.BlockSpec((1, 8, tn), lambda l, j: (l, 0, j)),
        out_shape=jax.ShapeDtypeStruct((depth, 8, n), F32),
        compiler_params=_cparams(("parallel", "parallel")),
        name="ada_mod",
    )(c8, w_ada, b_ada.reshape(depth, 1, n))


def _modulated_norm(x, g, scale, shift):
    y = x * lax.rsqrt(jnp.mean(x * x, axis=-1, keepdims=True) + NORM_EPS) * g
    return y * (1.0 + scale) + shift


def _prenorm_kernel(x_ref, g_ref, sc_ref, sh_ref, h_ref):
    h_ref[...] = _modulated_norm(x_ref[...], g_ref[...], sc_ref[...], sh_ref[...]).astype(BF16)


def _prenorm(x2, g, scale, shift):
    s, d = x2.shape
    tm = 512
    vec = pl.BlockSpec((1, d), lambda i: (0, 0))
    return pl.pallas_call(
        _prenorm_kernel,
        grid=(s // tm,),
        in_specs=[pl.BlockSpec((tm, d), lambda i: (i, 0)), vec, vec, vec],
        out_specs=pl.BlockSpec((tm, d), lambda i: (i, 0)),
        out_shape=jax.ShapeDtypeStruct((s, d), BF16),
        compiler_params=_cparams(("parallel",)),
        name="prenorm",
    )(x2, g, scale, shift)


def _mm_kernel(h_ref, w_ref, o_ref):
    o_ref[...] = jnp.dot(h_ref[...], w_ref[...], preferred_element_type=F32).astype(o_ref.dtype)


def _matmul(h, w, out_dtype, tm, tn):
    s, d = h.shape
    n = w.shape[1]
    return pl.pallas_call(
        _mm_kernel,
        grid=(s // tm, n // tn),
        in_specs=[pl.BlockSpec((tm, d), lambda i, j: (i, 0)),
                  pl.BlockSpec((d, tn), lambda i, j: (0, j))],
        out_specs=pl.BlockSpec((tm, tn), lambda i, j: (i, j)),
        out_shape=jax.ShapeDtypeStruct((s, n), out_dtype),
        compiler_params=_cparams(("parallel", "arbitrary")),
        name="proj_rows",
    )(h, w)


def _mm_t_tiled_kernel(h_ref, wt_ref, o_ref, *, chunks):
    res = lax.dot_general(wt_ref[...], h_ref[...], NT_DIMS, preferred_element_type=F32)
    for c in range(chunks):
        o_ref[c] = res[:, c * TILE:(c + 1) * TILE].astype(o_ref.dtype)


def _matmul_t_tiled(h, wt, tm, tn):
    s, d = h.shape
    n = wt.shape[0]
    chunks = tm // TILE
    return pl.pallas_call(
        functools.partial(_mm_t_tiled_kernel, chunks=chunks),
        grid=(s // tm, n // tn),
        in_specs=[pl.BlockSpec((tm, d), lambda i, j: (i, 0)),
                  pl.BlockSpec((tn, d), lambda i, j: (j, 0))],
        out_specs=pl.BlockSpec((chunks, tn, TILE), lambda i, j: (i, j, 0)),
        out_shape=jax.ShapeDtypeStruct((s // TILE, n, TILE), BF16),
        compiler_params=_cparams(("parallel", "arbitrary")),
        name="proj_cols_tiled",
    )(h, wt)


def _mm_t_kernel(h_ref, wt_ref, o_ref):
    o_ref[...] = lax.dot_general(wt_ref[...], h_ref[...], NT_DIMS, preferred_element_type=F32)


def _matmul_t(h, wt, tm):
    s, d = h.shape
    n = wt.shape[0]
    return pl.pallas_call(
        _mm_t_kernel,
        grid=(s // tm,),
        in_specs=[pl.BlockSpec((tm, d), lambda i: (i, 0)),
                  pl.BlockSpec((n, d), lambda i: (0, 0))],
        out_specs=pl.BlockSpec((n, tm), lambda i: (0, i)),
        out_shape=jax.ShapeDtypeStruct((n, s), F32),
        compiler_params=_cparams(("parallel",)),
        name="proj_cols_small",
    )(h, wt)


def _fox_prep_kernel(s_ref, b_ref, qa_ref, ka_ref, carry_ref, *, tm):
    @pl.when(pl.program_id(0) == 0)
    def _():
        carry_ref[...] = jnp.zeros_like(carry_ref)

    v = s_ref[...] + b_ref[...]
    c = jnp.minimum(v, 0.0) - jnp.log1p(jnp.exp(-jnp.abs(v)))
    row = lax.broadcasted_iota(jnp.int32, (tm, TILE), 0)
    sh = 1
    while sh < tm:
        c = c + jnp.where(row >= sh, pltpu.roll(c, sh, axis=0), 0.0)
        sh *= 2
    c = c + carry_ref[...]
    carry_ref[...] = c[tm - 1:tm, :]

    lane = lax.broadcasted_iota(jnp.int32, (tm, TILE), 1)
    for h in range(H_FOX):
        ch = jnp.broadcast_to(c[:, h:h + 1], (tm, TILE))
        hi = ch.astype(BF16).astype(F32)
        r1 = ch - hi
        mid = r1.astype(BF16).astype(F32)
        lo = r1 - mid
        pieces = jnp.where((lane == 0) | (lane == 3), hi,
                           jnp.where((lane == 1) | (lane == 4), mid, lo))
        qa = jnp.where(lane < 3, pieces, jnp.where(lane < 6, 1.0, 0.0))
        ka = jnp.where(lane < 3, 1.0, jnp.where(lane < 6, -pieces, 0.0))
        qa_ref[:, h * TILE:(h + 1) * TILE] = qa.astype(BF16)
        ka_ref[:, h * TILE:(h + 1) * TILE] = ka.astype(BF16)


def _fox_prep(small, b_pad):
    s = small.shape[0]
    tm = 512
    out = jax.ShapeDtypeStruct((s, H_FOX * TILE), BF16)
    return pl.pallas_call(
        functools.partial(_fox_prep_kernel, tm=tm),
        grid=(s // tm,),
        in_specs=[pl.BlockSpec((tm, TILE), lambda i: (i, 0)),
                  pl.BlockSpec((1, TILE), lambda i: (0, 0))],
        out_specs=[pl.BlockSpec((tm, H_FOX * TILE), lambda i: (i, 0))] * 2,
        out_shape=[out, out],
        scratch_shapes=[pltpu.VMEM((1, TILE), F32)],
        compiler_params=_cparams(("arbitrary",)),
        name="fox_prep",
    )(small, b_pad)


def _fox_kernel(q_ref, qa_ref, k_ref, ka_ref, vt_ref, z_ref, o_ref):
    i = pl.program_id(1)
    t = FOX_T
    qaug = jnp.concatenate([q_ref[...], qa_ref[...]], axis=1)

    def tile(j, carry, diag):
        m, l, acc = carry
        r0 = pl.multiple_of(j * t, t)
        kaug = jnp.concatenate([k_ref[pl.ds(r0, t), :], ka_ref[pl.ds(r0, t), :]], axis=1)
        s = lax.dot_general(kaug, qaug, NT_DIMS, preferred_element_type=F32)
        if diag:
            kio = lax.broadcasted_iota(jnp.int32, (t, t), 0)
            qio = lax.broadcasted_iota(jnp.int32, (t, t), 1)
            s = jnp.where(kio <= qio, s, NEG_BIG)
        m_new = jnp.maximum(m, jnp.max(s, axis=0, keepdims=True))
        alpha = jnp.exp(m - m_new)
        p = jnp.exp(s - m_new)
        l = alpha * l + jnp.sum(p, axis=0, keepdims=True)
        vt = jnp.concatenate([vt_ref[2 * j], vt_ref[2 * j + 1]], axis=1)
        acc = alpha * acc + jnp.dot(vt, p.astype(BF16), preferred_element_type=F32)
        return m_new, l, acc

    init = (jnp.full((1, t), NEG_BIG, F32), jnp.zeros((1, t), F32), jnp.zeros((HD, t), F32))
    carry = lax.fori_loop(0, i, lambda j, c: tile(j, c, False), init)
    _, l, acc = tile(i, carry, True)
    o = (acc * (1.0 / l)).T
    o_ref[...] = (o * _silu(z_ref[...].astype(F32))).astype(BF16)


def _fox_attention(proj, qa, ka, proj_t):
    s = proj.shape[0]
    t = FOX_T
    nk = s // TILE
    return pl.pallas_call(
        _fox_kernel,
        grid=(H_FOX, s // t),
        in_specs=[pl.BlockSpec((t, TILE), lambda h, i: (i, BLK_QF + h)),
                  pl.BlockSpec((t, TILE), lambda h, i: (i, h)),
                  pl.BlockSpec((s, TILE), lambda h, i: (0, BLK_KF + h)),
                  pl.BlockSpec((s, TILE), lambda h, i: (0, h)),
                  pl.BlockSpec((nk, HD, TILE), lambda h, i: (0, h, 0)),
                  pl.BlockSpec((t, TILE), lambda h, i: (i, BLK_ZF + h))],
        out_specs=pl.BlockSpec((t, TILE), lambda h, i: (i, h)),
        out_shape=jax.ShapeDtypeStruct((s, D_FOX), BF16),
        compiler_params=_cparams(("parallel", "parallel")),
        name="fox_attention",
    )(proj, qa, proj, ka, proj_t, proj)


def _compress_kernel(x_ref, w1_ref, pe_ref, w2_ref, o_ref, ot_ref):
    n = x_ref.shape[0]
    ab = jnp.dot(x_ref[...], w1_ref[0], preferred_element_type=F32)
    pe = jnp.dot(pe_ref[0], w1_ref[0], preferred_element_type=F32)
    a = ab[:, :CMP_HIDDEN]
    b_next = pltpu.roll(ab[:, CMP_HIDDEN:], n - 1, axis=0)
    hid = a + b_next + pe[0:1, :CMP_HIDDEN] + pe[8:9, CMP_HIDDEN:]
    out = jnp.dot(_silu(hid).astype(BF16), w2_ref[0], preferred_element_type=F32)
    o_ref[...] = out.astype(BF16)
    ot_ref[...] = out.T.astype(BF16)


def _compress(x4, w1cat, pe16, w2):
    _, n, k = x4.shape
    kind = lambda c: (c // NSA_G, 0, 0)
    return pl.pallas_call(
        _compress_kernel,
        grid=(2 * NSA_G,),
        in_specs=[pl.BlockSpec((None, n, k), lambda c: (c, 0, 0)),
                  pl.BlockSpec((1, k, 2 * CMP_HIDDEN), kind),
                  pl.BlockSpec((1, 16, k), kind),
                  pl.BlockSpec((1, CMP_HIDDEN, HD), kind)],
        out_specs=[pl.BlockSpec((None, n, HD), lambda c: (c, 0, 0)),
                   pl.BlockSpec((None, HD, n), lambda c: (c, 0, 0))],
        out_shape=[jax.ShapeDtypeStruct((2 * NSA_G, n, HD), BF16),
                   jax.ShapeDtypeStruct((2 * NSA_G, HD, n), BF16)],
        compiler_params=_cparams(("parallel",)),
        name="nsa_compress",
    )(x4, w1cat, pe16, w2)


def _nsa_kernel(q0_ref, q1_ref, q2_ref, z0_ref, z1_ref, z2_ref,
                ks_ref, kw_ref, vst_ref, vwt_ref, kc_ref, vct_ref,
                c2s_ref, cstrip_ref, t0_ref, t1_ref, gl_ref, o_ref,
                sc_ref, sel_ref, m_s, l_s, acc_s, m_w, l_w, acc_w, *, n_cmp):
    i = pl.program_id(1)
    w3 = NSA_R * TILE
    qs = jnp.concatenate([q0_ref[...], q1_ref[...], q2_ref[...]], axis=0)
    b_io = lax.broadcasted_iota(jnp.int32, (TILE, w3), 0)
    a_io = lax.broadcasted_iota(jnp.int32, (TILE, w3), 1) & (TILE - 1)
    t0 = t0_ref[...]
    t1 = t1_ref[...]

    sc = lax.dot_general(kc_ref[...], qs, NT_DIMS, preferred_element_type=F32)
    sc_ref[0:16, :] = jnp.zeros((16, w3), F32)
    sc_ref[16:16 + n_cmp, :] = sc
    w0 = pl.multiple_of(8 * i, 8)
    sc_ref[pl.ds(w0, 24), :] = sc_ref[pl.ds(w0, 24), :] + cstrip_ref[...]
    sc = sc_ref[16:16 + n_cmp, :]
    n_io = lax.broadcasted_iota(jnp.int32, (n_cmp, w3), 0)
    qa_io = lax.broadcasted_iota(jnp.int32, (n_cmp, w3), 1) & (TILE - 1)
    valid = (CMP_STRIDE * n_io + (CMP_LEN - 1)) <= (TILE * i + qa_io)
    scm = jnp.where(valid, sc, NEG_BIG)
    mc = jnp.max(scm, axis=0, keepdims=True)
    pc = jnp.where(valid, jnp.exp(scm - mc), 0.0)
    lc = jnp.sum(pc, axis=0, keepdims=True)
    pcn = pc * jnp.where(lc > 0.0, 1.0 / lc, 0.0)
    oc_t = jnp.dot(vct_ref[...], pcn.astype(BF16), preferred_element_type=F32)

    psum = pcn[:, 0:TILE] + pcn[:, TILE:2 * TILE] + pcn[:, 2 * TILE:3 * TILE]
    p_hi = psum.astype(BF16)
    p_lo = (psum - p_hi.astype(F32)).astype(BF16)
    c2s = c2s_ref[...]
    imp = (jnp.dot(c2s, p_hi, preferred_element_type=F32)
           + jnp.dot(c2s, p_lo, preferred_element_type=F32))
    n_sel = imp.shape[0]
    m_io = lax.broadcasted_iota(jnp.int32, (n_sel, TILE), 0)
    m_f = m_io.astype(F32)
    qpos = TILE * i + lax.broadcasted_iota(jnp.int32, (n_sel, TILE), 1)
    own = qpos >> 6
    forced = (m_io == 0) | (m_io == own) | (m_io == own - 1)
    score = jnp.where(SEL_LEN * m_io <= qpos, jnp.where(forced, FORCED_SCORE, imp), -1.0)

    def pick_one(_, carry):
        cur, sel = carry
        mx = jnp.max(cur, axis=0, keepdims=True)
        idx = jnp.min(jnp.where(cur == mx, m_f, float(n_sel)), axis=0, keepdims=True)
        pick = m_f == idx
        keep = jnp.where(mx >= 0.0, 1.0, 0.0)
        return jnp.where(pick, -2.0, cur), jnp.where(pick, keep, sel)

    _, sel = lax.fori_loop(0, min(SEL_TOPK, n_sel), pick_one,
                           (score, jnp.zeros((n_sel, TILE), F32)))
    sel_ref[...] = sel

    def attend(k_ref, vt_ref, j, mask, bias, m_ref, l_ref, acc_ref):
        r0 = pl.multiple_of(j * TILE, TILE)
        s = lax.dot_general(k_ref[pl.ds(r0, TILE), :], qs, NT_DIMS, preferred_element_type=F32)
        if bias is not None:
            s = s + bias
        if mask is not None:
            s = jnp.where(mask, s, NEG_BIG)
        m_old = m_ref[...]
        m_new = jnp.maximum(m_old, jnp.max(s, axis=0, keepdims=True))
        alpha = jnp.exp(m_old - m_new)
        p = jnp.exp(s - m_new)
        l_ref[...] = alpha * l_ref[...] + jnp.sum(p, axis=0, keepdims=True)
        acc_ref[...] = alpha * acc_ref[...] + jnp.dot(vt_ref[j], p.astype(BF16),
                                                      preferred_element_type=F32)
        m_ref[...] = m_new

    def reset(m_ref, l_ref, acc_ref):
        m_ref[...] = jnp.full(m_ref.shape, NEG_BIG, F32)
        l_ref[...] = jnp.zeros(l_ref.shape, F32)
        acc_ref[...] = jnp.zeros(acc_ref.shape, F32)

    def sel_mask(j):
        r0 = jnp.broadcast_to(sel_ref[pl.ds(2 * j, 1), :], (SEL_LEN, TILE))
        r1 = jnp.broadcast_to(sel_ref[pl.ds(2 * j + 1, 1), :], (SEL_LEN, TILE))
        mk = jnp.concatenate([r0, r1], axis=0)
        return jnp.concatenate([mk, mk, mk], axis=1) > 0.5

    reset(m_s, l_s, acc_s)

    def far_tile(j, carry):
        attend(ks_ref, vst_ref, j, sel_mask(j), None, m_s, l_s, acc_s)
        return carry

    lax.fori_loop(0, jnp.maximum(i - 1, 0), far_tile, 0)

    @pl.when(i >= 1)
    def _():
        attend(ks_ref, vst_ref, i - 1, sel_mask(i - 1), t1, m_s, l_s, acc_s)

    attend(ks_ref, vst_ref, i, sel_mask(i) & (b_io <= a_io), t0, m_s, l_s, acc_s)

    reset(m_w, l_w, acc_w)
    attend(kw_ref, vwt_ref, i, b_io <= a_io, t0, m_w, l_w, acc_w)

    @pl.when(i >= 1)
    def _():
        attend(kw_ref, vwt_ref, i - 1, None, t1, m_w, l_w, acc_w)

    for back in range(2, WINDOW // TILE):
        @pl.when(i >= back)
        def _(back=back):
            attend(kw_ref, vwt_ref, i - back, None, None, m_w, l_w, acc_w)

    @pl.when(i >= WINDOW // TILE)
    def _():
        attend(kw_ref, vwt_ref, i - WINDOW // TILE, a_io < b_io, None, m_w, l_w, acc_w)

    gates = jax.nn.sigmoid(gl_ref[...])

    def gate_row(branch):
        return jnp.concatenate([gates[r * 3 + branch:r * 3 + branch + 1, :] for r in range(NSA_R)],
                               axis=1)

    out_t = (oc_t * gate_row(0)
             + acc_s[...] * (gate_row(1) / l_s[...])
             + acc_w[...] * (gate_row(2) / l_w[...]))
    z_refs = (z0_ref, z1_ref, z2_ref)
    for r in range(NSA_R):
        o_r = out_t[:, r * TILE:(r + 1) * TILE].T
        o_ref[:, r * TILE:(r + 1) * TILE] = (o_r * _silu(z_refs[r][...].astype(F32))).astype(BF16)


def _nsa_attention(proj, proj_t, cmp_n, cmp_t, c2s_t, cstrip, t0, t1, glog_t):
    s = proj.shape[0]
    nk = s // TILE
    n_cmp = cmp_n.shape[1]
    n_sel = s // SEL_LEN
    w3 = NSA_R * TILE
    qspec = lambda r: pl.BlockSpec((TILE, TILE), lambda g, i, r=r: (i, BLK_QN + NSA_R * g + r))
    zspec = lambda r: pl.BlockSpec((TILE, TILE), lambda g, i, r=r: (i, BLK_ZN + NSA_R * g + r))
    per_group = lambda shape: pl.BlockSpec((None,) + shape, lambda g, i: (g, 0, 0))
    return pl.pallas_call(
        functools.partial(_nsa_kernel, n_cmp=n_cmp),
        grid=(NSA_G, s // TILE),
        in_specs=[qspec(0), qspec(1), qspec(2), zspec(0), zspec(1), zspec(2),
                  pl.BlockSpec((s, TILE), lambda g, i: (0, BLK_KS + g)),
                  pl.BlockSpec((s, TILE), lambda g, i: (0, BLK_KW + g)),
                  pl.BlockSpec((nk, HD, TILE), lambda g, i: (0, H_FOX + g, 0)),
                  pl.BlockSpec((nk, HD, TILE), lambda g, i: (0, H_FOX + NSA_G + g, 0)),
                  pl.BlockSpec((None, n_cmp, HD), lambda g, i: (g, 0, 0)),
                  pl.BlockSpec((None, HD, n_cmp), lambda g, i: (NSA_G + g, 0, 0)),
                  pl.BlockSpec((n_sel, n_cmp), lambda g, i: (0, 0)),
                  per_group((24, w3)), per_group((TILE, w3)), per_group((TILE, w3)),
                  pl.BlockSpec((None, NSA_R * 3, TILE), lambda g, i: (g, 0, i))],
        out_specs=pl.BlockSpec((TILE, w3), lambda g, i: (i, g)),
        out_shape=jax.ShapeDtypeStruct((s, D_NSA), BF16),
        scratch_shapes=[pltpu.VMEM((16 + n_cmp + 8, w3), F32),
                        pltpu.VMEM((n_sel, TILE), F32),
                        pltpu.VMEM((1, w3), F32), pltpu.VMEM((1, w3), F32), pltpu.VMEM((HD, w3), F32),
                        pltpu.VMEM((1, w3), F32), pltpu.VMEM((1, w3), F32), pltpu.VMEM((HD, w3), F32)],
        compiler_params=_cparams(("parallel", "parallel")),
        name="nsa_attention",
    )(proj, proj, proj, proj, proj, proj, proj, proj, proj_t, proj_t,
      cmp_n, cmp_t, c2s_t, cstrip, t0, t1, glog_t)


def _conv_kernel(u_ref, gb_ref, gc_ref, z_ref, uh_ref, gch_ref, w_ref, o_ref, *, tm):
    i = pl.program_id(0)
    y = gc_ref[...].astype(F32) * u_ref[...].astype(F32)
    yh = gch_ref[...].astype(F32) * uh_ref[...].astype(F32)
    yh = jnp.where(i > 0, yh, 0.0)
    h1 = yh[15:16, :]
    h2 = yh[14:15, :]
    row = lax.broadcasted_iota(jnp.int32, (tm, D_CONV), 0)
    y1 = jnp.where(row == 0, h1, pltpu.roll(y, 1, axis=0))
    y2 = jnp.where(row == 0, h2, jnp.where(row == 1, h1, pltpu.roll(y, 2, axis=0)))
    w = w_ref[...]
    conv = w[0:1, :] * y2 + w[1:2, :] * y1 + w[2:3, :] * y
    o_ref[...] = (gb_ref[...].astype(F32) * conv * _silu(z_ref[...].astype(F32))).astype(BF16)


def _conv_branch(proj, conv_w):
    s = proj.shape[0]
    tm = 512
    main = lambda c: pl.BlockSpec((tm, D_CONV), lambda i, c=c: (i, c))
    halo = lambda c: pl.BlockSpec((16, D_CONV), lambda i, c=c: (jnp.maximum(i * (tm // 16) - 1, 0), c))
    return pl.pallas_call(
        functools.partial(_conv_kernel, tm=tm),
        grid=(s // tm,),
        in_specs=[main(0), main(1), main(2), main(3), halo(0), halo(2),
                  pl.BlockSpec((CONV_WIDTH, D_CONV), lambda i: (0, 0))],
        out_specs=pl.BlockSpec((tm, D_CONV), lambda i: (i, 0)),
        out_shape=jax.ShapeDtypeStruct((s, D_CONV), BF16),
        compiler_params=_cparams(("parallel",)),
        name="conv_branch",
    )(proj, proj, proj, proj, proj, proj, conv_w)


def _out_kernel(ya_ref, yf_ref, yn_ref, w_ref, x_ref, pg_ref, gate_ref, *rest, with_next):
    y = (jnp.dot(ya_ref[...], w_ref[0:D_CONV, :], preferred_element_type=F32)
         + jnp.dot(yf_ref[...], w_ref[D_CONV:D_CONV + D_FOX, :], preferred_element_type=F32)
         + jnp.dot(yn_ref[...], w_ref[D_CONV + D_FOX:, :], preferred_element_type=F32))
    yn = y * lax.rsqrt(jnp.mean(y * y, axis=-1, keepdims=True) + NORM_EPS) * pg_ref[...]
    xn = x_ref[...] + gate_ref[...] * yn
    if with_next:
        ng_ref, nsc_ref, nsh_ref, xo_ref, h_ref = rest
        h_ref[...] = _modulated_norm(xn, ng_ref[...], nsc_ref[...], nsh_ref[...]).astype(BF16)
    else:
        (xo_ref,) = rest
    xo_ref[...] = xn


def _out_proj(ya, yf, yn, w_out, x2, post_g, gate, nxt):
    s, d = x2.shape
    tm = 256
    rows = lambda n: pl.BlockSpec((tm, n), lambda i: (i, 0))
    vec = pl.BlockSpec((1, d), lambda i: (0, 0))
    in_specs = [rows(D_CONV), rows(D_FOX), rows(D_NSA),
                pl.BlockSpec((d, d), lambda i: (0, 0)), rows(d), vec, vec]
    args = [ya, yf, yn, w_out, x2, post_g, gate]
    out_specs = [rows(d)]
    out_shape = [jax.ShapeDtypeStruct((s, d), F32)]
    if nxt is not None:
        in_specs += [vec, vec, vec]
        args += list(nxt)
        out_specs.append(rows(d))
        out_shape.append(jax.ShapeDtypeStruct((s, d), BF16))
    res = pl.pallas_call(
        functools.partial(_out_kernel, with_next=nxt is not None),
        grid=(s // tm,),
        in_specs=in_specs,
        out_specs=out_specs,
        out_shape=out_shape,
        compiler_params=_cparams(("parallel",)),
        name="out_proj",
    )(*args)
    return res if nxt is not None else (res[0], None)


def _bucket_of_distance():
    max_exact = REL_BUCKETS // 2
    d = np.arange(REL_MAX_DIST)
    nf = np.maximum(d, max_exact).astype(np.float32)
    large = max_exact + (np.log(nf / np.float32(max_exact)) / np.float32(math.log(REL_MAX_DIST / max_exact))
                         * np.float32(REL_BUCKETS - max_exact)).astype(np.int32)
    return np.where(d < max_exact, d, np.minimum(large, REL_BUCKETS - 1))


def _distance_tables():
    b = np.arange(TILE)[:, None]
    a = np.arange(TILE)[None, :]
    none = REL_MAX_DIST
    d0 = np.where(a >= b, a - b, none)
    d1 = np.where(a < b, TILE + a - b, none)
    m = np.arange(24)[:, None]
    dc = a - CMP_STRIDE * m + (2 * TILE - (CMP_LEN - 1))
    dc = np.where((dc >= 0) & (dc < REL_MAX_DIST), dc, none)
    return d0, d1, dc


def _bias_tables(rel_bias):
    bucket = np.concatenate([_bucket_of_distance(), [REL_BUCKETS - 1]])
    by_dist = (rel_bias[bucket, :] - rel_bias[REL_BUCKETS - 1:REL_BUCKETS, :]).T
    by_dist = by_dist.reshape(NSA_G, NSA_R, REL_MAX_DIST + 1)

    def expand(idx):
        tab = by_dist[:, :, idx]
        return tab.transpose(0, 2, 1, 3).reshape(NSA_G, idx.shape[0], NSA_R * TILE)

    d0, d1, dc = _distance_tables()
    return expand(d0), expand(d1), expand(dc)


def _cmp_to_sel_t(s, n_cmp_pad):
    n_cmp = (s - CMP_LEN) // CMP_STRIDE + 1
    n_sel = s // SEL_LEN
    c_lo = np.arange(n_cmp_pad)[None, :] * CMP_STRIDE
    sel_start = np.arange(n_sel)[:, None] * SEL_LEN
    overlap = (c_lo < sel_start + SEL_LEN) & (c_lo + CMP_LEN > sel_start)
    overlap &= np.arange(n_cmp_pad)[None, :] < n_cmp
    return jnp.asarray(overlap, dtype=BF16)


def _pack_w_in(w_in):
    sizes = ([D_CONV] * 4 + [D_FOX] * 3 + [H_FOX, D_FOX] + [D_NSA] + [2 * HD] * 6
             + [3 * H_NSA, D_NSA])
    offs = np.concatenate([[0], np.cumsum(sizes)])
    names = ["u", "gb", "gc", "za", "qf", "kf", "vf", "ff", "zf", "qn", "kc", "vc", "ks", "vs",
             "kw", "vw", "gn", "zn"]
    col = {n: w_in[:, :, int(offs[k]):int(offs[k + 1])] for k, n in enumerate(names)}
    scale = HD ** -0.5
    w_a = jnp.concatenate([col["u"], col["gb"], col["gc"], col["za"],
                           col["qf"] * scale, col["kf"], col["zf"],
                           col["qn"] * scale, col["zn"], col["ks"], col["kw"],
                           col["kc"], col["vc"]], axis=-1).astype(BF16)
    w_t = jnp.concatenate([col["vf"], col["vs"], col["vw"]], axis=-1)
    w_t = jnp.swapaxes(w_t, 1, 2).astype(BF16)
    small = jnp.concatenate([col["ff"], col["gn"]], axis=-1)
    small = jnp.pad(small, ((0, 0), (0, 0), (0, TILE - small.shape[-1]))).astype(BF16)
    return w_a, w_t, small, jnp.swapaxes(small, 1, 2)


def kernel(x, c, w_ada, b_ada, pre_norm, post_norm, w_in, b_forget, conv_w,
           cmp_pe_k, cmp_w1_k, cmp_w2_k, cmp_pe_v, cmp_w1_v, cmp_w2_v, w_out, rel_bias):
    bsz, s, d = x.shape
    assert bsz == 1 and d == D_MODEL and s % (2 * FOX_T) == 0 and s >= 2 * WINDOW
    depth = w_in.shape[0]
    x2 = x.reshape(s, d)
    n_cmp_pad = s // CMP_STRIDE

    mod = _ada_mod(jnp.broadcast_to(c, (8, d)), w_ada, b_ada)[:, 0:1, :]
    shift, scale, gate = mod[:, :, :d], mod[:, :, d:2 * d], mod[:, :, 2 * d:]

    w_a, w_t, w_small, w_small_t = _pack_w_in(w_in)
    w_out_b = w_out.astype(BF16)
    half = CMP_LEN * HD // 2
    w1cat = jnp.stack([jnp.concatenate([w[:, :half], w[:, half:]], axis=-1)
                       for w in (cmp_w1_k, cmp_w1_v)], axis=1).astype(BF16)
    pe = jnp.stack([cmp_pe_k, cmp_pe_v], axis=1).reshape(depth, 2, 2, 1, half)
    pe16 = jnp.broadcast_to(pe, (depth, 2, 2, 8, half)).reshape(depth, 2, 16, half).astype(BF16)
    w2 = jnp.stack([cmp_w2_k, cmp_w2_v], axis=1).astype(BF16)
    b_pad = jnp.pad(b_forget, ((0, 0), (0, TILE - H_FOX))).reshape(depth, 1, TILE)
    t0, t1, cstrip = _bias_tables(rel_bias)
    c2s_t = _cmp_to_sel_t(s, n_cmp_pad)

    h = _prenorm(x2, pre_norm[0:1], scale[0], shift[0])
    for l in range(depth):
        proj = _matmul(h, w_a[l], BF16, 1024, 768)
        proj_t = _matmul_t_tiled(h, w_t[l], 1024, 256)
        small = _matmul(h, w_small[l], F32, 1024, TILE)
        small_t = _matmul_t(h, w_small_t[l], 1024)

        ya = _conv_branch(proj, conv_w[l])

        qa, ka = _fox_prep(small, b_pad[l])
        yf = _fox_attention(proj, qa, ka, proj_t)

        kcvc = proj[:, COL_KCVC:COL_KCVC + 4 * HD]
        x4 = kcvc.reshape(n_cmp_pad, CMP_STRIDE, 4, HD).transpose(2, 0, 1, 3)
        x4 = x4.reshape(4, n_cmp_pad, CMP_STRIDE * HD)
        cmp_n, cmp_t = _compress(x4, w1cat[l], pe16[l], w2[l])
        glog_t = small_t[H_FOX:H_FOX + 3 * H_NSA].reshape(NSA_G, NSA_R * 3, s)
        yn = _nsa_attention(proj, proj_t, cmp_n, cmp_t, c2s_t, cstrip, t0, t1, glog_t)

        nxt = None
        if l + 1 < depth:
            nxt = (pre_norm[l + 1:l + 2], scale[l + 1], shift[l + 1])
        x2, h = _out_proj(ya, yf, yn, w_out_b[l], x2, post_norm[l:l + 1], gate[l], nxt)
    return x2.reshape(bsz, s, d)
```

```python
import functools
import math

import numpy as np
import jax
import jax.numpy as jnp
from jax import lax
from jax.experimental import pallas as pl
from jax.experimental.pallas import tpu as pltpu

F32 = jnp.float32
BF16 = jnp.bfloat16

D_MODEL = 2048
DEPTH = 4
HD = 128
D_CONV = 512
D_FOX = 768
D_NSA = 768
H_FOX = 6
H_NSA = 6
NSA_G = 2
NSA_R = 3
CONV_WIDTH = 3
CMP_LEN = 32
CMP_STRIDE = 16
CMP_HIDDEN = 256
SEL_LEN = 64
SEL_TOPK = 16
WINDOW = 512
REL_BUCKETS = 32
REL_MAX_DIST = 128
NORM_EPS = 1e-6
NEG_BIG = -1e30
FORCED_SCORE = 1e4

TILE = 128
NSA_CHUNK = 512
FOX_T = 512
FOX_TK = 256
N_A = 6912
N_T = 1280
VMEM_LIMIT = 56 * 1024 * 1024

NT_DIMS = (((1,), (1,)), ((), ()))

BLK_QF, BLK_KF, BLK_ZF = 16, 22, 28
BLK_QN, BLK_ZN, BLK_KS, BLK_KW = 34, 40, 46, 48
COL_KCVC = 6400


def _cparams(sem, vmem=VMEM_LIMIT):
    return pltpu.CompilerParams(dimension_semantics=sem, vmem_limit_bytes=vmem)


def _silu(v):
    return v * jax.nn.sigmoid(v)


def _mod_kernel(c_ref, w_ref, b_ref, o_ref):
    ca = _silu(c_ref[...])
    o_ref[0] = jnp.dot(ca, w_ref[0], precision=lax.Precision.HIGHEST,
                       preferred_element_type=F32) + b_ref[0]


def _ada_mod(c8, w_ada, b_ada):
    depth, d, n = w_ada.shape
    tn = 768
    return pl.pallas_call(
        _mod_kernel,
        grid=(depth, n // tn),
        in_specs=[pl.BlockSpec((8, d), lambda l, j: (0, 0)),
                  pl.BlockSpec((1, d, tn), lambda l, j: (l, 0, j)),
                  pl.BlockSpec((1, 1, tn), lambda l, j: (l, 0, j))],
        out_specs=pl.BlockSpec((1, 8, tn), lambda l, j: (l, 0, j)),
        out_shape=jax.ShapeDtypeStruct((depth, 8, n), F32),
        compiler_params=_cparams(("parallel", "parallel")),
        name="ada_mod",
    )(c8, w_ada, b_ada.reshape(depth, 1, n))


def _modulated_norm(x, g, scale, shift):
    y = x * lax.rsqrt(jnp.mean(x * x, axis=-1, keepdims=True) + NORM_EPS) * g
    return y * (1.0 + scale) + shift


def _prenorm_kernel(x_ref, g_ref, sc_ref, sh_ref, h_ref):
    h_ref[...] = _modulated_norm(x_ref[...], g_ref[...], sc_ref[...], sh_ref[...]).astype(BF16)


def _prenorm(x2, g, scale, shift):
    s, d = x2.shape
    tm = 512
    vec = pl.BlockSpec((1, d), lambda i: (0, 0))
    return pl.pallas_call(
        _prenorm_kernel,
        grid=(s // tm,),
        in_specs=[pl.BlockSpec((tm, d), lambda i: (i, 0)), vec, vec, vec],
        out_specs=pl.BlockSpec((tm, d), lambda i: (i, 0)),
        out_shape=jax.ShapeDtypeStruct((s, d), BF16),
        compiler_params=_cparams(("parallel",)),
        name="prenorm",
    )(x2, g, scale, shift)


def _mm_kernel(h_ref, w_ref, o_ref):
    o_ref[...] = jnp.dot(h_ref[...], w_ref[...], preferred_element_type=F32).astype(o_ref.dtype)


def _matmul(h, w, out_dtype, tm, tn):
    s, d = h.shape
    n = w.shape[1]
    return pl.pallas_call(
        _mm_kernel,
        grid=(s // tm, n // tn),
        in_specs=[pl.BlockSpec((tm, d), lambda i, j: (i, 0)),
                  pl.BlockSpec((d, tn), lambda i, j: (0, j))],
        out_specs=pl.BlockSpec((tm, tn), lambda i, j: (i, j)),
        out_shape=jax.ShapeDtypeStruct((s, n), out_dtype),
        compiler_params=_cparams(("parallel", "arbitrary")),
        name="proj_rows",
    )(h, w)


def _mm_t_tiled_kernel(h_ref, wt_ref, o_ref, *, chunks):
    res = lax.dot_general(wt_ref[...], h_ref[...], NT_DIMS, preferred_element_type=F32)
    for c in range(chunks):
        o_ref[c] = res[:, c * TILE:(c + 1) * TILE].astype(o_ref.dtype)


def _matmul_t_tiled(h, wt, tm, tn):
    s, d = h.shape
    n = wt.shape[0]
    chunks = tm // TILE
    return pl.pallas_call(
        functools.partial(_mm_t_tiled_kernel, chunks=chunks),
        grid=(s // tm, n // tn),
        in_specs=[pl.BlockSpec((tm, d), lambda i, j: (i, 0)),
                  pl.BlockSpec((tn, d), lambda i, j: (j, 0))],
        out_specs=pl.BlockSpec((chunks, tn, TILE), lambda i, j: (i, j, 0)),
        out_shape=jax.ShapeDtypeStruct((s // TILE, n, TILE), BF16),
        compiler_params=_cparams(("parallel", "arbitrary")),
        name="proj_cols_tiled",
    )(h, wt)


def _mm_t_kernel(h_ref, wt_ref, o_ref):
    o_ref[...] = lax.dot_general(wt_ref[...], h_ref[...], NT_DIMS, preferred_element_type=F32)


def _matmul_t(h, wt, tm):
    s, d = h.shape
    n = wt.shape[0]
    return pl.pallas_call(
        _mm_t_kernel,
        grid=(s // tm,),
        in_specs=[pl.BlockSpec((tm, d), lambda i: (i, 0)),
                  pl.BlockSpec((n, d), lambda i: (0, 0))],
        out_specs=pl.BlockSpec((n, tm), lambda i: (0, i)),
        out_shape=jax.ShapeDtypeStruct((n, s), F32),
        compiler_params=_cparams(("parallel",)),
        name="proj_cols_small",
    )(h, wt)


def _fox_prep_kernel(s_ref, b_ref, qa_ref, ka_ref, carry_ref, *, tm):
    @pl.when(pl.program_id(0) == 0)
    def _():
        carry_ref[...] = jnp.zeros_like(carry_ref)

    v = s_ref[...] + b_ref[...]
    c = jnp.minimum(v, 0.0) - jnp.log1p(jnp.exp(-jnp.abs(v)))
    row = lax.broadcasted_iota(jnp.int32, (tm, TILE), 0)
    sh = 1
    while sh < tm:
        c = c + jnp.where(row >= sh, pltpu.roll(c, sh, axis=0), 0.0)
        sh *= 2
    c = c + carry_ref[...]
    carry_ref[...] = c[tm - 1:tm, :]

    lane = lax.broadcasted_iota(jnp.int32, (tm, TILE), 1)
    for h in range(H_FOX):
        ch = jnp.broadcast_to(c[:, h:h + 1], (tm, TILE))
        hi = ch.astype(BF16).astype(F32)
        r1 = ch - hi
        mid = r1.astype(BF16).astype(F32)
        lo = r1 - mid
        pieces = jnp.where((lane == 0) | (lane == 3), hi,
                           jnp.where((lane == 1) | (lane == 4), mid, lo))
        qa = jnp.where(lane < 3, pieces, jnp.where(lane < 6, 1.0, 0.0))
        ka = jnp.where(lane < 3, 1.0, jnp.where(lane < 6, -pieces, 0.0))
        qa_ref[:, h * TILE:(h + 1) * TILE] = qa.astype(BF16)
        ka_ref[:, h * TILE:(h + 1) * TILE] = ka.astype(BF16)


def _fox_prep(small, b_pad):
    s = small.shape[0]
    tm = 512
    out = jax.ShapeDtypeStruct((s, H_FOX * TILE), BF16)
    return pl.pallas_call(
        functools.partial(_fox_prep_kernel, tm=tm),
        grid=(s // tm,),
        in_specs=[pl.BlockSpec((tm, TILE), lambda i: (i, 0)),
                  pl.BlockSpec((1, TILE), lambda i: (0, 0))],
        out_specs=[pl.BlockSpec((tm, H_FOX * TILE), lambda i: (i, 0))] * 2,
        out_shape=[out, out],
        scratch_shapes=[pltpu.VMEM((1, TILE), F32)],
        compiler_params=_cparams(("arbitrary",)),
        name="fox_prep",
    )(small, b_pad)


def _fox_kernel(q_ref, qa_ref, k_ref, ka_ref, vt_ref, z_ref, o_ref, s0_ref, s1_ref, acc_ref):
    i = pl.program_id(1)
    t = FOX_T
    tk = FOX_TK
    qaug = jnp.concatenate([q_ref[...], qa_ref[...]], axis=1)

    def scores(j, rows):
        r0 = pl.multiple_of(j * rows, rows)
        kaug = jnp.concatenate([k_ref[pl.ds(r0, rows), :], ka_ref[pl.ds(r0, rows), :]], axis=1)
        return lax.dot_general(kaug, qaug, NT_DIMS, preferred_element_type=F32)

    def values_t(j, rows):
        n_sub = rows // TILE
        return jnp.concatenate([vt_ref[n_sub * j + c] for c in range(n_sub)], axis=1)

    kio = lax.broadcasted_iota(jnp.int32, (t, t), 0)
    qio = lax.broadcasted_iota(jnp.int32, (t, t), 1)
    s = jnp.where(kio <= qio, scores(i, t), NEG_BIG)
    m = jnp.max(s, axis=0, keepdims=True)
    p = jnp.exp(s - m)
    l = jnp.sum(p, axis=0, keepdims=True)
    acc_ref[...] = jnp.dot(values_t(i, t), p.astype(BF16), preferred_element_type=F32)

    def update(s, vt, m, l):
        m_new = jnp.maximum(m, jnp.max(s, axis=0, keepdims=True))
        alpha = jnp.exp(m - m_new)
        p = jnp.exp(s - m_new)
        l = alpha * l + jnp.sum(p, axis=0, keepdims=True)
        acc_ref[...] = alpha * acc_ref[...] + jnp.dot(vt, p.astype(BF16),
                                                      preferred_element_type=F32)
        return m_new, l

    s0_ref[...] = scores(0, tk)

    def pair(pp, carry):
        m, l = carry
        s1_ref[...] = scores(2 * pp + 1, tk)
        m, l = update(s0_ref[...], values_t(2 * pp, tk), m, l)
        s0_ref[...] = scores(2 * pp + 2, tk)
        m, l = update(s1_ref[...], values_t(2 * pp + 1, tk), m, l)
        return m, l

    m, l = lax.fori_loop(0, i * (t // (2 * tk)), pair, (m, l))
    o = (acc_ref[...] * (1.0 / l)).T
    o_ref[...] = (o * _silu(z_ref[...].astype(F32))).astype(BF16)


def _fox_attention(proj, qa, ka, proj_t):
    s = proj.shape[0]
    t = FOX_T
    nk = s // TILE
    return pl.pallas_call(
        _fox_kernel,
        grid=(H_FOX, s // t),
        scratch_shapes=[pltpu.VMEM((FOX_TK, t), F32), pltpu.VMEM((FOX_TK, t), F32),
                        pltpu.VMEM((HD, t), F32)],
        in_specs=[pl.BlockSpec((t, TILE), lambda h, i: (i, BLK_QF + h)),
                  pl.BlockSpec((t, TILE), lambda h, i: (i, h)),
                  pl.BlockSpec((s, TILE), lambda h, i: (0, BLK_KF + h)),
                  pl.BlockSpec((s, TILE), lambda h, i: (0, h)),
                  pl.BlockSpec((nk, HD, TILE), lambda h, i: (0, h, 0)),
                  pl.BlockSpec((t, TILE), lambda h, i: (i, BLK_ZF + h))],
        out_specs=pl.BlockSpec((t, TILE), lambda h, i: (i, h)),
        out_shape=jax.ShapeDtypeStruct((s, D_FOX), BF16),
        compiler_params=_cparams(("parallel", "parallel")),
        name="fox_attention",
    )(proj, qa, proj, ka, proj_t, proj)


def _compress_kernel(x_ref, w1_ref, pe_ref, w2_ref, o_ref, ot_ref):
    n = x_ref.shape[0]
    ab = jnp.dot(x_ref[...], w1_ref[0], preferred_element_type=F32)
    pe = jnp.dot(pe_ref[0], w1_ref[0], preferred_element_type=F32)
    a = ab[:, :CMP_HIDDEN]
    b_next = pltpu.roll(ab[:, CMP_HIDDEN:], n - 1, axis=0)
    hid = a + b_next + pe[0:1, :CMP_HIDDEN] + pe[8:9, CMP_HIDDEN:]
    out = jnp.dot(_silu(hid).astype(BF16), w2_ref[0], preferred_element_type=F32)
    o_ref[...] = out.astype(BF16)
    ot_ref[...] = out.T.astype(BF16)


def _compress(x4, w1cat, pe16, w2):
    _, n, k = x4.shape
    kind = lambda c: (c // NSA_G, 0, 0)
    return pl.pallas_call(
        _compress_kernel,
        grid=(2 * NSA_G,),
        in_specs=[pl.BlockSpec((None, n, k), lambda c: (c, 0, 0)),
                  pl.BlockSpec((1, k, 2 * CMP_HIDDEN), kind),
                  pl.BlockSpec((1, 16, k), kind),
                  pl.BlockSpec((1, CMP_HIDDEN, HD), kind)],
        out_specs=[pl.BlockSpec((None, n, HD), lambda c: (c, 0, 0)),
                   pl.BlockSpec((None, HD, n), lambda c: (c, 0, 0))],
        out_shape=[jax.ShapeDtypeStruct((2 * NSA_G, n, HD), BF16),
                   jax.ShapeDtypeStruct((2 * NSA_G, HD, n), BF16)],
        compiler_params=_cparams(("parallel",)),
        name="nsa_compress",
    )(x4, w1cat, pe16, w2)


def _nsa_kernel(q0_ref, q1_ref, q2_ref, z0_ref, z1_ref, z2_ref,
                ks_ref, kw_ref, vst_ref, vwt_ref, kc_ref, vct_ref,
                c2s_ref, cstrip_ref, t0_ref, t1_ref, gl_ref, o_ref,
                sc_ref, sel_ref, m_s, l_s, acc_s, ss_ref, *, n_cmp):
    i = pl.program_id(1)
    w3 = NSA_R * TILE
    qs = jnp.concatenate([q0_ref[...], q1_ref[...], q2_ref[...]], axis=0)
    b_io = lax.broadcasted_iota(jnp.int32, (TILE, w3), 0)
    a_io = lax.broadcasted_iota(jnp.int32, (TILE, w3), 1) & (TILE - 1)
    t0 = t0_ref[...]
    t1 = t1_ref[...]

    sc = lax.dot_general(kc_ref[...], qs, NT_DIMS, preferred_element_type=F32)
    sc_ref[0:16, :] = jnp.zeros((16, w3), F32)
    sc_ref[16:16 + n_cmp, :] = sc
    w0 = pl.multiple_of(8 * i, 8)
    sc_ref[pl.ds(w0, 24), :] = sc_ref[pl.ds(w0, 24), :] + cstrip_ref[...]
    sc = sc_ref[16:16 + n_cmp, :]
    n_io = lax.broadcasted_iota(jnp.int32, (n_cmp, w3), 0)
    qa_io = lax.broadcasted_iota(jnp.int32, (n_cmp, w3), 1) & (TILE - 1)
    valid = (CMP_STRIDE * n_io + (CMP_LEN - 1)) <= (TILE * i + qa_io)
    scm = jnp.where(valid, sc, NEG_BIG)
    mc = jnp.max(scm, axis=0, keepdims=True)
    pc = jnp.where(valid, jnp.exp(scm - mc), 0.0)
    lc = jnp.sum(pc, axis=0, keepdims=True)
    pcn = pc * jnp.where(lc > 0.0, 1.0 / lc, 0.0)
    oc_t = jnp.dot(vct_ref[...], pcn.astype(BF16), preferred_element_type=F32)

    psum = pcn[:, 0:TILE] + pcn[:, TILE:2 * TILE] + pcn[:, 2 * TILE:3 * TILE]
    p_hi = psum.astype(BF16)
    p_lo = (psum - p_hi.astype(F32)).astype(BF16)
    c2s = c2s_ref[...]
    imp = (jnp.dot(c2s, p_hi, preferred_element_type=F32)
           + jnp.dot(c2s, p_lo, preferred_element_type=F32))
    n_sel = imp.shape[0]
    m_io = lax.broadcasted_iota(jnp.int32, (n_sel, TILE), 0)
    m_f = m_io.astype(F32)
    qpos = TILE * i + lax.broadcasted_iota(jnp.int32, (n_sel, TILE), 1)
    own = qpos >> 6
    forced = (m_io == 0) | (m_io == own) | (m_io == own - 1)
    score = jnp.where(SEL_LEN * m_io <= qpos, jnp.where(forced, FORCED_SCORE, imp), -1.0)

    def pick_one(_, carry):
        cur, sel = carry
        mx = jnp.max(cur, axis=0, keepdims=True)
        idx = jnp.min(jnp.where(cur == mx, m_f, float(n_sel)), axis=0, keepdims=True)
        pick = m_f == idx
        keep = jnp.where(mx >= 0.0, 1.0, 0.0)
        return jnp.where(pick, -2.0, cur), jnp.where(pick, keep, sel)

    _, sel = lax.fori_loop(0, min(SEL_TOPK, n_sel), pick_one,
                           (score, jnp.zeros((n_sel, TILE), F32)))
    sel_ref[...] = sel

    tpc = NSA_CHUNK // TILE

    def key_scores(k_ref, tile0, n_tiles):
        r0 = pl.multiple_of(tile0 * TILE, TILE)
        return lax.dot_general(k_ref[pl.ds(r0, n_tiles * TILE), :], qs, NT_DIMS,
                               preferred_element_type=F32)

    def values_t(vt_ref, tile0, n_tiles):
        return jnp.concatenate([vt_ref[tile0 + c] for c in range(n_tiles)], axis=1)

    def add_near_bias(d_tile, n_tiles):
        for c in range(n_tiles):
            @pl.when(d_tile == c + 1)
            def _(c=c):
                ss_ref[c * TILE:(c + 1) * TILE, :] = ss_ref[c * TILE:(c + 1) * TILE, :] + t1

            @pl.when(d_tile == c)
            def _(c=c):
                ss_ref[c * TILE:(c + 1) * TILE, :] = jnp.where(
                    b_io <= a_io, ss_ref[c * TILE:(c + 1) * TILE, :] + t0, NEG_BIG)

    def sel_mask(c):
        rows = sel_ref[pl.ds(pl.multiple_of(c * (NSA_CHUNK // SEL_LEN), 8), NSA_CHUNK // SEL_LEN), :]
        mk = jnp.concatenate([jnp.broadcast_to(rows[b:b + 1, :], (SEL_LEN, TILE))
                              for b in range(NSA_CHUNK // SEL_LEN)], axis=0)
        return jnp.concatenate([mk, mk, mk], axis=1) > 0.5

    def sel_update(c, s):
        s = jnp.where(sel_mask(c), s, NEG_BIG)
        m_old = m_s[...]
        m_new = jnp.maximum(m_old, jnp.max(s, axis=0, keepdims=True))
        alpha = jnp.exp(m_old - m_new)
        p = jnp.exp(s - m_new)
        l_s[...] = alpha * l_s[...] + jnp.sum(p, axis=0, keepdims=True)
        acc_s[...] = alpha * acc_s[...] + jnp.dot(values_t(vst_ref, c * tpc, tpc), p.astype(BF16),
                                                  preferred_element_type=F32)
        m_s[...] = m_new

    m_s[...] = jnp.full(m_s.shape, NEG_BIG, F32)
    l_s[...] = jnp.zeros(l_s.shape, F32)
    acc_s[...] = jnp.zeros(acc_s.shape, F32)

    c_own = i // tpc
    c_prev = jnp.maximum(i - 1, 0) // tpc

    def far_chunk(c, carry):
        sel_update(c, key_scores(ks_ref, c * tpc, tpc))
        return carry

    lax.fori_loop(0, c_prev, far_chunk, 0)

    def near_chunk(c):
        ss_ref[0:NSA_CHUNK, :] = key_scores(ks_ref, c * tpc, tpc)
        add_near_bias(i - c * tpc, tpc)
        sel_update(c, ss_ref[0:NSA_CHUNK, :])

    @pl.when(c_prev != c_own)
    def _():
        near_chunk(c_prev)

    near_chunk(c_own)

    n_wt = WINDOW // TILE + 1
    w_tile0 = jnp.maximum(i - WINDOW // TILE, 0)
    ss_ref[...] = key_scores(kw_ref, w_tile0, n_wt)
    add_near_bias(i - w_tile0, n_wt)

    @pl.when(i >= WINDOW // TILE)
    def _():
        ss_ref[0:TILE, :] = jnp.where(a_io < b_io, ss_ref[0:TILE, :], NEG_BIG)

    for c in range(1, n_wt):
        @pl.when(i < c)
        def _(c=c):
            ss_ref[c * TILE:(c + 1) * TILE, :] = jnp.full((TILE, w3), NEG_BIG, F32)

    sw = ss_ref[...]
    m_w = jnp.max(sw, axis=0, keepdims=True)
    p_w = jnp.exp(sw - m_w)
    l_w = jnp.sum(p_w, axis=0, keepdims=True)
    acc_w = jnp.dot(values_t(vwt_ref, w_tile0, n_wt), p_w.astype(BF16), preferred_element_type=F32)

    gates = jax.nn.sigmoid(gl_ref[...])

    def gate_row(branch):
        return jnp.concatenate([gates[r * 3 + branch:r * 3 + branch + 1, :] for r in range(NSA_R)],
                               axis=1)

    out_t = (oc_t * gate_row(0)
             + acc_s[...] * (gate_row(1) / l_s[...])
             + acc_w * (gate_row(2) / l_w))
    z_refs = (z0_ref, z1_ref, z2_ref)
    for r in range(NSA_R):
        o_r = out_t[:, r * TILE:(r + 1) * TILE].T
        o_ref[:, r * TILE:(r + 1) * TILE] = (o_r * _silu(z_refs[r][...].astype(F32))).astype(BF16)


def _nsa_attention(proj, proj_t, cmp_n, cmp_t, c2s_t, cstrip, t0, t1, glog_t):
    s = proj.shape[0]
    nk = s // TILE
    n_cmp = cmp_n.shape[1]
    n_sel = s // SEL_LEN
    w3 = NSA_R * TILE
    qspec = lambda r: pl.BlockSpec((TILE, TILE), lambda g, i, r=r: (i, BLK_QN + NSA_R * g + r))
    zspec = lambda r: pl.BlockSpec((TILE, TILE), lambda g, i, r=r: (i, BLK_ZN + NSA_R * g + r))
    per_group = lambda shape: pl.BlockSpec((None,) + shape, lambda g, i: (g, 0, 0))
    return pl.pallas_call(
        functools.partial(_nsa_kernel, n_cmp=n_cmp),
        grid=(NSA_G, s // TILE),
        in_specs=[qspec(0), qspec(1), qspec(2), zspec(0), zspec(1), zspec(2),
                  pl.BlockSpec((s, TILE), lambda g, i: (0, BLK_KS + g)),
                  pl.BlockSpec((s, TILE), lambda g, i: (0, BLK_KW + g)),
                  pl.BlockSpec((nk, HD, TILE), lambda g, i: (0, H_FOX + g, 0)),
                  pl.BlockSpec((nk, HD, TILE), lambda g, i: (0, H_FOX + NSA_G + g, 0)),
                  pl.BlockSpec((None, n_cmp, HD), lambda g, i: (g, 0, 0)),
                  pl.BlockSpec((None, HD, n_cmp), lambda g, i: (NSA_G + g, 0, 0)),
                  pl.BlockSpec((n_sel, n_cmp), lambda g, i: (0, 0)),
                  per_group((24, w3)), per_group((TILE, w3)), per_group((TILE, w3)),
                  pl.BlockSpec((None, NSA_R * 3, TILE), lambda g, i: (g, 0, i))],
        out_specs=pl.BlockSpec((TILE, w3), lambda g, i: (i, g)),
        out_shape=jax.ShapeDtypeStruct((s, D_NSA), BF16),
        scratch_shapes=[pltpu.VMEM((16 + n_cmp + 8, w3), F32),
                        pltpu.VMEM((n_sel, TILE), F32),
                        pltpu.VMEM((1, w3), F32), pltpu.VMEM((1, w3), F32), pltpu.VMEM((HD, w3), F32),
                        pltpu.VMEM((WINDOW + TILE, w3), F32)],
        compiler_params=_cparams(("parallel", "parallel")),
        name="nsa_attention",
    )(proj, proj, proj, proj, proj, proj, proj, proj, proj_t, proj_t,
      cmp_n, cmp_t, c2s_t, cstrip, t0, t1, glog_t)


def _conv_kernel(u_ref, gb_ref, gc_ref, z_ref, uh_ref, gch_ref, w_ref, o_ref, *, tm):
    i = pl.program_id(0)
    y = gc_ref[...].astype(F32) * u_ref[...].astype(F32)
    yh = gch_ref[...].astype(F32) * uh_ref[...].astype(F32)
    yh = jnp.where(i > 0, yh, 0.0)
    h1 = yh[15:16, :]
    h2 = yh[14:15, :]
    row = lax.broadcasted_iota(jnp.int32, (tm, D_CONV), 0)
    y1 = jnp.where(row == 0, h1, pltpu.roll(y, 1, axis=0))
    y2 = jnp.where(row == 0, h2, jnp.where(row == 1, h1, pltpu.roll(y, 2, axis=0)))
    w = w_ref[...]
    conv = w[0:1, :] * y2 + w[1:2, :] * y1 + w[2:3, :] * y
    o_ref[...] = (gb_ref[...].astype(F32) * conv * _silu(z_ref[...].astype(F32))).astype(BF16)


def _conv_branch(proj, conv_w):
    s = proj.shape[0]
    tm = 512
    main = lambda c: pl.BlockSpec((tm, D_CONV), lambda i, c=c: (i, c))
    halo = lambda c: pl.BlockSpec((16, D_CONV), lambda i, c=c: (jnp.maximum(i * (tm // 16) - 1, 0), c))
    return pl.pallas_call(
        functools.partial(_conv_kernel, tm=tm),
        grid=(s // tm,),
        in_specs=[main(0), main(1), main(2), main(3), halo(0), halo(2),
                  pl.BlockSpec((CONV_WIDTH, D_CONV), lambda i: (0, 0))],
        out_specs=pl.BlockSpec((tm, D_CONV), lambda i: (i, 0)),
        out_shape=jax.ShapeDtypeStruct((s, D_CONV), BF16),
        compiler_params=_cparams(("parallel",)),
        name="conv_branch",
    )(proj, proj, proj, proj, proj, proj, conv_w)


def _out_kernel(ya_ref, yf_ref, yn_ref, w_ref, x_ref, pg_ref, gate_ref, *rest, with_next):
    y = (jnp.dot(ya_ref[...], w_ref[0:D_CONV, :], preferred_element_type=F32)
         + jnp.dot(yf_ref[...], w_ref[D_CONV:D_CONV + D_FOX, :], preferred_element_type=F32)
         + jnp.dot(yn_ref[...], w_ref[D_CONV + D_FOX:, :], preferred_element_type=F32))
    yn = y * lax.rsqrt(jnp.mean(y * y, axis=-1, keepdims=True) + NORM_EPS) * pg_ref[...]
    xn = x_ref[...] + gate_ref[...] * yn
    if with_next:
        ng_ref, nsc_ref, nsh_ref, xo_ref, h_ref = rest
        h_ref[...] = _modulated_norm(xn, ng_ref[...], nsc_ref[...], nsh_ref[...]).astype(BF16)
    else:
        (xo_ref,) = rest
    xo_ref[...] = xn


def _out_proj(ya, yf, yn, w_out, x2, post_g, gate, nxt):
    s, d = x2.shape
    tm = 256
    rows = lambda n: pl.BlockSpec((tm, n), lambda i: (i, 0))
    vec = pl.BlockSpec((1, d), lambda i: (0, 0))
    in_specs = [rows(D_CONV), rows(D_FOX), rows(D_NSA),
                pl.BlockSpec((d, d), lambda i: (0, 0)), rows(d), vec, vec]
    args = [ya, yf, yn, w_out, x2, post_g, gate]
    out_specs = [rows(d)]
    out_shape = [jax.ShapeDtypeStruct((s, d), F32)]
    if nxt is not None:
        in_specs += [vec, vec, vec]
        args += list(nxt)
        out_specs.append(rows(d))
        out_shape.append(jax.ShapeDtypeStruct((s, d), BF16))
    res = pl.pallas_call(
        functools.partial(_out_kernel, with_next=nxt is not None),
        grid=(s // tm,),
        in_specs=in_specs,
        out_specs=out_specs,
        out_shape=out_shape,
        compiler_params=_cparams(("parallel",)),
        name="out_proj",
    )(*args)
    return res if nxt is not None else (res[0], None)


def _bucket_of_distance():
    max_exact = REL_BUCKETS // 2
    d = np.arange(REL_MAX_DIST)
    nf = np.maximum(d, max_exact).astype(np.float32)
    large = max_exact + (np.log(nf / np.float32(max_exact)) / np.float32(math.log(REL_MAX_DIST / max_exact))
                         * np.float32(REL_BUCKETS - max_exact)).astype(np.int32)
    return np.where(d < max_exact, d, np.minimum(large, REL_BUCKETS - 1))


def _distance_tables():
    b = np.arange(TILE)[:, None]
    a = np.arange(TILE)[None, :]
    none = REL_MAX_DIST
    d0 = np.where(a >= b, a - b, none)
    d1 = np.where(a < b, TILE + a - b, none)
    m = np.arange(24)[:, None]
    dc = a - CMP_STRIDE * m + (2 * TILE - (CMP_LEN - 1))
    dc = np.where((dc >= 0) & (dc < REL_MAX_DIST), dc, none)
    return d0, d1, dc


def _bias_tables(rel_bias):
    bucket = np.concatenate([_bucket_of_distance(), [REL_BUCKETS - 1]])
    rel = (rel_bias - rel_bias[REL_BUCKETS - 1:REL_BUCKETS, :]).reshape(REL_BUCKETS, NSA_G, NSA_R)

    def expand(idx):
        bk = bucket[idx]
        tab = jnp.zeros((NSA_G, idx.shape[0], NSA_R, TILE), F32)
        for b in range(REL_BUCKETS - 1):
            hit = jnp.asarray(bk == b)[None, :, None, :]
            tab = jnp.where(hit, rel[b][:, None, :, None], tab)
        return tab.reshape(NSA_G, idx.shape[0], NSA_R * TILE)

    d0, d1, dc = _distance_tables()
    return expand(d0), expand(d1), expand(dc)


def _cmp_to_sel_t(s, n_cmp_pad):
    n_cmp = (s - CMP_LEN) // CMP_STRIDE + 1
    n_sel = s // SEL_LEN
    c_lo = np.arange(n_cmp_pad)[None, :] * CMP_STRIDE
    sel_start = np.arange(n_sel)[:, None] * SEL_LEN
    overlap = (c_lo < sel_start + SEL_LEN) & (c_lo + CMP_LEN > sel_start)
    overlap &= np.arange(n_cmp_pad)[None, :] < n_cmp
    return jnp.asarray(overlap, dtype=BF16)


def _pack_w_in(w_in):
    sizes = ([D_CONV] * 4 + [D_FOX] * 3 + [H_FOX, D_FOX] + [D_NSA] + [2 * HD] * 6
             + [3 * H_NSA, D_NSA])
    offs = np.concatenate([[0], np.cumsum(sizes)])
    names = ["u", "gb", "gc", "za", "qf", "kf", "vf", "ff", "zf", "qn", "kc", "vc", "ks", "vs",
             "kw", "vw", "gn", "zn"]
    col = {n: w_in[:, :, int(offs[k]):int(offs[k + 1])] for k, n in enumerate(names)}
    scale = HD ** -0.5
    w_a = jnp.concatenate([col["u"], col["gb"], col["gc"], col["za"],
                           col["qf"] * scale, col["kf"], col["zf"],
                           col["qn"] * scale, col["zn"], col["ks"], col["kw"],
                           col["kc"], col["vc"]], axis=-1).astype(BF16)
    w_t = jnp.concatenate([col["vf"], col["vs"], col["vw"]], axis=-1)
    w_t = jnp.swapaxes(w_t, 1, 2).astype(BF16)
    small = jnp.concatenate([col["ff"], col["gn"]], axis=-1)
    small = jnp.pad(small, ((0, 0), (0, 0), (0, TILE - small.shape[-1]))).astype(BF16)
    return w_a, w_t, small, jnp.swapaxes(small, 1, 2)


def kernel(x, c, w_ada, b_ada, pre_norm, post_norm, w_in, b_forget, conv_w,
           cmp_pe_k, cmp_w1_k, cmp_w2_k, cmp_pe_v, cmp_w1_v, cmp_w2_v, w_out, rel_bias):
    bsz, s, d = x.shape
    assert bsz == 1 and d == D_MODEL and s % FOX_T == 0 and s >= 2 * WINDOW
    depth = w_in.shape[0]
    x2 = x.reshape(s, d)
    n_cmp_pad = s // CMP_STRIDE

    mod = _ada_mod(jnp.broadcast_to(c, (8, d)), w_ada, b_ada)[:, 0:1, :]
    shift, scale, gate = mod[:, :, :d], mod[:, :, d:2 * d], mod[:, :, 2 * d:]

    w_a, w_t, w_small, w_small_t = _pack_w_in(w_in)
    w_out_b = w_out.astype(BF16)
    half = CMP_LEN * HD // 2
    w1cat = jnp.stack([jnp.concatenate([w[:, :half], w[:, half:]], axis=-1)
                       for w in (cmp_w1_k, cmp_w1_v)], axis=1).astype(BF16)
    pe = jnp.stack([cmp_pe_k, cmp_pe_v], axis=1).reshape(depth, 2, 2, 1, half)
    pe16 = jnp.broadcast_to(pe, (depth, 2, 2, 8, half)).reshape(depth, 2, 16, half).astype(BF16)
    w2 = jnp.stack([cmp_w2_k, cmp_w2_v], axis=1).astype(BF16)
    b_pad = jnp.pad(b_forget, ((0, 0), (0, TILE - H_FOX))).reshape(depth, 1, TILE)
    t0, t1, cstrip = _bias_tables(rel_bias)
    c2s_t = _cmp_to_sel_t(s, n_cmp_pad)

    h = _prenorm(x2, pre_norm[0:1], scale[0], shift[0])
    for l in range(depth):
        proj = _matmul(h, w_a[l], BF16, 1024, 768)
        proj_t = _matmul_t_tiled(h, w_t[l], 1024, 256)
        small = _matmul(h, w_small[l], F32, 1024, TILE)
        small_t = _matmul_t(h, w_small_t[l], 1024)

        ya = _conv_branch(proj, conv_w[l])

        qa, ka = _fox_prep(small, b_pad[l])
        yf = _fox_attention(proj, qa, ka, proj_t)

        kcvc = proj[:, COL_KCVC:COL_KCVC + 4 * HD]
        x4 = kcvc.reshape(n_cmp_pad, CMP_STRIDE, 4, HD).transpose(2, 0, 1, 3)
        x4 = x4.reshape(4, n_cmp_pad, CMP_STRIDE * HD)
        cmp_n, cmp_t = _compress(x4, w1cat[l], pe16[l], w2[l])
        glog_t = small_t[H_FOX:H_FOX + 3 * H_NSA].reshape(NSA_G, NSA_R * 3, s)
        yn = _nsa_attention(proj, proj_t, cmp_n, cmp_t, c2s_t, cstrip, t0, t1, glog_t)

        nxt = None
        if l + 1 < depth:
            nxt = (pre_norm[l + 1:l + 2], scale[l + 1], shift[l + 1])
        x2, h = _out_proj(ya, yf, yn, w_out_b[l], x2, post_norm[l:l + 1], gate[l], nxt)
    return x2.reshape(bsz, s, d)
```

```python
import functools
import math

import numpy as np
import jax
import jax.numpy as jnp
from jax import lax
from jax.experimental import pallas as pl
from jax.experimental.pallas import tpu as pltpu

F32 = jnp.float32
BF16 = jnp.bfloat16

D_MODEL = 2048
DEPTH = 4
HD = 128
D_CONV = 512
D_FOX = 768
D_NSA = 768
H_FOX = 6
H_NSA = 6
NSA_G = 2
NSA_R = 3
CONV_WIDTH = 3
CMP_LEN = 32
CMP_STRIDE = 16
CMP_HIDDEN = 256
SEL_LEN = 64
SEL_TOPK = 16
WINDOW = 512
REL_BUCKETS = 32
REL_MAX_DIST = 128
NORM_EPS = 1e-6
NEG_BIG = -1e30
FORCED_SCORE = 1e4
LOG2E = math.log2(math.e)
ONES_ROWS = 16

TILE = 128
NSA_CHUNK = 512
FOX_T = 512
FOX_TK = 256
FOX_HEADS = 2
N_A = 6912
N_T = 1280
VMEM_LIMIT = 56 * 1024 * 1024

NT_DIMS = (((1,), (1,)), ((), ()))

BLK_QF, BLK_KF, BLK_ZF = 16, 22, 28
BLK_QN, BLK_ZN, BLK_KS, BLK_KW = 34, 40, 46, 48
COL_KCVC = 6400


def _cparams(sem, vmem=VMEM_LIMIT):
    return pltpu.CompilerParams(dimension_semantics=sem, vmem_limit_bytes=vmem)


def _silu(v):
    return v * jax.nn.sigmoid(v)


def _mod_kernel(c_ref, w_ref, b_ref, o_ref):
    ca = _silu(c_ref[...])
    o_ref[0] = jnp.dot(ca, w_ref[0], precision=lax.Precision.HIGHEST,
                       preferred_element_type=F32) + b_ref[0]


def _ada_mod(c8, w_ada, b_ada):
    depth, d, n = w_ada.shape
    tn = 768
    return pl.pallas_call(
        _mod_kernel,
        grid=(depth, n // tn),
        in_specs=[pl.BlockSpec((8, d), lambda l, j: (0, 0)),
                  pl.BlockSpec((1, d, tn), lambda l, j: (l, 0, j)),
                  pl.BlockSpec((1, 1, tn), lambda l, j: (l, 0, j))],
        out_specs=pl.BlockSpec((1, 8, tn), lambda l, j: (l, 0, j)),
        out_shape=jax.ShapeDtypeStruct((depth, 8, n), F32),
        compiler_params=_cparams(("parallel", "parallel")),
        name="ada_mod",
    )(c8, w_ada, b_ada.reshape(depth, 1, n))


def _modulated_norm(x, g, scale, shift):
    y = x * lax.rsqrt(jnp.mean(x * x, axis=-1, keepdims=True) + NORM_EPS) * g
    return y * (1.0 + scale) + shift


def _prenorm_kernel(x_ref, g_ref, sc_ref, sh_ref, h_ref):
    h_ref[...] = _modulated_norm(x_ref[...], g_ref[...], sc_ref[...], sh_ref[...]).astype(BF16)


def _prenorm(x2, g, scale, shift):
    s, d = x2.shape
    tm = 512
    vec = pl.BlockSpec((1, d), lambda i: (0, 0))
    return pl.pallas_call(
        _prenorm_kernel,
        grid=(s // tm,),
        in_specs=[pl.BlockSpec((tm, d), lambda i: (i, 0)), vec, vec, vec],
        out_specs=pl.BlockSpec((tm, d), lambda i: (i, 0)),
        out_shape=jax.ShapeDtypeStruct((s, d), BF16),
        compiler_params=_cparams(("parallel",)),
        name="prenorm",
    )(x2, g, scale, shift)


def _mm_kernel(h_ref, w_ref, o_ref):
    o_ref[...] = jnp.dot(h_ref[...], w_ref[...], preferred_element_type=F32).astype(o_ref.dtype)


def _matmul(h, w, out_dtype, tm, tn):
    s, d = h.shape
    n = w.shape[1]
    return pl.pallas_call(
        _mm_kernel,
        grid=(s // tm, n // tn),
        in_specs=[pl.BlockSpec((tm, d), lambda i, j: (i, 0)),
                  pl.BlockSpec((d, tn), lambda i, j: (0, j))],
        out_specs=pl.BlockSpec((tm, tn), lambda i, j: (i, j)),
        out_shape=jax.ShapeDtypeStruct((s, n), out_dtype),
        compiler_params=_cparams(("parallel", "arbitrary")),
        name="proj_rows",
    )(h, w)


def _mm_t_tiled_kernel(h_ref, wt_ref, o_ref, *, chunks):
    res = lax.dot_general(wt_ref[...], h_ref[...], NT_DIMS, preferred_element_type=F32)
    for c in range(chunks):
        o_ref[c] = res[:, c * TILE:(c + 1) * TILE].astype(o_ref.dtype)


def _matmul_t_tiled(h, wt, tm, tn):
    s, d = h.shape
    n = wt.shape[0]
    chunks = tm // TILE
    return pl.pallas_call(
        functools.partial(_mm_t_tiled_kernel, chunks=chunks),
        grid=(s // tm, n // tn),
        in_specs=[pl.BlockSpec((tm, d), lambda i, j: (i, 0)),
                  pl.BlockSpec((tn, d), lambda i, j: (j, 0))],
        out_specs=pl.BlockSpec((chunks, tn, TILE), lambda i, j: (i, j, 0)),
        out_shape=jax.ShapeDtypeStruct((s // TILE, n, TILE), BF16),
        compiler_params=_cparams(("parallel", "arbitrary")),
        name="proj_cols_tiled",
    )(h, wt)


def _mm_t_kernel(h_ref, wt_ref, o_ref):
    o_ref[...] = lax.dot_general(wt_ref[...], h_ref[...], NT_DIMS, preferred_element_type=F32)


def _matmul_t(h, wt, tm):
    s, d = h.shape
    n = wt.shape[0]
    return pl.pallas_call(
        _mm_t_kernel,
        grid=(s // tm,),
        in_specs=[pl.BlockSpec((tm, d), lambda i: (i, 0)),
                  pl.BlockSpec((n, d), lambda i: (0, 0))],
        out_specs=pl.BlockSpec((n, tm), lambda i: (0, i)),
        out_shape=jax.ShapeDtypeStruct((n, s), F32),
        compiler_params=_cparams(("parallel",)),
        name="proj_cols_small",
    )(h, wt)


def _fox_prep_kernel(s_ref, b_ref, qa_ref, ka_ref, carry_ref, *, tm):
    @pl.when(pl.program_id(0) == 0)
    def _():
        carry_ref[...] = jnp.zeros_like(carry_ref)

    v = s_ref[...] + b_ref[...]
    c = jnp.minimum(v, 0.0) - jnp.log1p(jnp.exp(-jnp.abs(v)))
    row = lax.broadcasted_iota(jnp.int32, (tm, TILE), 0)
    sh = 1
    while sh < tm:
        c = c + jnp.where(row >= sh, pltpu.roll(c, sh, axis=0), 0.0)
        sh *= 2
    c = c + carry_ref[...]
    carry_ref[...] = c[tm - 1:tm, :]
    c = c * LOG2E

    lane = lax.broadcasted_iota(jnp.int32, (tm, TILE), 1)
    for h in range(H_FOX):
        ch = jnp.broadcast_to(c[:, h:h + 1], (tm, TILE))
        hi = ch.astype(BF16).astype(F32)
        r1 = ch - hi
        mid = r1.astype(BF16).astype(F32)
        lo = r1 - mid
        pieces = jnp.where((lane == 0) | (lane == 3), hi,
                           jnp.where((lane == 1) | (lane == 4), mid, lo))
        qa = jnp.where(lane < 3, pieces, jnp.where(lane < 6, 1.0, 0.0))
        ka = jnp.where(lane < 3, 1.0, jnp.where(lane < 6, -pieces, 0.0))
        qa_ref[:, h * TILE:(h + 1) * TILE] = qa.astype(BF16)
        ka_ref[:, h * TILE:(h + 1) * TILE] = ka.astype(BF16)


def _fox_prep(small, b_pad):
    s = small.shape[0]
    tm = 512
    out = jax.ShapeDtypeStruct((s, H_FOX * TILE), BF16)
    return pl.pallas_call(
        functools.partial(_fox_prep_kernel, tm=tm),
        grid=(s // tm,),
        in_specs=[pl.BlockSpec((tm, TILE), lambda i: (i, 0)),
                  pl.BlockSpec((1, TILE), lambda i: (0, 0))],
        out_specs=[pl.BlockSpec((tm, H_FOX * TILE), lambda i: (i, 0))] * 2,
        out_shape=[out, out],
        scratch_shapes=[pltpu.VMEM((1, TILE), F32)],
        compiler_params=_cparams(("arbitrary",)),
        name="fox_prep",
    )(small, b_pad)


def _fox_kernel(q_ref, qa_ref, k_ref, ka_ref, vt_ref, z_ref, o_ref, s0_ref, s1_ref, acc_ref):
    i = pl.program_id(1)
    t = FOX_T
    tk = FOX_TK
    heads = range(FOX_HEADS)
    col = lambda h: slice(h * TILE, (h + 1) * TILE)
    qaug = [jnp.concatenate([q_ref[:, col(h)], qa_ref[:, col(h)]], axis=1) for h in heads]

    def scores(h, j, rows):
        r0 = pl.multiple_of(j * rows, rows)
        kaug = jnp.concatenate([k_ref[pl.ds(r0, rows), col(h)], ka_ref[pl.ds(r0, rows), col(h)]],
                               axis=1)
        return lax.dot_general(kaug, qaug[h], NT_DIMS, preferred_element_type=F32)

    def values_t(h, j, rows):
        n_sub = rows // TILE
        return jnp.concatenate(
            [jnp.concatenate([vt_ref[n_sub * j + c, col(h), :] for c in range(n_sub)], axis=1),
             jnp.ones((ONES_ROWS, rows), BF16)], axis=0)

    kio = lax.broadcasted_iota(jnp.int32, (t, t), 0)
    qio = lax.broadcasted_iota(jnp.int32, (t, t), 1)
    ms = []
    for h in heads:
        s = jnp.where(kio <= qio, scores(h, i, t), NEG_BIG)
        m = jnp.max(s, axis=0, keepdims=True)
        p = jnp.exp2(s - m)
        ms.append(m)
        acc_ref[h] = jnp.dot(values_t(h, i, t), p.astype(BF16), preferred_element_type=F32)

    def update(h, s, vt, m):
        m_new = jnp.maximum(m, jnp.max(s, axis=0, keepdims=True))
        alpha = jnp.exp2(m - m_new)
        p = jnp.exp2(s - m_new)
        acc_ref[h] = alpha * acc_ref[h] + jnp.dot(vt, p.astype(BF16), preferred_element_type=F32)
        return m_new

    for h in heads:
        s0_ref[h] = scores(h, 0, tk)

    def pair(pp, ms):
        ms = list(ms)
        for h in heads:
            s1_ref[h] = scores(h, 2 * pp + 1, tk)
        for h in heads:
            ms[h] = update(h, s0_ref[h], values_t(h, 2 * pp, tk), ms[h])
        for h in heads:
            s0_ref[h] = scores(h, 2 * pp + 2, tk)
        for h in heads:
            ms[h] = update(h, s1_ref[h], values_t(h, 2 * pp + 1, tk), ms[h])
        return tuple(ms)

    lax.fori_loop(0, i * (t // (2 * tk)), pair, tuple(ms))
    for h in heads:
        o = (acc_ref[h, 0:HD, :] * (1.0 / acc_ref[h, HD:HD + 1, :])).T
        o_ref[:, col(h)] = (o * _silu(z_ref[:, col(h)].astype(F32))).astype(BF16)


def _fox_attention(proj, qa, ka, proj_t):
    s = proj.shape[0]
    t = FOX_T
    nk = s // TILE
    hw = FOX_HEADS * TILE
    return pl.pallas_call(
        _fox_kernel,
        grid=(H_FOX // FOX_HEADS, s // t),
        scratch_shapes=[pltpu.VMEM((FOX_HEADS, FOX_TK, t), F32),
                        pltpu.VMEM((FOX_HEADS, FOX_TK, t), F32),
                        pltpu.VMEM((FOX_HEADS, HD + ONES_ROWS, t), F32)],
        in_specs=[pl.BlockSpec((t, hw), lambda h, i: (i, BLK_QF // FOX_HEADS + h)),
                  pl.BlockSpec((t, hw), lambda h, i: (i, h)),
                  pl.BlockSpec((s, hw), lambda h, i: (0, BLK_KF // FOX_HEADS + h)),
                  pl.BlockSpec((s, hw), lambda h, i: (0, h)),
                  pl.BlockSpec((nk, hw, TILE), lambda h, i: (0, h, 0)),
                  pl.BlockSpec((t, hw), lambda h, i: (i, BLK_ZF // FOX_HEADS + h))],
        out_specs=pl.BlockSpec((t, hw), lambda h, i: (i, h)),
        out_shape=jax.ShapeDtypeStruct((s, D_FOX), BF16),
        compiler_params=_cparams(("parallel", "parallel")),
        name="fox_attention",
    )(proj, qa, proj, ka, proj_t, proj)


def _compress_kernel(x_ref, w1_ref, pe_ref, w2_ref, o_ref, ot_ref):
    n = x_ref.shape[0]
    ab = jnp.dot(x_ref[...], w1_ref[0], preferred_element_type=F32)
    pe = jnp.dot(pe_ref[0], w1_ref[0], preferred_element_type=F32)
    a = ab[:, :CMP_HIDDEN]
    b_next = pltpu.roll(ab[:, CMP_HIDDEN:], n - 1, axis=0)
    hid = a + b_next + pe[0:1, :CMP_HIDDEN] + pe[8:9, CMP_HIDDEN:]
    out = jnp.dot(_silu(hid).astype(BF16), w2_ref[0], preferred_element_type=F32)
    o_ref[...] = out.astype(BF16)
    ot_ref[...] = out.T.astype(BF16)


def _compress(x4, w1cat, pe16, w2):
    _, n, k = x4.shape
    kind = lambda c: (c // NSA_G, 0, 0)
    return pl.pallas_call(
        _compress_kernel,
        grid=(2 * NSA_G,),
        in_specs=[pl.BlockSpec((None, n, k), lambda c: (c, 0, 0)),
                  pl.BlockSpec((1, k, 2 * CMP_HIDDEN), kind),
                  pl.BlockSpec((1, 16, k), kind),
                  pl.BlockSpec((1, CMP_HIDDEN, HD), kind)],
        out_specs=[pl.BlockSpec((None, n, HD), lambda c: (c, 0, 0)),
                   pl.BlockSpec((None, HD, n), lambda c: (c, 0, 0))],
        out_shape=[jax.ShapeDtypeStruct((2 * NSA_G, n, HD), BF16),
                   jax.ShapeDtypeStruct((2 * NSA_G, HD, n), BF16)],
        compiler_params=_cparams(("parallel",)),
        name="nsa_compress",
    )(x4, w1cat, pe16, w2)


def _nsa_kernel_one_group(q0_ref, q1_ref, q2_ref, z0_ref, z1_ref, z2_ref,
                ks_ref, kw_ref, vst_ref, vwt_ref, kc_ref, vct_ref,
                c2s_ref, cstrip_ref, t0_ref, t1_ref, gl_ref, o_ref,
                sc_ref, sel_ref, m_s, l_s, acc_s, ss_ref, *, n_cmp):
    i = pl.program_id(1)
    w3 = NSA_R * TILE
    qs = jnp.concatenate([q0_ref[...], q1_ref[...], q2_ref[...]], axis=0)
    b_io = lax.broadcasted_iota(jnp.int32, (TILE, w3), 0)
    a_io = lax.broadcasted_iota(jnp.int32, (TILE, w3), 1) & (TILE - 1)
    t0 = t0_ref[...]
    t1 = t1_ref[...]

    sc = lax.dot_general(kc_ref[...], qs, NT_DIMS, preferred_element_type=F32)
    sc_ref[0:16, :] = jnp.zeros((16, w3), F32)
    sc_ref[16:16 + n_cmp, :] = sc
    w0 = pl.multiple_of(8 * i, 8)
    sc_ref[pl.ds(w0, 24), :] = sc_ref[pl.ds(w0, 24), :] + cstrip_ref[...]
    sc = sc_ref[16:16 + n_cmp, :]
    n_io = lax.broadcasted_iota(jnp.int32, (n_cmp, w3), 0)
    qa_io = lax.broadcasted_iota(jnp.int32, (n_cmp, w3), 1) & (TILE - 1)
    valid = (CMP_STRIDE * n_io + (CMP_LEN - 1)) <= (TILE * i + qa_io)
    scm = jnp.where(valid, sc, NEG_BIG)
    mc = jnp.max(scm, axis=0, keepdims=True)
    pc = jnp.where(valid, jnp.exp2(scm - mc), 0.0)
    lc = jnp.sum(pc, axis=0, keepdims=True)
    pcn = pc * jnp.where(lc > 0.0, 1.0 / lc, 0.0)
    oc_t = jnp.dot(vct_ref[...], pcn.astype(BF16), preferred_element_type=F32)

    psum = pcn[:, 0:TILE] + pcn[:, TILE:2 * TILE] + pcn[:, 2 * TILE:3 * TILE]
    p_hi = psum.astype(BF16)
    p_lo = (psum - p_hi.astype(F32)).astype(BF16)
    c2s = c2s_ref[...]
    imp = (jnp.dot(c2s, p_hi, preferred_element_type=F32)
           + jnp.dot(c2s, p_lo, preferred_element_type=F32))
    n_sel = imp.shape[0]
    m_io = lax.broadcasted_iota(jnp.int32, (n_sel, TILE), 0)
    m_f = m_io.astype(F32)
    qpos = TILE * i + lax.broadcasted_iota(jnp.int32, (n_sel, TILE), 1)
    own = qpos >> 6
    forced = (m_io == 0) | (m_io == own) | (m_io == own - 1)
    score = jnp.where(SEL_LEN * m_io <= qpos, jnp.where(forced, FORCED_SCORE, imp), -1.0)

    def pick_one(_, carry):
        cur, sel = carry
        mx = jnp.max(cur, axis=0, keepdims=True)
        idx = jnp.min(jnp.where(cur == mx, m_f, float(n_sel)), axis=0, keepdims=True)
        pick = m_f == idx
        keep = jnp.where(mx >= 0.0, 1.0, 0.0)
        return jnp.where(pick, -2.0, cur), jnp.where(pick, keep, sel)

    _, sel = lax.fori_loop(0, min(SEL_TOPK, n_sel), pick_one,
                           (score, jnp.zeros((n_sel, TILE), F32)))
    sel_ref[...] = sel

    tpc = NSA_CHUNK // TILE

    def key_scores(k_ref, tile0, n_tiles):
        r0 = pl.multiple_of(tile0 * TILE, TILE)
        return lax.dot_general(k_ref[pl.ds(r0, n_tiles * TILE), :], qs, NT_DIMS,
                               preferred_element_type=F32)

    def values_t(vt_ref, tile0, n_tiles):
        return jnp.concatenate([vt_ref[tile0 + c] for c in range(n_tiles)], axis=1)

    def add_near_bias(d_tile, n_tiles):
        for c in range(n_tiles):
            @pl.when(d_tile == c + 1)
            def _(c=c):
                ss_ref[c * TILE:(c + 1) * TILE, :] = ss_ref[c * TILE:(c + 1) * TILE, :] + t1

            @pl.when(d_tile == c)
            def _(c=c):
                ss_ref[c * TILE:(c + 1) * TILE, :] = jnp.where(
                    b_io <= a_io, ss_ref[c * TILE:(c + 1) * TILE, :] + t0, NEG_BIG)

    def sel_mask(c):
        rows = sel_ref[pl.ds(pl.multiple_of(c * (NSA_CHUNK // SEL_LEN), 8), NSA_CHUNK // SEL_LEN), :]
        mk = jnp.concatenate([jnp.broadcast_to(rows[b:b + 1, :], (SEL_LEN, TILE))
                              for b in range(NSA_CHUNK // SEL_LEN)], axis=0)
        return jnp.concatenate([mk, mk, mk], axis=1) > 0.5

    def sel_update(c, s):
        s = jnp.where(sel_mask(c), s, NEG_BIG)
        m_old = m_s[...]
        m_new = jnp.maximum(m_old, jnp.max(s, axis=0, keepdims=True))
        alpha = jnp.exp2(m_old - m_new)
        p = jnp.exp2(s - m_new)
        l_s[...] = alpha * l_s[...] + jnp.sum(p, axis=0, keepdims=True)
        acc_s[...] = alpha * acc_s[...] + jnp.dot(values_t(vst_ref, c * tpc, tpc), p.astype(BF16),
                                                  preferred_element_type=F32)
        m_s[...] = m_new

    m_s[...] = jnp.full(m_s.shape, NEG_BIG, F32)
    l_s[...] = jnp.zeros(l_s.shape, F32)
    acc_s[...] = jnp.zeros(acc_s.shape, F32)

    c_own = i // tpc
    c_prev = jnp.maximum(i - 1, 0) // tpc

    def far_chunk(c, carry):
        sel_update(c, key_scores(ks_ref, c * tpc, tpc))
        return carry

    lax.fori_loop(0, c_prev, far_chunk, 0)

    def near_chunk(c):
        ss_ref[0:NSA_CHUNK, :] = key_scores(ks_ref, c * tpc, tpc)
        add_near_bias(i - c * tpc, tpc)
        sel_update(c, ss_ref[0:NSA_CHUNK, :])

    @pl.when(c_prev != c_own)
    def _():
        near_chunk(c_prev)

    near_chunk(c_own)

    n_wt = WINDOW // TILE + 1
    w_tile0 = jnp.maximum(i - WINDOW // TILE, 0)
    ss_ref[...] = key_scores(kw_ref, w_tile0, n_wt)
    add_near_bias(i - w_tile0, n_wt)

    @pl.when(i >= WINDOW // TILE)
    def _():
        ss_ref[0:TILE, :] = jnp.where(a_io < b_io, ss_ref[0:TILE, :], NEG_BIG)

    for c in range(1, n_wt):
        @pl.when(i < c)
        def _(c=c):
            ss_ref[c * TILE:(c + 1) * TILE, :] = jnp.full((TILE, w3), NEG_BIG, F32)

    sw = ss_ref[...]
    m_w = jnp.max(sw, axis=0, keepdims=True)
    p_w = jnp.exp2(sw - m_w)
    l_w = jnp.sum(p_w, axis=0, keepdims=True)
    acc_w = jnp.dot(values_t(vwt_ref, w_tile0, n_wt), p_w.astype(BF16), preferred_element_type=F32)

    gates = jax.nn.sigmoid(gl_ref[...])

    def gate_row(branch):
        return jnp.concatenate([gates[r * 3 + branch:r * 3 + branch + 1, :] for r in range(NSA_R)],
                               axis=1)

    out_t = (oc_t * gate_row(0)
             + acc_s[...] * (gate_row(1) / l_s[...])
             + acc_w * (gate_row(2) / l_w))
    z_refs = (z0_ref, z1_ref, z2_ref)
    for r in range(NSA_R):
        o_r = out_t[:, r * TILE:(r + 1) * TILE].T
        o_ref[:, r * TILE:(r + 1) * TILE] = (o_r * _silu(z_refs[r][...].astype(F32))).astype(BF16)


def _nsa_attention_one_group(proj, proj_t, cmp_n, cmp_t, c2s_t, cstrip, t0, t1, glog_t):
    s = proj.shape[0]
    nk = s // TILE
    n_cmp = cmp_n.shape[1]
    n_sel = s // SEL_LEN
    w3 = NSA_R * TILE
    qspec = lambda r: pl.BlockSpec((TILE, TILE), lambda g, i, r=r: (i, BLK_QN + NSA_R * g + r))
    zspec = lambda r: pl.BlockSpec((TILE, TILE), lambda g, i, r=r: (i, BLK_ZN + NSA_R * g + r))
    per_group = lambda shape: pl.BlockSpec((None,) + shape, lambda g, i: (g, 0, 0))
    return pl.pallas_call(
        functools.partial(_nsa_kernel, n_cmp=n_cmp),
        grid=(NSA_G, s // TILE),
        in_specs=[qspec(0), qspec(1), qspec(2), zspec(0), zspec(1), zspec(2),
                  pl.BlockSpec((s, TILE), lambda g, i: (0, BLK_KS + g)),
                  pl.BlockSpec((s, TILE), lambda g, i: (0, BLK_KW + g)),
                  pl.BlockSpec((nk, HD, TILE), lambda g, i: (0, H_FOX + g, 0)),
                  pl.BlockSpec((nk, HD, TILE), lambda g, i: (0, H_FOX + NSA_G + g, 0)),
                  pl.BlockSpec((None, n_cmp, HD), lambda g, i: (g, 0, 0)),
                  pl.BlockSpec((None, HD, n_cmp), lambda g, i: (NSA_G + g, 0, 0)),
                  pl.BlockSpec((n_sel, n_cmp), lambda g, i: (0, 0)),
                  per_group((24, w3)), per_group((TILE, w3)), per_group((TILE, w3)),
                  pl.BlockSpec((None, NSA_R * 3, TILE), lambda g, i: (g, 0, i))],
        out_specs=pl.BlockSpec((TILE, w3), lambda g, i: (i, g)),
        out_shape=jax.ShapeDtypeStruct((s, D_NSA), BF16),
        scratch_shapes=[pltpu.VMEM((16 + n_cmp + 8, w3), F32),
                        pltpu.VMEM((n_sel, TILE), F32),
                        pltpu.VMEM((1, w3), F32), pltpu.VMEM((1, w3), F32), pltpu.VMEM((HD, w3), F32),
                        pltpu.VMEM((WINDOW + TILE, w3), F32)],
        compiler_params=_cparams(("parallel", "parallel")),
        name="nsa_attention",
    )(proj, proj, proj, proj, proj, proj, proj, proj, proj_t, proj_t,
      cmp_n, cmp_t, c2s_t, cstrip, t0, t1, glog_t)


def _nsa_kernel(*refs, n_cmp):
    q_refs, z_refs = refs[0:H_NSA], refs[H_NSA:2 * H_NSA]
    (ks_ref, kw_ref, vst_ref, vwt_ref, kc_ref, vct_ref, c2s_ref, cstrip_ref, t0_ref, t1_ref,
     gl_ref, o_ref, sc_ref, sel_ref, m_s, acc_s, ss_ref) = refs[2 * H_NSA:]
    i = pl.program_id(0)
    w3 = NSA_R * TILE
    groups = range(NSA_G)
    col = lambda g: slice(g * TILE, (g + 1) * TILE)
    qs = [jnp.concatenate([q_refs[NSA_R * g + r][...] for r in range(NSA_R)], axis=0)
          for g in groups]
    b_io = lax.broadcasted_iota(jnp.int32, (TILE, w3), 0)
    a_io = lax.broadcasted_iota(jnp.int32, (TILE, w3), 1) & (TILE - 1)

    w0 = pl.multiple_of(8 * i, 8)
    n_io = lax.broadcasted_iota(jnp.int32, (n_cmp, w3), 0)
    qa_io = lax.broadcasted_iota(jnp.int32, (n_cmp, w3), 1) & (TILE - 1)
    valid = (CMP_STRIDE * n_io + (CMP_LEN - 1)) <= (TILE * i + qa_io)
    oc_t, psums = [], []
    for g in groups:
        sc_ref[g, 0:16, :] = jnp.zeros((16, w3), F32)
        sc_ref[g, 16:16 + n_cmp, :] = lax.dot_general(kc_ref[g], qs[g], NT_DIMS,
                                                      preferred_element_type=F32)
        sc_ref[g, pl.ds(w0, 24), :] = sc_ref[g, pl.ds(w0, 24), :] + cstrip_ref[g]
        scm = jnp.where(valid, sc_ref[g, 16:16 + n_cmp, :], NEG_BIG)
        mc = jnp.max(scm, axis=0, keepdims=True)
        pc = jnp.where(valid, jnp.exp2(scm - mc), 0.0)
        lc = jnp.sum(pc, axis=0, keepdims=True)
        pcn = pc * jnp.where(lc > 0.0, 1.0 / lc, 0.0)
        oc_t.append(jnp.dot(vct_ref[g], pcn.astype(BF16), preferred_element_type=F32))
        psums.append(pcn[:, 0:TILE] + pcn[:, TILE:2 * TILE] + pcn[:, 2 * TILE:3 * TILE])

    psum = jnp.concatenate(psums, axis=1)
    p_hi = psum.astype(BF16)
    p_lo = (psum - p_hi.astype(F32)).astype(BF16)
    c2s = c2s_ref[...]
    imp = (jnp.dot(c2s, p_hi, preferred_element_type=F32)
           + jnp.dot(c2s, p_lo, preferred_element_type=F32))
    n_sel = imp.shape[0]
    gw = NSA_G * TILE
    m_io = lax.broadcasted_iota(jnp.int32, (n_sel, gw), 0)
    m_f = m_io.astype(F32)
    qpos = TILE * i + (lax.broadcasted_iota(jnp.int32, (n_sel, gw), 1) & (TILE - 1))
    own = qpos >> 6
    forced = (m_io == 0) | (m_io == own) | (m_io == own - 1)
    score = jnp.where(SEL_LEN * m_io <= qpos, jnp.where(forced, FORCED_SCORE, imp), -1.0)

    def pick_one(_, carry):
        cur, sel = carry
        mx = jnp.max(cur, axis=0, keepdims=True)
        idx = jnp.min(jnp.where(cur == mx, m_f, float(n_sel)), axis=0, keepdims=True)
        pick = m_f == idx
        keep = jnp.where(mx >= 0.0, 1.0, 0.0)
        return jnp.where(pick, -2.0, cur), jnp.where(pick, keep, sel)

    _, sel = lax.fori_loop(0, min(SEL_TOPK, n_sel), pick_one,
                           (score, jnp.zeros((n_sel, gw), F32)))
    sel_ref[...] = sel

    tpc = NSA_CHUNK // TILE
    bpc = NSA_CHUNK // SEL_LEN

    def key_scores(k_ref, g, tile0, n_tiles):
        r0 = pl.multiple_of(tile0 * TILE, TILE)
        return lax.dot_general(k_ref[pl.ds(r0, n_tiles * TILE), col(g)], qs[g], NT_DIMS,
                               preferred_element_type=F32)

    def values_t(vt_ref, g, tile0, n_tiles):
        return jnp.concatenate(
            [jnp.concatenate([vt_ref[tile0 + c, col(g), :] for c in range(n_tiles)], axis=1),
             jnp.ones((ONES_ROWS, n_tiles * TILE), BF16)], axis=0)

    def add_near_bias(d_tile, n_tiles):
        for c in range(n_tiles):
            rows = slice(c * TILE, (c + 1) * TILE)

            @pl.when(d_tile == c + 1)
            def _(rows=rows):
                for g in groups:
                    ss_ref[g, rows, :] = ss_ref[g, rows, :] + t1_ref[g]

            @pl.when(d_tile == c)
            def _(rows=rows):
                for g in groups:
                    ss_ref[g, rows, :] = jnp.where(b_io <= a_io, ss_ref[g, rows, :] + t0_ref[g],
                                                   NEG_BIG)

    def sel_mask(g, c):
        rows = sel_ref[pl.ds(pl.multiple_of(c * bpc, 8), bpc), col(g)]
        mk = jnp.concatenate([jnp.broadcast_to(rows[b:b + 1, :], (SEL_LEN, TILE))
                              for b in range(bpc)], axis=0)
        return jnp.concatenate([mk, mk, mk], axis=1) > 0.5

    def sel_update(g, c, s):
        s = jnp.where(sel_mask(g, c), s, NEG_BIG)
        m_old = m_s[g]
        m_new = jnp.maximum(m_old, jnp.max(s, axis=0, keepdims=True))
        alpha = jnp.exp2(m_old - m_new)
        p = jnp.exp2(s - m_new)
        acc_s[g] = alpha * acc_s[g] + jnp.dot(values_t(vst_ref, g, c * tpc, tpc), p.astype(BF16),
                                              preferred_element_type=F32)
        m_s[g] = m_new

    m_s[...] = jnp.full(m_s.shape, NEG_BIG, F32)
    acc_s[...] = jnp.zeros(acc_s.shape, F32)

    c_own = i // tpc
    c_prev = jnp.maximum(i - 1, 0) // tpc

    def far_chunk(c, carry):
        for g in groups:
            sel_update(g, c, key_scores(ks_ref, g, c * tpc, tpc))
        return carry

    lax.fori_loop(0, c_prev, far_chunk, 0)

    def near_chunk(c):
        for g in groups:
            ss_ref[g, 0:NSA_CHUNK, :] = key_scores(ks_ref, g, c * tpc, tpc)
        add_near_bias(i - c * tpc, tpc)
        for g in groups:
            sel_update(g, c, ss_ref[g, 0:NSA_CHUNK, :])

    @pl.when(c_prev != c_own)
    def _():
        near_chunk(c_prev)

    near_chunk(c_own)

    n_wt = WINDOW // TILE + 1
    w_tile0 = jnp.maximum(i - WINDOW // TILE, 0)
    for g in groups:
        ss_ref[g] = key_scores(kw_ref, g, w_tile0, n_wt)
    add_near_bias(i - w_tile0, n_wt)

    @pl.when(i >= WINDOW // TILE)
    def _():
        for g in groups:
            ss_ref[g, 0:TILE, :] = jnp.where(a_io < b_io, ss_ref[g, 0:TILE, :], NEG_BIG)

    for c in range(1, n_wt):
        @pl.when(i < c)
        def _(c=c):
            for g in groups:
                ss_ref[g, c * TILE:(c + 1) * TILE, :] = jnp.full((TILE, w3), NEG_BIG, F32)

    for g in groups:
        sw = ss_ref[g]
        p_w = jnp.exp2(sw - jnp.max(sw, axis=0, keepdims=True))
        acc_w = jnp.dot(values_t(vwt_ref, g, w_tile0, n_wt), p_w.astype(BF16),
                        preferred_element_type=F32)
        gates = jax.nn.sigmoid(gl_ref[g])

        def gate_row(branch):
            return jnp.concatenate([gates[r * 3 + branch:r * 3 + branch + 1, :]
                                    for r in range(NSA_R)], axis=1)

        out_t = (oc_t[g] * gate_row(0)
                 + acc_s[g, 0:HD, :] * (gate_row(1) / acc_s[g, HD:HD + 1, :])
                 + acc_w[0:HD, :] * (gate_row(2) / acc_w[HD:HD + 1, :]))
        for r in range(NSA_R):
            h = NSA_R * g + r
            o_r = out_t[:, r * TILE:(r + 1) * TILE].T
            o_ref[:, h * TILE:(h + 1) * TILE] = (
                o_r * _silu(z_refs[h][...].astype(F32))).astype(BF16)


def _nsa_attention(proj, proj_t, cmp_n, cmp_t, c2s_t, cstrip, t0, t1, glog_t):
    s = proj.shape[0]
    nk = s // TILE
    n_cmp = cmp_n.shape[1]
    n_sel = s // SEL_LEN
    w3 = NSA_R * TILE
    gw = NSA_G * TILE
    qspec = lambda h: pl.BlockSpec((TILE, TILE), lambda i, h=h: (i, BLK_QN + h))
    zspec = lambda h: pl.BlockSpec((TILE, TILE), lambda i, h=h: (i, BLK_ZN + h))
    whole = lambda shape: pl.BlockSpec(shape, lambda i: (0,) * len(shape))
    return pl.pallas_call(
        functools.partial(_nsa_kernel, n_cmp=n_cmp),
        grid=(s // TILE,),
        in_specs=[qspec(h) for h in range(H_NSA)] + [zspec(h) for h in range(H_NSA)] + [
            pl.BlockSpec((s, gw), lambda i: (0, BLK_KS // NSA_G)),
            pl.BlockSpec((s, gw), lambda i: (0, BLK_KW // NSA_G)),
            pl.BlockSpec((nk, gw, TILE), lambda i: (0, H_FOX // NSA_G, 0)),
            pl.BlockSpec((nk, gw, TILE), lambda i: (0, H_FOX // NSA_G + 1, 0)),
            pl.BlockSpec((NSA_G, n_cmp, HD), lambda i: (0, 0, 0)),
            pl.BlockSpec((NSA_G, HD, n_cmp), lambda i: (1, 0, 0)),
            whole((n_sel, n_cmp)), whole((NSA_G, 24, w3)), whole((NSA_G, TILE, w3)),
            whole((NSA_G, TILE, w3)),
            pl.BlockSpec((NSA_G, NSA_R * 3, TILE), lambda i: (0, 0, i))],
        out_specs=pl.BlockSpec((TILE, D_NSA), lambda i: (i, 0)),
        out_shape=jax.ShapeDtypeStruct((s, D_NSA), BF16),
        scratch_shapes=[pltpu.VMEM((NSA_G, 16 + n_cmp + 8, w3), F32),
                        pltpu.VMEM((n_sel, gw), F32),
                        pltpu.VMEM((NSA_G, 1, w3), F32),
                        pltpu.VMEM((NSA_G, HD + ONES_ROWS, w3), F32),
                        pltpu.VMEM((NSA_G, WINDOW + TILE, w3), F32)],
        compiler_params=_cparams(("parallel",)),
        name="nsa_attention",
    )(*([proj] * (2 * H_NSA)), proj, proj, proj_t, proj_t,
      cmp_n, cmp_t, c2s_t, cstrip, t0, t1, glog_t)


def _conv_kernel(u_ref, gb_ref, gc_ref, z_ref, uh_ref, gch_ref, w_ref, o_ref, *, tm):
    i = pl.program_id(0)
    y = gc_ref[...].astype(F32) * u_ref[...].astype(F32)
    yh = gch_ref[...].astype(F32) * uh_ref[...].astype(F32)
    yh = jnp.where(i > 0, yh, 0.0)
    h1 = yh[15:16, :]
    h2 = yh[14:15, :]
    row = lax.broadcasted_iota(jnp.int32, (tm, D_CONV), 0)
    y1 = jnp.where(row == 0, h1, pltpu.roll(y, 1, axis=0))
    y2 = jnp.where(row == 0, h2, jnp.where(row == 1, h1, pltpu.roll(y, 2, axis=0)))
    w = w_ref[...]
    conv = w[0:1, :] * y2 + w[1:2, :] * y1 + w[2:3, :] * y
    o_ref[...] = (gb_ref[...].astype(F32) * conv * _silu(z_ref[...].astype(F32))).astype(BF16)


def _conv_branch(proj, conv_w):
    s = proj.shape[0]
    tm = 512
    main = lambda c: pl.BlockSpec((tm, D_CONV), lambda i, c=c: (i, c))
    halo = lambda c: pl.BlockSpec((16, D_CONV), lambda i, c=c: (jnp.maximum(i * (tm // 16) - 1, 0), c))
    return pl.pallas_call(
        functools.partial(_conv_kernel, tm=tm),
        grid=(s // tm,),
        in_specs=[main(0), main(1), main(2), main(3), halo(0), halo(2),
                  pl.BlockSpec((CONV_WIDTH, D_CONV), lambda i: (0, 0))],
        out_specs=pl.BlockSpec((tm, D_CONV), lambda i: (i, 0)),
        out_shape=jax.ShapeDtypeStruct((s, D_CONV), BF16),
        compiler_params=_cparams(("parallel",)),
        name="conv_branch",
    )(proj, proj, proj, proj, proj, proj, conv_w)


def _out_kernel(ya_ref, yf_ref, yn_ref, w_ref, x_ref, pg_ref, gate_ref, *rest, with_next):
    y = (jnp.dot(ya_ref[...], w_ref[0:D_CONV, :], preferred_element_type=F32)
         + jnp.dot(yf_ref[...], w_ref[D_CONV:D_CONV + D_FOX, :], preferred_element_type=F32)
         + jnp.dot(yn_ref[...], w_ref[D_CONV + D_FOX:, :], preferred_element_type=F32))
    yn = y * lax.rsqrt(jnp.mean(y * y, axis=-1, keepdims=True) + NORM_EPS) * pg_ref[...]
    xn = x_ref[...] + gate_ref[...] * yn
    if with_next:
        ng_ref, nsc_ref, nsh_ref, xo_ref, h_ref = rest
        h_ref[...] = _modulated_norm(xn, ng_ref[...], nsc_ref[...], nsh_ref[...]).astype(BF16)
    else:
        (xo_ref,) = rest
    xo_ref[...] = xn


def _out_proj(ya, yf, yn, w_out, x2, post_g, gate, nxt):
    s, d = x2.shape
    tm = 256
    rows = lambda n: pl.BlockSpec((tm, n), lambda i: (i, 0))
    vec = pl.BlockSpec((1, d), lambda i: (0, 0))
    in_specs = [rows(D_CONV), rows(D_FOX), rows(D_NSA),
                pl.BlockSpec((d, d), lambda i: (0, 0)), rows(d), vec, vec]
    args = [ya, yf, yn, w_out, x2, post_g, gate]
    out_specs = [rows(d)]
    out_shape = [jax.ShapeDtypeStruct((s, d), F32)]
    if nxt is not None:
        in_specs += [vec, vec, vec]
        args += list(nxt)
        out_specs.append(rows(d))
        out_shape.append(jax.ShapeDtypeStruct((s, d), BF16))
    res = pl.pallas_call(
        functools.partial(_out_kernel, with_next=nxt is not None),
        grid=(s // tm,),
        in_specs=in_specs,
        out_specs=out_specs,
        out_shape=out_shape,
        compiler_params=_cparams(("parallel",)),
        name="out_proj",
    )(*args)
    return res if nxt is not None else (res[0], None)


def _bucket_of_distance():
    max_exact = REL_BUCKETS // 2
    d = np.arange(REL_MAX_DIST)
    nf = np.maximum(d, max_exact).astype(np.float32)
    large = max_exact + (np.log(nf / np.float32(max_exact)) / np.float32(math.log(REL_MAX_DIST / max_exact))
                         * np.float32(REL_BUCKETS - max_exact)).astype(np.int32)
    return np.where(d < max_exact, d, np.minimum(large, REL_BUCKETS - 1))


def _distance_tables():
    b = np.arange(TILE)[:, None]
    a = np.arange(TILE)[None, :]
    none = REL_MAX_DIST
    d0 = np.where(a >= b, a - b, none)
    d1 = np.where(a < b, TILE + a - b, none)
    m = np.arange(24)[:, None]
    dc = a - CMP_STRIDE * m + (2 * TILE - (CMP_LEN - 1))
    dc = np.where((dc >= 0) & (dc < REL_MAX_DIST), dc, none)
    return d0, d1, dc


def _bias_tables(rel_bias):
    bucket = np.concatenate([_bucket_of_distance(), [REL_BUCKETS - 1]])
    rel = (rel_bias - rel_bias[REL_BUCKETS - 1:REL_BUCKETS, :]) * LOG2E
    rel = rel.reshape(REL_BUCKETS, NSA_G, NSA_R)

    def expand(idx):
        bk = bucket[idx]
        tab = jnp.zeros((NSA_G, idx.shape[0], NSA_R, TILE), F32)
        for b in range(REL_BUCKETS - 1):
            hit = jnp.asarray(bk == b)[None, :, None, :]
            tab = jnp.where(hit, rel[b][:, None, :, None], tab)
        return tab.reshape(NSA_G, idx.shape[0], NSA_R * TILE)

    d0, d1, dc = _distance_tables()
    return expand(d0), expand(d1), expand(dc)


def _cmp_to_sel_t(s, n_cmp_pad):
    n_cmp = (s - CMP_LEN) // CMP_STRIDE + 1
    n_sel = s // SEL_LEN
    c_lo = np.arange(n_cmp_pad)[None, :] * CMP_STRIDE
    sel_start = np.arange(n_sel)[:, None] * SEL_LEN
    overlap = (c_lo < sel_start + SEL_LEN) & (c_lo + CMP_LEN > sel_start)
    overlap &= np.arange(n_cmp_pad)[None, :] < n_cmp
    return jnp.asarray(overlap, dtype=BF16)


def _pack_w_in(w_in):
    sizes = ([D_CONV] * 4 + [D_FOX] * 3 + [H_FOX, D_FOX] + [D_NSA] + [2 * HD] * 6
             + [3 * H_NSA, D_NSA])
    offs = np.concatenate([[0], np.cumsum(sizes)])
    names = ["u", "gb", "gc", "za", "qf", "kf", "vf", "ff", "zf", "qn", "kc", "vc", "ks", "vs",
             "kw", "vw", "gn", "zn"]
    col = {n: w_in[:, :, int(offs[k]):int(offs[k + 1])] for k, n in enumerate(names)}
    scale = HD ** -0.5 * LOG2E
    w_a = jnp.concatenate([col["u"], col["gb"], col["gc"], col["za"],
                           col["qf"] * scale, col["kf"], col["zf"],
                           col["qn"] * scale, col["zn"], col["ks"], col["kw"],
                           col["kc"], col["vc"]], axis=-1).astype(BF16)
    w_t = jnp.concatenate([col["vf"], col["vs"], col["vw"]], axis=-1)
    w_t = jnp.swapaxes(w_t, 1, 2).astype(BF16)
    small = jnp.concatenate([col["ff"], col["gn"]], axis=-1)
    small = jnp.pad(small, ((0, 0), (0, 0), (0, TILE - small.shape[-1]))).astype(BF16)
    return w_a, w_t, small, jnp.swapaxes(small, 1, 2)


def kernel(x, c, w_ada, b_ada, pre_norm, post_norm, w_in, b_forget, conv_w,
           cmp_pe_k, cmp_w1_k, cmp_w2_k, cmp_pe_v, cmp_w1_v, cmp_w2_v, w_out, rel_bias):
    bsz, s, d = x.shape
    assert bsz == 1 and d == D_MODEL and s % FOX_T == 0 and s >= 2 * WINDOW
    depth = w_in.shape[0]
    x2 = x.reshape(s, d)
    n_cmp_pad = s // CMP_STRIDE

    mod = _ada_mod(jnp.broadcast_to(c, (8, d)), w_ada, b_ada)[:, 0:1, :]
    shift, scale, gate = mod[:, :, :d], mod[:, :, d:2 * d], mod[:, :, 2 * d:]

    w_a, w_t, w_small, w_small_t = _pack_w_in(w_in)
    w_out_b = w_out.astype(BF16)
    half = CMP_LEN * HD // 2
    w1cat = jnp.stack([jnp.concatenate([w[:, :half], w[:, half:]], axis=-1)
                       for w in (cmp_w1_k, cmp_w1_v)], axis=1).astype(BF16)
    pe = jnp.stack([cmp_pe_k, cmp_pe_v], axis=1).reshape(depth, 2, 2, 1, half)
    pe16 = jnp.broadcast_to(pe, (depth, 2, 2, 8, half)).reshape(depth, 2, 16, half).astype(BF16)
    w2 = jnp.stack([cmp_w2_k, cmp_w2_v], axis=1).astype(BF16)
    b_pad = jnp.pad(b_forget, ((0, 0), (0, TILE - H_FOX))).reshape(depth, 1, TILE)
    t0, t1, cstrip = _bias_tables(rel_bias)
    c2s_t = _cmp_to_sel_t(s, n_cmp_pad)

    h = _prenorm(x2, pre_norm[0:1], scale[0], shift[0])
    for l in range(depth):
        proj = _matmul(h, w_a[l], BF16, 1024, 768)
        proj_t = _matmul_t_tiled(h, w_t[l], 1024, 256)
        small = _matmul(h, w_small[l], F32, 1024, TILE)
        small_t = _matmul_t(h, w_small_t[l], 1024)

        ya = _conv_branch(proj, conv_w[l])

        qa, ka = _fox_prep(small, b_pad[l])
        yf = _fox_attention(proj, qa, ka, proj_t)

        kcvc = proj[:, COL_KCVC:COL_KCVC + 4 * HD]
        x4 = kcvc.reshape(n_cmp_pad, CMP_STRIDE, 4, HD).transpose(2, 0, 1, 3)
        x4 = x4.reshape(4, n_cmp_pad, CMP_STRIDE * HD)
        cmp_n, cmp_t = _compress(x4, w1cat[l], pe16[l], w2[l])
        glog_t = small_t[H_FOX:H_FOX + 3 * H_NSA].reshape(NSA_G, NSA_R * 3, s)
        yn = _nsa_attention(proj, proj_t, cmp_n, cmp_t, c2s_t, cstrip, t0, t1, glog_t)

        nxt = None
        if l + 1 < depth:
            nxt = (pre_norm[l + 1:l + 2], scale[l + 1], shift[l + 1])
        x2, h = _out_proj(ya, yf, yn, w_out_b[l], x2, post_norm[l:l + 1], gate[l], nxt)
    return x2.reshape(bsz, s, d)
```

```python
import functools
import math

import numpy as np
import jax
import jax.numpy as jnp
from jax import lax
from jax.experimental import pallas as pl
from jax.experimental.pallas import tpu as pltpu

F32 = jnp.float32
BF16 = jnp.bfloat16

D_MODEL = 2048
DEPTH = 4
HD = 128
D_CONV = 512
D_FOX = 768
D_NSA = 768
H_FOX = 6
H_NSA = 6
NSA_G = 2
NSA_R = 3
CONV_WIDTH = 3
CMP_LEN = 32
CMP_STRIDE = 16
CMP_HIDDEN = 256
SEL_LEN = 64
SEL_TOPK = 16
WINDOW = 512
REL_BUCKETS = 32
REL_MAX_DIST = 128
NORM_EPS = 1e-6
NEG_BIG = -1e30
FORCED_SCORE = 1e4
LOG2E = math.log2(math.e)
ONES_ROWS = 16

TILE = 128
NSA_CHUNK = 512
FOX_T = 512
FOX_TK = 256
FOX_HEADS = 3
N_A = 6912
N_T = 1280
VMEM_LIMIT = 56 * 1024 * 1024

NT_DIMS = (((1,), (1,)), ((), ()))

BLK_QN, BLK_ZN, BLK_KS, BLK_KW = 16, 22, 28, 30
COL_KCVC = 4096
BLK_QF, BLK_KF, BLK_ZF = 36, 42, 48


def _cparams(sem, vmem=VMEM_LIMIT):
    return pltpu.CompilerParams(dimension_semantics=sem, vmem_limit_bytes=vmem)


def _silu(v):
    return v * jax.nn.sigmoid(v)


def _mod_kernel(c_ref, w_ref, b_ref, o_ref):
    ca = _silu(c_ref[...])
    o_ref[0] = jnp.dot(ca, w_ref[0], precision=lax.Precision.HIGHEST,
                       preferred_element_type=F32) + b_ref[0]


def _ada_mod(c8, w_ada, b_ada):
    depth, d, n = w_ada.shape
    tn = 768
    return pl.pallas_call(
        _mod_kernel,
        grid=(depth, n // tn),
        in_specs=[pl.BlockSpec((8, d), lambda l, j: (0, 0)),
                  pl.BlockSpec((1, d, tn), lambda l, j: (l, 0, j)),
                  pl.BlockSpec((1, 1, tn), lambda l, j: (l, 0, j))],
        out_specs=pl.BlockSpec((1, 8, tn), lambda l, j: (l, 0, j)),
        out_shape=jax.ShapeDtypeStruct((depth, 8, n), F32),
        compiler_params=_cparams(("parallel", "parallel")),
        name="ada_mod",
    )(c8, w_ada, b_ada.reshape(depth, 1, n))


def _modulated_norm(x, g, scale, shift):
    y = x * lax.rsqrt(jnp.mean(x * x, axis=-1, keepdims=True) + NORM_EPS) * g
    return y * (1.0 + scale) + shift


def _prenorm_kernel(x_ref, g_ref, sc_ref, sh_ref, h_ref):
    h_ref[...] = _modulated_norm(x_ref[...], g_ref[...], sc_ref[...], sh_ref[...]).astype(BF16)


def _prenorm(x2, g, scale, shift):
    s, d = x2.shape
    tm = 512
    vec = pl.BlockSpec((1, d), lambda i: (0, 0))
    return pl.pallas_call(
        _prenorm_kernel,
        grid=(s // tm,),
        in_specs=[pl.BlockSpec((tm, d), lambda i: (i, 0)), vec, vec, vec],
        out_specs=pl.BlockSpec((tm, d), lambda i: (i, 0)),
        out_shape=jax.ShapeDtypeStruct((s, d), BF16),
        compiler_params=_cparams(("parallel",)),
        name="prenorm",
    )(x2, g, scale, shift)


def _mm_kernel(h_ref, w_ref, o_ref):
    o_ref[...] = jnp.dot(h_ref[...], w_ref[...], preferred_element_type=F32).astype(o_ref.dtype)


def _matmul(h, w, l, out_dtype, tm, tn):
    s, d = h.shape
    n = w.shape[2]
    return pl.pallas_call(
        _mm_kernel,
        grid=(s // tm, n // tn),
        in_specs=[pl.BlockSpec((tm, d), lambda i, j: (i, 0)),
                  pl.BlockSpec((None, d, tn), lambda i, j: (l, 0, j))],
        out_specs=pl.BlockSpec((tm, tn), lambda i, j: (i, j)),
        out_shape=jax.ShapeDtypeStruct((s, n), out_dtype),
        compiler_params=_cparams(("parallel", "arbitrary")),
        name="proj_rows",
    )(h, w)


def _mm_t_tiled_kernel(h_ref, wt_ref, o_ref, *, chunks):
    res = lax.dot_general(wt_ref[...], h_ref[...], NT_DIMS, preferred_element_type=F32)
    for c in range(chunks):
        o_ref[c] = res[:, c * TILE:(c + 1) * TILE].astype(o_ref.dtype)


def _matmul_t_tiled(h, wt, l, tm, tn):
    s, d = h.shape
    n = wt.shape[1]
    chunks = tm // TILE
    return pl.pallas_call(
        functools.partial(_mm_t_tiled_kernel, chunks=chunks),
        grid=(s // tm, n // tn),
        in_specs=[pl.BlockSpec((tm, d), lambda i, j: (i, 0)),
                  pl.BlockSpec((None, tn, d), lambda i, j: (l, j, 0))],
        out_specs=pl.BlockSpec((chunks, tn, TILE), lambda i, j: (i, j, 0)),
        out_shape=jax.ShapeDtypeStruct((s // TILE, n, TILE), BF16),
        compiler_params=_cparams(("parallel", "arbitrary")),
        name="proj_cols_tiled",
    )(h, wt)


def _mm_t_kernel(h_ref, wt_ref, o_ref):
    o_ref[...] = lax.dot_general(wt_ref[...], h_ref[...], NT_DIMS, preferred_element_type=F32)


def _matmul_t(h, wt, l, tm):
    s, d = h.shape
    n = wt.shape[1]
    return pl.pallas_call(
        _mm_t_kernel,
        grid=(s // tm,),
        in_specs=[pl.BlockSpec((tm, d), lambda i: (i, 0)),
                  pl.BlockSpec((None, n, d), lambda i: (l, 0, 0))],
        out_specs=pl.BlockSpec((n, tm), lambda i: (0, i)),
        out_shape=jax.ShapeDtypeStruct((n, s), F32),
        compiler_params=_cparams(("parallel",)),
        name="proj_cols_small",
    )(h, wt)


def _fox_prep_kernel(s_ref, b_ref, qa_ref, ka_ref, carry_ref, *, tm):
    @pl.when(pl.program_id(0) == 0)
    def _():
        carry_ref[...] = jnp.zeros_like(carry_ref)

    v = s_ref[...] + b_ref[...]
    c = jnp.minimum(v, 0.0) - jnp.log1p(jnp.exp(-jnp.abs(v)))
    row = lax.broadcasted_iota(jnp.int32, (tm, TILE), 0)
    sh = 1
    while sh < tm:
        c = c + jnp.where(row >= sh, pltpu.roll(c, sh, axis=0), 0.0)
        sh *= 2
    c = c + carry_ref[...]
    carry_ref[...] = c[tm - 1:tm, :]
    c = c * LOG2E

    lane = lax.broadcasted_iota(jnp.int32, (tm, TILE), 1)
    for h in range(H_FOX):
        ch = jnp.broadcast_to(c[:, h:h + 1], (tm, TILE))
        hi = ch.astype(BF16).astype(F32)
        r1 = ch - hi
        mid = r1.astype(BF16).astype(F32)
        lo = r1 - mid
        pieces = jnp.where((lane == 0) | (lane == 3), hi,
                           jnp.where((lane == 1) | (lane == 4), mid, lo))
        qa = jnp.where(lane < 3, pieces, jnp.where(lane < 6, 1.0, 0.0))
        ka = jnp.where(lane < 3, 1.0, jnp.where(lane < 6, -pieces, 0.0))
        qa_ref[:, h * TILE:(h + 1) * TILE] = qa.astype(BF16)
        ka_ref[:, h * TILE:(h + 1) * TILE] = ka.astype(BF16)


def _fox_prep(small, b_pad):
    s = small.shape[0]
    tm = 512
    out = jax.ShapeDtypeStruct((s, H_FOX * TILE), BF16)
    return pl.pallas_call(
        functools.partial(_fox_prep_kernel, tm=tm),
        grid=(s // tm,),
        in_specs=[pl.BlockSpec((tm, TILE), lambda i: (i, 0)),
                  pl.BlockSpec((1, TILE), lambda i: (0, 0))],
        out_specs=[pl.BlockSpec((tm, H_FOX * TILE), lambda i: (i, 0))] * 2,
        out_shape=[out, out],
        scratch_shapes=[pltpu.VMEM((1, TILE), F32)],
        compiler_params=_cparams(("arbitrary",)),
        name="fox_prep",
    )(small, b_pad)


def _fox_kernel(q_ref, qa_ref, k_ref, ka_ref, vt_ref, z_ref, o_ref, s0_ref, s1_ref, acc_ref):
    i = pl.program_id(1)
    t = FOX_T
    tk = FOX_TK
    heads = range(FOX_HEADS)
    col = lambda h: slice(h * TILE, (h + 1) * TILE)
    qaug = [jnp.concatenate([q_ref[:, col(h)], qa_ref[:, col(h)]], axis=1) for h in heads]

    def scores(h, j, rows):
        r0 = pl.multiple_of(j * rows, rows)
        kaug = jnp.concatenate([k_ref[pl.ds(r0, rows), col(h)], ka_ref[pl.ds(r0, rows), col(h)]],
                               axis=1)
        return lax.dot_general(kaug, qaug[h], NT_DIMS, preferred_element_type=F32)

    def values_t(h, j, rows):
        n_sub = rows // TILE
        return jnp.concatenate(
            [jnp.concatenate([vt_ref[n_sub * j + c, col(h), :] for c in range(n_sub)], axis=1),
             jnp.ones((ONES_ROWS, rows), BF16)], axis=0)

    kio = lax.broadcasted_iota(jnp.int32, (t, t), 0)
    qio = lax.broadcasted_iota(jnp.int32, (t, t), 1)
    ms = []
    for h in heads:
        s = jnp.where(kio <= qio, scores(h, i, t), NEG_BIG)
        m = jnp.max(s, axis=0, keepdims=True)
        p = jnp.exp2(s - m)
        ms.append(m)
        acc_ref[h] = jnp.dot(values_t(h, i, t), p.astype(BF16), preferred_element_type=F32)

    def update(h, s, vt, m):
        m_new = jnp.maximum(m, jnp.max(s, axis=0, keepdims=True))
        alpha = jnp.exp2(m - m_new)
        p = jnp.exp2(s - m_new)
        acc_ref[h] = alpha * acc_ref[h] + jnp.dot(vt, p.astype(BF16), preferred_element_type=F32)
        return m_new

    for h in heads:
        s0_ref[h] = scores(h, 0, tk)

    def pair(pp, ms):
        ms = list(ms)
        for h in heads:
            s1_ref[h] = scores(h, 2 * pp + 1, tk)
        for h in heads:
            ms[h] = update(h, s0_ref[h], values_t(h, 2 * pp, tk), ms[h])
        for h in heads:
            s0_ref[h] = scores(h, 2 * pp + 2, tk)
        for h in heads:
            ms[h] = update(h, s1_ref[h], values_t(h, 2 * pp + 1, tk), ms[h])
        return tuple(ms)

    lax.fori_loop(0, i * (t // (2 * tk)), pair, tuple(ms))
    for h in heads:
        o = (acc_ref[h, 0:HD, :] * (1.0 / acc_ref[h, HD:HD + 1, :])).T
        o_ref[:, col(h)] = (o * _silu(z_ref[:, col(h)].astype(F32))).astype(BF16)


def _fox_attention(proj, qa, ka, proj_t):
    s = proj.shape[0]
    t = FOX_T
    nk = s // TILE
    hw = FOX_HEADS * TILE
    return pl.pallas_call(
        _fox_kernel,
        grid=(H_FOX // FOX_HEADS, s // t),
        scratch_shapes=[pltpu.VMEM((FOX_HEADS, FOX_TK, t), F32),
                        pltpu.VMEM((FOX_HEADS, FOX_TK, t), F32),
                        pltpu.VMEM((FOX_HEADS, HD + ONES_ROWS, t), F32)],
        in_specs=[pl.BlockSpec((t, hw), lambda h, i: (i, BLK_QF // FOX_HEADS + h)),
                  pl.BlockSpec((t, hw), lambda h, i: (i, h)),
                  pl.BlockSpec((s, hw), lambda h, i: (0, BLK_KF // FOX_HEADS + h),
                               pipeline_mode=pl.Buffered(1)),
                  pl.BlockSpec((s, hw), lambda h, i: (0, h), pipeline_mode=pl.Buffered(1)),
                  pl.BlockSpec((nk, hw, TILE), lambda h, i: (0, h, 0),
                               pipeline_mode=pl.Buffered(1)),
                  pl.BlockSpec((t, hw), lambda h, i: (i, BLK_ZF // FOX_HEADS + h))],
        out_specs=pl.BlockSpec((t, hw), lambda h, i: (i, h)),
        out_shape=jax.ShapeDtypeStruct((s, D_FOX), BF16),
        compiler_params=_cparams(("parallel", "parallel")),
        name="fox_attention",
    )(proj, qa, proj, ka, proj_t, proj)


def _compress_kernel(x_ref, w1_ref, pe_ref, w2_ref, o_ref, ot_ref):
    n = x_ref.shape[0]
    ab = jnp.dot(x_ref[...], w1_ref[0], preferred_element_type=F32)
    pe = jnp.dot(pe_ref[0], w1_ref[0], preferred_element_type=F32)
    a = ab[:, :CMP_HIDDEN]
    b_next = pltpu.roll(ab[:, CMP_HIDDEN:], n - 1, axis=0)
    hid = a + b_next + pe[0:1, :CMP_HIDDEN] + pe[8:9, CMP_HIDDEN:]
    out = jnp.dot(_silu(hid).astype(BF16), w2_ref[0], preferred_element_type=F32)
    o_ref[...] = out.astype(BF16)
    ot_ref[...] = out.T.astype(BF16)


def _compress(x4, w1cat, pe16, w2, l):
    _, n, k = x4.shape
    kind = lambda c: (l, c // NSA_G, 0, 0)
    return pl.pallas_call(
        _compress_kernel,
        grid=(2 * NSA_G,),
        in_specs=[pl.BlockSpec((None, n, k), lambda c: (c, 0, 0)),
                  pl.BlockSpec((None, 1, k, 2 * CMP_HIDDEN), kind),
                  pl.BlockSpec((None, 1, 16, k), kind),
                  pl.BlockSpec((None, 1, CMP_HIDDEN, HD), kind)],
        out_specs=[pl.BlockSpec((None, n, HD), lambda c: (c, 0, 0)),
                   pl.BlockSpec((None, HD, n), lambda c: (c, 0, 0))],
        out_shape=[jax.ShapeDtypeStruct((2 * NSA_G, n, HD), BF16),
                   jax.ShapeDtypeStruct((2 * NSA_G, HD, n), BF16)],
        compiler_params=_cparams(("parallel",)),
        name="nsa_compress",
    )(x4, w1cat, pe16, w2)


def _nsa_kernel_one_group(q0_ref, q1_ref, q2_ref, z0_ref, z1_ref, z2_ref,
                ks_ref, kw_ref, vst_ref, vwt_ref, kc_ref, vct_ref,
                c2s_ref, cstrip_ref, t0_ref, t1_ref, gl_ref, o_ref,
                sc_ref, sel_ref, m_s, l_s, acc_s, ss_ref, *, n_cmp):
    i = pl.program_id(1)
    w3 = NSA_R * TILE
    qs = jnp.concatenate([q0_ref[...], q1_ref[...], q2_ref[...]], axis=0)
    b_io = lax.broadcasted_iota(jnp.int32, (TILE, w3), 0)
    a_io = lax.broadcasted_iota(jnp.int32, (TILE, w3), 1) & (TILE - 1)
    t0 = t0_ref[...]
    t1 = t1_ref[...]

    sc = lax.dot_general(kc_ref[...], qs, NT_DIMS, preferred_element_type=F32)
    sc_ref[0:16, :] = jnp.zeros((16, w3), F32)
    sc_ref[16:16 + n_cmp, :] = sc
    w0 = pl.multiple_of(8 * i, 8)
    sc_ref[pl.ds(w0, 24), :] = sc_ref[pl.ds(w0, 24), :] + cstrip_ref[...]
    sc = sc_ref[16:16 + n_cmp, :]
    n_io = lax.broadcasted_iota(jnp.int32, (n_cmp, w3), 0)
    qa_io = lax.broadcasted_iota(jnp.int32, (n_cmp, w3), 1) & (TILE - 1)
    valid = (CMP_STRIDE * n_io + (CMP_LEN - 1)) <= (TILE * i + qa_io)
    scm = jnp.where(valid, sc, NEG_BIG)
    mc = jnp.max(scm, axis=0, keepdims=True)
    pc = jnp.where(valid, jnp.exp2(scm - mc), 0.0)
    lc = jnp.sum(pc, axis=0, keepdims=True)
    pcn = pc * jnp.where(lc > 0.0, 1.0 / lc, 0.0)
    oc_t = jnp.dot(vct_ref[...], pcn.astype(BF16), preferred_element_type=F32)

    psum = pcn[:, 0:TILE] + pcn[:, TILE:2 * TILE] + pcn[:, 2 * TILE:3 * TILE]
    p_hi = psum.astype(BF16)
    p_lo = (psum - p_hi.astype(F32)).astype(BF16)
    c2s = c2s_ref[...]
    imp = (jnp.dot(c2s, p_hi, preferred_element_type=F32)
           + jnp.dot(c2s, p_lo, preferred_element_type=F32))
    n_sel = imp.shape[0]
    m_io = lax.broadcasted_iota(jnp.int32, (n_sel, TILE), 0)
    m_f = m_io.astype(F32)
    qpos = TILE * i + lax.broadcasted_iota(jnp.int32, (n_sel, TILE), 1)
    own = qpos >> 6
    forced = (m_io == 0) | (m_io == own) | (m_io == own - 1)
    score = jnp.where(SEL_LEN * m_io <= qpos, jnp.where(forced, FORCED_SCORE, imp), -1.0)

    def pick_one(_, carry):
        cur, sel = carry
        mx = jnp.max(cur, axis=0, keepdims=True)
        idx = jnp.min(jnp.where(cur == mx, m_f, float(n_sel)), axis=0, keepdims=True)
        pick = m_f == idx
        keep = jnp.where(mx >= 0.0, 1.0, 0.0)
        return jnp.where(pick, -2.0, cur), jnp.where(pick, keep, sel)

    _, sel = lax.fori_loop(0, min(SEL_TOPK, n_sel), pick_one,
                           (score, jnp.zeros((n_sel, TILE), F32)))
    sel_ref[...] = sel

    tpc = NSA_CHUNK // TILE

    def key_scores(k_ref, tile0, n_tiles):
        r0 = pl.multiple_of(tile0 * TILE, TILE)
        return lax.dot_general(k_ref[pl.ds(r0, n_tiles * TILE), :], qs, NT_DIMS,
                               preferred_element_type=F32)

    def values_t(vt_ref, tile0, n_tiles):
        return jnp.concatenate([vt_ref[tile0 + c] for c in range(n_tiles)], axis=1)

    def add_near_bias(d_tile, n_tiles):
        for c in range(n_tiles):
            @pl.when(d_tile == c + 1)
            def _(c=c):
                ss_ref[c * TILE:(c + 1) * TILE, :] = ss_ref[c * TILE:(c + 1) * TILE, :] + t1

            @pl.when(d_tile == c)
            def _(c=c):
                ss_ref[c * TILE:(c + 1) * TILE, :] = jnp.where(
                    b_io <= a_io, ss_ref[c * TILE:(c + 1) * TILE, :] + t0, NEG_BIG)

    def sel_mask(c):
        rows = sel_ref[pl.ds(pl.multiple_of(c * (NSA_CHUNK // SEL_LEN), 8), NSA_CHUNK // SEL_LEN), :]
        mk = jnp.concatenate([jnp.broadcast_to(rows[b:b + 1, :], (SEL_LEN, TILE))
                              for b in range(NSA_CHUNK // SEL_LEN)], axis=0)
        return jnp.concatenate([mk, mk, mk], axis=1) > 0.5

    def sel_update(c, s):
        s = jnp.where(sel_mask(c), s, NEG_BIG)
        m_old = m_s[...]
        m_new = jnp.maximum(m_old, jnp.max(s, axis=0, keepdims=True))
        alpha = jnp.exp2(m_old - m_new)
        p = jnp.exp2(s - m_new)
        l_s[...] = alpha * l_s[...] + jnp.sum(p, axis=0, keepdims=True)
        acc_s[...] = alpha * acc_s[...] + jnp.dot(values_t(vst_ref, c * tpc, tpc), p.astype(BF16),
                                                  preferred_element_type=F32)
        m_s[...] = m_new

    m_s[...] = jnp.full(m_s.shape, NEG_BIG, F32)
    l_s[...] = jnp.zeros(l_s.shape, F32)
    acc_s[...] = jnp.zeros(acc_s.shape, F32)

    c_own = i // tpc
    c_prev = jnp.maximum(i - 1, 0) // tpc

    def far_chunk(c, carry):
        sel_update(c, key_scores(ks_ref, c * tpc, tpc))
        return carry

    lax.fori_loop(0, c_prev, far_chunk, 0)

    def near_chunk(c):
        ss_ref[0:NSA_CHUNK, :] = key_scores(ks_ref, c * tpc, tpc)
        add_near_bias(i - c * tpc, tpc)
        sel_update(c, ss_ref[0:NSA_CHUNK, :])

    @pl.when(c_prev != c_own)
    def _():
        near_chunk(c_prev)

    near_chunk(c_own)

    n_wt = WINDOW // TILE + 1
    w_tile0 = jnp.maximum(i - WINDOW // TILE, 0)
    ss_ref[...] = key_scores(kw_ref, w_tile0, n_wt)
    add_near_bias(i - w_tile0, n_wt)

    @pl.when(i >= WINDOW // TILE)
    def _():
        ss_ref[0:TILE, :] = jnp.where(a_io < b_io, ss_ref[0:TILE, :], NEG_BIG)

    for c in range(1, n_wt):
        @pl.when(i < c)
        def _(c=c):
            ss_ref[c * TILE:(c + 1) * TILE, :] = jnp.full((TILE, w3), NEG_BIG, F32)

    sw = ss_ref[...]
    m_w = jnp.max(sw, axis=0, keepdims=True)
    p_w = jnp.exp2(sw - m_w)
    l_w = jnp.sum(p_w, axis=0, keepdims=True)
    acc_w = jnp.dot(values_t(vwt_ref, w_tile0, n_wt), p_w.astype(BF16), preferred_element_type=F32)

    gates = jax.nn.sigmoid(gl_ref[...])

    def gate_row(branch):
        return jnp.concatenate([gates[r * 3 + branch:r * 3 + branch + 1, :] for r in range(NSA_R)],
                               axis=1)

    out_t = (oc_t * gate_row(0)
             + acc_s[...] * (gate_row(1) / l_s[...])
             + acc_w * (gate_row(2) / l_w))
    z_refs = (z0_ref, z1_ref, z2_ref)
    for r in range(NSA_R):
        o_r = out_t[:, r * TILE:(r + 1) * TILE].T
        o_ref[:, r * TILE:(r + 1) * TILE] = (o_r * _silu(z_refs[r][...].astype(F32))).astype(BF16)


def _nsa_attention_one_group(proj, proj_t, cmp_n, cmp_t, c2s_t, cstrip, t0, t1, glog_t):
    s = proj.shape[0]
    nk = s // TILE
    n_cmp = cmp_n.shape[1]
    n_sel = s // SEL_LEN
    w3 = NSA_R * TILE
    qspec = lambda r: pl.BlockSpec((TILE, TILE), lambda g, i, r=r: (i, BLK_QN + NSA_R * g + r))
    zspec = lambda r: pl.BlockSpec((TILE, TILE), lambda g, i, r=r: (i, BLK_ZN + NSA_R * g + r))
    per_group = lambda shape: pl.BlockSpec((None,) + shape, lambda g, i: (g, 0, 0))
    return pl.pallas_call(
        functools.partial(_nsa_kernel, n_cmp=n_cmp),
        grid=(NSA_G, s // TILE),
        in_specs=[qspec(0), qspec(1), qspec(2), zspec(0), zspec(1), zspec(2),
                  pl.BlockSpec((s, TILE), lambda g, i: (0, BLK_KS + g)),
                  pl.BlockSpec((s, TILE), lambda g, i: (0, BLK_KW + g)),
                  pl.BlockSpec((nk, HD, TILE), lambda g, i: (0, H_FOX + g, 0)),
                  pl.BlockSpec((nk, HD, TILE), lambda g, i: (0, H_FOX + NSA_G + g, 0)),
                  pl.BlockSpec((None, n_cmp, HD), lambda g, i: (g, 0, 0)),
                  pl.BlockSpec((None, HD, n_cmp), lambda g, i: (NSA_G + g, 0, 0)),
                  pl.BlockSpec((n_sel, n_cmp), lambda g, i: (0, 0)),
                  per_group((24, w3)), per_group((TILE, w3)), per_group((TILE, w3)),
                  pl.BlockSpec((None, NSA_R * 3, TILE), lambda g, i: (g, 0, i))],
        out_specs=pl.BlockSpec((TILE, w3), lambda g, i: (i, g)),
        out_shape=jax.ShapeDtypeStruct((s, D_NSA), BF16),
        scratch_shapes=[pltpu.VMEM((16 + n_cmp + 8, w3), F32),
                        pltpu.VMEM((n_sel, TILE), F32),
                        pltpu.VMEM((1, w3), F32), pltpu.VMEM((1, w3), F32), pltpu.VMEM((HD, w3), F32),
                        pltpu.VMEM((WINDOW + TILE, w3), F32)],
        compiler_params=_cparams(("parallel", "parallel")),
        name="nsa_attention",
    )(proj, proj, proj, proj, proj, proj, proj, proj, proj_t, proj_t,
      cmp_n, cmp_t, c2s_t, cstrip, t0, t1, glog_t)


def _nsa_kernel(*refs, n_cmp):
    q_refs, z_refs = refs[0:H_NSA], refs[H_NSA:2 * H_NSA]
    (ks_ref, kw_ref, vst_ref, vwt_ref, kc_ref, vct_ref, c2s_ref, cstrip_ref, t0_ref, t1_ref,
     gl_ref, o_ref, sc_ref, sel_ref, m_s, acc_s, ss_ref, sa_ref, sb_ref) = refs[2 * H_NSA:]
    i = pl.program_id(0)
    w3 = NSA_R * TILE
    groups = range(NSA_G)
    col = lambda g: slice(g * TILE, (g + 1) * TILE)
    qs = [jnp.concatenate([q_refs[NSA_R * g + r][...] for r in range(NSA_R)], axis=0)
          for g in groups]
    b_io = lax.broadcasted_iota(jnp.int32, (TILE, w3), 0)
    a_io = lax.broadcasted_iota(jnp.int32, (TILE, w3), 1) & (TILE - 1)

    w0 = pl.multiple_of(8 * i, 8)
    n_io = lax.broadcasted_iota(jnp.int32, (n_cmp, w3), 0)
    qa_io = lax.broadcasted_iota(jnp.int32, (n_cmp, w3), 1) & (TILE - 1)
    valid = (CMP_STRIDE * n_io + (CMP_LEN - 1)) <= (TILE * i + qa_io)
    oc_t, psums = [], []
    for g in groups:
        sc_ref[g, 0:16, :] = jnp.zeros((16, w3), F32)
        sc_ref[g, 16:16 + n_cmp, :] = lax.dot_general(kc_ref[g], qs[g], NT_DIMS,
                                                      preferred_element_type=F32)
        sc_ref[g, pl.ds(w0, 24), :] = sc_ref[g, pl.ds(w0, 24), :] + cstrip_ref[g]
        scm = jnp.where(valid, sc_ref[g, 16:16 + n_cmp, :], NEG_BIG)
        mc = jnp.max(scm, axis=0, keepdims=True)
        pc = jnp.where(valid, jnp.exp2(scm - mc), 0.0)
        lc = jnp.sum(pc, axis=0, keepdims=True)
        pcn = pc * jnp.where(lc > 0.0, 1.0 / lc, 0.0)
        oc_t.append(jnp.dot(vct_ref[g], pcn.astype(BF16), preferred_element_type=F32))
        psums.append(pcn[:, 0:TILE] + pcn[:, TILE:2 * TILE] + pcn[:, 2 * TILE:3 * TILE])

    psum = jnp.concatenate(psums, axis=1)
    p_hi = psum.astype(BF16)
    p_lo = (psum - p_hi.astype(F32)).astype(BF16)
    c2s = c2s_ref[...]
    imp = (jnp.dot(c2s, p_hi, preferred_element_type=F32)
           + jnp.dot(c2s, p_lo, preferred_element_type=F32))
    n_sel = imp.shape[0]
    gw = NSA_G * TILE
    m_io = lax.broadcasted_iota(jnp.int32, (n_sel, gw), 0)
    m_f = m_io.astype(F32)
    qpos = TILE * i + (lax.broadcasted_iota(jnp.int32, (n_sel, gw), 1) & (TILE - 1))
    own = qpos >> 6
    forced = (m_io == 0) | (m_io == own) | (m_io == own - 1)
    eligible = SEL_LEN * m_io <= qpos
    few = TILE * (i + 1) <= SEL_TOPK * SEL_LEN
    sel0 = jnp.where(eligible & (forced | few), 1.0, 0.0)
    score = jnp.where(eligible, jnp.where(forced, -2.0, imp), -1.0)
    n_forced = 3

    def pick_one(_, carry):
        cur, sel = carry
        mx = jnp.max(cur, axis=0, keepdims=True)
        idx = jnp.min(jnp.where(cur == mx, m_f, float(n_sel)), axis=0, keepdims=True)
        pick = m_f == idx
        keep = jnp.where(mx >= 0.0, 1.0, 0.0)
        return jnp.where(pick, -2.0, cur), jnp.where(pick, jnp.maximum(keep, sel), sel)

    rounds = jnp.where(few, 0, min(SEL_TOPK, n_sel) - n_forced)
    _, sel = lax.fori_loop(0, rounds, pick_one, (score, sel0))
    sel_ref[...] = sel

    tpc = NSA_CHUNK // TILE
    bpc = NSA_CHUNK // SEL_LEN

    def key_scores(k_ref, g, tile0, n_tiles):
        r0 = pl.multiple_of(tile0 * TILE, TILE)
        return lax.dot_general(k_ref[pl.ds(r0, n_tiles * TILE), col(g)], qs[g], NT_DIMS,
                               preferred_element_type=F32)

    def values_t(vt_ref, g, tile0, n_tiles):
        return jnp.concatenate(
            [jnp.concatenate([vt_ref[tile0 + c, col(g), :] for c in range(n_tiles)], axis=1),
             jnp.ones((ONES_ROWS, n_tiles * TILE), BF16)], axis=0)

    def add_near_bias(d_tile, n_tiles):
        for c in range(n_tiles):
            rows = slice(c * TILE, (c + 1) * TILE)

            @pl.when(d_tile == c + 1)
            def _(rows=rows):
                for g in groups:
                    ss_ref[g, rows, :] = ss_ref[g, rows, :] + t1_ref[g]

            @pl.when(d_tile == c)
            def _(rows=rows):
                for g in groups:
                    ss_ref[g, rows, :] = jnp.where(b_io <= a_io, ss_ref[g, rows, :] + t0_ref[g],
                                                   NEG_BIG)

    def sel_mask(g, c):
        rows = sel_ref[pl.ds(pl.multiple_of(c * bpc, 8), bpc), col(g)]
        mk = jnp.concatenate([jnp.broadcast_to(rows[b:b + 1, :], (SEL_LEN, TILE))
                              for b in range(bpc)], axis=0)
        return jnp.concatenate([mk, mk, mk], axis=1) > 0.5

    def sel_update(g, c, s):
        s = jnp.where(sel_mask(g, c), s, NEG_BIG)
        m_old = m_s[g]
        m_new = jnp.maximum(m_old, jnp.max(s, axis=0, keepdims=True))
        alpha = jnp.exp2(m_old - m_new)
        p = jnp.exp2(s - m_new)
        acc_s[g] = alpha * acc_s[g] + jnp.dot(values_t(vst_ref, g, c * tpc, tpc), p.astype(BF16),
                                              preferred_element_type=F32)
        m_s[g] = m_new

    m_s[...] = jnp.full(m_s.shape, NEG_BIG, F32)
    acc_s[...] = jnp.zeros(acc_s.shape, F32)

    c_own = i // tpc
    c_prev = jnp.maximum(i - 1, 0) // tpc

    for g in groups:
        sa_ref[g] = key_scores(ks_ref, g, 0, tpc)

    def far_pair(pp, carry):
        for g in groups:
            sb_ref[g] = key_scores(ks_ref, g, (2 * pp + 1) * tpc, tpc)
        for g in groups:
            sel_update(g, 2 * pp, sa_ref[g])
        for g in groups:
            sa_ref[g] = key_scores(ks_ref, g, (2 * pp + 2) * tpc, tpc)
        for g in groups:
            sel_update(g, 2 * pp + 1, sb_ref[g])
        return carry

    lax.fori_loop(0, c_prev // 2, far_pair, 0)

    @pl.when(c_prev % 2 == 1)
    def _():
        for g in groups:
            sel_update(g, c_prev - 1, sa_ref[g])

    def near_chunk(c):
        for g in groups:
            ss_ref[g, 0:NSA_CHUNK, :] = key_scores(ks_ref, g, c * tpc, tpc)
        add_near_bias(i - c * tpc, tpc)
        for g in groups:
            sel_update(g, c, ss_ref[g, 0:NSA_CHUNK, :])

    @pl.when(c_prev != c_own)
    def _():
        near_chunk(c_prev)

    near_chunk(c_own)

    n_wt = WINDOW // TILE + 1
    w_tile0 = jnp.maximum(i - WINDOW // TILE, 0)
    for g in groups:
        ss_ref[g] = key_scores(kw_ref, g, w_tile0, n_wt)
    add_near_bias(i - w_tile0, n_wt)

    @pl.when(i >= WINDOW // TILE)
    def _():
        for g in groups:
            ss_ref[g, 0:TILE, :] = jnp.where(a_io < b_io, ss_ref[g, 0:TILE, :], NEG_BIG)

    for c in range(1, n_wt):
        @pl.when(i < c)
        def _(c=c):
            for g in groups:
                ss_ref[g, c * TILE:(c + 1) * TILE, :] = jnp.full((TILE, w3), NEG_BIG, F32)

    for g in groups:
        sw = ss_ref[g]
        p_w = jnp.exp2(sw - jnp.max(sw, axis=0, keepdims=True))
        acc_w = jnp.dot(values_t(vwt_ref, g, w_tile0, n_wt), p_w.astype(BF16),
                        preferred_element_type=F32)
        gates = jax.nn.sigmoid(gl_ref[g])

        def gate_row(branch):
            return jnp.concatenate([gates[r * 3 + branch:r * 3 + branch + 1, :]
                                    for r in range(NSA_R)], axis=1)

        out_t = (oc_t[g] * gate_row(0)
                 + acc_s[g, 0:HD, :] * (gate_row(1) / acc_s[g, HD:HD + 1, :])
                 + acc_w[0:HD, :] * (gate_row(2) / acc_w[HD:HD + 1, :]))
        for r in range(NSA_R):
            h = NSA_R * g + r
            o_r = out_t[:, r * TILE:(r + 1) * TILE].T
            o_ref[:, h * TILE:(h + 1) * TILE] = (
                o_r * _silu(z_refs[h][...].astype(F32))).astype(BF16)


def _nsa_attention(proj, proj_t, cmp_n, cmp_t, c2s_t, cstrip, t0, t1, glog_t):
    s = proj.shape[0]
    nk = s // TILE
    n_cmp = cmp_n.shape[1]
    n_sel = s // SEL_LEN
    w3 = NSA_R * TILE
    gw = NSA_G * TILE
    qspec = lambda h: pl.BlockSpec((TILE, TILE), lambda i, h=h: (i, BLK_QN + h))
    zspec = lambda h: pl.BlockSpec((TILE, TILE), lambda i, h=h: (i, BLK_ZN + h))
    whole = lambda shape: pl.BlockSpec(shape, lambda i: (0,) * len(shape))
    return pl.pallas_call(
        functools.partial(_nsa_kernel, n_cmp=n_cmp),
        grid=(s // TILE,),
        in_specs=[qspec(h) for h in range(H_NSA)] + [zspec(h) for h in range(H_NSA)] + [
            pl.BlockSpec((s, gw), lambda i: (0, BLK_KS // NSA_G), pipeline_mode=pl.Buffered(1)),
            pl.BlockSpec((s, gw), lambda i: (0, BLK_KW // NSA_G), pipeline_mode=pl.Buffered(1)),
            pl.BlockSpec((nk, gw, TILE), lambda i: (0, H_FOX // NSA_G, 0),
                         pipeline_mode=pl.Buffered(1)),
            pl.BlockSpec((nk, gw, TILE), lambda i: (0, H_FOX // NSA_G + 1, 0),
                         pipeline_mode=pl.Buffered(1)),
            pl.BlockSpec((NSA_G, n_cmp, HD), lambda i: (0, 0, 0)),
            pl.BlockSpec((NSA_G, HD, n_cmp), lambda i: (1, 0, 0)),
            whole((n_sel, n_cmp)), whole((NSA_G, 24, w3)), whole((NSA_G, TILE, w3)),
            whole((NSA_G, TILE, w3)),
            pl.BlockSpec((NSA_G, NSA_R * 3, TILE), lambda i: (0, 0, i))],
        out_specs=pl.BlockSpec((TILE, D_NSA), lambda i: (i, 0)),
        out_shape=jax.ShapeDtypeStruct((s, D_NSA), BF16),
        scratch_shapes=[pltpu.VMEM((NSA_G, 16 + n_cmp + 8, w3), F32),
                        pltpu.VMEM((n_sel, gw), F32),
                        pltpu.VMEM((NSA_G, 1, w3), F32),
                        pltpu.VMEM((NSA_G, HD + ONES_ROWS, w3), F32),
                        pltpu.VMEM((NSA_G, WINDOW + TILE, w3), F32),
                        pltpu.VMEM((NSA_G, NSA_CHUNK, w3), F32),
                        pltpu.VMEM((NSA_G, NSA_CHUNK, w3), F32)],
        compiler_params=_cparams(("parallel",)),
        name="nsa_attention",
    )(*([proj] * (2 * H_NSA)), proj, proj, proj_t, proj_t,
      cmp_n, cmp_t, c2s_t, cstrip, t0, t1, glog_t)


def _conv_kernel(u_ref, gb_ref, gc_ref, z_ref, uh_ref, gch_ref, w_ref, o_ref, *, tm):
    i = pl.program_id(0)
    y = gc_ref[...].astype(F32) * u_ref[...].astype(F32)
    yh = gch_ref[...].astype(F32) * uh_ref[...].astype(F32)
    yh = jnp.where(i > 0, yh, 0.0)
    h1 = yh[15:16, :]
    h2 = yh[14:15, :]
    row = lax.broadcasted_iota(jnp.int32, (tm, D_CONV), 0)
    y1 = jnp.where(row == 0, h1, pltpu.roll(y, 1, axis=0))
    y2 = jnp.where(row == 0, h2, jnp.where(row == 1, h1, pltpu.roll(y, 2, axis=0)))
    w = w_ref[...]
    conv = w[0:1, :] * y2 + w[1:2, :] * y1 + w[2:3, :] * y
    o_ref[...] = (gb_ref[...].astype(F32) * conv * _silu(z_ref[...].astype(F32))).astype(BF16)


def _conv_branch(proj, conv_w):
    s = proj.shape[0]
    tm = 512
    main = lambda c: pl.BlockSpec((tm, D_CONV), lambda i, c=c: (i, c))
    halo = lambda c: pl.BlockSpec((16, D_CONV), lambda i, c=c: (jnp.maximum(i * (tm // 16) - 1, 0), c))
    return pl.pallas_call(
        functools.partial(_conv_kernel, tm=tm),
        grid=(s // tm,),
        in_specs=[main(0), main(1), main(2), main(3), halo(0), halo(2),
                  pl.BlockSpec((CONV_WIDTH, D_CONV), lambda i: (0, 0))],
        out_specs=pl.BlockSpec((tm, D_CONV), lambda i: (i, 0)),
        out_shape=jax.ShapeDtypeStruct((s, D_CONV), BF16),
        compiler_params=_cparams(("parallel",)),
        name="conv_branch",
    )(proj, proj, proj, proj, proj, proj, conv_w)


def _out_kernel(ya_ref, yf_ref, yn_ref, w_ref, x_ref, pg_ref, gate_ref, *rest, with_next):
    y = (jnp.dot(ya_ref[...], w_ref[0:D_CONV, :], preferred_element_type=F32)
         + jnp.dot(yf_ref[...], w_ref[D_CONV:D_CONV + D_FOX, :], preferred_element_type=F32)
         + jnp.dot(yn_ref[...], w_ref[D_CONV + D_FOX:, :], preferred_element_type=F32))
    yn = y * lax.rsqrt(jnp.mean(y * y, axis=-1, keepdims=True) + NORM_EPS) * pg_ref[...]
    xn = x_ref[...] + gate_ref[...] * yn
    if with_next:
        ng_ref, nsc_ref, nsh_ref, xo_ref, h_ref = rest
        h_ref[...] = _modulated_norm(xn, ng_ref[...], nsc_ref[...], nsh_ref[...]).astype(BF16)
    else:
        (xo_ref,) = rest
    xo_ref[...] = xn


def _out_proj(ya, yf, yn, w_out, l, x2, post_g, gate, nxt):
    s, d = x2.shape
    tm = 512
    rows = lambda n: pl.BlockSpec((tm, n), lambda i: (i, 0))
    vec = pl.BlockSpec((1, d), lambda i: (0, 0))
    in_specs = [rows(D_CONV), rows(D_FOX), rows(D_NSA),
                pl.BlockSpec((None, d, d), lambda i: (l, 0, 0), pipeline_mode=pl.Buffered(1)),
                rows(d), vec, vec]
    args = [ya, yf, yn, w_out, x2, post_g, gate]
    out_specs = [rows(d)]
    out_shape = [jax.ShapeDtypeStruct((s, d), F32)]
    if nxt is not None:
        in_specs += [vec, vec, vec]
        args += list(nxt)
        out_specs.append(rows(d))
        out_shape.append(jax.ShapeDtypeStruct((s, d), BF16))
    res = pl.pallas_call(
        functools.partial(_out_kernel, with_next=nxt is not None),
        grid=(s // tm,),
        in_specs=in_specs,
        out_specs=out_specs,
        out_shape=out_shape,
        compiler_params=_cparams(("parallel",)),
        name="out_proj",
    )(*args)
    return res if nxt is not None else (res[0], None)


def _bucket_of_distance():
    max_exact = REL_BUCKETS // 2
    d = np.arange(REL_MAX_DIST)
    nf = np.maximum(d, max_exact).astype(np.float32)
    large = max_exact + (np.log(nf / np.float32(max_exact)) / np.float32(math.log(REL_MAX_DIST / max_exact))
                         * np.float32(REL_BUCKETS - max_exact)).astype(np.int32)
    return np.where(d < max_exact, d, np.minimum(large, REL_BUCKETS - 1))


def _distance_tables():
    b = np.arange(TILE)[:, None]
    a = np.arange(TILE)[None, :]
    none = REL_MAX_DIST
    d0 = np.where(a >= b, a - b, none)
    d1 = np.where(a < b, TILE + a - b, none)
    m = np.arange(24)[:, None]
    dc = a - CMP_STRIDE * m + (2 * TILE - (CMP_LEN - 1))
    dc = np.where((dc >= 0) & (dc < REL_MAX_DIST), dc, none)
    return d0, d1, dc


def _bias_tables(rel_bias):
    bucket = np.concatenate([_bucket_of_distance(), [REL_BUCKETS - 1]])
    rel = (rel_bias - rel_bias[REL_BUCKETS - 1:REL_BUCKETS, :]) * LOG2E
    rel = rel.reshape(REL_BUCKETS, NSA_G, NSA_R)

    def expand(idx):
        bk = bucket[idx]
        tab = jnp.zeros((NSA_G, idx.shape[0], NSA_R, TILE), F32)
        for b in range(REL_BUCKETS - 1):
            hit = jnp.asarray(bk == b)[None, :, None, :]
            tab = jnp.where(hit, rel[b][:, None, :, None], tab)
        return tab.reshape(NSA_G, idx.shape[0], NSA_R * TILE)

    d0, d1, dc = _distance_tables()
    return expand(d0), expand(d1), expand(dc)


def _cmp_to_sel_t(s, n_cmp_pad):
    n_cmp = (s - CMP_LEN) // CMP_STRIDE + 1
    n_sel = s // SEL_LEN
    c_lo = np.arange(n_cmp_pad)[None, :] * CMP_STRIDE
    sel_start = np.arange(n_sel)[:, None] * SEL_LEN
    overlap = (c_lo < sel_start + SEL_LEN) & (c_lo + CMP_LEN > sel_start)
    overlap &= np.arange(n_cmp_pad)[None, :] < n_cmp
    return jnp.asarray(overlap, dtype=BF16)


def _pack_w_in(w_in):
    sizes = ([D_CONV] * 4 + [D_FOX] * 3 + [H_FOX, D_FOX] + [D_NSA] + [2 * HD] * 6
             + [3 * H_NSA, D_NSA])
    offs = np.concatenate([[0], np.cumsum(sizes)])
    names = ["u", "gb", "gc", "za", "qf", "kf", "vf", "ff", "zf", "qn", "kc", "vc", "ks", "vs",
             "kw", "vw", "gn", "zn"]
    col = {n: w_in[:, :, int(offs[k]):int(offs[k + 1])] for k, n in enumerate(names)}
    scale = HD ** -0.5 * LOG2E
    w_a = jnp.concatenate([col["u"], col["gb"], col["gc"], col["za"],
                           col["qn"] * scale, col["zn"], col["ks"], col["kw"],
                           col["kc"], col["vc"],
                           col["qf"] * scale, col["kf"], col["zf"]], axis=-1).astype(BF16)
    w_t = jnp.concatenate([col["vf"], col["vs"], col["vw"]], axis=-1)
    w_t = jnp.swapaxes(w_t, 1, 2).astype(BF16)
    small = jnp.concatenate([col["ff"], col["gn"]], axis=-1)
    small = jnp.pad(small, ((0, 0), (0, 0), (0, TILE - small.shape[-1]))).astype(BF16)
    return w_a, w_t, small, jnp.swapaxes(small, 1, 2)


def kernel(x, c, w_ada, b_ada, pre_norm, post_norm, w_in, b_forget, conv_w,
           cmp_pe_k, cmp_w1_k, cmp_w2_k, cmp_pe_v, cmp_w1_v, cmp_w2_v, w_out, rel_bias):
    bsz, s, d = x.shape
    assert bsz == 1 and d == D_MODEL and s % FOX_T == 0 and s >= 2 * WINDOW
    depth = w_in.shape[0]
    x2 = x.reshape(s, d)
    n_cmp_pad = s // CMP_STRIDE

    mod = _ada_mod(jnp.broadcast_to(c, (8, d)), w_ada, b_ada)[:, 0:1, :]
    shift, scale, gate = mod[:, :, :d], mod[:, :, d:2 * d], mod[:, :, 2 * d:]

    w_a, w_t, w_small, w_small_t = _pack_w_in(w_in)
    w_out_b = w_out.astype(BF16)
    half = CMP_LEN * HD // 2
    w1cat = jnp.stack([jnp.concatenate([w[:, :half], w[:, half:]], axis=-1)
                       for w in (cmp_w1_k, cmp_w1_v)], axis=1).astype(BF16)
    pe = jnp.stack([cmp_pe_k, cmp_pe_v], axis=1).reshape(depth, 2, 2, 1, half)
    pe16 = jnp.broadcast_to(pe, (depth, 2, 2, 8, half)).reshape(depth, 2, 16, half).astype(BF16)
    w2 = jnp.stack([cmp_w2_k, cmp_w2_v], axis=1).astype(BF16)
    b_pad = jnp.pad(b_forget, ((0, 0), (0, TILE - H_FOX))).reshape(depth, 1, TILE)
    t0, t1, cstrip = _bias_tables(rel_bias)
    c2s_t = _cmp_to_sel_t(s, n_cmp_pad)

    h = _prenorm(x2, pre_norm[0:1], scale[0], shift[0])
    for l in range(depth):
        proj = _matmul(h, w_a, l, BF16, 1024, 768)
        proj_t = _matmul_t_tiled(h, w_t, l, 1024, 256)
        small = _matmul(h, w_small, l, F32, 1024, TILE)
        small_t = _matmul_t(h, w_small_t, l, 1024)

        ya = _conv_branch(proj, conv_w[l])

        qa, ka = _fox_prep(small, b_pad[l])
        yf = _fox_attention(proj, qa, ka, proj_t)

        kcvc = proj[:, COL_KCVC:COL_KCVC + 4 * HD]
        x4 = kcvc.reshape(n_cmp_pad, CMP_STRIDE, 4, HD).transpose(2, 0, 1, 3)
        x4 = x4.reshape(4, n_cmp_pad, CMP_STRIDE * HD)
        cmp_n, cmp_t = _compress(x4, w1cat, pe16, w2, l)
        glog_t = small_t[H_FOX:H_FOX + 3 * H_NSA].reshape(NSA_G, NSA_R * 3, s)
        yn = _nsa_attention(proj, proj_t, cmp_n, cmp_t, c2s_t, cstrip, t0, t1, glog_t)

        nxt = None
        if l + 1 < depth:
            nxt = (pre_norm[l + 1:l + 2], scale[l + 1], shift[l + 1])
        x2, h = _out_proj(ya, yf, yn, w_out_b, l, x2, post_norm[l:l + 1], gate[l], nxt)
    return x2.reshape(bsz, s, d)
```

```python
import functools
import math

import numpy as np
import jax
import jax.numpy as jnp
from jax import lax
from jax.experimental import pallas as pl
from jax.experimental.pallas import tpu as pltpu

F32 = jnp.float32
BF16 = jnp.bfloat16

D_MODEL = 2048
DEPTH = 4
HD = 128
D_CONV = 512
D_FOX = 768
D_NSA = 768
H_FOX = 6
H_NSA = 6
NSA_G = 2
NSA_R = 3
CONV_WIDTH = 3
CMP_LEN = 32
CMP_STRIDE = 16
CMP_HIDDEN = 256
SEL_LEN = 64
SEL_TOPK = 16
WINDOW = 512
REL_BUCKETS = 32
REL_MAX_DIST = 128
NORM_EPS = 1e-6
NEG_BIG = -1e30
FORCED_SCORE = 1e4
LOG2E = math.log2(math.e)
ONES_ROWS = 16

TILE = 128
NSA_CHUNK = 512
FOX_T = 512
FOX_TK = 512
FOX_HEADS = 2
N_A = 6912
N_T = 1280
VMEM_LIMIT = 56 * 1024 * 1024

NT_DIMS = (((1,), (1,)), ((), ()))

BLK_QN, BLK_ZN, BLK_KS, BLK_KW = 16, 22, 28, 30
COL_KCVC = 4096
BLK_QF, BLK_KF, BLK_ZF = 36, 42, 48


def _cparams(sem, vmem=VMEM_LIMIT):
    return pltpu.CompilerParams(dimension_semantics=sem, vmem_limit_bytes=vmem)


def _silu(v):
    return v * jax.nn.sigmoid(v)


def _mod_kernel(c_ref, w_ref, b_ref, o_ref):
    ca = _silu(c_ref[...])
    o_ref[0] = jnp.dot(ca, w_ref[0], precision=lax.Precision.HIGHEST,
                       preferred_element_type=F32) + b_ref[0]


def _ada_mod(c8, w_ada, b_ada):
    depth, d, n = w_ada.shape
    tn = 768
    return pl.pallas_call(
        _mod_kernel,
        grid=(depth, n // tn),
        in_specs=[pl.BlockSpec((8, d), lambda l, j: (0, 0)),
                  pl.BlockSpec((1, d, tn), lambda l, j: (l, 0, j)),
                  pl.BlockSpec((1, 1, tn), lambda l, j: (l, 0, j))],
        out_specs=pl.BlockSpec((1, 8, tn), lambda l, j: (l, 0, j)),
        out_shape=jax.ShapeDtypeStruct((depth, 8, n), F32),
        compiler_params=_cparams(("parallel", "parallel")),
        name="ada_mod",
    )(c8, w_ada, b_ada.reshape(depth, 1, n))


def _modulated_norm(x, g, scale, shift):
    y = x * lax.rsqrt(jnp.mean(x * x, axis=-1, keepdims=True) + NORM_EPS) * g
    return y * (1.0 + scale) + shift


def _prenorm_kernel(x_ref, g_ref, sc_ref, sh_ref, h_ref):
    h_ref[...] = _modulated_norm(x_ref[...], g_ref[...], sc_ref[...], sh_ref[...]).astype(BF16)


def _prenorm(x2, g, scale, shift):
    s, d = x2.shape
    tm = 512
    vec = pl.BlockSpec((1, d), lambda i: (0, 0))
    return pl.pallas_call(
        _prenorm_kernel,
        grid=(s // tm,),
        in_specs=[pl.BlockSpec((tm, d), lambda i: (i, 0)), vec, vec, vec],
        out_specs=pl.BlockSpec((tm, d), lambda i: (i, 0)),
        out_shape=jax.ShapeDtypeStruct((s, d), BF16),
        compiler_params=_cparams(("parallel",)),
        name="prenorm",
    )(x2, g, scale, shift)


def _mm_kernel(h_ref, w_ref, o_ref):
    o_ref[...] = jnp.dot(h_ref[...], w_ref[...], preferred_element_type=F32).astype(o_ref.dtype)


def _matmul(h, w, l, out_dtype, tm, tn):
    s, d = h.shape
    n = w.shape[2]
    return pl.pallas_call(
        _mm_kernel,
        grid=(s // tm, n // tn),
        in_specs=[pl.BlockSpec((tm, d), lambda i, j: (i, 0)),
                  pl.BlockSpec((None, d, tn), lambda i, j: (l, 0, j))],
        out_specs=pl.BlockSpec((tm, tn), lambda i, j: (i, j)),
        out_shape=jax.ShapeDtypeStruct((s, n), out_dtype),
        compiler_params=_cparams(("parallel", "arbitrary")),
        name="proj_rows",
    )(h, w)


def _mm_t_tiled_kernel(h_ref, wt_ref, o_ref, *, chunks):
    res = lax.dot_general(wt_ref[...], h_ref[...], NT_DIMS, preferred_element_type=F32)
    for c in range(chunks):
        o_ref[c] = res[:, c * TILE:(c + 1) * TILE].astype(o_ref.dtype)


def _matmul_t_tiled(h, wt, l, tm, tn):
    s, d = h.shape
    n = wt.shape[1]
    chunks = tm // TILE
    return pl.pallas_call(
        functools.partial(_mm_t_tiled_kernel, chunks=chunks),
        grid=(s // tm, n // tn),
        in_specs=[pl.BlockSpec((tm, d), lambda i, j: (i, 0)),
                  pl.BlockSpec((None, tn, d), lambda i, j: (l, j, 0))],
        out_specs=pl.BlockSpec((chunks, tn, TILE), lambda i, j: (i, j, 0)),
        out_shape=jax.ShapeDtypeStruct((s // TILE, n, TILE), BF16),
        compiler_params=_cparams(("parallel", "arbitrary")),
        name="proj_cols_tiled",
    )(h, wt)


def _mm_t_kernel(h_ref, wt_ref, o_ref):
    o_ref[...] = lax.dot_general(wt_ref[...], h_ref[...], NT_DIMS, preferred_element_type=F32)


def _matmul_t(h, wt, l, tm):
    s, d = h.shape
    n = wt.shape[1]
    return pl.pallas_call(
        _mm_t_kernel,
        grid=(s // tm,),
        in_specs=[pl.BlockSpec((tm, d), lambda i: (i, 0)),
                  pl.BlockSpec((None, n, d), lambda i: (l, 0, 0))],
        out_specs=pl.BlockSpec((n, tm), lambda i: (0, i)),
        out_shape=jax.ShapeDtypeStruct((n, s), F32),
        compiler_params=_cparams(("parallel",)),
        name="proj_cols_small",
    )(h, wt)


def _fox_prep_kernel(s_ref, b_ref, qa_ref, ka_ref, carry_ref, *, tm):
    @pl.when(pl.program_id(0) == 0)
    def _():
        carry_ref[...] = jnp.zeros_like(carry_ref)

    v = s_ref[...] + b_ref[...]
    c = jnp.minimum(v, 0.0) - jnp.log1p(jnp.exp(-jnp.abs(v)))
    row = lax.broadcasted_iota(jnp.int32, (tm, TILE), 0)
    sh = 1
    while sh < tm:
        c = c + jnp.where(row >= sh, pltpu.roll(c, sh, axis=0), 0.0)
        sh *= 2
    c = c + carry_ref[...]
    carry_ref[...] = c[tm - 1:tm, :]
    c = c * LOG2E

    lane = lax.broadcasted_iota(jnp.int32, (tm, TILE), 1)
    for h in range(H_FOX):
        ch = jnp.broadcast_to(c[:, h:h + 1], (tm, TILE))
        hi = ch.astype(BF16).astype(F32)
        r1 = ch - hi
        mid = r1.astype(BF16).astype(F32)
        lo = r1 - mid
        pieces = jnp.where((lane == 0) | (lane == 3), hi,
                           jnp.where((lane == 1) | (lane == 4), mid, lo))
        qa = jnp.where(lane < 3, pieces, jnp.where(lane < 6, 1.0, 0.0))
        ka = jnp.where(lane < 3, 1.0, jnp.where(lane < 6, -pieces, 0.0))
        qa_ref[:, h * TILE:(h + 1) * TILE] = qa.astype(BF16)
        ka_ref[:, h * TILE:(h + 1) * TILE] = ka.astype(BF16)


def _fox_prep(small, b_pad):
    s = small.shape[0]
    tm = 512
    out = jax.ShapeDtypeStruct((s, H_FOX * TILE), BF16)
    return pl.pallas_call(
        functools.partial(_fox_prep_kernel, tm=tm),
        grid=(s // tm,),
        in_specs=[pl.BlockSpec((tm, TILE), lambda i: (i, 0)),
                  pl.BlockSpec((1, TILE), lambda i: (0, 0))],
        out_specs=[pl.BlockSpec((tm, H_FOX * TILE), lambda i: (i, 0))] * 2,
        out_shape=[out, out],
        scratch_shapes=[pltpu.VMEM((1, TILE), F32)],
        compiler_params=_cparams(("arbitrary",)),
        name="fox_prep",
    )(small, b_pad)


def _fox_kernel(q_ref, qa_ref, k_ref, ka_ref, vt_ref, z_ref, o_ref, s0_ref, s1_ref, s2_ref,
                acc_ref):
    i = pl.program_id(1)
    t = FOX_T
    tk = FOX_TK
    heads = range(FOX_HEADS)
    col = lambda h: slice(h * TILE, (h + 1) * TILE)
    qaug = [jnp.concatenate([q_ref[:, col(h)], qa_ref[:, col(h)]], axis=1) for h in heads]

    def scores(h, j, rows):
        r0 = pl.multiple_of(j * rows, rows)
        kaug = jnp.concatenate([k_ref[pl.ds(r0, rows), col(h)], ka_ref[pl.ds(r0, rows), col(h)]],
                               axis=1)
        return lax.dot_general(kaug, qaug[h], NT_DIMS, preferred_element_type=F32)

    def values_t(h, j, rows):
        n_sub = rows // TILE
        return jnp.concatenate(
            [jnp.concatenate([vt_ref[n_sub * j + c, col(h), :] for c in range(n_sub)], axis=1),
             jnp.ones((ONES_ROWS, rows), BF16)], axis=0)

    kio = lax.broadcasted_iota(jnp.int32, (t, t), 0)
    qio = lax.broadcasted_iota(jnp.int32, (t, t), 1)
    ms = []
    for h in heads:
        s = jnp.where(kio <= qio, scores(h, i, t), NEG_BIG)
        m = jnp.max(s, axis=0, keepdims=True)
        p = jnp.exp2(s - m)
        ms.append(m)
        acc_ref[h] = jnp.dot(values_t(h, i, t), p.astype(BF16), preferred_element_type=F32)

    def update(h, s, vt, m):
        m_new = jnp.maximum(m, jnp.max(s, axis=0, keepdims=True))
        alpha = jnp.exp2(m - m_new)
        p = jnp.exp2(s - m_new)
        acc_ref[h] = alpha * acc_ref[h] + jnp.dot(vt, p.astype(BF16), preferred_element_type=F32)
        return m_new

    bufs = (s0_ref, s1_ref, s2_ref)
    for h in heads:
        s0_ref[h] = scores(h, 0, tk)

    def triple(pp, ms):
        ms = list(ms)
        for stage in range(3):
            cur, nxt = bufs[stage], bufs[(stage + 1) % 3]
            for h in heads:
                nxt[h] = scores(h, 3 * pp + stage + 1, tk)
            for h in heads:
                ms[h] = update(h, cur[h], values_t(h, 3 * pp + stage, tk), ms[h])
        return tuple(ms)

    n_far = i * (t // tk)
    ms = lax.fori_loop(0, n_far // 3, triple, tuple(ms))
    done = (n_far // 3) * 3

    @pl.when(n_far - done == 1)
    def _():
        for h in heads:
            update(h, s0_ref[h], values_t(h, done, tk), ms[h])

    @pl.when(n_far - done == 2)
    def _():
        for h in heads:
            s1_ref[h] = scores(h, done + 1, tk)
        for h in heads:
            m_mid = update(h, s0_ref[h], values_t(h, done, tk), ms[h])
            update(h, s1_ref[h], values_t(h, done + 1, tk), m_mid)

    for h in heads:
        o = (acc_ref[h, 0:HD, :] * (1.0 / acc_ref[h, HD:HD + 1, :])).T
        o_ref[:, col(h)] = (o * _silu(z_ref[:, col(h)].astype(F32))).astype(BF16)


def _fox_attention(proj, qa, ka, proj_t):
    s = proj.shape[0]
    t = FOX_T
    nk = s // TILE
    hw = FOX_HEADS * TILE
    return pl.pallas_call(
        _fox_kernel,
        grid=(H_FOX // FOX_HEADS, s // t),
        scratch_shapes=[pltpu.VMEM((FOX_HEADS, FOX_TK, t), F32),
                        pltpu.VMEM((FOX_HEADS, FOX_TK, t), F32),
                        pltpu.VMEM((FOX_HEADS, FOX_TK, t), F32),
                        pltpu.VMEM((FOX_HEADS, HD + ONES_ROWS, t), F32)],
        in_specs=[pl.BlockSpec((t, hw), lambda h, i: (i, BLK_QF // FOX_HEADS + h)),
                  pl.BlockSpec((t, hw), lambda h, i: (i, h)),
                  pl.BlockSpec((s, hw), lambda h, i: (0, BLK_KF // FOX_HEADS + h)),
                  pl.BlockSpec((s, hw), lambda h, i: (0, h)),
                  pl.BlockSpec((nk, hw, TILE), lambda h, i: (0, h, 0)),
                  pl.BlockSpec((t, hw), lambda h, i: (i, BLK_ZF // FOX_HEADS + h))],
        out_specs=pl.BlockSpec((t, hw), lambda h, i: (i, h)),
        out_shape=jax.ShapeDtypeStruct((s, D_FOX), BF16),
        compiler_params=_cparams(("parallel", "parallel")),
        name="fox_attention",
    )(proj, qa, proj, ka, proj_t, proj)


def _compress_kernel(x_ref, w1_ref, pe_ref, w2_ref, o_ref, ot_ref):
    n = x_ref.shape[0]
    ab = jnp.dot(x_ref[...], w1_ref[0], preferred_element_type=F32)
    pe = jnp.dot(pe_ref[0], w1_ref[0], preferred_element_type=F32)
    a = ab[:, :CMP_HIDDEN]
    b_next = pltpu.roll(ab[:, CMP_HIDDEN:], n - 1, axis=0)
    hid = a + b_next + pe[0:1, :CMP_HIDDEN] + pe[8:9, CMP_HIDDEN:]
    out = jnp.dot(_silu(hid).astype(BF16), w2_ref[0], preferred_element_type=F32)
    o_ref[...] = out.astype(BF16)
    ot_ref[...] = out.T.astype(BF16)


def _compress(x4, w1cat, pe16, w2, l):
    _, n, k = x4.shape
    kind = lambda c: (l, c // NSA_G, 0, 0)
    return pl.pallas_call(
        _compress_kernel,
        grid=(2 * NSA_G,),
        in_specs=[pl.BlockSpec((None, n, k), lambda c: (c, 0, 0)),
                  pl.BlockSpec((None, 1, k, 2 * CMP_HIDDEN), kind),
                  pl.BlockSpec((None, 1, 16, k), kind),
                  pl.BlockSpec((None, 1, CMP_HIDDEN, HD), kind)],
        out_specs=[pl.BlockSpec((None, n, HD), lambda c: (c, 0, 0)),
                   pl.BlockSpec((None, HD, n), lambda c: (c, 0, 0))],
        out_shape=[jax.ShapeDtypeStruct((2 * NSA_G, n, HD), BF16),
                   jax.ShapeDtypeStruct((2 * NSA_G, HD, n), BF16)],
        compiler_params=_cparams(("parallel",)),
        name="nsa_compress",
    )(x4, w1cat, pe16, w2)


def _nsa_kernel_one_group(q0_ref, q1_ref, q2_ref, z0_ref, z1_ref, z2_ref,
                ks_ref, kw_ref, vst_ref, vwt_ref, kc_ref, vct_ref,
                c2s_ref, cstrip_ref, t0_ref, t1_ref, gl_ref, o_ref,
                sc_ref, sel_ref, m_s, l_s, acc_s, ss_ref, *, n_cmp):
    i = pl.program_id(1)
    w3 = NSA_R * TILE
    qs = jnp.concatenate([q0_ref[...], q1_ref[...], q2_ref[...]], axis=0)
    b_io = lax.broadcasted_iota(jnp.int32, (TILE, w3), 0)
    a_io = lax.broadcasted_iota(jnp.int32, (TILE, w3), 1) & (TILE - 1)
    t0 = t0_ref[...]
    t1 = t1_ref[...]

    sc = lax.dot_general(kc_ref[...], qs, NT_DIMS, preferred_element_type=F32)
    sc_ref[0:16, :] = jnp.zeros((16, w3), F32)
    sc_ref[16:16 + n_cmp, :] = sc
    w0 = pl.multiple_of(8 * i, 8)
    sc_ref[pl.ds(w0, 24), :] = sc_ref[pl.ds(w0, 24), :] + cstrip_ref[...]
    sc = sc_ref[16:16 + n_cmp, :]
    n_io = lax.broadcasted_iota(jnp.int32, (n_cmp, w3), 0)
    qa_io = lax.broadcasted_iota(jnp.int32, (n_cmp, w3), 1) & (TILE - 1)
    valid = (CMP_STRIDE * n_io + (CMP_LEN - 1)) <= (TILE * i + qa_io)
    scm = jnp.where(valid, sc, NEG_BIG)
    mc = jnp.max(scm, axis=0, keepdims=True)
    pc = jnp.where(valid, jnp.exp2(scm - mc), 0.0)
    lc = jnp.sum(pc, axis=0, keepdims=True)
    pcn = pc * jnp.where(lc > 0.0, 1.0 / lc, 0.0)
    oc_t = jnp.dot(vct_ref[...], pcn.astype(BF16), preferred_element_type=F32)

    psum = pcn[:, 0:TILE] + pcn[:, TILE:2 * TILE] + pcn[:, 2 * TILE:3 * TILE]
    p_hi = psum.astype(BF16)
    p_lo = (psum - p_hi.astype(F32)).astype(BF16)
    c2s = c2s_ref[...]
    imp = (jnp.dot(c2s, p_hi, preferred_element_type=F32)
           + jnp.dot(c2s, p_lo, preferred_element_type=F32))
    n_sel = imp.shape[0]
    m_io = lax.broadcasted_iota(jnp.int32, (n_sel, TILE), 0)
    m_f = m_io.astype(F32)
    qpos = TILE * i + lax.broadcasted_iota(jnp.int32, (n_sel, TILE), 1)
    own = qpos >> 6
    forced = (m_io == 0) | (m_io == own) | (m_io == own - 1)
    score = jnp.where(SEL_LEN * m_io <= qpos, jnp.where(forced, FORCED_SCORE, imp), -1.0)

    def pick_one(_, carry):
        cur, sel = carry
        mx = jnp.max(cur, axis=0, keepdims=True)
        idx = jnp.min(jnp.where(cur == mx, m_f, float(n_sel)), axis=0, keepdims=True)
        pick = m_f == idx
        keep = jnp.where(mx >= 0.0, 1.0, 0.0)
        return jnp.where(pick, -2.0, cur), jnp.where(pick, keep, sel)

    _, sel = lax.fori_loop(0, min(SEL_TOPK, n_sel), pick_one,
                           (score, jnp.zeros((n_sel, TILE), F32)))
    sel_ref[...] = sel

    tpc = NSA_CHUNK // TILE

    def key_scores(k_ref, tile0, n_tiles):
        r0 = pl.multiple_of(tile0 * TILE, TILE)
        return lax.dot_general(k_ref[pl.ds(r0, n_tiles * TILE), :], qs, NT_DIMS,
                               preferred_element_type=F32)

    def values_t(vt_ref, tile0, n_tiles):
        return jnp.concatenate([vt_ref[tile0 + c] for c in range(n_tiles)], axis=1)

    def add_near_bias(d_tile, n_tiles):
        for c in range(n_tiles):
            @pl.when(d_tile == c + 1)
            def _(c=c):
                ss_ref[c * TILE:(c + 1) * TILE, :] = ss_ref[c * TILE:(c + 1) * TILE, :] + t1

            @pl.when(d_tile == c)
            def _(c=c):
                ss_ref[c * TILE:(c + 1) * TILE, :] = jnp.where(
                    b_io <= a_io, ss_ref[c * TILE:(c + 1) * TILE, :] + t0, NEG_BIG)

    def sel_mask(c):
        rows = sel_ref[pl.ds(pl.multiple_of(c * (NSA_CHUNK // SEL_LEN), 8), NSA_CHUNK // SEL_LEN), :]
        mk = jnp.concatenate([jnp.broadcast_to(rows[b:b + 1, :], (SEL_LEN, TILE))
                              for b in range(NSA_CHUNK // SEL_LEN)], axis=0)
        return jnp.concatenate([mk, mk, mk], axis=1) > 0.5

    def sel_update(c, s):
        s = jnp.where(sel_mask(c), s, NEG_BIG)
        m_old = m_s[...]
        m_new = jnp.maximum(m_old, jnp.max(s, axis=0, keepdims=True))
        alpha = jnp.exp2(m_old - m_new)
        p = jnp.exp2(s - m_new)
        l_s[...] = alpha * l_s[...] + jnp.sum(p, axis=0, keepdims=True)
        acc_s[...] = alpha * acc_s[...] + jnp.dot(values_t(vst_ref, c * tpc, tpc), p.astype(BF16),
                                                  preferred_element_type=F32)
        m_s[...] = m_new

    m_s[...] = jnp.full(m_s.shape, NEG_BIG, F32)
    l_s[...] = jnp.zeros(l_s.shape, F32)
    acc_s[...] = jnp.zeros(acc_s.shape, F32)

    c_own = i // tpc
    c_prev = jnp.maximum(i - 1, 0) // tpc

    def far_chunk(c, carry):
        sel_update(c, key_scores(ks_ref, c * tpc, tpc))
        return carry

    lax.fori_loop(0, c_prev, far_chunk, 0)

    def near_chunk(c):
        ss_ref[0:NSA_CHUNK, :] = key_scores(ks_ref, c * tpc, tpc)
        add_near_bias(i - c * tpc, tpc)
        sel_update(c, ss_ref[0:NSA_CHUNK, :])

    @pl.when(c_prev != c_own)
    def _():
        near_chunk(c_prev)

    near_chunk(c_own)

    n_wt = WINDOW // TILE + 1
    w_tile0 = jnp.maximum(i - WINDOW // TILE, 0)
    ss_ref[...] = key_scores(kw_ref, w_tile0, n_wt)
    add_near_bias(i - w_tile0, n_wt)

    @pl.when(i >= WINDOW // TILE)
    def _():
        ss_ref[0:TILE, :] = jnp.where(a_io < b_io, ss_ref[0:TILE, :], NEG_BIG)

    for c in range(1, n_wt):
        @pl.when(i < c)
        def _(c=c):
            ss_ref[c * TILE:(c + 1) * TILE, :] = jnp.full((TILE, w3), NEG_BIG, F32)

    sw = ss_ref[...]
    m_w = jnp.max(sw, axis=0, keepdims=True)
    p_w = jnp.exp2(sw - m_w)
    l_w = jnp.sum(p_w, axis=0, keepdims=True)
    acc_w = jnp.dot(values_t(vwt_ref, w_tile0, n_wt), p_w.astype(BF16), preferred_element_type=F32)

    gates = jax.nn.sigmoid(gl_ref[...])

    def gate_row(branch):
        return jnp.concatenate([gates[r * 3 + branch:r * 3 + branch + 1, :] for r in range(NSA_R)],
                               axis=1)

    out_t = (oc_t * gate_row(0)
             + acc_s[...] * (gate_row(1) / l_s[...])
             + acc_w * (gate_row(2) / l_w))
    z_refs = (z0_ref, z1_ref, z2_ref)
    for r in range(NSA_R):
        o_r = out_t[:, r * TILE:(r + 1) * TILE].T
        o_ref[:, r * TILE:(r + 1) * TILE] = (o_r * _silu(z_refs[r][...].astype(F32))).astype(BF16)


def _nsa_attention_one_group(proj, proj_t, cmp_n, cmp_t, c2s_t, cstrip, t0, t1, glog_t):
    s = proj.shape[0]
    nk = s // TILE
    n_cmp = cmp_n.shape[1]
    n_sel = s // SEL_LEN
    w3 = NSA_R * TILE
    qspec = lambda r: pl.BlockSpec((TILE, TILE), lambda g, i, r=r: (i, BLK_QN + NSA_R * g + r))
    zspec = lambda r: pl.BlockSpec((TILE, TILE), lambda g, i, r=r: (i, BLK_ZN + NSA_R * g + r))
    per_group = lambda shape: pl.BlockSpec((None,) + shape, lambda g, i: (g, 0, 0))
    return pl.pallas_call(
        functools.partial(_nsa_kernel, n_cmp=n_cmp),
        grid=(NSA_G, s // TILE),
        in_specs=[qspec(0), qspec(1), qspec(2), zspec(0), zspec(1), zspec(2),
                  pl.BlockSpec((s, TILE), lambda g, i: (0, BLK_KS + g)),
                  pl.BlockSpec((s, TILE), lambda g, i: (0, BLK_KW + g)),
                  pl.BlockSpec((nk, HD, TILE), lambda g, i: (0, H_FOX + g, 0)),
                  pl.BlockSpec((nk, HD, TILE), lambda g, i: (0, H_FOX + NSA_G + g, 0)),
                  pl.BlockSpec((None, n_cmp, HD), lambda g, i: (g, 0, 0)),
                  pl.BlockSpec((None, HD, n_cmp), lambda g, i: (NSA_G + g, 0, 0)),
                  pl.BlockSpec((n_sel, n_cmp), lambda g, i: (0, 0)),
                  per_group((24, w3)), per_group((TILE, w3)), per_group((TILE, w3)),
                  pl.BlockSpec((None, NSA_R * 3, TILE), lambda g, i: (g, 0, i))],
        out_specs=pl.BlockSpec((TILE, w3), lambda g, i: (i, g)),
        out_shape=jax.ShapeDtypeStruct((s, D_NSA), BF16),
        scratch_shapes=[pltpu.VMEM((16 + n_cmp + 8, w3), F32),
                        pltpu.VMEM((n_sel, TILE), F32),
                        pltpu.VMEM((1, w3), F32), pltpu.VMEM((1, w3), F32), pltpu.VMEM((HD, w3), F32),
                        pltpu.VMEM((WINDOW + TILE, w3), F32)],
        compiler_params=_cparams(("parallel", "parallel")),
        name="nsa_attention",
    )(proj, proj, proj, proj, proj, proj, proj, proj, proj_t, proj_t,
      cmp_n, cmp_t, c2s_t, cstrip, t0, t1, glog_t)


def _nsa_kernel(*refs, n_cmp):
    q_refs, z_refs = refs[0:H_NSA], refs[H_NSA:2 * H_NSA]
    (ks_ref, kw_ref, vst_ref, vwt_ref, kc_ref, vct_ref, c2s_ref, cstrip_ref, t0_ref, t1_ref,
     gl_ref, o_ref, sc_ref, sel_ref, m_s, acc_s, ss_ref, sw_ref, sa_ref, sb_ref,
     sc3_ref) = refs[2 * H_NSA:]
    i = pl.program_id(0)
    w3 = NSA_R * TILE
    groups = range(NSA_G)
    col = lambda g: slice(g * TILE, (g + 1) * TILE)
    qs = [jnp.concatenate([q_refs[NSA_R * g + r][...] for r in range(NSA_R)], axis=0)
          for g in groups]
    b_io = lax.broadcasted_iota(jnp.int32, (TILE, w3), 0)
    a_io = lax.broadcasted_iota(jnp.int32, (TILE, w3), 1) & (TILE - 1)

    w0 = pl.multiple_of(8 * i, 8)
    n_io = lax.broadcasted_iota(jnp.int32, (n_cmp, w3), 0)
    qa_io = lax.broadcasted_iota(jnp.int32, (n_cmp, w3), 1) & (TILE - 1)
    valid = (CMP_STRIDE * n_io + (CMP_LEN - 1)) <= (TILE * i + qa_io)
    oc_t, psums = [], []
    for g in groups:
        sc_ref[g, 0:16, :] = jnp.zeros((16, w3), F32)
        sc_ref[g, 16:16 + n_cmp, :] = lax.dot_general(kc_ref[g], qs[g], NT_DIMS,
                                                      preferred_element_type=F32)
        sc_ref[g, pl.ds(w0, 24), :] = sc_ref[g, pl.ds(w0, 24), :] + cstrip_ref[g]
        scm = jnp.where(valid, sc_ref[g, 16:16 + n_cmp, :], NEG_BIG)
        mc = jnp.max(scm, axis=0, keepdims=True)
        pc = jnp.where(valid, jnp.exp2(scm - mc), 0.0)
        lc = jnp.sum(pc, axis=0, keepdims=True)
        pcn = pc * jnp.where(lc > 0.0, 1.0 / lc, 0.0)
        oc_t.append(jnp.dot(vct_ref[g], pcn.astype(BF16), preferred_element_type=F32))
        psums.append(pcn[:, 0:TILE] + pcn[:, TILE:2 * TILE] + pcn[:, 2 * TILE:3 * TILE])

    psum = jnp.concatenate(psums, axis=1)
    p_hi = psum.astype(BF16)
    p_lo = (psum - p_hi.astype(F32)).astype(BF16)
    c2s = c2s_ref[...]
    imp = (jnp.dot(c2s, p_hi, preferred_element_type=F32)
           + jnp.dot(c2s, p_lo, preferred_element_type=F32))
    n_sel = imp.shape[0]
    gw = NSA_G * TILE
    m_io = lax.broadcasted_iota(jnp.int32, (n_sel, gw), 0)
    m_f = m_io.astype(F32)
    qpos = TILE * i + (lax.broadcasted_iota(jnp.int32, (n_sel, gw), 1) & (TILE - 1))
    own = qpos >> 6
    forced = (m_io == 0) | (m_io == own) | (m_io == own - 1)
    eligible = SEL_LEN * m_io <= qpos
    few = TILE * (i + 1) <= SEL_TOPK * SEL_LEN
    sel0 = jnp.where(eligible & (forced | few), 1.0, 0.0)
    score = jnp.where(eligible, jnp.where(forced, -2.0, imp), -1.0)
    n_forced = 3

    def pick_one(_, carry):
        cur, sel = carry
        mx = jnp.max(cur, axis=0, keepdims=True)
        idx = jnp.min(jnp.where(cur == mx, m_f, float(n_sel)), axis=0, keepdims=True)
        pick = m_f == idx
        keep = jnp.where(mx >= 0.0, 1.0, 0.0)
        return jnp.where(pick, -2.0, cur), jnp.where(pick, jnp.maximum(keep, sel), sel)

    rounds = jnp.where(few, 0, min(SEL_TOPK, n_sel) - n_forced)
    _, sel = lax.fori_loop(0, rounds, pick_one, (score, sel0))
    sel_ref[...] = sel

    tpc = NSA_CHUNK // TILE
    bpc = NSA_CHUNK // SEL_LEN

    def key_scores(k_ref, g, tile0, n_tiles):
        r0 = pl.multiple_of(tile0 * TILE, TILE)
        return lax.dot_general(k_ref[pl.ds(r0, n_tiles * TILE), col(g)], qs[g], NT_DIMS,
                               preferred_element_type=F32)

    def values_t(vt_ref, g, tile0, n_tiles):
        return jnp.concatenate(
            [jnp.concatenate([vt_ref[tile0 + c, col(g), :] for c in range(n_tiles)], axis=1),
             jnp.ones((ONES_ROWS, n_tiles * TILE), BF16)], axis=0)

    def add_near_bias(s_ref, d_tile, n_tiles):
        for c in range(n_tiles):
            rows = slice(c * TILE, (c + 1) * TILE)

            @pl.when(d_tile == c + 1)
            def _(rows=rows):
                for g in groups:
                    s_ref[g, rows, :] = s_ref[g, rows, :] + t1_ref[g]

            @pl.when(d_tile == c)
            def _(rows=rows):
                for g in groups:
                    s_ref[g, rows, :] = jnp.where(b_io <= a_io, s_ref[g, rows, :] + t0_ref[g],
                                                  NEG_BIG)

    def sel_mask(g, c):
        rows = sel_ref[pl.ds(pl.multiple_of(c * bpc, 8), bpc), col(g)]
        mk = jnp.concatenate([jnp.broadcast_to(rows[b:b + 1, :], (SEL_LEN, TILE))
                              for b in range(bpc)], axis=0)
        return jnp.concatenate([mk, mk, mk], axis=1) > 0.5

    def sel_update(g, c, s):
        s = jnp.where(sel_mask(g, c), s, NEG_BIG)
        m_old = m_s[g]
        m_new = jnp.maximum(m_old, jnp.max(s, axis=0, keepdims=True))
        alpha = jnp.exp2(m_old - m_new)
        p = jnp.exp2(s - m_new)
        acc_s[g] = alpha * acc_s[g] + jnp.dot(values_t(vst_ref, g, c * tpc, tpc), p.astype(BF16),
                                              preferred_element_type=F32)
        m_s[g] = m_new

    m_s[...] = jnp.full(m_s.shape, NEG_BIG, F32)
    acc_s[...] = jnp.zeros(acc_s.shape, F32)

    c_own = i // tpc
    c_prev = jnp.maximum(i - 1, 0) // tpc

    bufs = (sa_ref, sb_ref, sc3_ref)
    for g in groups:
        sa_ref[g] = key_scores(ks_ref, g, 0, tpc)

    def far_triple(pp, carry):
        for stage in range(3):
            cur, nxt = bufs[stage], bufs[(stage + 1) % 3]
            for g in groups:
                nxt[g] = key_scores(ks_ref, g, (3 * pp + stage + 1) * tpc, tpc)
            for g in groups:
                sel_update(g, 3 * pp + stage, cur[g])
        return carry

    lax.fori_loop(0, c_prev // 3, far_triple, 0)
    done = (c_prev // 3) * 3

    @pl.when(c_prev - done == 1)
    def _():
        for g in groups:
            sel_update(g, done, sa_ref[g])

    @pl.when(c_prev - done == 2)
    def _():
        for g in groups:
            sb_ref[g] = key_scores(ks_ref, g, (done + 1) * tpc, tpc)
        for g in groups:
            sel_update(g, done, sa_ref[g])
            sel_update(g, done + 1, sb_ref[g])

    @pl.when(c_prev != c_own)
    def _():
        last = slice((tpc - 1) * TILE, tpc * TILE)
        for g in groups:
            ss_ref[g] = key_scores(ks_ref, g, c_prev * tpc, tpc)
            ss_ref[g, last, :] = ss_ref[g, last, :] + t1_ref[g]
            sel_update(g, c_prev, ss_ref[g])

    def fix_near(s_ref, g, d):
        own = pl.ds(pl.multiple_of(d * TILE, TILE), TILE)
        prev = pl.ds(pl.multiple_of(jnp.maximum(d - 1, 0) * TILE, TILE), TILE)
        s_ref[g, prev, :] = s_ref[g, prev, :] + jnp.where(d >= 1, t1_ref[g], 0.0)
        s_ref[g, own, :] = jnp.where(b_io <= a_io, s_ref[g, own, :] + t0_ref[g], NEG_BIG)

    n_wt = WINDOW // TILE + 1
    w_tile0 = jnp.maximum(i - WINDOW // TILE, 0)
    d_win = i - w_tile0
    for g in groups:
        ss_ref[g] = key_scores(ks_ref, g, c_own * tpc, tpc)
        sw_ref[g] = key_scores(kw_ref, g, w_tile0, n_wt)
    for g in groups:
        fix_near(ss_ref, g, i - c_own * tpc)
        for c in range(1, n_wt):
            rows = slice(c * TILE, (c + 1) * TILE)
            sw_ref[g, rows, :] = jnp.where(d_win < c, NEG_BIG, sw_ref[g, rows, :])
        fix_near(sw_ref, g, d_win)
        sw_ref[g, 0:TILE, :] = jnp.where((a_io < b_io) | (i < WINDOW // TILE),
                                         sw_ref[g, 0:TILE, :], NEG_BIG)
    for g in groups:
        sel_update(g, c_own, ss_ref[g])

    for g in groups:
        sw = sw_ref[g]
        p_w = jnp.exp2(sw - jnp.max(sw, axis=0, keepdims=True))
        acc_w = jnp.dot(values_t(vwt_ref, g, w_tile0, n_wt), p_w.astype(BF16),
                        preferred_element_type=F32)
        gates = jax.nn.sigmoid(gl_ref[g])

        def gate_row(branch):
            return jnp.concatenate([gates[r * 3 + branch:r * 3 + branch + 1, :]
                                    for r in range(NSA_R)], axis=1)

        out_t = (oc_t[g] * gate_row(0)
                 + acc_s[g, 0:HD, :] * (gate_row(1) / acc_s[g, HD:HD + 1, :])
                 + acc_w[0:HD, :] * (gate_row(2) / acc_w[HD:HD + 1, :]))
        for r in range(NSA_R):
            h = NSA_R * g + r
            o_r = out_t[:, r * TILE:(r + 1) * TILE].T
            o_ref[:, h * TILE:(h + 1) * TILE] = (
                o_r * _silu(z_refs[h][...].astype(F32))).astype(BF16)


def _nsa_attention(proj, proj_t, cmp_n, cmp_t, c2s_t, cstrip, t0, t1, glog_t):
    s = proj.shape[0]
    nk = s // TILE
    n_cmp = cmp_n.shape[1]
    n_sel = s // SEL_LEN
    w3 = NSA_R * TILE
    gw = NSA_G * TILE
    qspec = lambda h: pl.BlockSpec((TILE, TILE), lambda i, h=h: (i, BLK_QN + h))
    zspec = lambda h: pl.BlockSpec((TILE, TILE), lambda i, h=h: (i, BLK_ZN + h))
    whole = lambda shape: pl.BlockSpec(shape, lambda i: (0,) * len(shape))
    return pl.pallas_call(
        functools.partial(_nsa_kernel, n_cmp=n_cmp),
        grid=(s // TILE,),
        in_specs=[qspec(h) for h in range(H_NSA)] + [zspec(h) for h in range(H_NSA)] + [
            pl.BlockSpec((s, gw), lambda i: (0, BLK_KS // NSA_G), pipeline_mode=pl.Buffered(1)),
            pl.BlockSpec((s, gw), lambda i: (0, BLK_KW // NSA_G), pipeline_mode=pl.Buffered(1)),
            pl.BlockSpec((nk, gw, TILE), lambda i: (0, H_FOX // NSA_G, 0),
                         pipeline_mode=pl.Buffered(1)),
            pl.BlockSpec((nk, gw, TILE), lambda i: (0, H_FOX // NSA_G + 1, 0),
                         pipeline_mode=pl.Buffered(1)),
            pl.BlockSpec((NSA_G, n_cmp, HD), lambda i: (0, 0, 0)),
            pl.BlockSpec((NSA_G, HD, n_cmp), lambda i: (1, 0, 0)),
            whole((n_sel, n_cmp)), whole((NSA_G, 24, w3)), whole((NSA_G, TILE, w3)),
            whole((NSA_G, TILE, w3)),
            pl.BlockSpec((NSA_G, NSA_R * 3, TILE), lambda i: (0, 0, i))],
        out_specs=pl.BlockSpec((TILE, D_NSA), lambda i: (i, 0)),
        out_shape=jax.ShapeDtypeStruct((s, D_NSA), BF16),
        scratch_shapes=[pltpu.VMEM((NSA_G, 16 + n_cmp + 8, w3), F32),
                        pltpu.VMEM((n_sel, gw), F32),
                        pltpu.VMEM((NSA_G, 1, w3), F32),
                        pltpu.VMEM((NSA_G, HD + ONES_ROWS, w3), F32),
                        pltpu.VMEM((NSA_G, NSA_CHUNK, w3), F32),
                        pltpu.VMEM((NSA_G, WINDOW + TILE, w3), F32),
                        pltpu.VMEM((NSA_G, NSA_CHUNK, w3), F32),
                        pltpu.VMEM((NSA_G, NSA_CHUNK, w3), F32),
                        pltpu.VMEM((NSA_G, NSA_CHUNK, w3), F32)],
        compiler_params=_cparams(("parallel",)),
        name="nsa_attention",
    )(*([proj] * (2 * H_NSA)), proj, proj, proj_t, proj_t,
      cmp_n, cmp_t, c2s_t, cstrip, t0, t1, glog_t)


def _conv_kernel(u_ref, gb_ref, gc_ref, z_ref, uh_ref, gch_ref, w_ref, o_ref, *, tm):
    i = pl.program_id(0)
    y = gc_ref[...].astype(F32) * u_ref[...].astype(F32)
    yh = gch_ref[...].astype(F32) * uh_ref[...].astype(F32)
    yh = jnp.where(i > 0, yh, 0.0)
    h1 = yh[15:16, :]
    h2 = yh[14:15, :]
    row = lax.broadcasted_iota(jnp.int32, (tm, D_CONV), 0)
    y1 = jnp.where(row == 0, h1, pltpu.roll(y, 1, axis=0))
    y2 = jnp.where(row == 0, h2, jnp.where(row == 1, h1, pltpu.roll(y, 2, axis=0)))
    w = w_ref[...]
    conv = w[0:1, :] * y2 + w[1:2, :] * y1 + w[2:3, :] * y
    o_ref[...] = (gb_ref[...].astype(F32) * conv * _silu(z_ref[...].astype(F32))).astype(BF16)


def _conv_branch(proj, conv_w):
    s = proj.shape[0]
    tm = 512
    main = lambda c: pl.BlockSpec((tm, D_CONV), lambda i, c=c: (i, c))
    halo = lambda c: pl.BlockSpec((16, D_CONV), lambda i, c=c: (jnp.maximum(i * (tm // 16) - 1, 0), c))
    return pl.pallas_call(
        functools.partial(_conv_kernel, tm=tm),
        grid=(s // tm,),
        in_specs=[main(0), main(1), main(2), main(3), halo(0), halo(2),
                  pl.BlockSpec((CONV_WIDTH, D_CONV), lambda i: (0, 0))],
        out_specs=pl.BlockSpec((tm, D_CONV), lambda i: (i, 0)),
        out_shape=jax.ShapeDtypeStruct((s, D_CONV), BF16),
        compiler_params=_cparams(("parallel",)),
        name="conv_branch",
    )(proj, proj, proj, proj, proj, proj, conv_w)


def _out_kernel(ya_ref, yf_ref, yn_ref, w_ref, x_ref, pg_ref, gate_ref, *rest, with_next):
    y = (jnp.dot(ya_ref[...], w_ref[0:D_CONV, :], preferred_element_type=F32)
         + jnp.dot(yf_ref[...], w_ref[D_CONV:D_CONV + D_FOX, :], preferred_element_type=F32)
         + jnp.dot(yn_ref[...], w_ref[D_CONV + D_FOX:, :], preferred_element_type=F32))
    yn = y * lax.rsqrt(jnp.mean(y * y, axis=-1, keepdims=True) + NORM_EPS) * pg_ref[...]
    xn = x_ref[...] + gate_ref[...] * yn
    if with_next:
        ng_ref, nsc_ref, nsh_ref, xo_ref, h_ref = rest
        h_ref[...] = _modulated_norm(xn, ng_ref[...], nsc_ref[...], nsh_ref[...]).astype(BF16)
    else:
        (xo_ref,) = rest
    xo_ref[...] = xn


def _out_proj(ya, yf, yn, w_out, l, x2, post_g, gate, nxt):
    s, d = x2.shape
    tm = 512
    rows = lambda n: pl.BlockSpec((tm, n), lambda i: (i, 0))
    vec = pl.BlockSpec((1, d), lambda i: (0, 0))
    in_specs = [rows(D_CONV), rows(D_FOX), rows(D_NSA),
                pl.BlockSpec((None, d, d), lambda i: (l, 0, 0), pipeline_mode=pl.Buffered(1)),
                rows(d), vec, vec]
    args = [ya, yf, yn, w_out, x2, post_g, gate]
    out_specs = [rows(d)]
    out_shape = [jax.ShapeDtypeStruct((s, d), F32)]
    if nxt is not None:
        in_specs += [vec, vec, vec]
        args += list(nxt)
        out_specs.append(rows(d))
        out_shape.append(jax.ShapeDtypeStruct((s, d), BF16))
    res = pl.pallas_call(
        functools.partial(_out_kernel, with_next=nxt is not None),
        grid=(s // tm,),
        in_specs=in_specs,
        out_specs=out_specs,
        out_shape=out_shape,
        compiler_params=_cparams(("parallel",)),
        name="out_proj",
    )(*args)
    return res if nxt is not None else (res[0], None)


def _bucket_of_distance():
    max_exact = REL_BUCKETS // 2
    d = np.arange(REL_MAX_DIST)
    nf = np.maximum(d, max_exact).astype(np.float32)
    large = max_exact + (np.log(nf / np.float32(max_exact)) / np.float32(math.log(REL_MAX_DIST / max_exact))
                         * np.float32(REL_BUCKETS - max_exact)).astype(np.int32)
    return np.where(d < max_exact, d, np.minimum(large, REL_BUCKETS - 1))


def _distance_tables():
    b = np.arange(TILE)[:, None]
    a = np.arange(TILE)[None, :]
    none = REL_MAX_DIST
    d0 = np.where(a >= b, a - b, none)
    d1 = np.where(a < b, TILE + a - b, none)
    m = np.arange(24)[:, None]
    dc = a - CMP_STRIDE * m + (2 * TILE - (CMP_LEN - 1))
    dc = np.where((dc >= 0) & (dc < REL_MAX_DIST), dc, none)
    return d0, d1, dc


def _bias_tables(rel_bias):
    bucket = np.concatenate([_bucket_of_distance(), [REL_BUCKETS - 1]])
    rel = (rel_bias - rel_bias[REL_BUCKETS - 1:REL_BUCKETS, :]) * LOG2E
    rel = rel.reshape(REL_BUCKETS, NSA_G, NSA_R)

    def expand(idx):
        bk = bucket[idx]
        tab = jnp.zeros((NSA_G, idx.shape[0], NSA_R, TILE), F32)
        for b in range(REL_BUCKETS - 1):
            hit = jnp.asarray(bk == b)[None, :, None, :]
            tab = jnp.where(hit, rel[b][:, None, :, None], tab)
        return tab.reshape(NSA_G, idx.shape[0], NSA_R * TILE)

    d0, d1, dc = _distance_tables()
    return expand(d0), expand(d1), expand(dc)


def _cmp_to_sel_t(s, n_cmp_pad):
    n_cmp = (s - CMP_LEN) // CMP_STRIDE + 1
    n_sel = s // SEL_LEN
    c_lo = np.arange(n_cmp_pad)[None, :] * CMP_STRIDE
    sel_start = np.arange(n_sel)[:, None] * SEL_LEN
    overlap = (c_lo < sel_start + SEL_LEN) & (c_lo + CMP_LEN > sel_start)
    overlap &= np.arange(n_cmp_pad)[None, :] < n_cmp
    return jnp.asarray(overlap, dtype=BF16)


def _pack_w_in(w_in):
    sizes = ([D_CONV] * 4 + [D_FOX] * 3 + [H_FOX, D_FOX] + [D_NSA] + [2 * HD] * 6
             + [3 * H_NSA, D_NSA])
    offs = np.concatenate([[0], np.cumsum(sizes)])
    names = ["u", "gb", "gc", "za", "qf", "kf", "vf", "ff", "zf", "qn", "kc", "vc", "ks", "vs",
             "kw", "vw", "gn", "zn"]
    col = {n: w_in[:, :, int(offs[k]):int(offs[k + 1])] for k, n in enumerate(names)}
    scale = HD ** -0.5 * LOG2E
    w_a = jnp.concatenate([col["u"], col["gb"], col["gc"], col["za"],
                           col["qn"] * scale, col["zn"], col["ks"], col["kw"],
                           col["kc"], col["vc"],
                           col["qf"] * scale, col["kf"], col["zf"]], axis=-1).astype(BF16)
    w_t = jnp.concatenate([col["vf"], col["vs"], col["vw"]], axis=-1)
    w_t = jnp.swapaxes(w_t, 1, 2).astype(BF16)
    small = jnp.concatenate([col["ff"], col["gn"]], axis=-1)
    small = jnp.pad(small, ((0, 0), (0, 0), (0, TILE - small.shape[-1]))).astype(BF16)
    return w_a, w_t, small, jnp.swapaxes(small, 1, 2)


def kernel(x, c, w_ada, b_ada, pre_norm, post_norm, w_in, b_forget, conv_w,
           cmp_pe_k, cmp_w1_k, cmp_w2_k, cmp_pe_v, cmp_w1_v, cmp_w2_v, w_out, rel_bias):
    bsz, s, d = x.shape
    assert bsz == 1 and d == D_MODEL and s % FOX_T == 0 and s >= 2 * WINDOW
    depth = w_in.shape[0]
    x2 = x.reshape(s, d)
    n_cmp_pad = s // CMP_STRIDE

    mod = _ada_mod(jnp.broadcast_to(c, (8, d)), w_ada, b_ada)[:, 0:1, :]
    shift, scale, gate = mod[:, :, :d], mod[:, :, d:2 * d], mod[:, :, 2 * d:]

    w_a, w_t, w_small, w_small_t = _pack_w_in(w_in)
    w_out_b = w_out.astype(BF16)
    half = CMP_LEN * HD // 2
    w1cat = jnp.stack([jnp.concatenate([w[:, :half], w[:, half:]], axis=-1)
                       for w in (cmp_w1_k, cmp_w1_v)], axis=1).astype(BF16)
    pe = jnp.stack([cmp_pe_k, cmp_pe_v], axis=1).reshape(depth, 2, 2, 1, half)
    pe16 = jnp.broadcast_to(pe, (depth, 2, 2, 8, half)).reshape(depth, 2, 16, half).astype(BF16)
    w2 = jnp.stack([cmp_w2_k, cmp_w2_v], axis=1).astype(BF16)
    b_pad = jnp.pad(b_forget, ((0, 0), (0, TILE - H_FOX))).reshape(depth, 1, TILE)
    t0, t1, cstrip = _bias_tables(rel_bias)
    c2s_t = _cmp_to_sel_t(s, n_cmp_pad)

    h = _prenorm(x2, pre_norm[0:1], scale[0], shift[0])
    for l in range(depth):
        proj = _matmul(h, w_a, l, BF16, 1024, 768)
        proj_t = _matmul_t_tiled(h, w_t, l, 1024, 256)
        small = _matmul(h, w_small, l, F32, 1024, TILE)
        small_t = _matmul_t(h, w_small_t, l, 1024)

        ya = _conv_branch(proj, conv_w[l])

        qa, ka = _fox_prep(small, b_pad[l])
        yf = _fox_attention(proj, qa, ka, proj_t)

        kcvc = proj[:, COL_KCVC:COL_KCVC + 4 * HD]
        x4 = kcvc.reshape(n_cmp_pad, CMP_STRIDE, 4, HD).transpose(2, 0, 1, 3)
        x4 = x4.reshape(4, n_cmp_pad, CMP_STRIDE * HD)
        cmp_n, cmp_t = _compress(x4, w1cat, pe16, w2, l)
        glog_t = small_t[H_FOX:H_FOX + 3 * H_NSA].reshape(NSA_G, NSA_R * 3, s)
        yn = _nsa_attention(proj, proj_t, cmp_n, cmp_t, c2s_t, cstrip, t0, t1, glog_t)

        nxt = None
        if l + 1 < depth:
            nxt = (pre_norm[l + 1:l + 2], scale[l + 1], shift[l + 1])
        x2, h = _out_proj(ya, yf, yn, w_out_b, l, x2, post_norm[l:l + 1], gate[l], nxt)
    return x2.reshape(bsz, s, d)
```

```python
import functools
import math

import numpy as np
import jax
import jax.numpy as jnp
from jax import lax
from jax.experimental import pallas as pl
from jax.experimental.pallas import tpu as pltpu

F32 = jnp.float32
BF16 = jnp.bfloat16

D_MODEL = 2048
HD = 128
D_CONV = 512
D_FOX = 768
D_NSA = 768
H_FOX = 6
H_NSA = 6
NSA_G = 2
NSA_R = 3
CONV_WIDTH = 3
CMP_LEN = 32
CMP_STRIDE = 16
CMP_HIDDEN = 256
SEL_LEN = 64
SEL_TOPK = 16
N_FORCED = 3
WINDOW = 512
REL_BUCKETS = 32
REL_MAX_DIST = 128
NORM_EPS = 1e-6
NEG_BIG = -1e30
LOG2E = math.log2(math.e)

TILE = 128
ONES_ROWS = 16
NSA_CHUNK = 512
FOX_T = 512
FOX_TK = 512
FOX_HEADS = 2
VMEM_LIMIT = 56 * 1024 * 1024

NT_DIMS = (((1,), (1,)), ((), ()))

BLK_QN, BLK_ZN, BLK_KS, BLK_KW = 16, 22, 28, 30
BLK_QF, BLK_KF, BLK_ZF = 32, 38, 44
N_A = 6400


def _cparams(sem, vmem=VMEM_LIMIT):
    return pltpu.CompilerParams(dimension_semantics=sem, vmem_limit_bytes=vmem)


def _silu(v):
    return v * jax.nn.sigmoid(v)


def _mod_kernel(c_ref, w_ref, b_ref, o_ref):
    ca = _silu(c_ref[...])
    o_ref[0] = jnp.dot(ca, w_ref[0], precision=lax.Precision.HIGHEST,
                       preferred_element_type=F32) + b_ref[0]


def _ada_mod(c8, w_ada, b_ada):
    depth, d, n = w_ada.shape
    tn = 768
    return pl.pallas_call(
        _mod_kernel,
        grid=(depth, n // tn),
        in_specs=[pl.BlockSpec((8, d), lambda l, j: (0, 0)),
                  pl.BlockSpec((1, d, tn), lambda l, j: (l, 0, j)),
                  pl.BlockSpec((1, 1, tn), lambda l, j: (l, 0, j))],
        out_specs=pl.BlockSpec((1, 8, tn), lambda l, j: (l, 0, j)),
        out_shape=jax.ShapeDtypeStruct((depth, 8, n), F32),
        compiler_params=_cparams(("parallel", "parallel")),
        name="ada_mod",
    )(c8, w_ada, b_ada.reshape(depth, 1, n))


def _modulated_norm(x, g, scale, shift):
    y = x * lax.rsqrt(jnp.mean(x * x, axis=-1, keepdims=True) + NORM_EPS) * g
    return y * (1.0 + scale) + shift


def _prenorm_kernel(x_ref, g_ref, sc_ref, sh_ref, h_ref):
    h_ref[...] = _modulated_norm(x_ref[...], g_ref[...], sc_ref[...], sh_ref[...]).astype(BF16)


def _prenorm(x2, g, scale, shift):
    s, d = x2.shape
    tm = 512
    vec = pl.BlockSpec((1, d), lambda i: (0, 0))
    return pl.pallas_call(
        _prenorm_kernel,
        grid=(s // tm,),
        in_specs=[pl.BlockSpec((tm, d), lambda i: (i, 0)), vec, vec, vec],
        out_specs=pl.BlockSpec((tm, d), lambda i: (i, 0)),
        out_shape=jax.ShapeDtypeStruct((s, d), BF16),
        compiler_params=_cparams(("parallel",)),
        name="prenorm",
    )(x2, g, scale, shift)


def _mm_kernel(h_ref, w_ref, o_ref):
    o_ref[...] = jnp.dot(h_ref[...], w_ref[...], preferred_element_type=F32).astype(o_ref.dtype)


def _matmul(h, w, l, out_dtype, tm, tn):
    s, d = h.shape
    n = w.shape[2]
    return pl.pallas_call(
        _mm_kernel,
        grid=(s // tm, n // tn),
        in_specs=[pl.BlockSpec((tm, d), lambda i, j: (i, 0)),
                  pl.BlockSpec((None, d, tn), lambda i, j: (l, 0, j))],
        out_specs=pl.BlockSpec((tm, tn), lambda i, j: (i, j)),
        out_shape=jax.ShapeDtypeStruct((s, n), out_dtype),
        compiler_params=_cparams(("parallel", "arbitrary")),
        name="proj_rows",
    )(h, w)


def _mm_split_kernel(h_ref, w_ref, o_ref):
    res = jnp.dot(h_ref[...], w_ref[...], preferred_element_type=F32)
    for c in range(o_ref.shape[0]):
        o_ref[c] = res[:, c * TILE:(c + 1) * TILE].astype(o_ref.dtype)


def _matmul_split(h, w, l, tm):
    s, d = h.shape
    n = w.shape[2]
    return pl.pallas_call(
        _mm_split_kernel,
        grid=(s // tm,),
        in_specs=[pl.BlockSpec((tm, d), lambda i: (i, 0)),
                  pl.BlockSpec((None, d, n), lambda i: (l, 0, 0))],
        out_specs=pl.BlockSpec((n // TILE, tm, TILE), lambda i: (0, i, 0)),
        out_shape=jax.ShapeDtypeStruct((n // TILE, s, TILE), BF16),
        compiler_params=_cparams(("parallel",)),
        name="proj_rows_split",
    )(h, w)


def _mm_t_tiled_kernel(h_ref, wt_ref, o_ref, *, chunks):
    res = lax.dot_general(wt_ref[...], h_ref[...], NT_DIMS, preferred_element_type=F32)
    for c in range(chunks):
        o_ref[c] = res[:, c * TILE:(c + 1) * TILE].astype(o_ref.dtype)


def _matmul_t_tiled(h, wt, l, tm, tn):
    s, d = h.shape
    n = wt.shape[1]
    chunks = tm // TILE
    return pl.pallas_call(
        functools.partial(_mm_t_tiled_kernel, chunks=chunks),
        grid=(s // tm, n // tn),
        in_specs=[pl.BlockSpec((tm, d), lambda i, j: (i, 0)),
                  pl.BlockSpec((None, tn, d), lambda i, j: (l, j, 0))],
        out_specs=pl.BlockSpec((chunks, tn, TILE), lambda i, j: (i, j, 0)),
        out_shape=jax.ShapeDtypeStruct((s // TILE, n, TILE), BF16),
        compiler_params=_cparams(("parallel", "arbitrary")),
        name="proj_cols_tiled",
    )(h, wt)


def _fox_prep_kernel(h_ref, w_ref, wt_ref, b_ref, qa_ref, ka_ref, st_ref, carry_ref, *, tm):
    @pl.when(pl.program_id(0) == 0)
    def _():
        carry_ref[...] = jnp.zeros_like(carry_ref)

    h = h_ref[...]
    st_ref[...] = lax.dot_general(wt_ref[...], h, NT_DIMS, preferred_element_type=F32)
    v = jnp.dot(h, w_ref[...], preferred_element_type=F32) + b_ref[...]
    c = jnp.minimum(v, 0.0) - jnp.log1p(jnp.exp(-jnp.abs(v)))
    row = lax.broadcasted_iota(jnp.int32, (tm, TILE), 0)
    sh = 1
    while sh < tm:
        c = c + jnp.where(row >= sh, pltpu.roll(c, sh, axis=0), 0.0)
        sh *= 2
    c = c + carry_ref[...]
    carry_ref[...] = c[tm - 1:tm, :]
    c = c * LOG2E

    lane = lax.broadcasted_iota(jnp.int32, (tm, TILE), 1)
    for h in range(H_FOX):
        ch = jnp.broadcast_to(c[:, h:h + 1], (tm, TILE))
        hi = ch.astype(BF16).astype(F32)
        r1 = ch - hi
        mid = r1.astype(BF16).astype(F32)
        lo = r1 - mid
        pieces = jnp.where((lane == 0) | (lane == 3), hi,
                           jnp.where((lane == 1) | (lane == 4), mid, lo))
        qa = jnp.where(lane < 3, pieces, jnp.where(lane < 6, 1.0, 0.0))
        ka = jnp.where(lane < 3, 1.0, jnp.where(lane < 6, -pieces, 0.0))
        qa_ref[:, h * TILE:(h + 1) * TILE] = qa.astype(BF16)
        ka_ref[:, h * TILE:(h + 1) * TILE] = ka.astype(BF16)


def _fox_prep(h, w_small, w_small_t, l, b_pad):
    s, d = h.shape
    tm = 512
    out = jax.ShapeDtypeStruct((s, H_FOX * TILE), BF16)
    return pl.pallas_call(
        functools.partial(_fox_prep_kernel, tm=tm),
        grid=(s // tm,),
        in_specs=[pl.BlockSpec((tm, d), lambda i: (i, 0)),
                  pl.BlockSpec((None, d, TILE), lambda i: (l, 0, 0)),
                  pl.BlockSpec((None, TILE, d), lambda i: (l, 0, 0)),
                  pl.BlockSpec((1, TILE), lambda i: (0, 0))],
        out_specs=[pl.BlockSpec((tm, H_FOX * TILE), lambda i: (i, 0)),
                   pl.BlockSpec((tm, H_FOX * TILE), lambda i: (i, 0)),
                   pl.BlockSpec((TILE, tm), lambda i: (0, i))],
        out_shape=[out, out, jax.ShapeDtypeStruct((TILE, s), F32)],
        scratch_shapes=[pltpu.VMEM((1, TILE), F32)],
        compiler_params=_cparams(("arbitrary",)),
        name="fox_prep",
    )(h, w_small, w_small_t, b_pad)


def _fox_kernel(q_ref, qa_ref, k_ref, ka_ref, vt_ref, z_ref, o_ref, s0_ref, s1_ref, s2_ref,
                acc_ref):
    i = pl.program_id(1)
    t = FOX_T
    tk = FOX_TK
    heads = range(FOX_HEADS)
    col = lambda h: slice(h * TILE, (h + 1) * TILE)
    qaug = [jnp.concatenate([q_ref[:, col(h)], qa_ref[:, col(h)]], axis=1) for h in heads]

    def scores(h, j, rows):
        r0 = pl.multiple_of(j * rows, rows)
        kaug = jnp.concatenate([k_ref[pl.ds(r0, rows), col(h)], ka_ref[pl.ds(r0, rows), col(h)]],
                               axis=1)
        return lax.dot_general(kaug, qaug[h], NT_DIMS, preferred_element_type=F32)

    def values_t(h, j, rows):
        n_sub = rows // TILE
        return jnp.concatenate(
            [jnp.concatenate([vt_ref[n_sub * j + c, col(h), :] for c in range(n_sub)], axis=1),
             jnp.ones((ONES_ROWS, rows), BF16)], axis=0)

    kio = lax.broadcasted_iota(jnp.int32, (t, t), 0)
    qio = lax.broadcasted_iota(jnp.int32, (t, t), 1)
    ms = []
    for h in heads:
        s = jnp.where(kio <= qio, scores(h, i, t), NEG_BIG)
        m = jnp.max(s, axis=0, keepdims=True)
        p = jnp.exp2(s - m)
        ms.append(m)
        acc_ref[h] = jnp.dot(values_t(h, i, t), p.astype(BF16), preferred_element_type=F32)

    def update(h, s, vt, m):
        m_new = jnp.maximum(m, jnp.max(s, axis=0, keepdims=True))
        alpha = jnp.exp2(m - m_new)
        p = jnp.exp2(s - m_new)
        acc_ref[h] = alpha * acc_ref[h] + jnp.dot(vt, p.astype(BF16), preferred_element_type=F32)
        return m_new

    bufs = (s0_ref, s1_ref, s2_ref)
    for h in heads:
        s0_ref[h] = scores(h, 0, tk)

    def triple(pp, ms):
        ms = list(ms)
        for stage in range(3):
            cur, nxt = bufs[stage], bufs[(stage + 1) % 3]
            for h in heads:
                nxt[h] = scores(h, 3 * pp + stage + 1, tk)
            for h in heads:
                ms[h] = update(h, cur[h], values_t(h, 3 * pp + stage, tk), ms[h])
        return tuple(ms)

    n_far = i * (t // tk)
    ms = lax.fori_loop(0, n_far // 3, triple, tuple(ms))
    done = (n_far // 3) * 3

    @pl.when(n_far - done == 1)
    def _():
        for h in heads:
            update(h, s0_ref[h], values_t(h, done, tk), ms[h])

    @pl.when(n_far - done == 2)
    def _():
        for h in heads:
            s1_ref[h] = scores(h, done + 1, tk)
        for h in heads:
            m_mid = update(h, s0_ref[h], values_t(h, done, tk), ms[h])
            update(h, s1_ref[h], values_t(h, done + 1, tk), m_mid)

    for h in heads:
        o = (acc_ref[h, 0:HD, :] * (1.0 / acc_ref[h, HD:HD + 1, :])).T
        o_ref[:, col(h)] = (o * _silu(z_ref[:, col(h)].astype(F32))).astype(BF16)


def _fox_attention(proj, qa, ka, proj_t):
    s = proj.shape[0]
    t = FOX_T
    nk = s // TILE
    hw = FOX_HEADS * TILE
    score_buf = pltpu.VMEM((FOX_HEADS, FOX_TK, t), F32)
    return pl.pallas_call(
        _fox_kernel,
        grid=(H_FOX // FOX_HEADS, s // t),
        scratch_shapes=[score_buf, score_buf, score_buf,
                        pltpu.VMEM((FOX_HEADS, HD + ONES_ROWS, t), F32)],
        in_specs=[pl.BlockSpec((t, hw), lambda h, i: (i, BLK_QF // FOX_HEADS + h)),
                  pl.BlockSpec((t, hw), lambda h, i: (i, h)),
                  pl.BlockSpec((s, hw), lambda h, i: (0, BLK_KF // FOX_HEADS + h)),
                  pl.BlockSpec((s, hw), lambda h, i: (0, h)),
                  pl.BlockSpec((nk, hw, TILE), lambda h, i: (0, h, 0)),
                  pl.BlockSpec((t, hw), lambda h, i: (i, BLK_ZF // FOX_HEADS + h))],
        out_specs=pl.BlockSpec((t, hw), lambda h, i: (i, h)),
        out_shape=jax.ShapeDtypeStruct((s, D_FOX), BF16),
        compiler_params=_cparams(("parallel", "parallel")),
        name="fox_attention",
    )(proj, qa, proj, ka, proj_t, proj)


def _compress_kernel(x_ref, w1_ref, pe_ref, w2_ref, o_ref, ot_ref):
    n = x_ref.shape[0]
    ab = jnp.dot(x_ref[...], w1_ref[0], preferred_element_type=F32)
    pe = jnp.dot(pe_ref[0], w1_ref[0], preferred_element_type=F32)
    a = ab[:, :CMP_HIDDEN]
    b_next = pltpu.roll(ab[:, CMP_HIDDEN:], n - 1, axis=0)
    hid = a + b_next + pe[0:1, :CMP_HIDDEN] + pe[8:9, CMP_HIDDEN:]
    out = jnp.dot(_silu(hid).astype(BF16), w2_ref[0], preferred_element_type=F32)
    o_ref[...] = out.astype(BF16)
    ot_ref[...] = out.T.astype(BF16)


def _compress(x4, w1cat, pe16, w2, l):
    _, n, k = x4.shape
    kind = lambda c: (l, c // NSA_G, 0, 0)
    return pl.pallas_call(
        _compress_kernel,
        grid=(2 * NSA_G,),
        in_specs=[pl.BlockSpec((None, n, k), lambda c: (c, 0, 0)),
                  pl.BlockSpec((None, 1, k, 2 * CMP_HIDDEN), kind),
                  pl.BlockSpec((None, 1, 16, k), kind),
                  pl.BlockSpec((None, 1, CMP_HIDDEN, HD), kind)],
        out_specs=[pl.BlockSpec((None, n, HD), lambda c: (c, 0, 0)),
                   pl.BlockSpec((None, HD, n), lambda c: (c, 0, 0))],
        out_shape=[jax.ShapeDtypeStruct((2 * NSA_G, n, HD), BF16),
                   jax.ShapeDtypeStruct((2 * NSA_G, HD, n), BF16)],
        compiler_params=_cparams(("parallel",)),
        name="nsa_compress",
    )(x4, w1cat, pe16, w2)


def _nsa_kernel(*refs, n_cmp):
    q_refs, z_refs = refs[0:H_NSA], refs[H_NSA:2 * H_NSA]
    (ks_ref, kw_ref, vst_ref, vwt_ref, kc_ref, vct_ref, c2s_ref, cstrip_ref, t0_ref, t1_ref,
     gl_ref, o_ref, sc_ref, sel_ref, m_s, acc_s, acc_w, ss_ref, sw_ref, sa_ref, sb_ref,
     sc3_ref) = refs[2 * H_NSA:]
    i = pl.program_id(0)
    w3 = NSA_R * TILE
    gw = NSA_G * TILE
    groups = range(NSA_G)
    col = lambda g: slice(g * TILE, (g + 1) * TILE)
    qs = [jnp.concatenate([q_refs[NSA_R * g + r][...] for r in range(NSA_R)], axis=0)
          for g in groups]
    b_io = lax.broadcasted_iota(jnp.int32, (TILE, w3), 0)
    a_io = lax.broadcasted_iota(jnp.int32, (TILE, w3), 1) & (TILE - 1)
    tpc = NSA_CHUNK // TILE
    bpc = NSA_CHUNK // SEL_LEN

    def key_scores(k_ref, g, tile0, n_tiles):
        r0 = pl.multiple_of(tile0 * TILE, TILE)
        return lax.dot_general(k_ref[pl.ds(r0, n_tiles * TILE), col(g)], qs[g], NT_DIMS,
                               preferred_element_type=F32)

    def values_t(vt_ref, g, tile0, n_tiles):
        return jnp.concatenate(
            [jnp.concatenate([vt_ref[tile0 + c, col(g), :] for c in range(n_tiles)], axis=1),
             jnp.ones((ONES_ROWS, n_tiles * TILE), BF16)], axis=0)

    def fix_near(s_ref, g, d):
        own = pl.ds(pl.multiple_of(d * TILE, TILE), TILE)
        prev = pl.ds(pl.multiple_of(jnp.maximum(d - 1, 0) * TILE, TILE), TILE)
        s_ref[g, prev, :] = s_ref[g, prev, :] + jnp.where(d >= 1, t1_ref[g], 0.0)
        s_ref[g, own, :] = jnp.where(b_io <= a_io, s_ref[g, own, :] + t0_ref[g], NEG_BIG)

    w0 = pl.multiple_of(8 * i, 8)
    n_io = lax.broadcasted_iota(jnp.int32, (n_cmp, w3), 0)
    qa_io = lax.broadcasted_iota(jnp.int32, (n_cmp, w3), 1) & (TILE - 1)
    valid = (CMP_STRIDE * n_io + (CMP_LEN - 1)) <= (TILE * i + qa_io)
    n_wt = WINDOW // TILE + 1
    w_tile0 = jnp.maximum(i - WINDOW // TILE, 0)
    d_win = i - w_tile0
    for g in groups:
        sc_ref[g, 0:16, :] = jnp.zeros((16, w3), F32)
        sc_ref[g, 16:16 + n_cmp, :] = lax.dot_general(kc_ref[g], qs[g], NT_DIMS,
                                                      preferred_element_type=F32)
        sw_ref[g] = key_scores(kw_ref, g, w_tile0, n_wt)
    oc_t, psums = [], []
    for g in groups:
        sc_ref[g, pl.ds(w0, 24), :] = sc_ref[g, pl.ds(w0, 24), :] + cstrip_ref[g]
        scm = jnp.where(valid, sc_ref[g, 16:16 + n_cmp, :], NEG_BIG)
        mc = jnp.max(scm, axis=0, keepdims=True)
        pc = jnp.where(valid, jnp.exp2(scm - mc), 0.0)
        lc = jnp.sum(pc, axis=0, keepdims=True)
        pcn = pc * jnp.where(lc > 0.0, 1.0 / lc, 0.0)
        oc_t.append(jnp.dot(vct_ref[g], pcn.astype(BF16), preferred_element_type=F32))
        psums.append(pcn[:, 0:TILE] + pcn[:, TILE:2 * TILE] + pcn[:, 2 * TILE:3 * TILE])

        for c in range(1, n_wt):
            rows = slice(c * TILE, (c + 1) * TILE)
            sw_ref[g, rows, :] = jnp.where(d_win < c, NEG_BIG, sw_ref[g, rows, :])
        fix_near(sw_ref, g, d_win)
        sw_ref[g, 0:TILE, :] = jnp.where((a_io < b_io) | (i < WINDOW // TILE),
                                         sw_ref[g, 0:TILE, :], NEG_BIG)
        sw = sw_ref[g]
        p_w = jnp.exp2(sw - jnp.max(sw, axis=0, keepdims=True))
        acc_w[g] = jnp.dot(values_t(vwt_ref, g, w_tile0, n_wt), p_w.astype(BF16),
                           preferred_element_type=F32)

    psum = jnp.concatenate(psums, axis=1)
    p_hi = psum.astype(BF16)
    p_lo = (psum - p_hi.astype(F32)).astype(BF16)
    c2s = c2s_ref[...]
    imp = (jnp.dot(c2s, p_hi, preferred_element_type=F32)
           + jnp.dot(c2s, p_lo, preferred_element_type=F32))
    n_sel = imp.shape[0]
    m_io = lax.broadcasted_iota(jnp.int32, (n_sel, gw), 0)
    m_f = m_io.astype(F32)
    qpos = TILE * i + (lax.broadcasted_iota(jnp.int32, (n_sel, gw), 1) & (TILE - 1))
    own = qpos >> 6
    forced = (m_io == 0) | (m_io == own) | (m_io == own - 1)
    eligible = SEL_LEN * m_io <= qpos
    sel = jnp.where(eligible & forced, 1.0, 0.0)
    cur = jnp.where(eligible, jnp.where(forced, -2.0, imp), -1.0)
    for _ in range(min(SEL_TOPK, n_sel) - N_FORCED):
        mx = jnp.max(cur, axis=0, keepdims=True)
        idx = jnp.min(jnp.where(cur == mx, m_f, float(n_sel)), axis=0, keepdims=True)
        pick = m_f == idx
        sel = jnp.where(pick, jnp.maximum(jnp.where(mx >= 0.0, 1.0, 0.0), sel), sel)
        cur = jnp.where(pick, -2.0, cur)
    sel_ref[...] = sel

    def sel_mask(g, c):
        rows = sel_ref[pl.ds(pl.multiple_of(c * bpc, 8), bpc), col(g)]
        mk = jnp.concatenate([jnp.broadcast_to(rows[b:b + 1, :], (SEL_LEN, TILE))
                              for b in range(bpc)], axis=0)
        return jnp.concatenate([mk, mk, mk], axis=1) > 0.5

    def sel_update(g, c, s):
        s = jnp.where(sel_mask(g, c), s, NEG_BIG)
        m_old = m_s[g]
        m_new = jnp.maximum(m_old, jnp.max(s, axis=0, keepdims=True))
        alpha = jnp.exp2(m_old - m_new)
        p = jnp.exp2(s - m_new)
        acc_s[g] = alpha * acc_s[g] + jnp.dot(values_t(vst_ref, g, c * tpc, tpc), p.astype(BF16),
                                              preferred_element_type=F32)
        m_s[g] = m_new

    m_s[...] = jnp.full(m_s.shape, NEG_BIG, F32)
    acc_s[...] = jnp.zeros(acc_s.shape, F32)

    c_own = i // tpc
    c_prev = jnp.maximum(i - 1, 0) // tpc

    bufs = (sa_ref, sb_ref, sc3_ref)
    for g in groups:
        sa_ref[g] = key_scores(ks_ref, g, 0, tpc)

    def far_triple(pp, carry):
        for stage in range(3):
            cur_buf, nxt_buf = bufs[stage], bufs[(stage + 1) % 3]
            for g in groups:
                nxt_buf[g] = key_scores(ks_ref, g, (3 * pp + stage + 1) * tpc, tpc)
            for g in groups:
                sel_update(g, 3 * pp + stage, cur_buf[g])
        return carry

    lax.fori_loop(0, c_prev // 3, far_triple, 0)
    done = (c_prev // 3) * 3

    @pl.when(c_prev - done == 1)
    def _():
        for g in groups:
            sel_update(g, done, sa_ref[g])

    @pl.when(c_prev - done == 2)
    def _():
        for g in groups:
            sb_ref[g] = key_scores(ks_ref, g, (done + 1) * tpc, tpc)
        for g in groups:
            sel_update(g, done, sa_ref[g])
            sel_update(g, done + 1, sb_ref[g])

    @pl.when(c_prev != c_own)
    def _():
        last = slice((tpc - 1) * TILE, tpc * TILE)
        for g in groups:
            ss_ref[g] = key_scores(ks_ref, g, c_prev * tpc, tpc)
            ss_ref[g, last, :] = ss_ref[g, last, :] + t1_ref[g]
            sel_update(g, c_prev, ss_ref[g])

    for g in groups:
        ss_ref[g] = key_scores(ks_ref, g, c_own * tpc, tpc)
    for g in groups:
        fix_near(ss_ref, g, i - c_own * tpc)
        sel_update(g, c_own, ss_ref[g])

    for g in groups:
        gates = jax.nn.sigmoid(gl_ref[g])

        def gate_row(branch):
            return jnp.concatenate([gates[r * 3 + branch:r * 3 + branch + 1, :]
                                    for r in range(NSA_R)], axis=1)

        out_t = (oc_t[g] * gate_row(0)
                 + acc_s[g, 0:HD, :] * (gate_row(1) / acc_s[g, HD:HD + 1, :])
                 + acc_w[g, 0:HD, :] * (gate_row(2) / acc_w[g, HD:HD + 1, :]))
        for r in range(NSA_R):
            h = NSA_R * g + r
            o_r = out_t[:, r * TILE:(r + 1) * TILE].T
            o_ref[:, h * TILE:(h + 1) * TILE] = (
                o_r * _silu(z_refs[h][...].astype(F32))).astype(BF16)


def _nsa_attention(proj, proj_t, cmp_n, cmp_t, c2s_t, cstrip, t0, t1, glog_t):
    s = proj.shape[0]
    nk = s // TILE
    n_cmp = cmp_n.shape[1]
    n_sel = s // SEL_LEN
    w3 = NSA_R * TILE
    gw = NSA_G * TILE
    qspec = lambda h: pl.BlockSpec((TILE, TILE), lambda i, h=h: (i, BLK_QN + h))
    zspec = lambda h: pl.BlockSpec((TILE, TILE), lambda i, h=h: (i, BLK_ZN + h))
    whole = lambda shape: pl.BlockSpec(shape, lambda i: (0,) * len(shape))
    resident = dict(pipeline_mode=pl.Buffered(1))
    chunk_buf = pltpu.VMEM((NSA_G, NSA_CHUNK, w3), F32)
    acc_buf = pltpu.VMEM((NSA_G, HD + ONES_ROWS, w3), F32)
    return pl.pallas_call(
        functools.partial(_nsa_kernel, n_cmp=n_cmp),
        grid=(s // TILE,),
        in_specs=[qspec(h) for h in range(H_NSA)] + [zspec(h) for h in range(H_NSA)] + [
            pl.BlockSpec((s, gw), lambda i: (0, BLK_KS // NSA_G), **resident),
            pl.BlockSpec((s, gw), lambda i: (0, BLK_KW // NSA_G), **resident),
            pl.BlockSpec((nk, gw, TILE), lambda i: (0, H_FOX // NSA_G, 0), **resident),
            pl.BlockSpec((nk, gw, TILE), lambda i: (0, H_FOX // NSA_G + 1, 0), **resident),
            pl.BlockSpec((NSA_G, n_cmp, HD), lambda i: (0, 0, 0)),
            pl.BlockSpec((NSA_G, HD, n_cmp), lambda i: (1, 0, 0)),
            whole((n_sel, n_cmp)), whole((NSA_G, 24, w3)), whole((NSA_G, TILE, w3)),
            whole((NSA_G, TILE, w3)),
            pl.BlockSpec((NSA_G, NSA_R * 3, TILE), lambda i: (0, 0, i))],
        out_specs=pl.BlockSpec((TILE, D_NSA), lambda i: (i, 0)),
        out_shape=jax.ShapeDtypeStruct((s, D_NSA), BF16),
        scratch_shapes=[pltpu.VMEM((NSA_G, 16 + n_cmp + 8, w3), F32),
                        pltpu.VMEM((n_sel, gw), F32),
                        pltpu.VMEM((NSA_G, 1, w3), F32),
                        acc_buf, acc_buf,
                        chunk_buf,
                        pltpu.VMEM((NSA_G, WINDOW + TILE, w3), F32),
                        chunk_buf, chunk_buf, chunk_buf],
        compiler_params=_cparams(("parallel",)),
        name="nsa_attention",
    )(*([proj] * (2 * H_NSA)), proj, proj, proj_t, proj_t,
      cmp_n, cmp_t, c2s_t, cstrip, t0, t1, glog_t)


def _conv_kernel(u_ref, gb_ref, gc_ref, z_ref, uh_ref, gch_ref, w_ref, o_ref, *, tm):
    i = pl.program_id(0)
    y = gc_ref[...].astype(F32) * u_ref[...].astype(F32)
    yh = gch_ref[...].astype(F32) * uh_ref[...].astype(F32)
    yh = jnp.where(i > 0, yh, 0.0)
    h1 = yh[15:16, :]
    h2 = yh[14:15, :]
    row = lax.broadcasted_iota(jnp.int32, (tm, D_CONV), 0)
    y1 = jnp.where(row == 0, h1, pltpu.roll(y, 1, axis=0))
    y2 = jnp.where(row == 0, h2, jnp.where(row == 1, h1, pltpu.roll(y, 2, axis=0)))
    w = w_ref[...]
    conv = w[0:1, :] * y2 + w[1:2, :] * y1 + w[2:3, :] * y
    o_ref[...] = (gb_ref[...].astype(F32) * conv * _silu(z_ref[...].astype(F32))).astype(BF16)


def _conv_branch(proj, conv_w):
    s = proj.shape[0]
    tm = 512
    main = lambda c: pl.BlockSpec((tm, D_CONV), lambda i, c=c: (i, c))
    halo = lambda c: pl.BlockSpec((16, D_CONV), lambda i, c=c: (jnp.maximum(i * (tm // 16) - 1, 0), c))
    return pl.pallas_call(
        functools.partial(_conv_kernel, tm=tm),
        grid=(s // tm,),
        in_specs=[main(0), main(1), main(2), main(3), halo(0), halo(2),
                  pl.BlockSpec((CONV_WIDTH, D_CONV), lambda i: (0, 0))],
        out_specs=pl.BlockSpec((tm, D_CONV), lambda i: (i, 0)),
        out_shape=jax.ShapeDtypeStruct((s, D_CONV), BF16),
        compiler_params=_cparams(("parallel",)),
        name="conv_branch",
    )(proj, proj, proj, proj, proj, proj, conv_w)


def _out_kernel(ya_ref, yf_ref, yn_ref, w_ref, x_ref, pg_ref, gate_ref, *rest, with_next):
    y = (jnp.dot(ya_ref[...], w_ref[0:D_CONV, :], preferred_element_type=F32)
         + jnp.dot(yf_ref[...], w_ref[D_CONV:D_CONV + D_FOX, :], preferred_element_type=F32)
         + jnp.dot(yn_ref[...], w_ref[D_CONV + D_FOX:, :], preferred_element_type=F32))
    yn = y * lax.rsqrt(jnp.mean(y * y, axis=-1, keepdims=True) + NORM_EPS) * pg_ref[...]
    xn = x_ref[...] + gate_ref[...] * yn
    if with_next:
        ng_ref, nsc_ref, nsh_ref, xo_ref, h_ref = rest
        h_ref[...] = _modulated_norm(xn, ng_ref[...], nsc_ref[...], nsh_ref[...]).astype(BF16)
    else:
        (xo_ref,) = rest
    xo_ref[...] = xn


def _out_proj(ya, yf, yn, w_out, l, x2, post_g, gate, nxt):
    s, d = x2.shape
    tm = 512
    rows = lambda n: pl.BlockSpec((tm, n), lambda i: (i, 0))
    vec = pl.BlockSpec((1, d), lambda i: (0, 0))
    in_specs = [rows(D_CONV), rows(D_FOX), rows(D_NSA),
                pl.BlockSpec((None, d, d), lambda i: (l, 0, 0), pipeline_mode=pl.Buffered(1)),
                rows(d), vec, vec]
    args = [ya, yf, yn, w_out, x2, post_g, gate]
    out_specs = [rows(d)]
    out_shape = [jax.ShapeDtypeStruct((s, d), F32)]
    if nxt is not None:
        in_specs += [vec, vec, vec]
        args += list(nxt)
        out_specs.append(rows(d))
        out_shape.append(jax.ShapeDtypeStruct((s, d), BF16))
    res = pl.pallas_call(
        functools.partial(_out_kernel, with_next=nxt is not None),
        grid=(s // tm,),
        in_specs=in_specs,
        out_specs=out_specs,
        out_shape=out_shape,
        compiler_params=_cparams(("parallel",)),
        name="out_proj",
    )(*args)
    return res if nxt is not None else (res[0], None)


def _bucket_of_distance():
    max_exact = REL_BUCKETS // 2
    d = np.arange(REL_MAX_DIST)
    nf = np.maximum(d, max_exact).astype(np.float32)
    large = max_exact + (np.log(nf / np.float32(max_exact)) / np.float32(math.log(REL_MAX_DIST / max_exact))
                         * np.float32(REL_BUCKETS - max_exact)).astype(np.int32)
    return np.where(d < max_exact, d, np.minimum(large, REL_BUCKETS - 1))


def _distance_tables():
    b = np.arange(TILE)[:, None]
    a = np.arange(TILE)[None, :]
    none = REL_MAX_DIST
    d0 = np.where(a >= b, a - b, none)
    d1 = np.where(a < b, TILE + a - b, none)
    m = np.arange(24)[:, None]
    dc = a - CMP_STRIDE * m + (2 * TILE - (CMP_LEN - 1))
    dc = np.where((dc >= 0) & (dc < REL_MAX_DIST), dc, none)
    return d0, d1, dc


def _bias_tables(rel_bias):
    bucket = np.concatenate([_bucket_of_distance(), [REL_BUCKETS - 1]])
    rel = (rel_bias - rel_bias[REL_BUCKETS - 1:REL_BUCKETS, :]) * LOG2E
    rel = rel.reshape(REL_BUCKETS, NSA_G, NSA_R)

    def expand(idx):
        bk = bucket[idx]
        tab = jnp.zeros((NSA_G, idx.shape[0], NSA_R, TILE), F32)
        for b in range(REL_BUCKETS - 1):
            hit = jnp.asarray(bk == b)[None, :, None, :]
            tab = jnp.where(hit, rel[b][:, None, :, None], tab)
        return tab.reshape(NSA_G, idx.shape[0], NSA_R * TILE)

    d0, d1, dc = _distance_tables()
    return expand(d0), expand(d1), expand(dc)


def _cmp_to_sel_t(s, n_cmp_pad):
    n_cmp = (s - CMP_LEN) // CMP_STRIDE + 1
    n_sel = s // SEL_LEN
    c_lo = np.arange(n_cmp_pad)[None, :] * CMP_STRIDE
    sel_start = np.arange(n_sel)[:, None] * SEL_LEN
    overlap = (c_lo < sel_start + SEL_LEN) & (c_lo + CMP_LEN > sel_start)
    overlap &= np.arange(n_cmp_pad)[None, :] < n_cmp
    return jnp.asarray(overlap, dtype=BF16)


def _pack_w_in(w_in):
    sizes = ([D_CONV] * 4 + [D_FOX] * 3 + [H_FOX, D_FOX] + [D_NSA] + [2 * HD] * 6
             + [3 * H_NSA, D_NSA])
    offs = np.concatenate([[0], np.cumsum(sizes)])
    names = ["u", "gb", "gc", "za", "qf", "kf", "vf", "ff", "zf", "qn", "kc", "vc", "ks", "vs",
             "kw", "vw", "gn", "zn"]
    col = {n: w_in[:, :, int(offs[k]):int(offs[k + 1])] for k, n in enumerate(names)}
    scale = HD ** -0.5 * LOG2E
    w_a = jnp.concatenate([col["u"], col["gb"], col["gc"], col["za"],
                           col["qn"] * scale, col["zn"], col["ks"], col["kw"],
                           col["qf"] * scale, col["kf"], col["zf"]], axis=-1).astype(BF16)
    w_kv = jnp.concatenate([col["kc"], col["vc"]], axis=-1).astype(BF16)
    w_t = jnp.concatenate([col["vf"], col["vs"], col["vw"]], axis=-1)
    w_t = jnp.swapaxes(w_t, 1, 2).astype(BF16)
    small = jnp.concatenate([col["ff"], col["gn"]], axis=-1)
    small = jnp.pad(small, ((0, 0), (0, 0), (0, TILE - small.shape[-1]))).astype(BF16)
    return w_a, w_kv, w_t, small, jnp.swapaxes(small, 1, 2)


def kernel(x, c, w_ada, b_ada, pre_norm, post_norm, w_in, b_forget, conv_w,
           cmp_pe_k, cmp_w1_k, cmp_w2_k, cmp_pe_v, cmp_w1_v, cmp_w2_v, w_out, rel_bias):
    bsz, s, d = x.shape
    assert bsz == 1 and d == D_MODEL and s % 1024 == 0 and s >= 2 * WINDOW
    depth = w_in.shape[0]
    x2 = x.reshape(s, d)
    n_cmp_pad = s // CMP_STRIDE

    mod = _ada_mod(jnp.broadcast_to(c, (8, d)), w_ada, b_ada)[:, 0:1, :]
    shift, scale, gate = mod[:, :, :d], mod[:, :, d:2 * d], mod[:, :, 2 * d:]

    w_a, w_kv, w_t, w_small, w_small_t = _pack_w_in(w_in)
    w_out_b = w_out.astype(BF16)
    half = CMP_LEN * HD // 2
    w1cat = jnp.stack([jnp.concatenate([w[:, :half], w[:, half:]], axis=-1)
                       for w in (cmp_w1_k, cmp_w1_v)], axis=1).astype(BF16)
    pe = jnp.stack([cmp_pe_k, cmp_pe_v], axis=1).reshape(depth, 2, 2, 1, half)
    pe16 = jnp.broadcast_to(pe, (depth, 2, 2, 8, half)).reshape(depth, 2, 16, half).astype(BF16)
    w2 = jnp.stack([cmp_w2_k, cmp_w2_v], axis=1).astype(BF16)
    b_pad = jnp.pad(b_forget, ((0, 0), (0, TILE - H_FOX))).reshape(depth, 1, TILE)
    t0, t1, cstrip = _bias_tables(rel_bias)
    c2s_t = _cmp_to_sel_t(s, n_cmp_pad)

    h = _prenorm(x2, pre_norm[0:1], scale[0], shift[0])
    for l in range(depth):
        proj = _matmul(h, w_a, l, BF16, 1024, 640)
        kcvc = _matmul_split(h, w_kv, l, 1024)
        proj_t = _matmul_t_tiled(h, w_t, l, 1024, 256)

        ya = _conv_branch(proj, conv_w[l])

        qa, ka, small_t = _fox_prep(h, w_small, w_small_t, l, b_pad[l])
        yf = _fox_attention(proj, qa, ka, proj_t)

        x4 = kcvc.reshape(2 * NSA_G, n_cmp_pad, CMP_STRIDE * HD)
        cmp_n, cmp_t = _compress(x4, w1cat, pe16, w2, l)
        glog_t = small_t[H_FOX:H_FOX + 3 * H_NSA].reshape(NSA_G, NSA_R * 3, s)
        yn = _nsa_attention(proj, proj_t, cmp_n, cmp_t, c2s_t, cstrip, t0, t1, glog_t)

        nxt = None
        if l + 1 < depth:
            nxt = (pre_norm[l + 1:l + 2], scale[l + 1], shift[l + 1])
        x2, h = _out_proj(ya, yf, yn, w_out_b, l, x2, post_norm[l:l + 1], gate[l], nxt)
    return x2.reshape(bsz, s, d)
```

```python
import functools
import math

import numpy as np
import jax
import jax.numpy as jnp
from jax import lax
from jax.experimental import pallas as pl
from jax.experimental.pallas import tpu as pltpu

F32 = jnp.float32
BF16 = jnp.bfloat16

D_MODEL = 2048
HD = 128
D_CONV = 512
D_FOX = 768
D_NSA = 768
H_FOX = 6
H_NSA = 6
NSA_G = 2
NSA_R = 3
CONV_WIDTH = 3
CMP_LEN = 32
CMP_STRIDE = 16
CMP_HIDDEN = 256
SEL_LEN = 64
SEL_TOPK = 16
N_FORCED = 3
WINDOW = 512
REL_BUCKETS = 32
REL_MAX_DIST = 128
NORM_EPS = 1e-6
NEG_BIG = -1e30
LOG2E = math.log2(math.e)

TILE = 128
ONES_ROWS = 16
NSA_CHUNK = 512
FOX_T = 512
FOX_TK = 512
FOX_HEADS = 2
VMEM_LIMIT = 56 * 1024 * 1024

NT_DIMS = (((1,), (1,)), ((), ()))

BLK_QN, BLK_ZN, BLK_KS, BLK_KW = 16, 22, 28, 30
BLK_QF, BLK_KF, BLK_ZF = 32, 38, 44
N_A = 6400


def _cparams(sem, vmem=VMEM_LIMIT):
    return pltpu.CompilerParams(dimension_semantics=sem, vmem_limit_bytes=vmem)


def _silu(v):
    return v * jax.nn.sigmoid(v)


def _mod_kernel(c_ref, w_ref, b_ref, o_ref):
    ca = _silu(c_ref[...])
    o_ref[0] = jnp.dot(ca, w_ref[0], precision=lax.Precision.HIGHEST,
                       preferred_element_type=F32) + b_ref[0]


def _ada_mod(c8, w_ada, b_ada):
    depth, d, n = w_ada.shape
    tn = 768
    return pl.pallas_call(
        _mod_kernel,
        grid=(depth, n // tn),
        in_specs=[pl.BlockSpec((8, d), lambda l, j: (0, 0)),
                  pl.BlockSpec((1, d, tn), lambda l, j: (l, 0, j)),
                  pl.BlockSpec((1, 1, tn), lambda l, j: (l, 0, j))],
        out_specs=pl.BlockSpec((1, 8, tn), lambda l, j: (l, 0, j)),
        out_shape=jax.ShapeDtypeStruct((depth, 8, n), F32),
        compiler_params=_cparams(("parallel", "parallel")),
        name="ada_mod",
    )(c8, w_ada, b_ada.reshape(depth, 1, n))


def _modulated_norm(x, g, scale, shift):
    y = x * lax.rsqrt(jnp.mean(x * x, axis=-1, keepdims=True) + NORM_EPS) * g
    return y * (1.0 + scale) + shift


def _prenorm_kernel(x_ref, g_ref, sc_ref, sh_ref, h_ref):
    h_ref[...] = _modulated_norm(x_ref[...], g_ref[...], sc_ref[...], sh_ref[...]).astype(BF16)


def _prenorm(x2, g, scale, shift):
    s, d = x2.shape
    tm = 512
    vec = pl.BlockSpec((1, d), lambda i: (0, 0))
    return pl.pallas_call(
        _prenorm_kernel,
        grid=(s // tm,),
        in_specs=[pl.BlockSpec((tm, d), lambda i: (i, 0)), vec, vec, vec],
        out_specs=pl.BlockSpec((tm, d), lambda i: (i, 0)),
        out_shape=jax.ShapeDtypeStruct((s, d), BF16),
        compiler_params=_cparams(("parallel",)),
        name="prenorm",
    )(x2, g, scale, shift)


def _mm_kernel(h_ref, w_ref, o_ref):
    o_ref[...] = jnp.dot(h_ref[...], w_ref[...], preferred_element_type=F32).astype(o_ref.dtype)


def _matmul(h, w, l, out_dtype, tm, tn):
    s, d = h.shape
    n = w.shape[2]
    return pl.pallas_call(
        _mm_kernel,
        grid=(s // tm, n // tn),
        in_specs=[pl.BlockSpec((tm, d), lambda i, j: (i, 0)),
                  pl.BlockSpec((None, d, tn), lambda i, j: (l, 0, j))],
        out_specs=pl.BlockSpec((tm, tn), lambda i, j: (i, j)),
        out_shape=jax.ShapeDtypeStruct((s, n), out_dtype),
        compiler_params=_cparams(("parallel", "arbitrary")),
        name="proj_rows",
    )(h, w)


def _mm_split_kernel(h_ref, w_ref, o_ref):
    res = jnp.dot(h_ref[...], w_ref[...], preferred_element_type=F32)
    for c in range(o_ref.shape[0]):
        o_ref[c] = res[:, c * TILE:(c + 1) * TILE].astype(o_ref.dtype)


def _matmul_split(h, w, l, tm):
    s, d = h.shape
    n = w.shape[2]
    return pl.pallas_call(
        _mm_split_kernel,
        grid=(s // tm,),
        in_specs=[pl.BlockSpec((tm, d), lambda i: (i, 0)),
                  pl.BlockSpec((None, d, n), lambda i: (l, 0, 0))],
        out_specs=pl.BlockSpec((n // TILE, tm, TILE), lambda i: (0, i, 0)),
        out_shape=jax.ShapeDtypeStruct((n // TILE, s, TILE), BF16),
        compiler_params=_cparams(("parallel",)),
        name="proj_rows_split",
    )(h, w)


def _mm_t_tiled_kernel(h_ref, w_ref, o_ref, *, chunks):
    res = jnp.dot(h_ref[...], w_ref[...], preferred_element_type=F32).T
    for c in range(chunks):
        o_ref[c] = res[:, c * TILE:(c + 1) * TILE].astype(o_ref.dtype)


def _matmul_t_tiled(h, w, l, tm, tn):
    s, d = h.shape
    n = w.shape[2]
    chunks = tm // TILE
    return pl.pallas_call(
        functools.partial(_mm_t_tiled_kernel, chunks=chunks),
        grid=(s // tm, n // tn),
        in_specs=[pl.BlockSpec((tm, d), lambda i, j: (i, 0)),
                  pl.BlockSpec((None, d, tn), lambda i, j: (l, 0, j))],
        out_specs=pl.BlockSpec((chunks, tn, TILE), lambda i, j: (i, j, 0)),
        out_shape=jax.ShapeDtypeStruct((s // TILE, n, TILE), BF16),
        compiler_params=_cparams(("parallel", "arbitrary")),
        name="proj_cols_tiled",
    )(h, w)


def _fox_prep_kernel(h_ref, w_ref, b_ref, qa_ref, ka_ref, st_ref, carry_ref, *, tm):
    @pl.when(pl.program_id(0) == 0)
    def _():
        carry_ref[...] = jnp.zeros_like(carry_ref)

    small = jnp.dot(h_ref[...], w_ref[...], preferred_element_type=F32)
    st_ref[...] = small.T
    v = small + b_ref[...]
    c = jnp.minimum(v, 0.0) - jnp.log1p(jnp.exp(-jnp.abs(v)))
    row = lax.broadcasted_iota(jnp.int32, (tm, TILE), 0)
    sh = 1
    while sh < tm:
        c = c + jnp.where(row >= sh, pltpu.roll(c, sh, axis=0), 0.0)
        sh *= 2
    c = c + carry_ref[...]
    carry_ref[...] = c[tm - 1:tm, :]
    c = c * LOG2E

    lane = lax.broadcasted_iota(jnp.int32, (tm, TILE), 1)
    for h in range(H_FOX):
        ch = jnp.broadcast_to(c[:, h:h + 1], (tm, TILE))
        hi = ch.astype(BF16).astype(F32)
        r1 = ch - hi
        mid = r1.astype(BF16).astype(F32)
        lo = r1 - mid
        pieces = jnp.where((lane == 0) | (lane == 3), hi,
                           jnp.where((lane == 1) | (lane == 4), mid, lo))
        qa = jnp.where(lane < 3, pieces, jnp.where(lane < 6, 1.0, 0.0))
        ka = jnp.where(lane < 3, 1.0, jnp.where(lane < 6, -pieces, 0.0))
        qa_ref[:, h * TILE:(h + 1) * TILE] = qa.astype(BF16)
        ka_ref[:, h * TILE:(h + 1) * TILE] = ka.astype(BF16)


def _fox_prep(h, w_small, l, b_pad):
    s, d = h.shape
    tm = 512
    out = jax.ShapeDtypeStruct((s, H_FOX * TILE), BF16)
    return pl.pallas_call(
        functools.partial(_fox_prep_kernel, tm=tm),
        grid=(s // tm,),
        in_specs=[pl.BlockSpec((tm, d), lambda i: (i, 0)),
                  pl.BlockSpec((None, d, TILE), lambda i: (l, 0, 0)),
                  pl.BlockSpec((1, TILE), lambda i: (0, 0))],
        out_specs=[pl.BlockSpec((tm, H_FOX * TILE), lambda i: (i, 0)),
                   pl.BlockSpec((tm, H_FOX * TILE), lambda i: (i, 0)),
                   pl.BlockSpec((TILE, tm), lambda i: (0, i))],
        out_shape=[out, out, jax.ShapeDtypeStruct((TILE, s), F32)],
        scratch_shapes=[pltpu.VMEM((1, TILE), F32)],
        compiler_params=_cparams(("arbitrary",)),
        name="fox_prep",
    )(h, w_small, b_pad)


def _fox_kernel(q_ref, qa_ref, k_ref, ka_ref, vt_ref, z_ref, o_ref, s0_ref, s1_ref, s2_ref,
                acc_ref):
    i = pl.program_id(1)
    t = FOX_T
    tk = FOX_TK
    heads = range(FOX_HEADS)
    col = lambda h: slice(h * TILE, (h + 1) * TILE)
    qaug = [jnp.concatenate([q_ref[:, col(h)], qa_ref[:, col(h)]], axis=1) for h in heads]

    def scores(h, j, rows):
        r0 = pl.multiple_of(j * rows, rows)
        kaug = jnp.concatenate([k_ref[pl.ds(r0, rows), col(h)], ka_ref[pl.ds(r0, rows), col(h)]],
                               axis=1)
        return lax.dot_general(kaug, qaug[h], NT_DIMS, preferred_element_type=F32)

    def values_t(h, j, rows):
        n_sub = rows // TILE
        return jnp.concatenate(
            [jnp.concatenate([vt_ref[n_sub * j + c, col(h), :] for c in range(n_sub)], axis=1),
             jnp.ones((ONES_ROWS, rows), BF16)], axis=0)

    kio = lax.broadcasted_iota(jnp.int32, (t, t), 0)
    qio = lax.broadcasted_iota(jnp.int32, (t, t), 1)
    ms = []
    for h in heads:
        s = jnp.where(kio <= qio, scores(h, i, t), NEG_BIG)
        m = jnp.max(s, axis=0, keepdims=True)
        p = jnp.exp2(s - m)
        ms.append(m)
        acc_ref[h] = jnp.dot(values_t(h, i, t), p.astype(BF16), preferred_element_type=F32)

    def update(h, s, vt, m):
        m_new = jnp.maximum(m, jnp.max(s, axis=0, keepdims=True))
        alpha = jnp.exp2(m - m_new)
        p = jnp.exp2(s - m_new)
        acc_ref[h] = alpha * acc_ref[h] + jnp.dot(vt, p.astype(BF16), preferred_element_type=F32)
        return m_new

    bufs = (s0_ref, s1_ref, s2_ref)
    for h in heads:
        s0_ref[h] = scores(h, 0, tk)

    def triple(pp, ms):
        ms = list(ms)
        for stage in range(3):
            cur, nxt = bufs[stage], bufs[(stage + 1) % 3]
            for h in heads:
                nxt[h] = scores(h, 3 * pp + stage + 1, tk)
            for h in heads:
                ms[h] = update(h, cur[h], values_t(h, 3 * pp + stage, tk), ms[h])
        return tuple(ms)

    n_far = i * (t // tk)
    ms = lax.fori_loop(0, n_far // 3, triple, tuple(ms))
    done = (n_far // 3) * 3

    @pl.when(n_far - done == 1)
    def _():
        for h in heads:
            update(h, s0_ref[h], values_t(h, done, tk), ms[h])

    @pl.when(n_far - done == 2)
    def _():
        for h in heads:
            s1_ref[h] = scores(h, done + 1, tk)
        for h in heads:
            m_mid = update(h, s0_ref[h], values_t(h, done, tk), ms[h])
            update(h, s1_ref[h], values_t(h, done + 1, tk), m_mid)

    for h in heads:
        o = (acc_ref[h, 0:HD, :] * (1.0 / acc_ref[h, HD:HD + 1, :])).T
        o_ref[:, col(h)] = (o * _silu(z_ref[:, col(h)].astype(F32))).astype(BF16)


def _fox_attention(proj, qa, ka, proj_t):
    s = proj.shape[0]
    t = FOX_T
    nk = s // TILE
    hw = FOX_HEADS * TILE
    score_buf = pltpu.VMEM((FOX_HEADS, FOX_TK, t), F32)
    return pl.pallas_call(
        _fox_kernel,
        grid=(H_FOX // FOX_HEADS, s // t),
        scratch_shapes=[score_buf, score_buf, score_buf,
                        pltpu.VMEM((FOX_HEADS, HD + ONES_ROWS, t), F32)],
        in_specs=[pl.BlockSpec((t, hw), lambda h, i: (i, BLK_QF // FOX_HEADS + h)),
                  pl.BlockSpec((t, hw), lambda h, i: (i, h)),
                  pl.BlockSpec((s, hw), lambda h, i: (0, BLK_KF // FOX_HEADS + h)),
                  pl.BlockSpec((s, hw), lambda h, i: (0, h)),
                  pl.BlockSpec((nk, hw, TILE), lambda h, i: (0, h, 0)),
                  pl.BlockSpec((t, hw), lambda h, i: (i, BLK_ZF // FOX_HEADS + h))],
        out_specs=pl.BlockSpec((t, hw), lambda h, i: (i, h)),
        out_shape=jax.ShapeDtypeStruct((s, D_FOX), BF16),
        compiler_params=_cparams(("parallel", "parallel")),
        name="fox_attention",
    )(proj, qa, proj, ka, proj_t, proj)


def _compress_kernel(x_ref, w1_ref, pe_ref, w2_ref, o_ref, ot_ref):
    n = x_ref.shape[0]
    ab = jnp.dot(x_ref[...], w1_ref[0], preferred_element_type=F32)
    pe = jnp.dot(pe_ref[0], w1_ref[0], preferred_element_type=F32)
    a = ab[:, :CMP_HIDDEN]
    b_next = pltpu.roll(ab[:, CMP_HIDDEN:], n - 1, axis=0)
    hid = a + b_next + pe[0:1, :CMP_HIDDEN] + pe[8:9, CMP_HIDDEN:]
    out = jnp.dot(_silu(hid).astype(BF16), w2_ref[0], preferred_element_type=F32)
    o_ref[...] = out.astype(BF16)
    ot_ref[...] = out.T.astype(BF16)


def _compress(x4, w1cat, pe16, w2, l):
    _, n, k = x4.shape
    kind = lambda c: (l, c // NSA_G, 0, 0)
    return pl.pallas_call(
        _compress_kernel,
        grid=(2 * NSA_G,),
        in_specs=[pl.BlockSpec((None, n, k), lambda c: (c, 0, 0)),
                  pl.BlockSpec((None, 1, k, 2 * CMP_HIDDEN), kind),
                  pl.BlockSpec((None, 1, 16, k), kind),
                  pl.BlockSpec((None, 1, CMP_HIDDEN, HD), kind)],
        out_specs=[pl.BlockSpec((None, n, HD), lambda c: (c, 0, 0)),
                   pl.BlockSpec((None, HD, n), lambda c: (c, 0, 0))],
        out_shape=[jax.ShapeDtypeStruct((2 * NSA_G, n, HD), BF16),
                   jax.ShapeDtypeStruct((2 * NSA_G, HD, n), BF16)],
        compiler_params=_cparams(("parallel",)),
        name="nsa_compress",
    )(x4, w1cat, pe16, w2)


def _nsa_kernel(*refs, n_cmp):
    q_refs, z_refs = refs[0:H_NSA], refs[H_NSA:2 * H_NSA]
    (ks_ref, kw_ref, vst_ref, vwt_ref, kc_ref, vct_ref, c2s_ref, cstrip_ref, t0_ref, t1_ref,
     gl_ref, o_ref, sc_ref, sel_ref, m_s, acc_s, acc_w, ss_ref, sw_ref, sa_ref, sb_ref,
     sc3_ref) = refs[2 * H_NSA:]
    i = pl.program_id(0)
    w3 = NSA_R * TILE
    gw = NSA_G * TILE
    groups = range(NSA_G)
    col = lambda g: slice(g * TILE, (g + 1) * TILE)
    qs = [jnp.concatenate([q_refs[NSA_R * g + r][...] for r in range(NSA_R)], axis=0)
          for g in groups]
    b_io = lax.broadcasted_iota(jnp.int32, (TILE, w3), 0)
    a_io = lax.broadcasted_iota(jnp.int32, (TILE, w3), 1) & (TILE - 1)
    tpc = NSA_CHUNK // TILE
    bpc = NSA_CHUNK // SEL_LEN

    def key_scores(k_ref, g, tile0, n_tiles):
        r0 = pl.multiple_of(tile0 * TILE, TILE)
        return lax.dot_general(k_ref[pl.ds(r0, n_tiles * TILE), col(g)], qs[g], NT_DIMS,
                               preferred_element_type=F32)

    def values_t(vt_ref, g, tile0, n_tiles):
        return jnp.concatenate(
            [jnp.concatenate([vt_ref[tile0 + c, col(g), :] for c in range(n_tiles)], axis=1),
             jnp.ones((ONES_ROWS, n_tiles * TILE), BF16)], axis=0)

    def fix_near(s_ref, g, d):
        own = pl.ds(pl.multiple_of(d * TILE, TILE), TILE)
        prev = pl.ds(pl.multiple_of(jnp.maximum(d - 1, 0) * TILE, TILE), TILE)
        s_ref[g, prev, :] = s_ref[g, prev, :] + jnp.where(d >= 1, t1_ref[g], 0.0)
        s_ref[g, own, :] = jnp.where(b_io <= a_io, s_ref[g, own, :] + t0_ref[g], NEG_BIG)

    w0 = pl.multiple_of(8 * i, 8)
    n_io = lax.broadcasted_iota(jnp.int32, (n_cmp, w3), 0)
    qa_io = lax.broadcasted_iota(jnp.int32, (n_cmp, w3), 1) & (TILE - 1)
    valid = (CMP_STRIDE * n_io + (CMP_LEN - 1)) <= (TILE * i + qa_io)
    n_wt = WINDOW // TILE + 1
    w_tile0 = jnp.maximum(i - WINDOW // TILE, 0)
    d_win = i - w_tile0
    for g in groups:
        sc_ref[g, 0:16, :] = jnp.zeros((16, w3), F32)
        sc_ref[g, 16:16 + n_cmp, :] = lax.dot_general(kc_ref[g], qs[g], NT_DIMS,
                                                      preferred_element_type=F32)
        sw_ref[g] = key_scores(kw_ref, g, w_tile0, n_wt)
    oc_t, psums = [], []
    for g in groups:
        sc_ref[g, pl.ds(w0, 24), :] = sc_ref[g, pl.ds(w0, 24), :] + cstrip_ref[g]
        scm = jnp.where(valid, sc_ref[g, 16:16 + n_cmp, :], NEG_BIG)
        mc = jnp.max(scm, axis=0, keepdims=True)
        pc = jnp.where(valid, jnp.exp2(scm - mc), 0.0)
        lc = jnp.sum(pc, axis=0, keepdims=True)
        pcn = pc * jnp.where(lc > 0.0, 1.0 / lc, 0.0)
        oc_t.append(jnp.dot(vct_ref[g], pcn.astype(BF16), preferred_element_type=F32))
        psums.append(pcn[:, 0:TILE] + pcn[:, TILE:2 * TILE] + pcn[:, 2 * TILE:3 * TILE])

        for c in range(1, n_wt):
            rows = slice(c * TILE, (c + 1) * TILE)
            sw_ref[g, rows, :] = jnp.where(d_win < c, NEG_BIG, sw_ref[g, rows, :])
        fix_near(sw_ref, g, d_win)
        sw_ref[g, 0:TILE, :] = jnp.where((a_io < b_io) | (i < WINDOW // TILE),
                                         sw_ref[g, 0:TILE, :], NEG_BIG)
        sw = sw_ref[g]
        p_w = jnp.exp2(sw - jnp.max(sw, axis=0, keepdims=True))
        acc_w[g] = jnp.dot(values_t(vwt_ref, g, w_tile0, n_wt), p_w.astype(BF16),
                           preferred_element_type=F32)

    psum = jnp.concatenate(psums, axis=1)
    p_hi = psum.astype(BF16)
    p_lo = (psum - p_hi.astype(F32)).astype(BF16)
    c2s = c2s_ref[...]
    imp = (jnp.dot(c2s, p_hi, preferred_element_type=F32)
           + jnp.dot(c2s, p_lo, preferred_element_type=F32))
    n_sel = imp.shape[0]
    m_io = lax.broadcasted_iota(jnp.int32, (n_sel, gw), 0)
    m_f = m_io.astype(F32)
    qpos = TILE * i + (lax.broadcasted_iota(jnp.int32, (n_sel, gw), 1) & (TILE - 1))
    own = qpos >> 6
    forced = (m_io == 0) | (m_io == own) | (m_io == own - 1)
    eligible = SEL_LEN * m_io <= qpos
    sel = jnp.where(eligible & forced, 1.0, 0.0)
    cur = jnp.where(eligible, jnp.where(forced, -2.0, imp), -1.0)
    for _ in range(min(SEL_TOPK, n_sel) - N_FORCED):
        mx = jnp.max(cur, axis=0, keepdims=True)
        idx = jnp.min(jnp.where(cur == mx, m_f, float(n_sel)), axis=0, keepdims=True)
        pick = m_f == idx
        sel = jnp.where(pick, jnp.maximum(jnp.where(mx >= 0.0, 1.0, 0.0), sel), sel)
        cur = jnp.where(pick, -2.0, cur)
    sel_ref[...] = sel

    def sel_mask(g, c):
        rows = sel_ref[pl.ds(pl.multiple_of(c * bpc, 8), bpc), col(g)]
        mk = jnp.concatenate([jnp.broadcast_to(rows[b:b + 1, :], (SEL_LEN, TILE))
                              for b in range(bpc)], axis=0)
        return jnp.concatenate([mk, mk, mk], axis=1) > 0.5

    def sel_update(g, c, s):
        s = jnp.where(sel_mask(g, c), s, NEG_BIG)
        m_old = m_s[g]
        m_new = jnp.maximum(m_old, jnp.max(s, axis=0, keepdims=True))
        alpha = jnp.exp2(m_old - m_new)
        p = jnp.exp2(s - m_new)
        acc_s[g] = alpha * acc_s[g] + jnp.dot(values_t(vst_ref, g, c * tpc, tpc), p.astype(BF16),
                                              preferred_element_type=F32)
        m_s[g] = m_new

    m_s[...] = jnp.full(m_s.shape, NEG_BIG, F32)
    acc_s[...] = jnp.zeros(acc_s.shape, F32)

    c_own = i // tpc
    c_prev = jnp.maximum(i - 1, 0) // tpc

    bufs = (sa_ref, sb_ref, sc3_ref)
    for g in groups:
        sa_ref[g] = key_scores(ks_ref, g, 0, tpc)

    def far_triple(pp, carry):
        for stage in range(3):
            cur_buf, nxt_buf = bufs[stage], bufs[(stage + 1) % 3]
            for g in groups:
                nxt_buf[g] = key_scores(ks_ref, g, (3 * pp + stage + 1) * tpc, tpc)
            for g in groups:
                sel_update(g, 3 * pp + stage, cur_buf[g])
        return carry

    lax.fori_loop(0, c_prev // 3, far_triple, 0)
    done = (c_prev // 3) * 3

    @pl.when(c_prev - done == 1)
    def _():
        for g in groups:
            sel_update(g, done, sa_ref[g])

    @pl.when(c_prev - done == 2)
    def _():
        for g in groups:
            sb_ref[g] = key_scores(ks_ref, g, (done + 1) * tpc, tpc)
        for g in groups:
            sel_update(g, done, sa_ref[g])
            sel_update(g, done + 1, sb_ref[g])

    @pl.when(c_prev != c_own)
    def _():
        last = slice((tpc - 1) * TILE, tpc * TILE)
        for g in groups:
            ss_ref[g] = key_scores(ks_ref, g, c_prev * tpc, tpc)
            ss_ref[g, last, :] = ss_ref[g, last, :] + t1_ref[g]
            sel_update(g, c_prev, ss_ref[g])

    for g in groups:
        ss_ref[g] = key_scores(ks_ref, g, c_own * tpc, tpc)
    for g in groups:
        fix_near(ss_ref, g, i - c_own * tpc)
        sel_update(g, c_own, ss_ref[g])

    for g in groups:
        gates = jax.nn.sigmoid(gl_ref[g])

        def gate_row(branch):
            return jnp.concatenate([gates[r * 3 + branch:r * 3 + branch + 1, :]
                                    for r in range(NSA_R)], axis=1)

        out_t = (oc_t[g] * gate_row(0)
                 + acc_s[g, 0:HD, :] * (gate_row(1) / acc_s[g, HD:HD + 1, :])
                 + acc_w[g, 0:HD, :] * (gate_row(2) / acc_w[g, HD:HD + 1, :]))
        for r in range(NSA_R):
            h = NSA_R * g + r
            o_r = out_t[:, r * TILE:(r + 1) * TILE].T
            o_ref[:, h * TILE:(h + 1) * TILE] = (
                o_r * _silu(z_refs[h][...].astype(F32))).astype(BF16)


def _nsa_attention(proj, proj_t, cmp_n, cmp_t, c2s_t, cstrip, t0, t1, glog_t):
    s = proj.shape[0]
    nk = s // TILE
    n_cmp = cmp_n.shape[1]
    n_sel = s // SEL_LEN
    w3 = NSA_R * TILE
    gw = NSA_G * TILE
    qspec = lambda h: pl.BlockSpec((TILE, TILE), lambda i, h=h: (i, BLK_QN + h))
    zspec = lambda h: pl.BlockSpec((TILE, TILE), lambda i, h=h: (i, BLK_ZN + h))
    whole = lambda shape: pl.BlockSpec(shape, lambda i: (0,) * len(shape))
    resident = dict(pipeline_mode=pl.Buffered(1))
    chunk_buf = pltpu.VMEM((NSA_G, NSA_CHUNK, w3), F32)
    acc_buf = pltpu.VMEM((NSA_G, HD + ONES_ROWS, w3), F32)
    return pl.pallas_call(
        functools.partial(_nsa_kernel, n_cmp=n_cmp),
        grid=(s // TILE,),
        in_specs=[qspec(h) for h in range(H_NSA)] + [zspec(h) for h in range(H_NSA)] + [
            pl.BlockSpec((s, gw), lambda i: (0, BLK_KS // NSA_G), **resident),
            pl.BlockSpec((s, gw), lambda i: (0, BLK_KW // NSA_G), **resident),
            pl.BlockSpec((nk, gw, TILE), lambda i: (0, H_FOX // NSA_G, 0), **resident),
            pl.BlockSpec((nk, gw, TILE), lambda i: (0, H_FOX // NSA_G + 1, 0), **resident),
            pl.BlockSpec((NSA_G, n_cmp, HD), lambda i: (0, 0, 0)),
            pl.BlockSpec((NSA_G, HD, n_cmp), lambda i: (1, 0, 0)),
            whole((n_sel, n_cmp)), whole((NSA_G, 24, w3)), whole((NSA_G, TILE, w3)),
            whole((NSA_G, TILE, w3)),
            pl.BlockSpec((NSA_G, NSA_R * 3, TILE), lambda i: (0, 0, i))],
        out_specs=pl.BlockSpec((TILE, D_NSA), lambda i: (i, 0)),
        out_shape=jax.ShapeDtypeStruct((s, D_NSA), BF16),
        scratch_shapes=[pltpu.VMEM((NSA_G, 16 + n_cmp + 8, w3), F32),
                        pltpu.VMEM((n_sel, gw), F32),
                        pltpu.VMEM((NSA_G, 1, w3), F32),
                        acc_buf, acc_buf,
                        chunk_buf,
                        pltpu.VMEM((NSA_G, WINDOW + TILE, w3), F32),
                        chunk_buf, chunk_buf, chunk_buf],
        compiler_params=_cparams(("parallel",)),
        name="nsa_attention",
    )(*([proj] * (2 * H_NSA)), proj, proj, proj_t, proj_t,
      cmp_n, cmp_t, c2s_t, cstrip, t0, t1, glog_t)


def _conv_kernel(u_ref, gb_ref, gc_ref, z_ref, uh_ref, gch_ref, w_ref, o_ref, *, tm):
    i = pl.program_id(0)
    y = gc_ref[...].astype(F32) * u_ref[...].astype(F32)
    yh = gch_ref[...].astype(F32) * uh_ref[...].astype(F32)
    yh = jnp.where(i > 0, yh, 0.0)
    h1 = yh[15:16, :]
    h2 = yh[14:15, :]
    row = lax.broadcasted_iota(jnp.int32, (tm, D_CONV), 0)
    y1 = jnp.where(row == 0, h1, pltpu.roll(y, 1, axis=0))
    y2 = jnp.where(row == 0, h2, jnp.where(row == 1, h1, pltpu.roll(y, 2, axis=0)))
    w = w_ref[...]
    conv = w[0:1, :] * y2 + w[1:2, :] * y1 + w[2:3, :] * y
    o_ref[...] = (gb_ref[...].astype(F32) * conv * _silu(z_ref[...].astype(F32))).astype(BF16)


def _conv_branch(proj, conv_w):
    s = proj.shape[0]
    tm = 512
    main = lambda c: pl.BlockSpec((tm, D_CONV), lambda i, c=c: (i, c))
    halo = lambda c: pl.BlockSpec((16, D_CONV), lambda i, c=c: (jnp.maximum(i * (tm // 16) - 1, 0), c))
    return pl.pallas_call(
        functools.partial(_conv_kernel, tm=tm),
        grid=(s // tm,),
        in_specs=[main(0), main(1), main(2), main(3), halo(0), halo(2),
                  pl.BlockSpec((CONV_WIDTH, D_CONV), lambda i: (0, 0))],
        out_specs=pl.BlockSpec((tm, D_CONV), lambda i: (i, 0)),
        out_shape=jax.ShapeDtypeStruct((s, D_CONV), BF16),
        compiler_params=_cparams(("parallel",)),
        name="conv_branch",
    )(proj, proj, proj, proj, proj, proj, conv_w)


def _out_kernel(ya_ref, yf_ref, yn_ref, w_ref, x_ref, pg_ref, gate_ref, *rest, with_next):
    y = (jnp.dot(ya_ref[...], w_ref[0:D_CONV, :], preferred_element_type=F32)
         + jnp.dot(yf_ref[...], w_ref[D_CONV:D_CONV + D_FOX, :], preferred_element_type=F32)
         + jnp.dot(yn_ref[...], w_ref[D_CONV + D_FOX:, :], preferred_element_type=F32))
    yn = y * lax.rsqrt(jnp.mean(y * y, axis=-1, keepdims=True) + NORM_EPS) * pg_ref[...]
    xn = x_ref[...] + gate_ref[...] * yn
    if with_next:
        ng_ref, nsc_ref, nsh_ref, xo_ref, h_ref = rest
        h_ref[...] = _modulated_norm(xn, ng_ref[...], nsc_ref[...], nsh_ref[...]).astype(BF16)
    else:
        (xo_ref,) = rest
    xo_ref[...] = xn


def _out_proj(ya, yf, yn, w_out, l, x2, post_g, gate, nxt):
    s, d = x2.shape
    tm = 512
    rows = lambda n: pl.BlockSpec((tm, n), lambda i: (i, 0))
    vec = pl.BlockSpec((1, d), lambda i: (0, 0))
    in_specs = [rows(D_CONV), rows(D_FOX), rows(D_NSA),
                pl.BlockSpec((None, d, d), lambda i: (l, 0, 0), pipeline_mode=pl.Buffered(1)),
                rows(d), vec, vec]
    args = [ya, yf, yn, w_out, x2, post_g, gate]
    out_specs = [rows(d)]
    out_shape = [jax.ShapeDtypeStruct((s, d), F32)]
    if nxt is not None:
        in_specs += [vec, vec, vec]
        args += list(nxt)
        out_specs.append(rows(d))
        out_shape.append(jax.ShapeDtypeStruct((s, d), BF16))
    res = pl.pallas_call(
        functools.partial(_out_kernel, with_next=nxt is not None),
        grid=(s // tm,),
        in_specs=in_specs,
        out_specs=out_specs,
        out_shape=out_shape,
        compiler_params=_cparams(("parallel",)),
        name="out_proj",
    )(*args)
    return res if nxt is not None else (res[0], None)


def _bucket_of_distance():
    max_exact = REL_BUCKETS // 2
    d = np.arange(REL_MAX_DIST)
    nf = np.maximum(d, max_exact).astype(np.float32)
    large = max_exact + (np.log(nf / np.float32(max_exact)) / np.float32(math.log(REL_MAX_DIST / max_exact))
                         * np.float32(REL_BUCKETS - max_exact)).astype(np.int32)
    return np.where(d < max_exact, d, np.minimum(large, REL_BUCKETS - 1))


def _distance_tables():
    b = np.arange(TILE)[:, None]
    a = np.arange(TILE)[None, :]
    none = REL_MAX_DIST
    d0 = np.where(a >= b, a - b, none)
    d1 = np.where(a < b, TILE + a - b, none)
    m = np.arange(24)[:, None]
    dc = a - CMP_STRIDE * m + (2 * TILE - (CMP_LEN - 1))
    dc = np.where((dc >= 0) & (dc < REL_MAX_DIST), dc, none)
    return d0, d1, dc


def _bias_tables(rel_bias):
    bucket = np.concatenate([_bucket_of_distance(), [REL_BUCKETS - 1]])
    rel = (rel_bias - rel_bias[REL_BUCKETS - 1:REL_BUCKETS, :]) * LOG2E
    rel = rel.reshape(REL_BUCKETS, NSA_G, NSA_R)

    def expand(idx):
        bk = bucket[idx]
        tab = jnp.zeros((NSA_G, idx.shape[0], NSA_R, TILE), F32)
        for b in range(REL_BUCKETS - 1):
            hit = jnp.asarray(bk == b)[None, :, None, :]
            tab = jnp.where(hit, rel[b][:, None, :, None], tab)
        return tab.reshape(NSA_G, idx.shape[0], NSA_R * TILE)

    d0, d1, dc = _distance_tables()
    return expand(d0), expand(d1), expand(dc)


def _cmp_to_sel_t(s, n_cmp_pad):
    n_cmp = (s - CMP_LEN) // CMP_STRIDE + 1
    n_sel = s // SEL_LEN
    c_lo = np.arange(n_cmp_pad)[None, :] * CMP_STRIDE
    sel_start = np.arange(n_sel)[:, None] * SEL_LEN
    overlap = (c_lo < sel_start + SEL_LEN) & (c_lo + CMP_LEN > sel_start)
    overlap &= np.arange(n_cmp_pad)[None, :] < n_cmp
    return jnp.asarray(overlap, dtype=BF16)


def _pack_w_in(w_in):
    sizes = ([D_CONV] * 4 + [D_FOX] * 3 + [H_FOX, D_FOX] + [D_NSA] + [2 * HD] * 6
             + [3 * H_NSA, D_NSA])
    offs = np.concatenate([[0], np.cumsum(sizes)])
    names = ["u", "gb", "gc", "za", "qf", "kf", "vf", "ff", "zf", "qn", "kc", "vc", "ks", "vs",
             "kw", "vw", "gn", "zn"]
    col = {n: w_in[:, :, int(offs[k]):int(offs[k + 1])] for k, n in enumerate(names)}
    scale = HD ** -0.5 * LOG2E
    w_a = jnp.concatenate([col["u"], col["gb"], col["gc"], col["za"],
                           col["qn"] * scale, col["zn"], col["ks"], col["kw"],
                           col["qf"] * scale, col["kf"], col["zf"]], axis=-1).astype(BF16)
    w_kv = jnp.concatenate([col["kc"], col["vc"]], axis=-1).astype(BF16)
    w_v = jnp.concatenate([col["vf"], col["vs"], col["vw"]], axis=-1).astype(BF16)
    small = jnp.concatenate([col["ff"], col["gn"]], axis=-1)
    small = jnp.pad(small, ((0, 0), (0, 0), (0, TILE - small.shape[-1]))).astype(BF16)
    return w_a, w_kv, w_v, small


def kernel(x, c, w_ada, b_ada, pre_norm, post_norm, w_in, b_forget, conv_w,
           cmp_pe_k, cmp_w1_k, cmp_w2_k, cmp_pe_v, cmp_w1_v, cmp_w2_v, w_out, rel_bias):
    bsz, s, d = x.shape
    assert bsz == 1 and d == D_MODEL and s % 1024 == 0 and s >= 2 * WINDOW
    depth = w_in.shape[0]
    x2 = x.reshape(s, d)
    n_cmp_pad = s // CMP_STRIDE

    mod = _ada_mod(jnp.broadcast_to(c, (8, d)), w_ada, b_ada)[:, 0:1, :]
    shift, scale, gate = mod[:, :, :d], mod[:, :, d:2 * d], mod[:, :, 2 * d:]

    w_a, w_kv, w_v, w_small = _pack_w_in(w_in)
    w_out_b = w_out.astype(BF16)
    half = CMP_LEN * HD // 2
    w1cat = jnp.stack([jnp.concatenate([w[:, :half], w[:, half:]], axis=-1)
                       for w in (cmp_w1_k, cmp_w1_v)], axis=1).astype(BF16)
    pe = jnp.stack([cmp_pe_k, cmp_pe_v], axis=1).reshape(depth, 2, 2, 1, half)
    pe16 = jnp.broadcast_to(pe, (depth, 2, 2, 8, half)).reshape(depth, 2, 16, half).astype(BF16)
    w2 = jnp.stack([cmp_w2_k, cmp_w2_v], axis=1).astype(BF16)
    b_pad = jnp.pad(b_forget, ((0, 0), (0, TILE - H_FOX))).reshape(depth, 1, TILE)
    t0, t1, cstrip = _bias_tables(rel_bias)
    c2s_t = _cmp_to_sel_t(s, n_cmp_pad)

    h = _prenorm(x2, pre_norm[0:1], scale[0], shift[0])
    for l in range(depth):
        proj = _matmul(h, w_a, l, BF16, 1024, 1280)
        kcvc = _matmul_split(h, w_kv, l, 1024)
        proj_t = _matmul_t_tiled(h, w_v, l, 1024, 256)

        ya = _conv_branch(proj, conv_w[l])

        qa, ka, small_t = _fox_prep(h, w_small, l, b_pad[l])
        yf = _fox_attention(proj, qa, ka, proj_t)

        x4 = kcvc.reshape(2 * NSA_G, n_cmp_pad, CMP_STRIDE * HD)
        cmp_n, cmp_t = _compress(x4, w1cat, pe16, w2, l)
        glog_t = small_t[H_FOX:H_FOX + 3 * H_NSA].reshape(NSA_G, NSA_R * 3, s)
        yn = _nsa_attention(proj, proj_t, cmp_n, cmp_t, c2s_t, cstrip, t0, t1, glog_t)

        nxt = None
        if l + 1 < depth:
            nxt = (pre_norm[l + 1:l + 2], scale[l + 1], shift[l + 1])
        x2, h = _out_proj(ya, yf, yn, w_out_b, l, x2, post_norm[l:l + 1], gate[l], nxt)
    return x2.reshape(bsz, s, d)
```

```python
import functools
import math

import numpy as np
import jax
import jax.numpy as jnp
from jax import lax
from jax.experimental import pallas as pl
from jax.experimental.pallas import tpu as pltpu

F32 = jnp.float32
BF16 = jnp.bfloat16

D_MODEL = 2048
HD = 128
D_CONV = 512
D_FOX = 768
D_NSA = 768
H_FOX = 6
H_NSA = 6
NSA_G = 2
NSA_R = 3
CONV_WIDTH = 3
CMP_LEN = 32
CMP_STRIDE = 16
CMP_HIDDEN = 256
SEL_LEN = 64
SEL_TOPK = 16
N_FORCED = 3
WINDOW = 512
REL_BUCKETS = 32
REL_MAX_DIST = 128
NORM_EPS = 1e-6
NEG_BIG = -1e30
LOG2E = math.log2(math.e)

TILE = 128
ONES_ROWS = 16
NSA_CHUNK = 512
FOX_T = 512
FOX_TK = 512
FOX_HEADS = 2
VMEM_LIMIT = 56 * 1024 * 1024

NT_DIMS = (((1,), (1,)), ((), ()))

BLK_QN, BLK_ZN, BLK_KS, BLK_KW = 16, 22, 28, 30
BLK_QF, BLK_KF, BLK_ZF = 32, 38, 44
N_A = 6400


def _cparams(sem, vmem=VMEM_LIMIT):
    return pltpu.CompilerParams(dimension_semantics=sem, vmem_limit_bytes=vmem)


def _silu(v):
    return v * jax.nn.sigmoid(v)


def _mod_kernel(c_ref, w_ref, b_ref, o_ref):
    ca = _silu(c_ref[...])
    o_ref[0] = jnp.dot(ca, w_ref[0], precision=lax.Precision.HIGHEST,
                       preferred_element_type=F32) + b_ref[0]


def _ada_mod(c8, w_ada, b_ada):
    depth, d, n = w_ada.shape
    tn = 1536
    return pl.pallas_call(
        _mod_kernel,
        grid=(depth, n // tn),
        in_specs=[pl.BlockSpec((8, d), lambda l, j: (0, 0)),
                  pl.BlockSpec((1, d, tn), lambda l, j: (l, 0, j)),
                  pl.BlockSpec((1, 1, tn), lambda l, j: (l, 0, j))],
        out_specs=pl.BlockSpec((1, 8, tn), lambda l, j: (l, 0, j)),
        out_shape=jax.ShapeDtypeStruct((depth, 8, n), F32),
        compiler_params=_cparams(("parallel", "parallel")),
        name="ada_mod",
    )(c8, w_ada, b_ada.reshape(depth, 1, n))


def _modulated_norm(x, g, scale, shift):
    y = x * lax.rsqrt(jnp.mean(x * x, axis=-1, keepdims=True) + NORM_EPS) * g
    return y * (1.0 + scale) + shift


def _prenorm_kernel(x_ref, g_ref, sc_ref, sh_ref, h_ref):
    h_ref[...] = _modulated_norm(x_ref[...], g_ref[...], sc_ref[...], sh_ref[...]).astype(BF16)


def _prenorm(x2, g, scale, shift):
    s, d = x2.shape
    tm = 512
    vec = pl.BlockSpec((1, d), lambda i: (0, 0))
    return pl.pallas_call(
        _prenorm_kernel,
        grid=(s // tm,),
        in_specs=[pl.BlockSpec((tm, d), lambda i: (i, 0)), vec, vec, vec],
        out_specs=pl.BlockSpec((tm, d), lambda i: (i, 0)),
        out_shape=jax.ShapeDtypeStruct((s, d), BF16),
        compiler_params=_cparams(("parallel",)),
        name="prenorm",
    )(x2, g, scale, shift)


def _mm_kernel(h_ref, wt_ref, o_ref):
    o_ref[...] = lax.dot_general(h_ref[...], wt_ref[...], NT_DIMS,
                                 preferred_element_type=F32).astype(o_ref.dtype)


def _matmul(h, wt, l, tm, tn):
    s, d = h.shape
    n = wt.shape[1]
    return pl.pallas_call(
        _mm_kernel,
        grid=(s // tm, n // tn),
        in_specs=[pl.BlockSpec((tm, d), lambda i, j: (i, 0)),
                  pl.BlockSpec((None, tn, d), lambda i, j: (l, j, 0))],
        out_specs=pl.BlockSpec((tm, tn), lambda i, j: (i, j)),
        out_shape=jax.ShapeDtypeStruct((s, n), BF16),
        compiler_params=_cparams(("parallel", "arbitrary")),
        name="proj_rows",
    )(h, wt)


def _mm_split_kernel(h_ref, wt_ref, o_ref):
    res = lax.dot_general(h_ref[...], wt_ref[...], NT_DIMS, preferred_element_type=F32)
    for c in range(o_ref.shape[0]):
        o_ref[c] = res[:, c * TILE:(c + 1) * TILE].astype(o_ref.dtype)


def _matmul_split(h, wt, l, tm):
    s, d = h.shape
    n = wt.shape[1]
    return pl.pallas_call(
        _mm_split_kernel,
        grid=(s // tm,),
        in_specs=[pl.BlockSpec((tm, d), lambda i: (i, 0)),
                  pl.BlockSpec((None, n, d), lambda i: (l, 0, 0))],
        out_specs=pl.BlockSpec((n // TILE, tm, TILE), lambda i: (0, i, 0)),
        out_shape=jax.ShapeDtypeStruct((n // TILE, s, TILE), BF16),
        compiler_params=_cparams(("parallel",)),
        name="proj_rows_split",
    )(h, wt)


def _mm_t_tiled_kernel(h_ref, wt_ref, o_ref, *, chunks):
    res = lax.dot_general(wt_ref[...], h_ref[...], NT_DIMS, preferred_element_type=F32)
    for c in range(chunks):
        o_ref[c] = res[:, c * TILE:(c + 1) * TILE].astype(o_ref.dtype)


def _matmul_t_tiled(h, wt, l, tm, tn):
    s, d = h.shape
    n = wt.shape[1]
    chunks = tm // TILE
    return pl.pallas_call(
        functools.partial(_mm_t_tiled_kernel, chunks=chunks),
        grid=(s // tm, n // tn),
        in_specs=[pl.BlockSpec((tm, d), lambda i, j: (i, 0)),
                  pl.BlockSpec((None, tn, d), lambda i, j: (l, j, 0))],
        out_specs=pl.BlockSpec((chunks, tn, TILE), lambda i, j: (i, j, 0)),
        out_shape=jax.ShapeDtypeStruct((s // TILE, n, TILE), BF16),
        compiler_params=_cparams(("parallel", "arbitrary")),
        name="proj_cols_tiled",
    )(h, wt)


def _fox_prep_kernel(h_ref, w_ref, b_ref, qa_ref, ka_ref, st_ref, carry_ref, *, tm):
    @pl.when(pl.program_id(0) == 0)
    def _():
        carry_ref[...] = jnp.zeros_like(carry_ref)

    small = lax.dot_general(h_ref[...], w_ref[...], NT_DIMS,
                            preferred_element_type=F32)
    st_ref[...] = small.T
    v = small + b_ref[...]
    c = jnp.minimum(v, 0.0) - jnp.log1p(jnp.exp(-jnp.abs(v)))
    row = lax.broadcasted_iota(jnp.int32, (tm, TILE), 0)
    sh = 1
    while sh < tm:
        c = c + jnp.where(row >= sh, pltpu.roll(c, sh, axis=0), 0.0)
        sh *= 2
    c = c + carry_ref[...]
    carry_ref[...] = c[tm - 1:tm, :]
    c = c * LOG2E

    lane = lax.broadcasted_iota(jnp.int32, (tm, TILE), 1)
    for h in range(H_FOX):
        ch = jnp.broadcast_to(c[:, h:h + 1], (tm, TILE))
        hi = ch.astype(BF16).astype(F32)
        r1 = ch - hi
        mid = r1.astype(BF16).astype(F32)
        lo = r1 - mid
        pieces = jnp.where((lane == 0) | (lane == 3), hi,
                           jnp.where((lane == 1) | (lane == 4), mid, lo))
        qa = jnp.where(lane < 3, pieces, jnp.where(lane < 6, 1.0, 0.0))
        ka = jnp.where(lane < 3, 1.0, jnp.where(lane < 6, -pieces, 0.0))
        qa_ref[:, h * TILE:(h + 1) * TILE] = qa.astype(BF16)
        ka_ref[:, h * TILE:(h + 1) * TILE] = ka.astype(BF16)


def _fox_prep(h, w_small, l, b_pad):
    s, d = h.shape
    tm = 512
    out = jax.ShapeDtypeStruct((s, H_FOX * TILE), BF16)
    return pl.pallas_call(
        functools.partial(_fox_prep_kernel, tm=tm),
        grid=(s // tm,),
        in_specs=[pl.BlockSpec((tm, d), lambda i: (i, 0)),
                  pl.BlockSpec((None, TILE, d), lambda i: (l, 0, 0)),
                  pl.BlockSpec((1, TILE), lambda i: (0, 0))],
        out_specs=[pl.BlockSpec((tm, H_FOX * TILE), lambda i: (i, 0)),
                   pl.BlockSpec((tm, H_FOX * TILE), lambda i: (i, 0)),
                   pl.BlockSpec((TILE, tm), lambda i: (0, i))],
        out_shape=[out, out, jax.ShapeDtypeStruct((TILE, s), F32)],
        scratch_shapes=[pltpu.VMEM((1, TILE), F32)],
        compiler_params=_cparams(("arbitrary",)),
        name="fox_prep",
    )(h, w_small, b_pad)


def _fox_kernel(q_ref, qa_ref, k_ref, ka_ref, vt_ref, z_ref, o_ref, s0_ref, s1_ref, s2_ref,
                acc_ref):
    i = pl.program_id(1)
    t = FOX_T
    tk = FOX_TK
    heads = range(FOX_HEADS)
    col = lambda h: slice(h * TILE, (h + 1) * TILE)
    qaug = [jnp.concatenate([q_ref[:, col(h)], qa_ref[:, col(h)]], axis=1) for h in heads]

    def scores(h, j, rows):
        r0 = pl.multiple_of(j * rows, rows)
        kaug = jnp.concatenate([k_ref[pl.ds(r0, rows), col(h)], ka_ref[pl.ds(r0, rows), col(h)]],
                               axis=1)
        return lax.dot_general(kaug, qaug[h], NT_DIMS, preferred_element_type=F32)

    def values_t(h, j, rows):
        n_sub = rows // TILE
        return jnp.concatenate(
            [jnp.concatenate([vt_ref[n_sub * j + c, col(h), :] for c in range(n_sub)], axis=1),
             jnp.ones((ONES_ROWS, rows), BF16)], axis=0)

    kio = lax.broadcasted_iota(jnp.int32, (t, t), 0)
    qio = lax.broadcasted_iota(jnp.int32, (t, t), 1)
    ms = []
    for h in heads:
        s = jnp.where(kio <= qio, scores(h, i, t), NEG_BIG)
        m = jnp.max(s, axis=0, keepdims=True)
        p = jnp.exp2(s - m)
        ms.append(m)
        acc_ref[h] = jnp.dot(values_t(h, i, t), p.astype(BF16), preferred_element_type=F32)

    def update(h, s, vt, m):
        m_new = jnp.maximum(m, jnp.max(s, axis=0, keepdims=True))
        alpha = jnp.exp2(m - m_new)
        p = jnp.exp2(s - m_new)
        acc_ref[h] = alpha * acc_ref[h] + jnp.dot(vt, p.astype(BF16), preferred_element_type=F32)
        return m_new

    bufs = (s0_ref, s1_ref, s2_ref)
    for h in heads:
        s0_ref[h] = scores(h, 0, tk)

    def triple(pp, ms):
        ms = list(ms)
        for stage in range(3):
            cur, nxt = bufs[stage], bufs[(stage + 1) % 3]
            for h in heads:
                nxt[h] = scores(h, 3 * pp + stage + 1, tk)
            for h in heads:
                ms[h] = update(h, cur[h], values_t(h, 3 * pp + stage, tk), ms[h])
        return tuple(ms)

    n_far = i * (t // tk)
    ms = lax.fori_loop(0, n_far // 3, triple, tuple(ms))
    done = (n_far // 3) * 3

    @pl.when(n_far - done == 1)
    def _():
        for h in heads:
            update(h, s0_ref[h], values_t(h, done, tk), ms[h])

    @pl.when(n_far - done == 2)
    def _():
        for h in heads:
            s1_ref[h] = scores(h, done + 1, tk)
        for h in heads:
            m_mid = update(h, s0_ref[h], values_t(h, done, tk), ms[h])
            update(h, s1_ref[h], values_t(h, done + 1, tk), m_mid)

    for h in heads:
        o = (acc_ref[h, 0:HD, :] * (1.0 / acc_ref[h, HD:HD + 1, :])).T
        o_ref[:, col(h)] = (o * _silu(z_ref[:, col(h)].astype(F32))).astype(BF16)


def _fox_attention(proj, qa, ka, proj_t):
    s = proj.shape[0]
    t = FOX_T
    nk = s // TILE
    hw = FOX_HEADS * TILE
    score_buf = pltpu.VMEM((FOX_HEADS, FOX_TK, t), F32)
    return pl.pallas_call(
        _fox_kernel,
        grid=(H_FOX // FOX_HEADS, s // t),
        scratch_shapes=[score_buf, score_buf, score_buf,
                        pltpu.VMEM((FOX_HEADS, HD + ONES_ROWS, t), F32)],
        in_specs=[pl.BlockSpec((t, hw), lambda h, i: (i, BLK_QF // FOX_HEADS + h)),
                  pl.BlockSpec((t, hw), lambda h, i: (i, h)),
                  pl.BlockSpec((s, hw), lambda h, i: (0, BLK_KF // FOX_HEADS + h)),
                  pl.BlockSpec((s, hw), lambda h, i: (0, h)),
                  pl.BlockSpec((nk, hw, TILE), lambda h, i: (0, h, 0)),
                  pl.BlockSpec((t, hw), lambda h, i: (i, BLK_ZF // FOX_HEADS + h))],
        out_specs=pl.BlockSpec((t, hw), lambda h, i: (i, h)),
        out_shape=jax.ShapeDtypeStruct((s, D_FOX), BF16),
        compiler_params=_cparams(("parallel", "parallel")),
        name="fox_attention",
    )(proj, qa, proj, ka, proj_t, proj)


def _compress_kernel(x_ref, w1_ref, pe_ref, w2_ref, o_ref, ot_ref):
    n = x_ref.shape[0]
    ab = jnp.dot(x_ref[...], w1_ref[0], preferred_element_type=F32)
    pe = jnp.dot(pe_ref[0], w1_ref[0], preferred_element_type=F32)
    a = ab[:, :CMP_HIDDEN]
    b_next = pltpu.roll(ab[:, CMP_HIDDEN:], n - 1, axis=0)
    hid = a + b_next + pe[0:1, :CMP_HIDDEN] + pe[8:9, CMP_HIDDEN:]
    out = jnp.dot(_silu(hid).astype(BF16), w2_ref[0], preferred_element_type=F32)
    o_ref[...] = out.astype(BF16)
    ot_ref[...] = out.T.astype(BF16)


def _compress(x4, w1cat, pe16, w2, l):
    _, n, k = x4.shape
    kind = lambda c: (l, c // NSA_G, 0, 0)
    return pl.pallas_call(
        _compress_kernel,
        grid=(2 * NSA_G,),
        in_specs=[pl.BlockSpec((None, n, k), lambda c: (c, 0, 0)),
                  pl.BlockSpec((None, 1, k, 2 * CMP_HIDDEN), kind),
                  pl.BlockSpec((None, 1, 16, k), kind),
                  pl.BlockSpec((None, 1, CMP_HIDDEN, HD), kind)],
        out_specs=[pl.BlockSpec((None, n, HD), lambda c: (c, 0, 0)),
                   pl.BlockSpec((None, HD, n), lambda c: (c, 0, 0))],
        out_shape=[jax.ShapeDtypeStruct((2 * NSA_G, n, HD), BF16),
                   jax.ShapeDtypeStruct((2 * NSA_G, HD, n), BF16)],
        compiler_params=_cparams(("parallel",)),
        name="nsa_compress",
    )(x4, w1cat, pe16, w2)


def _nsa_kernel(*refs, n_cmp):
    q_refs, z_refs = refs[0:H_NSA], refs[H_NSA:2 * H_NSA]
    (ks_ref, kw_ref, vst_ref, vwt_ref, kc_ref, vct_ref, c2s_ref, cstrip_ref, t0_ref, t1_ref,
     gl_ref, o_ref, sc_ref, sel_ref, m_s, acc_s, acc_w, ss_ref, sw_ref, sa_ref, sb_ref,
     sc3_ref) = refs[2 * H_NSA:]
    i = pl.program_id(0)
    w3 = NSA_R * TILE
    gw = NSA_G * TILE
    groups = range(NSA_G)
    col = lambda g: slice(g * TILE, (g + 1) * TILE)
    qs = [jnp.concatenate([q_refs[NSA_R * g + r][...] for r in range(NSA_R)], axis=0)
          for g in groups]
    b_io = lax.broadcasted_iota(jnp.int32, (TILE, w3), 0)
    a_io = lax.broadcasted_iota(jnp.int32, (TILE, w3), 1) & (TILE - 1)
    tpc = NSA_CHUNK // TILE
    bpc = NSA_CHUNK // SEL_LEN

    def key_scores(k_ref, g, tile0, n_tiles):
        r0 = pl.multiple_of(tile0 * TILE, TILE)
        return lax.dot_general(k_ref[pl.ds(r0, n_tiles * TILE), col(g)], qs[g], NT_DIMS,
                               preferred_element_type=F32)

    def values_t(vt_ref, g, tile0, n_tiles):
        return jnp.concatenate(
            [jnp.concatenate([vt_ref[tile0 + c, col(g), :] for c in range(n_tiles)], axis=1),
             jnp.ones((ONES_ROWS, n_tiles * TILE), BF16)], axis=0)

    def fix_near(s_ref, g, d):
        own = pl.ds(pl.multiple_of(d * TILE, TILE), TILE)
        prev = pl.ds(pl.multiple_of(jnp.maximum(d - 1, 0) * TILE, TILE), TILE)
        s_ref[g, prev, :] = s_ref[g, prev, :] + jnp.where(d >= 1, t1_ref[g], 0.0)
        s_ref[g, own, :] = jnp.where(b_io <= a_io, s_ref[g, own, :] + t0_ref[g], NEG_BIG)

    w0 = pl.multiple_of(8 * i, 8)
    n_io = lax.broadcasted_iota(jnp.int32, (n_cmp, w3), 0)
    qa_io = lax.broadcasted_iota(jnp.int32, (n_cmp, w3), 1) & (TILE - 1)
    valid = (CMP_STRIDE * n_io + (CMP_LEN - 1)) <= (TILE * i + qa_io)
    n_wt = WINDOW // TILE + 1
    w_tile0 = jnp.maximum(i - WINDOW // TILE, 0)
    d_win = i - w_tile0
    for g in groups:
        sc_ref[g, 0:16, :] = jnp.zeros((16, w3), F32)
        sc_ref[g, 16:16 + n_cmp, :] = lax.dot_general(kc_ref[g], qs[g], NT_DIMS,
                                                      preferred_element_type=F32)
        sw_ref[g] = key_scores(kw_ref, g, w_tile0, n_wt)
    oc_t, psums = [], []
    for g in groups:
        sc_ref[g, pl.ds(w0, 24), :] = sc_ref[g, pl.ds(w0, 24), :] + cstrip_ref[g]
        scm = jnp.where(valid, sc_ref[g, 16:16 + n_cmp, :], NEG_BIG)
        mc = jnp.max(scm, axis=0, keepdims=True)
        pc = jnp.where(valid, jnp.exp2(scm - mc), 0.0)
        lc = jnp.sum(pc, axis=0, keepdims=True)
        pcn = pc * jnp.where(lc > 0.0, 1.0 / lc, 0.0)
        oc_t.append(jnp.dot(vct_ref[g], pcn.astype(BF16), preferred_element_type=F32))
        psums.append(pcn[:, 0:TILE] + pcn[:, TILE:2 * TILE] + pcn[:, 2 * TILE:3 * TILE])

        for c in range(1, n_wt):
            rows = slice(c * TILE, (c + 1) * TILE)
            sw_ref[g, rows, :] = jnp.where(d_win < c, NEG_BIG, sw_ref[g, rows, :])
        fix_near(sw_ref, g, d_win)
        sw_ref[g, 0:TILE, :] = jnp.where((a_io < b_io) | (i < WINDOW // TILE),
                                         sw_ref[g, 0:TILE, :], NEG_BIG)
        sw = sw_ref[g]
        p_w = jnp.exp2(sw - jnp.max(sw, axis=0, keepdims=True))
        acc_w[g] = jnp.dot(values_t(vwt_ref, g, w_tile0, n_wt), p_w.astype(BF16),
                           preferred_element_type=F32)

    psum = jnp.concatenate(psums, axis=1)
    p_hi = psum.astype(BF16)
    p_lo = (psum - p_hi.astype(F32)).astype(BF16)
    c2s = c2s_ref[...]
    imp = (jnp.dot(c2s, p_hi, preferred_element_type=F32)
           + jnp.dot(c2s, p_lo, preferred_element_type=F32))
    n_sel = imp.shape[0]
    m_io = lax.broadcasted_iota(jnp.int32, (n_sel, gw), 0)
    m_f = m_io.astype(F32)
    qpos = TILE * i + (lax.broadcasted_iota(jnp.int32, (n_sel, gw), 1) & (TILE - 1))
    own = qpos >> 6
    forced = (m_io == 0) | (m_io == own) | (m_io == own - 1)
    eligible = SEL_LEN * m_io <= qpos
    sel = jnp.where(eligible & forced, 1.0, 0.0)
    cur = jnp.where(eligible, jnp.where(forced, -2.0, imp), -1.0)
    for _ in range(min(SEL_TOPK, n_sel) - N_FORCED):
        mx = jnp.max(cur, axis=0, keepdims=True)
        idx = jnp.min(jnp.where(cur == mx, m_f, float(n_sel)), axis=0, keepdims=True)
        pick = m_f == idx
        sel = jnp.where(pick, jnp.maximum(jnp.where(mx >= 0.0, 1.0, 0.0), sel), sel)
        cur = jnp.where(pick, -2.0, cur)
    sel_ref[...] = sel

    def sel_mask(g, c):
        rows = sel_ref[pl.ds(pl.multiple_of(c * bpc, 8), bpc), col(g)]
        mk = jnp.concatenate([jnp.broadcast_to(rows[b:b + 1, :], (SEL_LEN, TILE))
                              for b in range(bpc)], axis=0)
        return jnp.concatenate([mk, mk, mk], axis=1) > 0.5

    def sel_update(g, c, s):
        s = jnp.where(sel_mask(g, c), s, NEG_BIG)
        m_old = m_s[g]
        m_new = jnp.maximum(m_old, jnp.max(s, axis=0, keepdims=True))
        alpha = jnp.exp2(m_old - m_new)
        p = jnp.exp2(s - m_new)
        acc_s[g] = alpha * acc_s[g] + jnp.dot(values_t(vst_ref, g, c * tpc, tpc), p.astype(BF16),
                                              preferred_element_type=F32)
        m_s[g] = m_new

    m_s[...] = jnp.full(m_s.shape, NEG_BIG, F32)
    acc_s[...] = jnp.zeros(acc_s.shape, F32)

    c_own = i // tpc
    c_prev = jnp.maximum(i - 1, 0) // tpc

    bufs = (sa_ref, sb_ref, sc3_ref)
    for g in groups:
        sa_ref[g] = key_scores(ks_ref, g, 0, tpc)

    def far_triple(pp, carry):
        for stage in range(3):
            cur_buf, nxt_buf = bufs[stage], bufs[(stage + 1) % 3]
            for g in groups:
                nxt_buf[g] = key_scores(ks_ref, g, (3 * pp + stage + 1) * tpc, tpc)
            for g in groups:
                sel_update(g, 3 * pp + stage, cur_buf[g])
        return carry

    lax.fori_loop(0, c_prev // 3, far_triple, 0)
    done = (c_prev // 3) * 3

    @pl.when(c_prev - done == 1)
    def _():
        for g in groups:
            sel_update(g, done, sa_ref[g])

    @pl.when(c_prev - done == 2)
    def _():
        for g in groups:
            sb_ref[g] = key_scores(ks_ref, g, (done + 1) * tpc, tpc)
        for g in groups:
            sel_update(g, done, sa_ref[g])
            sel_update(g, done + 1, sb_ref[g])

    @pl.when(c_prev != c_own)
    def _():
        last = slice((tpc - 1) * TILE, tpc * TILE)
        for g in groups:
            ss_ref[g] = key_scores(ks_ref, g, c_prev * tpc, tpc)
            ss_ref[g, last, :] = ss_ref[g, last, :] + t1_ref[g]
            sel_update(g, c_prev, ss_ref[g])

    for g in groups:
        ss_ref[g] = key_scores(ks_ref, g, c_own * tpc, tpc)
    for g in groups:
        fix_near(ss_ref, g, i - c_own * tpc)
        sel_update(g, c_own, ss_ref[g])

    for g in groups:
        gates = jax.nn.sigmoid(gl_ref[g])

        def gate_row(branch):
            return jnp.concatenate([gates[r * 3 + branch:r * 3 + branch + 1, :]
                                    for r in range(NSA_R)], axis=1)

        out_t = (oc_t[g] * gate_row(0)
                 + acc_s[g, 0:HD, :] * (gate_row(1) / acc_s[g, HD:HD + 1, :])
                 + acc_w[g, 0:HD, :] * (gate_row(2) / acc_w[g, HD:HD + 1, :]))
        for r in range(NSA_R):
            h = NSA_R * g + r
            o_r = out_t[:, r * TILE:(r + 1) * TILE].T
            o_ref[:, h * TILE:(h + 1) * TILE] = (
                o_r * _silu(z_refs[h][...].astype(F32))).astype(BF16)


def _nsa_attention(proj, proj_t, cmp_n, cmp_t, c2s_t, cstrip, t0, t1, glog_t):
    s = proj.shape[0]
    nk = s // TILE
    n_cmp = cmp_n.shape[1]
    n_sel = s // SEL_LEN
    w3 = NSA_R * TILE
    gw = NSA_G * TILE
    qspec = lambda h: pl.BlockSpec((TILE, TILE), lambda i, h=h: (i, BLK_QN + h))
    zspec = lambda h: pl.BlockSpec((TILE, TILE), lambda i, h=h: (i, BLK_ZN + h))
    whole = lambda shape: pl.BlockSpec(shape, lambda i: (0,) * len(shape))
    resident = dict(pipeline_mode=pl.Buffered(1))
    chunk_buf = pltpu.VMEM((NSA_G, NSA_CHUNK, w3), F32)
    acc_buf = pltpu.VMEM((NSA_G, HD + ONES_ROWS, w3), F32)
    return pl.pallas_call(
        functools.partial(_nsa_kernel, n_cmp=n_cmp),
        grid=(s // TILE,),
        in_specs=[qspec(h) for h in range(H_NSA)] + [zspec(h) for h in range(H_NSA)] + [
            pl.BlockSpec((s, gw), lambda i: (0, BLK_KS // NSA_G), **resident),
            pl.BlockSpec((s, gw), lambda i: (0, BLK_KW // NSA_G), **resident),
            pl.BlockSpec((nk, gw, TILE), lambda i: (0, H_FOX // NSA_G, 0), **resident),
            pl.BlockSpec((nk, gw, TILE), lambda i: (0, H_FOX // NSA_G + 1, 0), **resident),
            pl.BlockSpec((NSA_G, n_cmp, HD), lambda i: (0, 0, 0)),
            pl.BlockSpec((NSA_G, HD, n_cmp), lambda i: (1, 0, 0)),
            whole((n_sel, n_cmp)), whole((NSA_G, 24, w3)), whole((NSA_G, TILE, w3)),
            whole((NSA_G, TILE, w3)),
            pl.BlockSpec((NSA_G, NSA_R * 3, TILE), lambda i: (0, 0, i))],
        out_specs=pl.BlockSpec((TILE, D_NSA), lambda i: (i, 0)),
        out_shape=jax.ShapeDtypeStruct((s, D_NSA), BF16),
        scratch_shapes=[pltpu.VMEM((NSA_G, 16 + n_cmp + 8, w3), F32),
                        pltpu.VMEM((n_sel, gw), F32),
                        pltpu.VMEM((NSA_G, 1, w3), F32),
                        acc_buf, acc_buf,
                        chunk_buf,
                        pltpu.VMEM((NSA_G, WINDOW + TILE, w3), F32),
                        chunk_buf, chunk_buf, chunk_buf],
        compiler_params=_cparams(("parallel",)),
        name="nsa_attention",
    )(*([proj] * (2 * H_NSA)), proj, proj, proj_t, proj_t,
      cmp_n, cmp_t, c2s_t, cstrip, t0, t1, glog_t)


def _conv_kernel(u_ref, gb_ref, gc_ref, z_ref, uh_ref, gch_ref, w_ref, o_ref, *, tm):
    i = pl.program_id(0)
    y = gc_ref[...].astype(F32) * u_ref[...].astype(F32)
    yh = gch_ref[...].astype(F32) * uh_ref[...].astype(F32)
    yh = jnp.where(i > 0, yh, 0.0)
    h1 = yh[15:16, :]
    h2 = yh[14:15, :]
    row = lax.broadcasted_iota(jnp.int32, (tm, D_CONV), 0)
    y1 = jnp.where(row == 0, h1, pltpu.roll(y, 1, axis=0))
    y2 = jnp.where(row == 0, h2, jnp.where(row == 1, h1, pltpu.roll(y, 2, axis=0)))
    w = w_ref[...]
    conv = w[0:1, :] * y2 + w[1:2, :] * y1 + w[2:3, :] * y
    o_ref[...] = (gb_ref[...].astype(F32) * conv * _silu(z_ref[...].astype(F32))).astype(BF16)


def _conv_branch(proj, conv_w):
    s = proj.shape[0]
    tm = 512
    main = lambda c: pl.BlockSpec((tm, D_CONV), lambda i, c=c: (i, c))
    halo = lambda c: pl.BlockSpec((16, D_CONV), lambda i, c=c: (jnp.maximum(i * (tm // 16) - 1, 0), c))
    return pl.pallas_call(
        functools.partial(_conv_kernel, tm=tm),
        grid=(s // tm,),
        in_specs=[main(0), main(1), main(2), main(3), halo(0), halo(2),
                  pl.BlockSpec((CONV_WIDTH, D_CONV), lambda i: (0, 0))],
        out_specs=pl.BlockSpec((tm, D_CONV), lambda i: (i, 0)),
        out_shape=jax.ShapeDtypeStruct((s, D_CONV), BF16),
        compiler_params=_cparams(("parallel",)),
        name="conv_branch",
    )(proj, proj, proj, proj, proj, proj, conv_w)


def _out_kernel(ya_ref, yf_ref, yn_ref, w_ref, x_ref, pg_ref, gate_ref, *rest, with_next):
    y = (jnp.dot(ya_ref[...], w_ref[0:D_CONV, :], preferred_element_type=F32)
         + jnp.dot(yf_ref[...], w_ref[D_CONV:D_CONV + D_FOX, :], preferred_element_type=F32)
         + jnp.dot(yn_ref[...], w_ref[D_CONV + D_FOX:, :], preferred_element_type=F32))
    yn = y * lax.rsqrt(jnp.mean(y * y, axis=-1, keepdims=True) + NORM_EPS) * pg_ref[...]
    xn = x_ref[...] + gate_ref[...] * yn
    if with_next:
        ng_ref, nsc_ref, nsh_ref, xo_ref, h_ref = rest
        h_ref[...] = _modulated_norm(xn, ng_ref[...], nsc_ref[...], nsh_ref[...]).astype(BF16)
    else:
        (xo_ref,) = rest
    xo_ref[...] = xn


def _out_proj(ya, yf, yn, w_out, l, x2, post_g, gate, nxt):
    s, d = x2.shape
    tm = 512
    rows = lambda n: pl.BlockSpec((tm, n), lambda i: (i, 0))
    vec = pl.BlockSpec((1, d), lambda i: (0, 0))
    in_specs = [rows(D_CONV), rows(D_FOX), rows(D_NSA),
                pl.BlockSpec((None, d, d), lambda i: (l, 0, 0), pipeline_mode=pl.Buffered(1)),
                rows(d), vec, vec]
    args = [ya, yf, yn, w_out, x2, post_g, gate]
    out_specs = [rows(d)]
    out_shape = [jax.ShapeDtypeStruct((s, d), F32)]
    if nxt is not None:
        in_specs += [vec, vec, vec]
        args += list(nxt)
        out_specs.append(rows(d))
        out_shape.append(jax.ShapeDtypeStruct((s, d), BF16))
    res = pl.pallas_call(
        functools.partial(_out_kernel, with_next=nxt is not None),
        grid=(s // tm,),
        in_specs=in_specs,
        out_specs=out_specs,
        out_shape=out_shape,
        compiler_params=_cparams(("parallel",)),
        name="out_proj",
    )(*args)
    return res if nxt is not None else (res[0], None)


def _bucket_of_distance():
    max_exact = REL_BUCKETS // 2
    d = np.arange(REL_MAX_DIST)
    nf = np.maximum(d, max_exact).astype(np.float32)
    large = max_exact + (np.log(nf / np.float32(max_exact)) / np.float32(math.log(REL_MAX_DIST / max_exact))
                         * np.float32(REL_BUCKETS - max_exact)).astype(np.int32)
    return np.where(d < max_exact, d, np.minimum(large, REL_BUCKETS - 1))


def _distance_tables():
    b = np.arange(TILE)[:, None]
    a = np.arange(TILE)[None, :]
    none = REL_MAX_DIST
    d0 = np.where(a >= b, a - b, none)
    d1 = np.where(a < b, TILE + a - b, none)
    m = np.arange(24)[:, None]
    dc = a - CMP_STRIDE * m + (2 * TILE - (CMP_LEN - 1))
    dc = np.where((dc >= 0) & (dc < REL_MAX_DIST), dc, none)
    return d0, d1, dc


def _bias_tables(rel_bias):
    bucket = np.concatenate([_bucket_of_distance(), [REL_BUCKETS - 1]])
    rel = (rel_bias - rel_bias[REL_BUCKETS - 1:REL_BUCKETS, :]) * LOG2E
    rel = rel.reshape(REL_BUCKETS, NSA_G, NSA_R)

    def expand(idx):
        bk = bucket[idx]
        tab = jnp.zeros((NSA_G, idx.shape[0], NSA_R, TILE), F32)
        for b in range(REL_BUCKETS - 1):
            hit = jnp.asarray(bk == b)[None, :, None, :]
            tab = jnp.where(hit, rel[b][:, None, :, None], tab)
        return tab.reshape(NSA_G, idx.shape[0], NSA_R * TILE)

    d0, d1, dc = _distance_tables()
    return expand(d0), expand(d1), expand(dc)


def _cmp_to_sel_t(s, n_cmp_pad):
    n_cmp = (s - CMP_LEN) // CMP_STRIDE + 1
    n_sel = s // SEL_LEN
    c_lo = np.arange(n_cmp_pad)[None, :] * CMP_STRIDE
    sel_start = np.arange(n_sel)[:, None] * SEL_LEN
    overlap = (c_lo < sel_start + SEL_LEN) & (c_lo + CMP_LEN > sel_start)
    overlap &= np.arange(n_cmp_pad)[None, :] < n_cmp
    return jnp.asarray(overlap, dtype=BF16)


def _pack_w_in(w_in):
    sizes = ([D_CONV] * 4 + [D_FOX] * 3 + [H_FOX, D_FOX] + [D_NSA] + [2 * HD] * 6
             + [3 * H_NSA, D_NSA])
    offs = np.concatenate([[0], np.cumsum(sizes)])
    names = ["u", "gb", "gc", "za", "qf", "kf", "vf", "ff", "zf", "qn", "kc", "vc", "ks", "vs",
             "kw", "vw", "gn", "zn"]
    wt = jnp.swapaxes(w_in, 1, 2)
    col = {n: wt[:, int(offs[k]):int(offs[k + 1]), :] for k, n in enumerate(names)}
    scale = HD ** -0.5 * LOG2E
    w_a = jnp.concatenate([col["u"], col["gb"], col["gc"], col["za"],
                           col["qn"] * scale, col["zn"], col["ks"], col["kw"],
                           col["qf"] * scale, col["kf"], col["zf"]], axis=1).astype(BF16)
    w_kv = jnp.concatenate([col["kc"], col["vc"]], axis=1).astype(BF16)
    w_v = jnp.concatenate([col["vf"], col["vs"], col["vw"]], axis=1).astype(BF16)
    small = jnp.concatenate([col["ff"], col["gn"]], axis=1)
    small = jnp.pad(small, ((0, 0), (0, TILE - small.shape[1]), (0, 0))).astype(BF16)
    return w_a, w_kv, w_v, small


def kernel(x, c, w_ada, b_ada, pre_norm, post_norm, w_in, b_forget, conv_w,
           cmp_pe_k, cmp_w1_k, cmp_w2_k, cmp_pe_v, cmp_w1_v, cmp_w2_v, w_out, rel_bias):
    bsz, s, d = x.shape
    assert bsz == 1 and d == D_MODEL and s % 1024 == 0 and s >= 2 * WINDOW
    depth = w_in.shape[0]
    x2 = x.reshape(s, d)
    n_cmp_pad = s // CMP_STRIDE

    mod = _ada_mod(jnp.broadcast_to(c, (8, d)), w_ada, b_ada)[:, 0:1, :]
    shift, scale, gate = mod[:, :, :d], mod[:, :, d:2 * d], mod[:, :, 2 * d:]

    w_a, w_kv, w_v, w_small = _pack_w_in(w_in)
    w_out_b = w_out.astype(BF16)
    half = CMP_LEN * HD // 2
    w1cat = jnp.stack([jnp.concatenate([w[:, :half], w[:, half:]], axis=-1)
                       for w in (cmp_w1_k, cmp_w1_v)], axis=1).astype(BF16)
    pe = jnp.stack([cmp_pe_k, cmp_pe_v], axis=1).reshape(depth, 2, 2, 1, half)
    pe16 = jnp.broadcast_to(pe, (depth, 2, 2, 8, half)).reshape(depth, 2, 16, half).astype(BF16)
    w2 = jnp.stack([cmp_w2_k, cmp_w2_v], axis=1).astype(BF16)
    b_pad = jnp.pad(b_forget, ((0, 0), (0, TILE - H_FOX))).reshape(depth, 1, TILE)
    t0, t1, cstrip = _bias_tables(rel_bias)
    c2s_t = _cmp_to_sel_t(s, n_cmp_pad)

    h = _prenorm(x2, pre_norm[0:1], scale[0], shift[0])
    for l in range(depth):
        proj = _matmul(h, w_a, l, 1024, 1280)
        kcvc = _matmul_split(h, w_kv, l, 1024)
        proj_t = _matmul_t_tiled(h, w_v, l, 1024, 256)

        ya = _conv_branch(proj, conv_w[l])

        qa, ka, small_t = _fox_prep(h, w_small, l, b_pad[l])
        yf = _fox_attention(proj, qa, ka, proj_t)

        x4 = kcvc.reshape(2 * NSA_G, n_cmp_pad, CMP_STRIDE * HD)
        cmp_n, cmp_t = _compress(x4, w1cat, pe16, w2, l)
        glog_t = small_t[H_FOX:H_FOX + 3 * H_NSA].reshape(NSA_G, NSA_R * 3, s)
        yn = _nsa_attention(proj, proj_t, cmp_n, cmp_t, c2s_t, cstrip, t0, t1, glog_t)

        nxt = None
        if l + 1 < depth:
            nxt = (pre_norm[l + 1:l + 2], scale[l + 1], shift[l + 1])
        x2, h = _out_proj(ya, yf, yn, w_out_b, l, x2, post_norm[l:l + 1], gate[l], nxt)
    return x2.reshape(bsz, s, d)
```

```python
import functools
import math

import numpy as np
import jax
import jax.numpy as jnp
from jax import lax
from jax.experimental import pallas as pl
from jax.experimental.pallas import tpu as pltpu

F32 = jnp.float32
BF16 = jnp.bfloat16

D_MODEL = 2048
HD = 128
D_CONV = 512
D_FOX = 768
D_NSA = 768
H_FOX = 6
H_NSA = 6
NSA_G = 2
NSA_R = 3
CONV_WIDTH = 3
CMP_LEN = 32
CMP_STRIDE = 16
CMP_HIDDEN = 256
SEL_LEN = 64
SEL_TOPK = 16
N_FORCED = 3
WINDOW = 512
REL_BUCKETS = 32
REL_MAX_DIST = 128
NORM_EPS = 1e-6
NEG_BIG = -1e30
LOG2E = math.log2(math.e)

TILE = 128
ONES_ROWS = 16
NSA_CHUNK = 512
FOX_T = 512
FOX_TK = 512
FOX_HEADS = 2
PROJ_TM = 1024
PROJ_TN = 1280
PROJ_TV = 256
VMEM_LIMIT = 56 * 1024 * 1024

NT_DIMS = (((1,), (1,)), ((), ()))

BLK_QN, BLK_ZN, BLK_KS, BLK_KW = 16, 22, 28, 30
BLK_QF, BLK_KF, BLK_ZF = 32, 38, 44
N_A = 6400


def _cparams(sem, vmem=VMEM_LIMIT):
    return pltpu.CompilerParams(dimension_semantics=sem, vmem_limit_bytes=vmem)


def _silu(v):
    return v * jax.nn.sigmoid(v)


def _mod_kernel(c_ref, w_ref, b_ref, o_ref):
    ca = _silu(c_ref[...])
    o_ref[0] = jnp.sum(w_ref[0] * ca, axis=0, keepdims=True) + b_ref[0]


def _ada_mod(c_col, w_ada, b_ada):
    depth, d, n = w_ada.shape
    tn = 1536
    return pl.pallas_call(
        _mod_kernel,
        grid=(depth, n // tn),
        in_specs=[pl.BlockSpec((d, 1), lambda l, j: (0, 0)),
                  pl.BlockSpec((1, d, tn), lambda l, j: (l, 0, j)),
                  pl.BlockSpec((1, 1, tn), lambda l, j: (l, 0, j))],
        out_specs=pl.BlockSpec((1, 1, tn), lambda l, j: (l, 0, j)),
        out_shape=jax.ShapeDtypeStruct((depth, 1, n), F32),
        compiler_params=_cparams(("parallel", "parallel")),
        name="ada_mod",
    )(c_col, w_ada, b_ada.reshape(depth, 1, n))


def _modulated_norm(x, g, scale, shift):
    y = x * lax.rsqrt(jnp.mean(x * x, axis=-1, keepdims=True) + NORM_EPS) * g
    return y * (1.0 + scale) + shift


def _prenorm_kernel(x_ref, g_ref, sc_ref, sh_ref, h_ref):
    h_ref[...] = _modulated_norm(x_ref[...], g_ref[...], sc_ref[...], sh_ref[...]).astype(BF16)


def _prenorm(x2, g, scale, shift):
    s, d = x2.shape
    tm = 512
    vec = pl.BlockSpec((1, d), lambda i: (0, 0))
    return pl.pallas_call(
        _prenorm_kernel,
        grid=(s // tm,),
        in_specs=[pl.BlockSpec((tm, d), lambda i: (i, 0)), vec, vec, vec],
        out_specs=pl.BlockSpec((tm, d), lambda i: (i, 0)),
        out_shape=jax.ShapeDtypeStruct((s, d), BF16),
        compiler_params=_cparams(("parallel",)),
        name="prenorm",
    )(x2, g, scale, shift)


def _proj_kernel(h_ref, wa_ref, wv_ref, wkv_ref, rows_ref, cols_ref, slabs_ref, *, n_a, n_v):
    j = pl.program_id(1)
    h = h_ref[...]

    @pl.when(j < n_a)
    def _():
        rows_ref[...] = lax.dot_general(h, wa_ref[...], NT_DIMS,
                                        preferred_element_type=F32).astype(BF16)

    @pl.when((j >= n_a) & (j < n_a + n_v))
    def _():
        res = lax.dot_general(wv_ref[...], h, NT_DIMS, preferred_element_type=F32)
        for c in range(cols_ref.shape[0]):
            cols_ref[c] = res[:, c * TILE:(c + 1) * TILE].astype(BF16)

    @pl.when(j == n_a + n_v)
    def _():
        res = lax.dot_general(h, wkv_ref[...], NT_DIMS, preferred_element_type=F32)
        for c in range(slabs_ref.shape[0]):
            slabs_ref[c] = res[:, c * TILE:(c + 1) * TILE].astype(BF16)


def _project(h, wa, wv, wkv, l):
    s, d = h.shape
    tm = PROJ_TM
    n_a, n_v = wa.shape[1] // PROJ_TN, wv.shape[1] // PROJ_TV
    n_kv = wkv.shape[1]
    a_idx = lambda j: jnp.minimum(j, n_a - 1)
    v_idx = lambda j: jnp.clip(j - n_a, 0, n_v - 1)
    return pl.pallas_call(
        functools.partial(_proj_kernel, n_a=n_a, n_v=n_v),
        grid=(s // tm, n_a + n_v + 1),
        in_specs=[pl.BlockSpec((tm, d), lambda i, j: (i, 0)),
                  pl.BlockSpec((None, PROJ_TN, d), lambda i, j: (l, a_idx(j), 0)),
                  pl.BlockSpec((None, PROJ_TV, d), lambda i, j: (l, v_idx(j), 0)),
                  pl.BlockSpec((None, n_kv, d), lambda i, j: (l, 0, 0))],
        out_specs=[pl.BlockSpec((tm, PROJ_TN), lambda i, j: (i, a_idx(j))),
                   pl.BlockSpec((tm // TILE, PROJ_TV, TILE), lambda i, j: (i, v_idx(j), 0)),
                   pl.BlockSpec((n_kv // TILE, tm, TILE), lambda i, j: (0, i, 0))],
        out_shape=[jax.ShapeDtypeStruct((s, wa.shape[1]), BF16),
                   jax.ShapeDtypeStruct((s // TILE, wv.shape[1], TILE), BF16),
                   jax.ShapeDtypeStruct((n_kv // TILE, s, TILE), BF16)],
        compiler_params=_cparams(("parallel", "arbitrary")),
        name="in_proj",
    )(h, wa, wv, wkv)


def _fox_prep_kernel(h_ref, w_ref, b_ref, qa_ref, ka_ref, st_ref, carry_ref, *, tm):
    @pl.when(pl.program_id(0) == 0)
    def _():
        carry_ref[...] = jnp.zeros_like(carry_ref)

    small = lax.dot_general(h_ref[...], w_ref[...], NT_DIMS,
                            preferred_element_type=F32)
    st_ref[...] = small.T
    v = small + b_ref[...]
    c = jnp.minimum(v, 0.0) - jnp.log1p(jnp.exp(-jnp.abs(v)))
    row = lax.broadcasted_iota(jnp.int32, (tm, TILE), 0)
    sh = 1
    while sh < tm:
        c = c + jnp.where(row >= sh, pltpu.roll(c, sh, axis=0), 0.0)
        sh *= 2
    c = c + carry_ref[...]
    carry_ref[...] = c[tm - 1:tm, :]
    c = c * LOG2E

    lane = lax.broadcasted_iota(jnp.int32, (tm, TILE), 1)
    for h in range(H_FOX):
        ch = jnp.broadcast_to(c[:, h:h + 1], (tm, TILE))
        hi = ch.astype(BF16).astype(F32)
        r1 = ch - hi
        mid = r1.astype(BF16).astype(F32)
        lo = r1 - mid
        pieces = jnp.where((lane == 0) | (lane == 3), hi,
                           jnp.where((lane == 1) | (lane == 4), mid, lo))
        qa = jnp.where(lane < 3, pieces, jnp.where(lane < 6, 1.0, 0.0))
        ka = jnp.where(lane < 3, 1.0, jnp.where(lane < 6, -pieces, 0.0))
        qa_ref[:, h * TILE:(h + 1) * TILE] = qa.astype(BF16)
        ka_ref[:, h * TILE:(h + 1) * TILE] = ka.astype(BF16)


def _fox_prep(h, w_small, l, b_pad):
    s, d = h.shape
    tm = 512
    out = jax.ShapeDtypeStruct((s, H_FOX * TILE), BF16)
    return pl.pallas_call(
        functools.partial(_fox_prep_kernel, tm=tm),
        grid=(s // tm,),
        in_specs=[pl.BlockSpec((tm, d), lambda i: (i, 0)),
                  pl.BlockSpec((None, TILE, d), lambda i: (l, 0, 0)),
                  pl.BlockSpec((1, TILE), lambda i: (0, 0))],
        out_specs=[pl.BlockSpec((tm, H_FOX * TILE), lambda i: (i, 0)),
                   pl.BlockSpec((tm, H_FOX * TILE), lambda i: (i, 0)),
                   pl.BlockSpec((TILE, tm), lambda i: (0, i))],
        out_shape=[out, out, jax.ShapeDtypeStruct((TILE, s), F32)],
        scratch_shapes=[pltpu.VMEM((1, TILE), F32)],
        compiler_params=_cparams(("arbitrary",)),
        name="fox_prep",
    )(h, w_small, b_pad)


def _fox_kernel(q_ref, qa_ref, k_ref, ka_ref, vt_ref, z_ref, o_ref, s0_ref, s1_ref, s2_ref,
                acc_ref):
    i = pl.program_id(1)
    t = FOX_T
    tk = FOX_TK
    heads = range(FOX_HEADS)
    col = lambda h: slice(h * TILE, (h + 1) * TILE)
    qaug = [jnp.concatenate([q_ref[:, col(h)], qa_ref[:, col(h)]], axis=1) for h in heads]

    def scores(h, j, rows):
        r0 = pl.multiple_of(j * rows, rows)
        kaug = jnp.concatenate([k_ref[pl.ds(r0, rows), col(h)], ka_ref[pl.ds(r0, rows), col(h)]],
                               axis=1)
        return lax.dot_general(kaug, qaug[h], NT_DIMS, preferred_element_type=F32)

    def values_t(h, j, rows):
        n_sub = rows // TILE
        return jnp.concatenate(
            [jnp.concatenate([vt_ref[n_sub * j + c, col(h), :] for c in range(n_sub)], axis=1),
             jnp.ones((ONES_ROWS, rows), BF16)], axis=0)

    kio = lax.broadcasted_iota(jnp.int32, (t, t), 0)
    qio = lax.broadcasted_iota(jnp.int32, (t, t), 1)
    ms = []
    for h in heads:
        s = jnp.where(kio <= qio, scores(h, i, t), NEG_BIG)
        m = jnp.max(s, axis=0, keepdims=True)
        p = jnp.exp2(s - m)
        ms.append(m)
        acc_ref[h] = jnp.dot(values_t(h, i, t), p.astype(BF16), preferred_element_type=F32)

    def update(h, s, vt, m):
        m_new = jnp.maximum(m, jnp.max(s, axis=0, keepdims=True))
        alpha = jnp.exp2(m - m_new)
        p = jnp.exp2(s - m_new)
        acc_ref[h] = alpha * acc_ref[h] + jnp.dot(vt, p.astype(BF16), preferred_element_type=F32)
        return m_new

    bufs = (s0_ref, s1_ref, s2_ref)
    for h in heads:
        s0_ref[h] = scores(h, 0, tk)

    def triple(pp, ms):
        ms = list(ms)
        for stage in range(3):
            cur, nxt = bufs[stage], bufs[(stage + 1) % 3]
            for h in heads:
                nxt[h] = scores(h, 3 * pp + stage + 1, tk)
            for h in heads:
                ms[h] = update(h, cur[h], values_t(h, 3 * pp + stage, tk), ms[h])
        return tuple(ms)

    n_far = i * (t // tk)
    ms = lax.fori_loop(0, n_far // 3, triple, tuple(ms))
    done = (n_far // 3) * 3

    @pl.when(n_far - done == 1)
    def _():
        for h in heads:
            update(h, s0_ref[h], values_t(h, done, tk), ms[h])

    @pl.when(n_far - done == 2)
    def _():
        for h in heads:
            s1_ref[h] = scores(h, done + 1, tk)
        for h in heads:
            m_mid = update(h, s0_ref[h], values_t(h, done, tk), ms[h])
            update(h, s1_ref[h], values_t(h, done + 1, tk), m_mid)

    for h in heads:
        o = (acc_ref[h, 0:HD, :] * (1.0 / acc_ref[h, HD:HD + 1, :])).T
        o_ref[:, col(h)] = (o * _silu(z_ref[:, col(h)].astype(F32))).astype(BF16)


def _fox_attention(proj, qa, ka, proj_t):
    s = proj.shape[0]
    t = FOX_T
    nk = s // TILE
    hw = FOX_HEADS * TILE
    score_buf = pltpu.VMEM((FOX_HEADS, FOX_TK, t), F32)
    return pl.pallas_call(
        _fox_kernel,
        grid=(H_FOX // FOX_HEADS, s // t),
        scratch_shapes=[score_buf, score_buf, score_buf,
                        pltpu.VMEM((FOX_HEADS, HD + ONES_ROWS, t), F32)],
        in_specs=[pl.BlockSpec((t, hw), lambda h, i: (i, BLK_QF // FOX_HEADS + h)),
                  pl.BlockSpec((t, hw), lambda h, i: (i, h)),
                  pl.BlockSpec((s, hw), lambda h, i: (0, BLK_KF // FOX_HEADS + h)),
                  pl.BlockSpec((s, hw), lambda h, i: (0, h)),
                  pl.BlockSpec((nk, hw, TILE), lambda h, i: (0, h, 0)),
                  pl.BlockSpec((t, hw), lambda h, i: (i, BLK_ZF // FOX_HEADS + h))],
        out_specs=pl.BlockSpec((t, hw), lambda h, i: (i, h)),
        out_shape=jax.ShapeDtypeStruct((s, D_FOX), BF16),
        compiler_params=_cparams(("parallel", "parallel")),
        name="fox_attention",
    )(proj, qa, proj, ka, proj_t, proj)


def _compress_kernel(x_ref, w1_ref, pe_ref, w2_ref, o_ref, ot_ref):
    n = x_ref.shape[0]
    ab = jnp.dot(x_ref[...], w1_ref[0], preferred_element_type=F32)
    pe = jnp.dot(pe_ref[0], w1_ref[0], preferred_element_type=F32)
    a = ab[:, :CMP_HIDDEN]
    b_next = pltpu.roll(ab[:, CMP_HIDDEN:], n - 1, axis=0)
    hid = a + b_next + pe[0:1, :CMP_HIDDEN] + pe[8:9, CMP_HIDDEN:]
    out = jnp.dot(_silu(hid).astype(BF16), w2_ref[0], preferred_element_type=F32)
    o_ref[...] = out.astype(BF16)
    ot_ref[...] = out.T.astype(BF16)


def _compress(x4, w1cat, pe16, w2, l):
    _, n, k = x4.shape
    kind = lambda c: (l, c // NSA_G, 0, 0)
    return pl.pallas_call(
        _compress_kernel,
        grid=(2 * NSA_G,),
        in_specs=[pl.BlockSpec((None, n, k), lambda c: (c, 0, 0)),
                  pl.BlockSpec((None, 1, k, 2 * CMP_HIDDEN), kind),
                  pl.BlockSpec((None, 1, 16, k), kind),
                  pl.BlockSpec((None, 1, CMP_HIDDEN, HD), kind)],
        out_specs=[pl.BlockSpec((None, n, HD), lambda c: (c, 0, 0)),
                   pl.BlockSpec((None, HD, n), lambda c: (c, 0, 0))],
        out_shape=[jax.ShapeDtypeStruct((2 * NSA_G, n, HD), BF16),
                   jax.ShapeDtypeStruct((2 * NSA_G, HD, n), BF16)],
        compiler_params=_cparams(("parallel",)),
        name="nsa_compress",
    )(x4, w1cat, pe16, w2)


def _nsa_kernel(*refs, n_cmp):
    q_refs, z_refs = refs[0:H_NSA], refs[H_NSA:2 * H_NSA]
    (ks_ref, kw_ref, vst_ref, vwt_ref, kc_ref, vct_ref, c2s_ref, cstrip_ref, t0_ref, t1_ref,
     gl_ref, o_ref, sc_ref, sel_ref, m_s, acc_s, acc_w, ss_ref, sw_ref, sa_ref, sb_ref,
     sc3_ref) = refs[2 * H_NSA:]
    i = pl.program_id(0)
    w3 = NSA_R * TILE
    gw = NSA_G * TILE
    groups = range(NSA_G)
    col = lambda g: slice(g * TILE, (g + 1) * TILE)
    qs = [jnp.concatenate([q_refs[NSA_R * g + r][...] for r in range(NSA_R)], axis=0)
          for g in groups]
    b_io = lax.broadcasted_iota(jnp.int32, (TILE, w3), 0)
    a_io = lax.broadcasted_iota(jnp.int32, (TILE, w3), 1) & (TILE - 1)
    tpc = NSA_CHUNK // TILE
    bpc = NSA_CHUNK // SEL_LEN

    def key_scores(k_ref, g, tile0, n_tiles):
        r0 = pl.multiple_of(tile0 * TILE, TILE)
        return lax.dot_general(k_ref[pl.ds(r0, n_tiles * TILE), col(g)], qs[g], NT_DIMS,
                               preferred_element_type=F32)

    def values_t(vt_ref, g, tile0, n_tiles):
        return jnp.concatenate(
            [jnp.concatenate([vt_ref[tile0 + c, col(g), :] for c in range(n_tiles)], axis=1),
             jnp.ones((ONES_ROWS, n_tiles * TILE), BF16)], axis=0)

    def fix_near(s_ref, g, d):
        own = pl.ds(pl.multiple_of(d * TILE, TILE), TILE)
        prev = pl.ds(pl.multiple_of(jnp.maximum(d - 1, 0) * TILE, TILE), TILE)
        s_ref[g, prev, :] = s_ref[g, prev, :] + jnp.where(d >= 1, t1_ref[g], 0.0)
        s_ref[g, own, :] = jnp.where(b_io <= a_io, s_ref[g, own, :] + t0_ref[g], NEG_BIG)

    w0 = pl.multiple_of(8 * i, 8)
    n_io = lax.broadcasted_iota(jnp.int32, (n_cmp, w3), 0)
    qa_io = lax.broadcasted_iota(jnp.int32, (n_cmp, w3), 1) & (TILE - 1)
    valid = (CMP_STRIDE * n_io + (CMP_LEN - 1)) <= (TILE * i + qa_io)
    n_wt = WINDOW // TILE + 1
    w_tile0 = jnp.maximum(i - WINDOW // TILE, 0)
    d_win = i - w_tile0
    for g in groups:
        sc_ref[g, 0:16, :] = jnp.zeros((16, w3), F32)
        sc_ref[g, 16:16 + n_cmp, :] = lax.dot_general(kc_ref[g], qs[g], NT_DIMS,
                                                      preferred_element_type=F32)
        sw_ref[g] = key_scores(kw_ref, g, w_tile0, n_wt)
    oc_t, psums = [], []
    for g in groups:
        sc_ref[g, pl.ds(w0, 24), :] = sc_ref[g, pl.ds(w0, 24), :] + cstrip_ref[g]
        scm = jnp.where(valid, sc_ref[g, 16:16 + n_cmp, :], NEG_BIG)
        mc = jnp.max(scm, axis=0, keepdims=True)
        pc = jnp.exp2(scm - mc)
        lc = jnp.sum(pc, axis=0, keepdims=True)
        pcn = pc * jnp.where(mc > 0.5 * NEG_BIG, 1.0 / lc, 0.0)
        oc_t.append(jnp.dot(vct_ref[g], pcn.astype(BF16), preferred_element_type=F32))
        psums.append(pcn[:, 0:TILE] + pcn[:, TILE:2 * TILE] + pcn[:, 2 * TILE:3 * TILE])

        for c in range(1, n_wt):
            rows = slice(c * TILE, (c + 1) * TILE)
            sw_ref[g, rows, :] = jnp.where(d_win < c, NEG_BIG, sw_ref[g, rows, :])
        fix_near(sw_ref, g, d_win)
        sw_ref[g, 0:TILE, :] = jnp.where((a_io < b_io) | (i < WINDOW // TILE),
                                         sw_ref[g, 0:TILE, :], NEG_BIG)
        sw = sw_ref[g]
        p_w = jnp.exp2(sw - jnp.max(sw, axis=0, keepdims=True))
        acc_w[g] = jnp.dot(values_t(vwt_ref, g, w_tile0, n_wt), p_w.astype(BF16),
                           preferred_element_type=F32)

    psum = jnp.concatenate(psums, axis=1)
    p_hi = psum.astype(BF16)
    p_lo = (psum - p_hi.astype(F32)).astype(BF16)
    c2s = c2s_ref[...]
    imp = (jnp.dot(c2s, p_hi, preferred_element_type=F32)
           + jnp.dot(c2s, p_lo, preferred_element_type=F32))
    n_sel = imp.shape[0]
    m_io = lax.broadcasted_iota(jnp.int32, (n_sel, gw), 0)
    m_f = m_io.astype(F32)
    qpos = TILE * i + (lax.broadcasted_iota(jnp.int32, (n_sel, gw), 1) & (TILE - 1))
    own = qpos >> 6
    forced = (m_io == 0) | (m_io == own) | (m_io == own - 1)
    eligible = SEL_LEN * m_io <= qpos
    sel = jnp.where(eligible & forced, 1.0, 0.0)
    cur = jnp.where(eligible, jnp.where(forced, -2.0, imp), -1.0)
    for _ in range(min(SEL_TOPK, n_sel) - N_FORCED):
        mx = jnp.max(cur, axis=0, keepdims=True)
        idx = jnp.min(jnp.where(cur == mx, m_f, float(n_sel)), axis=0, keepdims=True)
        pick = m_f == idx
        sel = jnp.where(pick, jnp.maximum(jnp.where(mx >= 0.0, 1.0, 0.0), sel), sel)
        cur = jnp.where(pick, -2.0, cur)
    sel_ref[...] = sel

    def sel_mask(g, c):
        rows = sel_ref[pl.ds(pl.multiple_of(c * bpc, 8), bpc), col(g)]
        mk = jnp.concatenate([jnp.broadcast_to(rows[b:b + 1, :], (SEL_LEN, TILE))
                              for b in range(bpc)], axis=0)
        return jnp.concatenate([mk, mk, mk], axis=1) > 0.5

    def sel_update(g, c, s):
        s = jnp.where(sel_mask(g, c), s, NEG_BIG)
        m_old = m_s[g]
        m_new = jnp.maximum(m_old, jnp.max(s, axis=0, keepdims=True))
        alpha = jnp.exp2(m_old - m_new)
        p = jnp.exp2(s - m_new)
        acc_s[g] = alpha * acc_s[g] + jnp.dot(values_t(vst_ref, g, c * tpc, tpc), p.astype(BF16),
                                              preferred_element_type=F32)
        m_s[g] = m_new

    m_s[...] = jnp.full(m_s.shape, NEG_BIG, F32)
    acc_s[...] = jnp.zeros(acc_s.shape, F32)

    c_own = i // tpc
    c_prev = jnp.maximum(i - 1, 0) // tpc

    bufs = (sa_ref, sb_ref, sc3_ref)
    for g in groups:
        sa_ref[g] = key_scores(ks_ref, g, 0, tpc)

    def far_triple(pp, carry):
        for stage in range(3):
            cur_buf, nxt_buf = bufs[stage], bufs[(stage + 1) % 3]
            for g in groups:
                nxt_buf[g] = key_scores(ks_ref, g, (3 * pp + stage + 1) * tpc, tpc)
            for g in groups:
                sel_update(g, 3 * pp + stage, cur_buf[g])
        return carry

    lax.fori_loop(0, c_prev // 3, far_triple, 0)
    done = (c_prev // 3) * 3

    @pl.when(c_prev - done == 1)
    def _():
        for g in groups:
            sel_update(g, done, sa_ref[g])

    @pl.when(c_prev - done == 2)
    def _():
        for g in groups:
            sb_ref[g] = key_scores(ks_ref, g, (done + 1) * tpc, tpc)
        for g in groups:
            sel_update(g, done, sa_ref[g])
            sel_update(g, done + 1, sb_ref[g])

    @pl.when(c_prev != c_own)
    def _():
        last = slice((tpc - 1) * TILE, tpc * TILE)
        for g in groups:
            ss_ref[g] = key_scores(ks_ref, g, c_prev * tpc, tpc)
            ss_ref[g, last, :] = ss_ref[g, last, :] + t1_ref[g]
            sel_update(g, c_prev, ss_ref[g])

    for g in groups:
        ss_ref[g] = key_scores(ks_ref, g, c_own * tpc, tpc)
    for g in groups:
        fix_near(ss_ref, g, i - c_own * tpc)
        sel_update(g, c_own, ss_ref[g])

    for g in groups:
        gates = jax.nn.sigmoid(gl_ref[g])

        def gate_row(branch):
            return jnp.concatenate([gates[r * 3 + branch:r * 3 + branch + 1, :]
                                    for r in range(NSA_R)], axis=1)

        out_t = (oc_t[g] * gate_row(0)
                 + acc_s[g, 0:HD, :] * (gate_row(1) / acc_s[g, HD:HD + 1, :])
                 + acc_w[g, 0:HD, :] * (gate_row(2) / acc_w[g, HD:HD + 1, :]))
        for r in range(NSA_R):
            h = NSA_R * g + r
            o_r = out_t[:, r * TILE:(r + 1) * TILE].T
            o_ref[:, h * TILE:(h + 1) * TILE] = (
                o_r * _silu(z_refs[h][...].astype(F32))).astype(BF16)


def _nsa_attention(proj, proj_t, cmp_n, cmp_t, c2s_t, cstrip, t0, t1, glog_t):
    s = proj.shape[0]
    nk = s // TILE
    n_cmp = cmp_n.shape[1]
    n_sel = s // SEL_LEN
    w3 = NSA_R * TILE
    gw = NSA_G * TILE
    qspec = lambda h: pl.BlockSpec((TILE, TILE), lambda i, h=h: (i, BLK_QN + h))
    zspec = lambda h: pl.BlockSpec((TILE, TILE), lambda i, h=h: (i, BLK_ZN + h))
    whole = lambda shape: pl.BlockSpec(shape, lambda i: (0,) * len(shape))
    resident = dict(pipeline_mode=pl.Buffered(1))
    chunk_buf = pltpu.VMEM((NSA_G, NSA_CHUNK, w3), F32)
    acc_buf = pltpu.VMEM((NSA_G, HD + ONES_ROWS, w3), F32)
    return pl.pallas_call(
        functools.partial(_nsa_kernel, n_cmp=n_cmp),
        grid=(s // TILE,),
        in_specs=[qspec(h) for h in range(H_NSA)] + [zspec(h) for h in range(H_NSA)] + [
            pl.BlockSpec((s, gw), lambda i: (0, BLK_KS // NSA_G), **resident),
            pl.BlockSpec((s, gw), lambda i: (0, BLK_KW // NSA_G), **resident),
            pl.BlockSpec((nk, gw, TILE), lambda i: (0, H_FOX // NSA_G, 0), **resident),
            pl.BlockSpec((nk, gw, TILE), lambda i: (0, H_FOX // NSA_G + 1, 0), **resident),
            pl.BlockSpec((NSA_G, n_cmp, HD), lambda i: (0, 0, 0)),
            pl.BlockSpec((NSA_G, HD, n_cmp), lambda i: (1, 0, 0)),
            whole((n_sel, n_cmp)), whole((NSA_G, 24, w3)), whole((NSA_G, TILE, w3)),
            whole((NSA_G, TILE, w3)),
            pl.BlockSpec((NSA_G, NSA_R * 3, TILE), lambda i: (0, 0, i))],
        out_specs=pl.BlockSpec((TILE, D_NSA), lambda i: (i, 0)),
        out_shape=jax.ShapeDtypeStruct((s, D_NSA), BF16),
        scratch_shapes=[pltpu.VMEM((NSA_G, 16 + n_cmp + 8, w3), F32),
                        pltpu.VMEM((n_sel, gw), F32),
                        pltpu.VMEM((NSA_G, 1, w3), F32),
                        acc_buf, acc_buf,
                        chunk_buf,
                        pltpu.VMEM((NSA_G, WINDOW + TILE, w3), F32),
                        chunk_buf, chunk_buf, chunk_buf],
        compiler_params=_cparams(("parallel",)),
        name="nsa_attention",
    )(*([proj] * (2 * H_NSA)), proj, proj, proj_t, proj_t,
      cmp_n, cmp_t, c2s_t, cstrip, t0, t1, glog_t)


def _conv_kernel(u_ref, gb_ref, gc_ref, z_ref, uh_ref, gch_ref, w_ref, o_ref, *, tm):
    i = pl.program_id(0)
    y = gc_ref[...].astype(F32) * u_ref[...].astype(F32)
    yh = gch_ref[...].astype(F32) * uh_ref[...].astype(F32)
    yh = jnp.where(i > 0, yh, 0.0)
    h1 = yh[15:16, :]
    h2 = yh[14:15, :]
    row = lax.broadcasted_iota(jnp.int32, (tm, D_CONV), 0)
    y1 = jnp.where(row == 0, h1, pltpu.roll(y, 1, axis=0))
    y2 = jnp.where(row == 0, h2, jnp.where(row == 1, h1, pltpu.roll(y, 2, axis=0)))
    w = w_ref[...]
    conv = w[0:1, :] * y2 + w[1:2, :] * y1 + w[2:3, :] * y
    o_ref[...] = (gb_ref[...].astype(F32) * conv * _silu(z_ref[...].astype(F32))).astype(BF16)


def _conv_branch(proj, conv_w):
    s = proj.shape[0]
    tm = 512
    main = lambda c: pl.BlockSpec((tm, D_CONV), lambda i, c=c: (i, c))
    halo = lambda c: pl.BlockSpec((16, D_CONV), lambda i, c=c: (jnp.maximum(i * (tm // 16) - 1, 0), c))
    return pl.pallas_call(
        functools.partial(_conv_kernel, tm=tm),
        grid=(s // tm,),
        in_specs=[main(0), main(1), main(2), main(3), halo(0), halo(2),
                  pl.BlockSpec((CONV_WIDTH, D_CONV), lambda i: (0, 0))],
        out_specs=pl.BlockSpec((tm, D_CONV), lambda i: (i, 0)),
        out_shape=jax.ShapeDtypeStruct((s, D_CONV), BF16),
        compiler_params=_cparams(("parallel",)),
        name="conv_branch",
    )(proj, proj, proj, proj, proj, proj, conv_w)


def _out_kernel(ya_ref, yf_ref, yn_ref, w_ref, x_ref, pg_ref, gate_ref, *rest, with_next):
    if with_next:
        ng_ref, nsc_ref, nsh_ref, xo_ref, h_ref = rest
    else:
        (xo_ref,) = rest
    tm = x_ref.shape[0]
    n_parts = 4
    for part in range(n_parts):
        rows = slice(part * tm // n_parts, (part + 1) * tm // n_parts)
        y = (jnp.dot(ya_ref[rows, :], w_ref[0:D_CONV, :], preferred_element_type=F32)
             + jnp.dot(yf_ref[rows, :], w_ref[D_CONV:D_CONV + D_FOX, :], preferred_element_type=F32)
             + jnp.dot(yn_ref[rows, :], w_ref[D_CONV + D_FOX:, :], preferred_element_type=F32))
        yn = y * lax.rsqrt(jnp.mean(y * y, axis=-1, keepdims=True) + NORM_EPS) * pg_ref[...]
        xn = x_ref[rows, :] + gate_ref[...] * yn
        if with_next:
            h_ref[rows, :] = _modulated_norm(xn, ng_ref[...], nsc_ref[...],
                                             nsh_ref[...]).astype(BF16)
        xo_ref[rows, :] = xn


def _out_proj(ya, yf, yn, w_out, l, x2, post_g, gate, nxt):
    s, d = x2.shape
    tm = 512
    rows = lambda n: pl.BlockSpec((tm, n), lambda i: (i, 0))
    vec = pl.BlockSpec((1, d), lambda i: (0, 0))
    in_specs = [rows(D_CONV), rows(D_FOX), rows(D_NSA),
                pl.BlockSpec((None, d, d), lambda i: (l, 0, 0), pipeline_mode=pl.Buffered(1)),
                rows(d), vec, vec]
    args = [ya, yf, yn, w_out, x2, post_g, gate]
    out_specs = [rows(d)]
    out_shape = [jax.ShapeDtypeStruct((s, d), F32)]
    if nxt is not None:
        in_specs += [vec, vec, vec]
        args += list(nxt)
        out_specs.append(rows(d))
        out_shape.append(jax.ShapeDtypeStruct((s, d), BF16))
    res = pl.pallas_call(
        functools.partial(_out_kernel, with_next=nxt is not None),
        grid=(s // tm,),
        in_specs=in_specs,
        out_specs=out_specs,
        out_shape=out_shape,
        compiler_params=_cparams(("parallel",)),
        name="out_proj",
    )(*args)
    return res if nxt is not None else (res[0], None)


def _bucket_of_distance():
    max_exact = REL_BUCKETS // 2
    d = np.arange(REL_MAX_DIST)
    nf = np.maximum(d, max_exact).astype(np.float32)
    large = max_exact + (np.log(nf / np.float32(max_exact)) / np.float32(math.log(REL_MAX_DIST / max_exact))
                         * np.float32(REL_BUCKETS - max_exact)).astype(np.int32)
    return np.where(d < max_exact, d, np.minimum(large, REL_BUCKETS - 1))


def _distance_tables():
    b = np.arange(TILE)[:, None]
    a = np.arange(TILE)[None, :]
    none = REL_MAX_DIST
    d0 = np.where(a >= b, a - b, none)
    d1 = np.where(a < b, TILE + a - b, none)
    m = np.arange(24)[:, None]
    dc = a - CMP_STRIDE * m + (2 * TILE - (CMP_LEN - 1))
    dc = np.where((dc >= 0) & (dc < REL_MAX_DIST), dc, none)
    return d0, d1, dc


def _bias_tables(rel_bias):
    bucket = np.concatenate([_bucket_of_distance(), [REL_BUCKETS - 1]])
    rel = (rel_bias - rel_bias[REL_BUCKETS - 1:REL_BUCKETS, :]) * LOG2E
    rel = rel.reshape(REL_BUCKETS, NSA_G, NSA_R)

    def expand(idx):
        bk = bucket[idx]
        tab = jnp.zeros((NSA_G, idx.shape[0], NSA_R, TILE), F32)
        for b in range(REL_BUCKETS - 1):
            hit = jnp.asarray(bk == b)[None, :, None, :]
            tab = jnp.where(hit, rel[b][:, None, :, None], tab)
        return tab.reshape(NSA_G, idx.shape[0], NSA_R * TILE)

    d0, d1, dc = _distance_tables()
    return expand(d0), expand(d1), expand(dc)


def _cmp_to_sel_t(s, n_cmp_pad):
    n_cmp = (s - CMP_LEN) // CMP_STRIDE + 1
    n_sel = s // SEL_LEN
    c_lo = np.arange(n_cmp_pad)[None, :] * CMP_STRIDE
    sel_start = np.arange(n_sel)[:, None] * SEL_LEN
    overlap = (c_lo < sel_start + SEL_LEN) & (c_lo + CMP_LEN > sel_start)
    overlap &= np.arange(n_cmp_pad)[None, :] < n_cmp
    return jnp.asarray(overlap, dtype=BF16)


def _pack_w_in(w_in):
    sizes = ([D_CONV] * 4 + [D_FOX] * 3 + [H_FOX, D_FOX] + [D_NSA] + [2 * HD] * 6
             + [3 * H_NSA, D_NSA])
    offs = np.concatenate([[0], np.cumsum(sizes)])
    names = ["u", "gb", "gc", "za", "qf", "kf", "vf", "ff", "zf", "qn", "kc", "vc", "ks", "vs",
             "kw", "vw", "gn", "zn"]
    wt = jnp.swapaxes(w_in, 1, 2)
    col = {n: wt[:, int(offs[k]):int(offs[k + 1]), :] for k, n in enumerate(names)}
    scale = HD ** -0.5 * LOG2E
    w_a = jnp.concatenate([col["u"], col["gb"], col["gc"], col["za"],
                           col["qn"] * scale, col["zn"], col["ks"], col["kw"],
                           col["qf"] * scale, col["kf"], col["zf"]], axis=1).astype(BF16)
    w_kv = jnp.concatenate([col["kc"], col["vc"]], axis=1).astype(BF16)
    w_v = jnp.concatenate([col["vf"], col["vs"], col["vw"]], axis=1).astype(BF16)
    small = jnp.concatenate([col["ff"], col["gn"]], axis=1)
    small = jnp.pad(small, ((0, 0), (0, TILE - small.shape[1]), (0, 0))).astype(BF16)
    return w_a, w_kv, w_v, small


def kernel(x, c, w_ada, b_ada, pre_norm, post_norm, w_in, b_forget, conv_w,
           cmp_pe_k, cmp_w1_k, cmp_w2_k, cmp_pe_v, cmp_w1_v, cmp_w2_v, w_out, rel_bias):
    bsz, s, d = x.shape
    assert bsz == 1 and d == D_MODEL and s % PROJ_TM == 0 and s >= 2 * WINDOW
    depth = w_in.shape[0]
    x2 = x.reshape(s, d)
    n_cmp_pad = s // CMP_STRIDE

    mod = _ada_mod(c.reshape(d, 1), w_ada, b_ada)
    shift, scale, gate = mod[:, :, :d], mod[:, :, d:2 * d], mod[:, :, 2 * d:]

    w_a, w_kv, w_v, w_small = _pack_w_in(w_in)
    w_out_b = w_out.astype(BF16)
    half = CMP_LEN * HD // 2
    w1cat = jnp.stack([jnp.concatenate([w[:, :half], w[:, half:]], axis=-1)
                       for w in (cmp_w1_k, cmp_w1_v)], axis=1).astype(BF16)
    pe = jnp.stack([cmp_pe_k, cmp_pe_v], axis=1).reshape(depth, 2, 2, 1, half)
    pe16 = jnp.broadcast_to(pe, (depth, 2, 2, 8, half)).reshape(depth, 2, 16, half).astype(BF16)
    w2 = jnp.stack([cmp_w2_k, cmp_w2_v], axis=1).astype(BF16)
    b_pad = jnp.pad(b_forget, ((0, 0), (0, TILE - H_FOX))).reshape(depth, 1, TILE)
    t0, t1, cstrip = _bias_tables(rel_bias)
    c2s_t = _cmp_to_sel_t(s, n_cmp_pad)

    h = _prenorm(x2, pre_norm[0:1], scale[0], shift[0])
    for l in range(depth):
        proj, proj_t, kcvc = _project(h, w_a, w_v, w_kv, l)

        ya = _conv_branch(proj, conv_w[l])

        qa, ka, small_t = _fox_prep(h, w_small, l, b_pad[l])
        yf = _fox_attention(proj, qa, ka, proj_t)

        x4 = kcvc.reshape(2 * NSA_G, n_cmp_pad, CMP_STRIDE * HD)
        cmp_n, cmp_t = _compress(x4, w1cat, pe16, w2, l)
        glog_t = small_t[H_FOX:H_FOX + 3 * H_NSA].reshape(NSA_G, NSA_R * 3, s)
        yn = _nsa_attention(proj, proj_t, cmp_n, cmp_t, c2s_t, cstrip, t0, t1, glog_t)

        nxt = None
        if l + 1 < depth:
            nxt = (pre_norm[l + 1:l + 2], scale[l + 1], shift[l + 1])
        x2, h = _out_proj(ya, yf, yn, w_out_b, l, x2, post_norm[l:l + 1], gate[l], nxt)
    return x2.reshape(bsz, s, d)
```

```python
import functools
import math

import numpy as np
import jax
import jax.numpy as jnp
from jax import lax
from jax.experimental import pallas as pl
from jax.experimental.pallas import tpu as pltpu

F32 = jnp.float32
BF16 = jnp.bfloat16

D_MODEL = 2048
HD = 128
D_CONV = 512
D_FOX = 768
D_NSA = 768
H_FOX = 6
H_NSA = 6
NSA_G = 2
NSA_R = 3
CONV_WIDTH = 3
CMP_LEN = 32
CMP_STRIDE = 16
CMP_HIDDEN = 256
SEL_LEN = 64
SEL_TOPK = 16
N_FORCED = 3
WINDOW = 512
REL_BUCKETS = 32
REL_MAX_DIST = 128
NORM_EPS = 1e-6
NEG_BIG = -1e30
LOG2E = math.log2(math.e)

TILE = 128
ONES_ROWS = 16
NSA_CHUNK = 512
FOX_T = 512
FOX_TK = 512
FOX_HEADS = 2
PROJ_TM = 1024
PROJ_TN = 1280
PROJ_TV = 1280
VMEM_LIMIT = 56 * 1024 * 1024

NT_DIMS = (((1,), (1,)), ((), ()))

BLK_QN, BLK_ZN, BLK_KS, BLK_KW = 16, 22, 28, 30
BLK_QF, BLK_KF, BLK_ZF = 32, 38, 44
N_A = 6400


def _cparams(sem, vmem=VMEM_LIMIT):
    return pltpu.CompilerParams(dimension_semantics=sem, vmem_limit_bytes=vmem)


def _silu(v):
    return v * jax.nn.sigmoid(v)


def _mod_kernel(c_ref, w_ref, b_ref, o_ref):
    ca = _silu(c_ref[...])
    o_ref[0] = jnp.sum(w_ref[0] * ca, axis=0, keepdims=True) + b_ref[0]


def _ada_mod(c_col, w_ada, b_ada):
    depth, d, n = w_ada.shape
    tn = 1536
    return pl.pallas_call(
        _mod_kernel,
        grid=(depth, n // tn),
        in_specs=[pl.BlockSpec((d, 1), lambda l, j: (0, 0)),
                  pl.BlockSpec((1, d, tn), lambda l, j: (l, 0, j)),
                  pl.BlockSpec((1, 1, tn), lambda l, j: (l, 0, j))],
        out_specs=pl.BlockSpec((1, 1, tn), lambda l, j: (l, 0, j)),
        out_shape=jax.ShapeDtypeStruct((depth, 1, n), F32),
        compiler_params=_cparams(("parallel", "parallel")),
        name="ada_mod",
    )(c_col, w_ada, b_ada.reshape(depth, 1, n))


def _modulated_norm(x, g, scale, shift):
    y = x * lax.rsqrt(jnp.mean(x * x, axis=-1, keepdims=True) + NORM_EPS) * g
    return y * (1.0 + scale) + shift


def _prenorm_kernel(x_ref, g_ref, sc_ref, sh_ref, h_ref):
    h_ref[...] = _modulated_norm(x_ref[...], g_ref[...], sc_ref[...], sh_ref[...]).astype(BF16)


def _prenorm(x2, g, scale, shift):
    s, d = x2.shape
    tm = 512
    vec = pl.BlockSpec((1, d), lambda i: (0, 0))
    return pl.pallas_call(
        _prenorm_kernel,
        grid=(s // tm,),
        in_specs=[pl.BlockSpec((tm, d), lambda i: (i, 0)), vec, vec, vec],
        out_specs=pl.BlockSpec((tm, d), lambda i: (i, 0)),
        out_shape=jax.ShapeDtypeStruct((s, d), BF16),
        compiler_params=_cparams(("parallel",)),
        name="prenorm",
    )(x2, g, scale, shift)


def _mm_kernel(h_ref, wt_ref, o_ref):
    o_ref[...] = lax.dot_general(h_ref[...], wt_ref[...], NT_DIMS,
                                 preferred_element_type=F32).astype(o_ref.dtype)


def _matmul(h, wt, l):
    s, d = h.shape
    n = wt.shape[1]
    tm, tn = PROJ_TM, PROJ_TN
    return pl.pallas_call(
        _mm_kernel,
        grid=(s // tm, n // tn),
        in_specs=[pl.BlockSpec((tm, d), lambda i, j: (i, 0)),
                  pl.BlockSpec((None, tn, d), lambda i, j: (l, j, 0))],
        out_specs=pl.BlockSpec((tm, tn), lambda i, j: (i, j)),
        out_shape=jax.ShapeDtypeStruct((s, n), BF16),
        compiler_params=_cparams(("parallel", "arbitrary")),
        name="proj_rows",
    )(h, wt)


def _mm_split_kernel(h_ref, wt_ref, o_ref):
    res = lax.dot_general(h_ref[...], wt_ref[...], NT_DIMS, preferred_element_type=F32)
    for c in range(o_ref.shape[0]):
        o_ref[c] = res[:, c * TILE:(c + 1) * TILE].astype(o_ref.dtype)


def _matmul_split(h, wt, l):
    s, d = h.shape
    n = wt.shape[1]
    tm = PROJ_TM
    return pl.pallas_call(
        _mm_split_kernel,
        grid=(s // tm,),
        in_specs=[pl.BlockSpec((tm, d), lambda i: (i, 0)),
                  pl.BlockSpec((None, n, d), lambda i: (l, 0, 0))],
        out_specs=pl.BlockSpec((n // TILE, tm, TILE), lambda i: (0, i, 0)),
        out_shape=jax.ShapeDtypeStruct((n // TILE, s, TILE), BF16),
        compiler_params=_cparams(("parallel",)),
        name="proj_rows_split",
    )(h, wt)


def _mm_t_tiled_kernel(h_ref, wt_ref, o_ref):
    res = lax.dot_general(wt_ref[...], h_ref[...], NT_DIMS, preferred_element_type=F32)
    for c in range(o_ref.shape[0]):
        o_ref[c] = res[:, c * TILE:(c + 1) * TILE].astype(o_ref.dtype)


def _matmul_t_tiled(h, wt, l):
    s, d = h.shape
    n = wt.shape[1]
    tm, tn = PROJ_TM, PROJ_TV
    return pl.pallas_call(
        _mm_t_tiled_kernel,
        grid=(s // tm, n // tn),
        in_specs=[pl.BlockSpec((tm, d), lambda i, j: (i, 0)),
                  pl.BlockSpec((None, tn, d), lambda i, j: (l, j, 0))],
        out_specs=pl.BlockSpec((tm // TILE, tn, TILE), lambda i, j: (i, j, 0)),
        out_shape=jax.ShapeDtypeStruct((s // TILE, n, TILE), BF16),
        compiler_params=_cparams(("parallel", "arbitrary")),
        name="proj_cols_tiled",
    )(h, wt)


def _fox_prep_kernel(h_ref, w_ref, b_ref, qa_ref, ka_ref, st_ref, carry_ref, *, tm):
    @pl.when(pl.program_id(0) == 0)
    def _():
        carry_ref[...] = jnp.zeros_like(carry_ref)

    small = lax.dot_general(h_ref[...], w_ref[...], NT_DIMS,
                            preferred_element_type=F32)
    st_ref[...] = small.T
    v = small + b_ref[...]
    c = jnp.minimum(v, 0.0) - jnp.log1p(jnp.exp(-jnp.abs(v)))
    row = lax.broadcasted_iota(jnp.int32, (tm, TILE), 0)
    sh = 1
    while sh < tm:
        c = c + jnp.where(row >= sh, pltpu.roll(c, sh, axis=0), 0.0)
        sh *= 2
    c = c + carry_ref[...]
    carry_ref[...] = c[tm - 1:tm, :]
    c = c * LOG2E

    lane = lax.broadcasted_iota(jnp.int32, (tm, TILE), 1)
    for h in range(H_FOX):
        ch = jnp.broadcast_to(c[:, h:h + 1], (tm, TILE))
        hi = ch.astype(BF16).astype(F32)
        r1 = ch - hi
        mid = r1.astype(BF16).astype(F32)
        lo = r1 - mid
        pieces = jnp.where((lane == 0) | (lane == 3), hi,
                           jnp.where((lane == 1) | (lane == 4), mid, lo))
        qa = jnp.where(lane < 3, pieces, jnp.where(lane < 6, 1.0, 0.0))
        ka = jnp.where(lane < 3, 1.0, jnp.where(lane < 6, -pieces, 0.0))
        qa_ref[:, h * TILE:(h + 1) * TILE] = qa.astype(BF16)
        ka_ref[:, h * TILE:(h + 1) * TILE] = ka.astype(BF16)


def _fox_prep(h, w_small, l, b_pad):
    s, d = h.shape
    tm = 512
    out = jax.ShapeDtypeStruct((s, H_FOX * TILE), BF16)
    return pl.pallas_call(
        functools.partial(_fox_prep_kernel, tm=tm),
        grid=(s // tm,),
        in_specs=[pl.BlockSpec((tm, d), lambda i: (i, 0)),
                  pl.BlockSpec((None, TILE, d), lambda i: (l, 0, 0)),
                  pl.BlockSpec((1, TILE), lambda i: (0, 0))],
        out_specs=[pl.BlockSpec((tm, H_FOX * TILE), lambda i: (i, 0)),
                   pl.BlockSpec((tm, H_FOX * TILE), lambda i: (i, 0)),
                   pl.BlockSpec((TILE, tm), lambda i: (0, i))],
        out_shape=[out, out, jax.ShapeDtypeStruct((TILE, s), F32)],
        scratch_shapes=[pltpu.VMEM((1, TILE), F32)],
        compiler_params=_cparams(("arbitrary",)),
        name="fox_prep",
    )(h, w_small, b_pad)


def _fox_kernel(q_ref, qa_ref, k_ref, ka_ref, vt_ref, z_ref, o_ref, s0_ref, s1_ref, s2_ref,
                acc_ref):
    i = pl.program_id(1)
    t = FOX_T
    tk = FOX_TK
    heads = range(FOX_HEADS)
    col = lambda h: slice(h * TILE, (h + 1) * TILE)
    qaug = [jnp.concatenate([q_ref[:, col(h)], qa_ref[:, col(h)]], axis=1) for h in heads]

    def scores(h, j, rows):
        r0 = pl.multiple_of(j * rows, rows)
        kaug = jnp.concatenate([k_ref[pl.ds(r0, rows), col(h)], ka_ref[pl.ds(r0, rows), col(h)]],
                               axis=1)
        return lax.dot_general(kaug, qaug[h], NT_DIMS, preferred_element_type=F32)

    def values_t(h, j, rows):
        n_sub = rows // TILE
        return jnp.concatenate(
            [jnp.concatenate([vt_ref[n_sub * j + c, col(h), :] for c in range(n_sub)], axis=1),
             jnp.ones((ONES_ROWS, rows), BF16)], axis=0)

    kio = lax.broadcasted_iota(jnp.int32, (t, t), 0)
    qio = lax.broadcasted_iota(jnp.int32, (t, t), 1)
    ms = []
    for h in heads:
        s = jnp.where(kio <= qio, scores(h, i, t), NEG_BIG)
        m = jnp.max(s, axis=0, keepdims=True)
        p = jnp.exp2(s - m)
        ms.append(m)
        acc_ref[h] = jnp.dot(values_t(h, i, t), p.astype(BF16), preferred_element_type=F32)

    def update(h, s, vt, m):
        m_new = jnp.maximum(m, jnp.max(s, axis=0, keepdims=True))
        alpha = jnp.exp2(m - m_new)
        p = jnp.exp2(s - m_new)
        acc_ref[h] = alpha * acc_ref[h] + jnp.dot(vt, p.astype(BF16), preferred_element_type=F32)
        return m_new

    bufs = (s0_ref, s1_ref, s2_ref)
    for h in heads:
        s0_ref[h] = scores(h, 0, tk)

    def triple(pp, ms):
        ms = list(ms)
        for stage in range(3):
            cur, nxt = bufs[stage], bufs[(stage + 1) % 3]
            for h in heads:
                nxt[h] = scores(h, 3 * pp + stage + 1, tk)
            for h in heads:
                ms[h] = update(h, cur[h], values_t(h, 3 * pp + stage, tk), ms[h])
        return tuple(ms)

    n_far = i * (t // tk)
    ms = lax.fori_loop(0, n_far // 3, triple, tuple(ms))
    done = (n_far // 3) * 3

    @pl.when(n_far - done == 1)
    def _():
        for h in heads:
            update(h, s0_ref[h], values_t(h, done, tk), ms[h])

    @pl.when(n_far - done == 2)
    def _():
        for h in heads:
            s1_ref[h] = scores(h, done + 1, tk)
        for h in heads:
            m_mid = update(h, s0_ref[h], values_t(h, done, tk), ms[h])
            update(h, s1_ref[h], values_t(h, done + 1, tk), m_mid)

    for h in heads:
        o = (acc_ref[h, 0:HD, :] * (1.0 / acc_ref[h, HD:HD + 1, :])).T
        o_ref[:, col(h)] = (o * _silu(z_ref[:, col(h)].astype(F32))).astype(BF16)


def _fox_attention(proj, qa, ka, proj_t):
    s = proj.shape[0]
    t = FOX_T
    nk = s // TILE
    hw = FOX_HEADS * TILE
    score_buf = pltpu.VMEM((FOX_HEADS, FOX_TK, t), F32)
    return pl.pallas_call(
        _fox_kernel,
        grid=(H_FOX // FOX_HEADS, s // t),
        scratch_shapes=[score_buf, score_buf, score_buf,
                        pltpu.VMEM((FOX_HEADS, HD + ONES_ROWS, t), F32)],
        in_specs=[pl.BlockSpec((t, hw), lambda h, i: (i, BLK_QF // FOX_HEADS + h)),
                  pl.BlockSpec((t, hw), lambda h, i: (i, h)),
                  pl.BlockSpec((s, hw), lambda h, i: (0, BLK_KF // FOX_HEADS + h)),
                  pl.BlockSpec((s, hw), lambda h, i: (0, h)),
                  pl.BlockSpec((nk, hw, TILE), lambda h, i: (0, h, 0)),
                  pl.BlockSpec((t, hw), lambda h, i: (i, BLK_ZF // FOX_HEADS + h))],
        out_specs=pl.BlockSpec((t, hw), lambda h, i: (i, h)),
        out_shape=jax.ShapeDtypeStruct((s, D_FOX), BF16),
        compiler_params=_cparams(("parallel", "parallel")),
        name="fox_attention",
    )(proj, qa, proj, ka, proj_t, proj)


def _compress_kernel(x_ref, w1_ref, pe_ref, w2_ref, o_ref, ot_ref):
    n = x_ref.shape[0]
    ab = jnp.dot(x_ref[...], w1_ref[0], preferred_element_type=F32)
    pe = jnp.dot(pe_ref[0], w1_ref[0], preferred_element_type=F32)
    a = ab[:, :CMP_HIDDEN]
    b_next = pltpu.roll(ab[:, CMP_HIDDEN:], n - 1, axis=0)
    hid = a + b_next + pe[0:1, :CMP_HIDDEN] + pe[8:9, CMP_HIDDEN:]
    out = jnp.dot(_silu(hid).astype(BF16), w2_ref[0], preferred_element_type=F32)
    o_ref[...] = out.astype(BF16)
    ot_ref[...] = out.T.astype(BF16)


def _compress(x4, w1cat, pe16, w2, l):
    _, n, k = x4.shape
    kind = lambda c: (l, c // NSA_G, 0, 0)
    return pl.pallas_call(
        _compress_kernel,
        grid=(2 * NSA_G,),
        in_specs=[pl.BlockSpec((None, n, k), lambda c: (c, 0, 0)),
                  pl.BlockSpec((None, 1, k, 2 * CMP_HIDDEN), kind),
                  pl.BlockSpec((None, 1, 16, k), kind),
                  pl.BlockSpec((None, 1, CMP_HIDDEN, HD), kind)],
        out_specs=[pl.BlockSpec((None, n, HD), lambda c: (c, 0, 0)),
                   pl.BlockSpec((None, HD, n), lambda c: (c, 0, 0))],
        out_shape=[jax.ShapeDtypeStruct((2 * NSA_G, n, HD), BF16),
                   jax.ShapeDtypeStruct((2 * NSA_G, HD, n), BF16)],
        compiler_params=_cparams(("parallel",)),
        name="nsa_compress",
    )(x4, w1cat, pe16, w2)


def _nsa_kernel(*refs, n_cmp):
    q_refs, z_refs = refs[0:H_NSA], refs[H_NSA:2 * H_NSA]
    (ks_ref, kw_ref, vst_ref, vwt_ref, kc_ref, vct_ref, c2s_ref, cstrip_ref, t0_ref, t1_ref,
     gl_ref, o_ref, sc_ref, sel_ref, m_s, acc_s, acc_w, ss_ref, sw_ref, sa_ref, sb_ref,
     sc3_ref) = refs[2 * H_NSA:]
    i = pl.program_id(0)
    w3 = NSA_R * TILE
    gw = NSA_G * TILE
    groups = range(NSA_G)
    col = lambda g: slice(g * TILE, (g + 1) * TILE)
    qs = [jnp.concatenate([q_refs[NSA_R * g + r][...] for r in range(NSA_R)], axis=0)
          for g in groups]
    b_io = lax.broadcasted_iota(jnp.int32, (TILE, w3), 0)
    a_io = lax.broadcasted_iota(jnp.int32, (TILE, w3), 1) & (TILE - 1)
    tpc = NSA_CHUNK // TILE
    bpc = NSA_CHUNK // SEL_LEN

    def key_scores(k_ref, g, tile0, n_tiles):
        r0 = pl.multiple_of(tile0 * TILE, TILE)
        return lax.dot_general(k_ref[pl.ds(r0, n_tiles * TILE), col(g)], qs[g], NT_DIMS,
                               preferred_element_type=F32)

    def values_t(vt_ref, g, tile0, n_tiles):
        return jnp.concatenate(
            [jnp.concatenate([vt_ref[tile0 + c, col(g), :] for c in range(n_tiles)], axis=1),
             jnp.ones((ONES_ROWS, n_tiles * TILE), BF16)], axis=0)

    def fix_near(s_ref, g, d):
        own = pl.ds(pl.multiple_of(d * TILE, TILE), TILE)
        prev = pl.ds(pl.multiple_of(jnp.maximum(d - 1, 0) * TILE, TILE), TILE)
        s_ref[g, prev, :] = s_ref[g, prev, :] + jnp.where(d >= 1, t1_ref[g], 0.0)
        s_ref[g, own, :] = jnp.where(b_io <= a_io, s_ref[g, own, :] + t0_ref[g], NEG_BIG)

    w0 = pl.multiple_of(8 * i, 8)
    n_io = lax.broadcasted_iota(jnp.int32, (n_cmp, w3), 0)
    qa_io = lax.broadcasted_iota(jnp.int32, (n_cmp, w3), 1) & (TILE - 1)
    valid = (CMP_STRIDE * n_io + (CMP_LEN - 1)) <= (TILE * i + qa_io)
    n_wt = WINDOW // TILE + 1
    w_tile0 = jnp.maximum(i - WINDOW // TILE, 0)
    d_win = i - w_tile0
    for g in groups:
        sc_ref[g, 0:16, :] = jnp.zeros((16, w3), F32)
        sc_ref[g, 16:16 + n_cmp, :] = lax.dot_general(kc_ref[g], qs[g], NT_DIMS,
                                                      preferred_element_type=F32)
        sw_ref[g] = key_scores(kw_ref, g, w_tile0, n_wt)
    oc_t, psums = [], []
    for g in groups:
        sc_ref[g, pl.ds(w0, 24), :] = sc_ref[g, pl.ds(w0, 24), :] + cstrip_ref[g]
        scm = jnp.where(valid, sc_ref[g, 16:16 + n_cmp, :], NEG_BIG)
        mc = jnp.max(scm, axis=0, keepdims=True)
        pc = jnp.exp2(scm - mc)
        lc = jnp.sum(pc, axis=0, keepdims=True)
        pcn = pc * jnp.where(mc > 0.5 * NEG_BIG, 1.0 / lc, 0.0)
        oc_t.append(jnp.dot(vct_ref[g], pcn.astype(BF16), preferred_element_type=F32))
        psums.append(pcn[:, 0:TILE] + pcn[:, TILE:2 * TILE] + pcn[:, 2 * TILE:3 * TILE])

        for c in range(1, n_wt):
            rows = slice(c * TILE, (c + 1) * TILE)
            sw_ref[g, rows, :] = jnp.where(d_win < c, NEG_BIG, sw_ref[g, rows, :])
        fix_near(sw_ref, g, d_win)
        sw_ref[g, 0:TILE, :] = jnp.where((a_io < b_io) | (i < WINDOW // TILE),
                                         sw_ref[g, 0:TILE, :], NEG_BIG)
        sw = sw_ref[g]
        p_w = jnp.exp2(sw - jnp.max(sw, axis=0, keepdims=True))
        acc_w[g] = jnp.dot(values_t(vwt_ref, g, w_tile0, n_wt), p_w.astype(BF16),
                           preferred_element_type=F32)

    psum = jnp.concatenate(psums, axis=1)
    p_hi = psum.astype(BF16)
    p_lo = (psum - p_hi.astype(F32)).astype(BF16)
    c2s = c2s_ref[...]
    imp = (jnp.dot(c2s, p_hi, preferred_element_type=F32)
           + jnp.dot(c2s, p_lo, preferred_element_type=F32))
    n_sel = imp.shape[0]
    m_io = lax.broadcasted_iota(jnp.int32, (n_sel, gw), 0)
    m_f = m_io.astype(F32)
    qpos = TILE * i + (lax.broadcasted_iota(jnp.int32, (n_sel, gw), 1) & (TILE - 1))
    own = qpos >> 6
    forced = (m_io == 0) | (m_io == own) | (m_io == own - 1)
    eligible = SEL_LEN * m_io <= qpos
    sel = jnp.where(eligible & forced, 1.0, 0.0)
    cur = jnp.where(eligible, jnp.where(forced, -2.0, imp), -1.0)
    for _ in range(min(SEL_TOPK, n_sel) - N_FORCED):
        mx = jnp.max(cur, axis=0, keepdims=True)
        idx = jnp.min(jnp.where(cur == mx, m_f, float(n_sel)), axis=0, keepdims=True)
        pick = m_f == idx
        sel = jnp.where(pick, jnp.maximum(jnp.where(mx >= 0.0, 1.0, 0.0), sel), sel)
        cur = jnp.where(pick, -2.0, cur)
    sel_ref[...] = sel

    def sel_mask(g, c):
        rows = sel_ref[pl.ds(pl.multiple_of(c * bpc, 8), bpc), col(g)]
        mk = jnp.concatenate([jnp.broadcast_to(rows[b:b + 1, :], (SEL_LEN, TILE))
                              for b in range(bpc)], axis=0)
        return jnp.concatenate([mk, mk, mk], axis=1) > 0.5

    def sel_update(g, c, s):
        s = jnp.where(sel_mask(g, c), s, NEG_BIG)
        m_old = m_s[g]
        m_new = jnp.maximum(m_old, jnp.max(s, axis=0, keepdims=True))
        alpha = jnp.exp2(m_old - m_new)
        p = jnp.exp2(s - m_new)
        acc_s[g] = alpha * acc_s[g] + jnp.dot(values_t(vst_ref, g, c * tpc, tpc), p.astype(BF16),
                                              preferred_element_type=F32)
        m_s[g] = m_new

    m_s[...] = jnp.full(m_s.shape, NEG_BIG, F32)
    acc_s[...] = jnp.zeros(acc_s.shape, F32)

    c_own = i // tpc
    c_prev = jnp.maximum(i - 1, 0) // tpc

    bufs = (sa_ref, sb_ref, sc3_ref)
    for g in groups:
        sa_ref[g] = key_scores(ks_ref, g, 0, tpc)

    def far_triple(pp, carry):
        for stage in range(3):
            cur_buf, nxt_buf = bufs[stage], bufs[(stage + 1) % 3]
            for g in groups:
                nxt_buf[g] = key_scores(ks_ref, g, (3 * pp + stage + 1) * tpc, tpc)
            for g in groups:
                sel_update(g, 3 * pp + stage, cur_buf[g])
        return carry

    lax.fori_loop(0, c_prev // 3, far_triple, 0)
    done = (c_prev // 3) * 3

    @pl.when(c_prev - done == 1)
    def _():
        for g in groups:
            sel_update(g, done, sa_ref[g])

    @pl.when(c_prev - done == 2)
    def _():
        for g in groups:
            sb_ref[g] = key_scores(ks_ref, g, (done + 1) * tpc, tpc)
        for g in groups:
            sel_update(g, done, sa_ref[g])
            sel_update(g, done + 1, sb_ref[g])

    @pl.when(c_prev != c_own)
    def _():
        last = slice((tpc - 1) * TILE, tpc * TILE)
        for g in groups:
            ss_ref[g] = key_scores(ks_ref, g, c_prev * tpc, tpc)
            ss_ref[g, last, :] = ss_ref[g, last, :] + t1_ref[g]
            sel_update(g, c_prev, ss_ref[g])

    for g in groups:
        ss_ref[g] = key_scores(ks_ref, g, c_own * tpc, tpc)
    for g in groups:
        fix_near(ss_ref, g, i - c_own * tpc)
        sel_update(g, c_own, ss_ref[g])

    for g in groups:
        gates = jax.nn.sigmoid(gl_ref[g])

        def gate_row(branch):
            return jnp.concatenate([gates[r * 3 + branch:r * 3 + branch + 1, :]
                                    for r in range(NSA_R)], axis=1)

        out_t = (oc_t[g] * gate_row(0)
                 + acc_s[g, 0:HD, :] * (gate_row(1) / acc_s[g, HD:HD + 1, :])
                 + acc_w[g, 0:HD, :] * (gate_row(2) / acc_w[g, HD:HD + 1, :]))
        for r in range(NSA_R):
            h = NSA_R * g + r
            o_r = out_t[:, r * TILE:(r + 1) * TILE].T
            o_ref[:, h * TILE:(h + 1) * TILE] = (
                o_r * _silu(z_refs[h][...].astype(F32))).astype(BF16)


def _nsa_attention(proj, proj_t, cmp_n, cmp_t, c2s_t, cstrip, t0, t1, glog_t):
    s = proj.shape[0]
    nk = s // TILE
    n_cmp = cmp_n.shape[1]
    n_sel = s // SEL_LEN
    w3 = NSA_R * TILE
    gw = NSA_G * TILE
    qspec = lambda h: pl.BlockSpec((TILE, TILE), lambda i, h=h: (i, BLK_QN + h))
    zspec = lambda h: pl.BlockSpec((TILE, TILE), lambda i, h=h: (i, BLK_ZN + h))
    whole = lambda shape: pl.BlockSpec(shape, lambda i: (0,) * len(shape))
    resident = dict(pipeline_mode=pl.Buffered(1))
    chunk_buf = pltpu.VMEM((NSA_G, NSA_CHUNK, w3), F32)
    acc_buf = pltpu.VMEM((NSA_G, HD + ONES_ROWS, w3), F32)
    return pl.pallas_call(
        functools.partial(_nsa_kernel, n_cmp=n_cmp),
        grid=(s // TILE,),
        in_specs=[qspec(h) for h in range(H_NSA)] + [zspec(h) for h in range(H_NSA)] + [
            pl.BlockSpec((s, gw), lambda i: (0, BLK_KS // NSA_G), **resident),
            pl.BlockSpec((s, gw), lambda i: (0, BLK_KW // NSA_G), **resident),
            pl.BlockSpec((nk, gw, TILE), lambda i: (0, H_FOX // NSA_G, 0), **resident),
            pl.BlockSpec((nk, gw, TILE), lambda i: (0, H_FOX // NSA_G + 1, 0), **resident),
            pl.BlockSpec((NSA_G, n_cmp, HD), lambda i: (0, 0, 0)),
            pl.BlockSpec((NSA_G, HD, n_cmp), lambda i: (1, 0, 0)),
            whole((n_sel, n_cmp)), whole((NSA_G, 24, w3)), whole((NSA_G, TILE, w3)),
            whole((NSA_G, TILE, w3)),
            pl.BlockSpec((NSA_G, NSA_R * 3, TILE), lambda i: (0, 0, i))],
        out_specs=pl.BlockSpec((TILE, D_NSA), lambda i: (i, 0)),
        out_shape=jax.ShapeDtypeStruct((s, D_NSA), BF16),
        scratch_shapes=[pltpu.VMEM((NSA_G, 16 + n_cmp + 8, w3), F32),
                        pltpu.VMEM((n_sel, gw), F32),
                        pltpu.VMEM((NSA_G, 1, w3), F32),
                        acc_buf, acc_buf,
                        chunk_buf,
                        pltpu.VMEM((NSA_G, WINDOW + TILE, w3), F32),
                        chunk_buf, chunk_buf, chunk_buf],
        compiler_params=_cparams(("parallel",)),
        name="nsa_attention",
    )(*([proj] * (2 * H_NSA)), proj, proj, proj_t, proj_t,
      cmp_n, cmp_t, c2s_t, cstrip, t0, t1, glog_t)


def _gated_conv(u_ref, gb_ref, gc_ref, z_ref, uh_ref, gch_ref, w_ref):
    tm = u_ref.shape[0]
    y = gc_ref[...].astype(F32) * u_ref[...].astype(F32)
    yh = gch_ref[...].astype(F32) * uh_ref[...].astype(F32)
    yh = jnp.where(pl.program_id(0) > 0, yh, 0.0)
    h1 = yh[15:16, :]
    h2 = yh[14:15, :]
    row = lax.broadcasted_iota(jnp.int32, (tm, D_CONV), 0)
    y1 = jnp.where(row == 0, h1, pltpu.roll(y, 1, axis=0))
    y2 = jnp.where(row == 0, h2, jnp.where(row == 1, h1, pltpu.roll(y, 2, axis=0)))
    w = w_ref[...]
    conv = w[0:1, :] * y2 + w[1:2, :] * y1 + w[2:3, :] * y
    return gb_ref[...].astype(F32) * conv * _silu(z_ref[...].astype(F32))


def _out_kernel(u_ref, gb_ref, gc_ref, z_ref, uh_ref, gch_ref, cw_ref, yf_ref, yn_ref, w_ref,
                x_ref, pg_ref, gate_ref, *rest, with_next):
    if with_next:
        ng_ref, nsc_ref, nsh_ref, xo_ref, h_ref, ya_ref = rest
    else:
        xo_ref, ya_ref = rest
    ya_ref[...] = _gated_conv(u_ref, gb_ref, gc_ref, z_ref, uh_ref, gch_ref, cw_ref).astype(BF16)
    tm = x_ref.shape[0]
    n_parts = 4
    for part in range(n_parts):
        rows = slice(part * tm // n_parts, (part + 1) * tm // n_parts)
        y = (jnp.dot(ya_ref[rows, :], w_ref[0:D_CONV, :], preferred_element_type=F32)
             + jnp.dot(yf_ref[rows, :], w_ref[D_CONV:D_CONV + D_FOX, :], preferred_element_type=F32)
             + jnp.dot(yn_ref[rows, :], w_ref[D_CONV + D_FOX:, :], preferred_element_type=F32))
        yn = y * lax.rsqrt(jnp.mean(y * y, axis=-1, keepdims=True) + NORM_EPS) * pg_ref[...]
        xn = x_ref[rows, :] + gate_ref[...] * yn
        if with_next:
            h_ref[rows, :] = _modulated_norm(xn, ng_ref[...], nsc_ref[...],
                                             nsh_ref[...]).astype(BF16)
        xo_ref[rows, :] = xn


def _out_proj(proj, conv_w, yf, yn, w_out, l, x2, post_g, gate, nxt):
    s, d = x2.shape
    tm = 512
    rows = lambda n: pl.BlockSpec((tm, n), lambda i: (i, 0))
    vec = pl.BlockSpec((1, d), lambda i: (0, 0))
    main = lambda c: pl.BlockSpec((tm, D_CONV), lambda i, c=c: (i, c))
    halo = lambda c: pl.BlockSpec((16, D_CONV), lambda i, c=c: (jnp.maximum(i * (tm // 16) - 1, 0), c))
    in_specs = [main(0), main(1), main(2), main(3), halo(0), halo(2),
                pl.BlockSpec((CONV_WIDTH, D_CONV), lambda i: (0, 0)),
                rows(D_FOX), rows(D_NSA),
                pl.BlockSpec((None, d, d), lambda i: (l, 0, 0), pipeline_mode=pl.Buffered(1)),
                rows(d), vec, vec]
    args = [proj, proj, proj, proj, proj, proj, conv_w, yf, yn, w_out, x2, post_g, gate]
    out_specs = [rows(d)]
    out_shape = [jax.ShapeDtypeStruct((s, d), F32)]
    if nxt is not None:
        in_specs += [vec, vec, vec]
        args += list(nxt)
        out_specs.append(rows(d))
        out_shape.append(jax.ShapeDtypeStruct((s, d), BF16))
    res = pl.pallas_call(
        functools.partial(_out_kernel, with_next=nxt is not None),
        grid=(s // tm,),
        in_specs=in_specs,
        out_specs=out_specs,
        out_shape=out_shape,
        scratch_shapes=[pltpu.VMEM((tm, D_CONV), BF16)],
        compiler_params=_cparams(("parallel",)),
        name="out_proj",
    )(*args)
    return res if nxt is not None else (res[0], None)


def _bucket_of_distance():
    max_exact = REL_BUCKETS // 2
    d = np.arange(REL_MAX_DIST)
    nf = np.maximum(d, max_exact).astype(np.float32)
    large = max_exact + (np.log(nf / np.float32(max_exact)) / np.float32(math.log(REL_MAX_DIST / max_exact))
                         * np.float32(REL_BUCKETS - max_exact)).astype(np.int32)
    return np.where(d < max_exact, d, np.minimum(large, REL_BUCKETS - 1))


def _distance_tables():
    b = np.arange(TILE)[:, None]
    a = np.arange(TILE)[None, :]
    none = REL_MAX_DIST
    d0 = np.where(a >= b, a - b, none)
    d1 = np.where(a < b, TILE + a - b, none)
    m = np.arange(24)[:, None]
    dc = a - CMP_STRIDE * m + (2 * TILE - (CMP_LEN - 1))
    dc = np.where((dc >= 0) & (dc < REL_MAX_DIST), dc, none)
    return d0, d1, dc


def _bias_tables(rel_bias):
    bucket = np.concatenate([_bucket_of_distance(), [REL_BUCKETS - 1]])
    rel = (rel_bias - rel_bias[REL_BUCKETS - 1:REL_BUCKETS, :]) * LOG2E
    rel = rel.reshape(REL_BUCKETS, NSA_G, NSA_R)

    def expand(idx):
        bk = bucket[idx]
        tab = jnp.zeros((NSA_G, idx.shape[0], NSA_R, TILE), F32)
        for b in range(REL_BUCKETS - 1):
            hit = jnp.asarray(bk == b)[None, :, None, :]
            tab = jnp.where(hit, rel[b][:, None, :, None], tab)
        return tab.reshape(NSA_G, idx.shape[0], NSA_R * TILE)

    d0, d1, dc = _distance_tables()
    return expand(d0), expand(d1), expand(dc)


def _cmp_to_sel_t(s, n_cmp_pad):
    n_cmp = (s - CMP_LEN) // CMP_STRIDE + 1
    n_sel = s // SEL_LEN
    c_lo = np.arange(n_cmp_pad)[None, :] * CMP_STRIDE
    sel_start = np.arange(n_sel)[:, None] * SEL_LEN
    overlap = (c_lo < sel_start + SEL_LEN) & (c_lo + CMP_LEN > sel_start)
    overlap &= np.arange(n_cmp_pad)[None, :] < n_cmp
    return jnp.asarray(overlap, dtype=BF16)


def _pack_w_in(w_in):
    sizes = ([D_CONV] * 4 + [D_FOX] * 3 + [H_FOX, D_FOX] + [D_NSA] + [2 * HD] * 6
             + [3 * H_NSA, D_NSA])
    offs = np.concatenate([[0], np.cumsum(sizes)])
    names = ["u", "gb", "gc", "za", "qf", "kf", "vf", "ff", "zf", "qn", "kc", "vc", "ks", "vs",
             "kw", "vw", "gn", "zn"]
    wt = jnp.swapaxes(w_in, 1, 2)
    col = {n: wt[:, int(offs[k]):int(offs[k + 1]), :] for k, n in enumerate(names)}
    scale = HD ** -0.5 * LOG2E
    w_a = jnp.concatenate([col["u"], col["gb"], col["gc"], col["za"],
                           col["qn"] * scale, col["zn"], col["ks"], col["kw"],
                           col["qf"] * scale, col["kf"], col["zf"]], axis=1).astype(BF16)
    w_kv = jnp.concatenate([col["kc"], col["vc"]], axis=1).astype(BF16)
    w_v = jnp.concatenate([col["vf"], col["vs"], col["vw"]], axis=1).astype(BF16)
    small = jnp.concatenate([col["ff"], col["gn"]], axis=1)
    small = jnp.pad(small, ((0, 0), (0, TILE - small.shape[1]), (0, 0))).astype(BF16)
    return w_a, w_kv, w_v, small


def kernel(x, c, w_ada, b_ada, pre_norm, post_norm, w_in, b_forget, conv_w,
           cmp_pe_k, cmp_w1_k, cmp_w2_k, cmp_pe_v, cmp_w1_v, cmp_w2_v, w_out, rel_bias):
    bsz, s, d = x.shape
    assert bsz == 1 and d == D_MODEL and s % PROJ_TM == 0 and s >= 2 * WINDOW
    depth = w_in.shape[0]
    x2 = x.reshape(s, d)
    n_cmp_pad = s // CMP_STRIDE

    mod = _ada_mod(c.reshape(d, 1), w_ada, b_ada)
    shift, scale, gate = mod[:, :, :d], mod[:, :, d:2 * d], mod[:, :, 2 * d:]

    w_a, w_kv, w_v, w_small = _pack_w_in(w_in)
    w_out_b = w_out.astype(BF16)
    half = CMP_LEN * HD // 2
    w1cat = jnp.stack([jnp.concatenate([w[:, :half], w[:, half:]], axis=-1)
                       for w in (cmp_w1_k, cmp_w1_v)], axis=1).astype(BF16)
    pe = jnp.stack([cmp_pe_k, cmp_pe_v], axis=1).reshape(depth, 2, 2, 1, half)
    pe16 = jnp.broadcast_to(pe, (depth, 2, 2, 8, half)).reshape(depth, 2, 16, half).astype(BF16)
    w2 = jnp.stack([cmp_w2_k, cmp_w2_v], axis=1).astype(BF16)
    b_pad = jnp.pad(b_forget, ((0, 0), (0, TILE - H_FOX))).reshape(depth, 1, TILE)
    t0, t1, cstrip = _bias_tables(rel_bias)
    c2s_t = _cmp_to_sel_t(s, n_cmp_pad)

    h = _prenorm(x2, pre_norm[0:1], scale[0], shift[0])
    for l in range(depth):
        proj = _matmul(h, w_a, l)
        kcvc = _matmul_split(h, w_kv, l)
        proj_t = _matmul_t_tiled(h, w_v, l)

        qa, ka, small_t = _fox_prep(h, w_small, l, b_pad[l])
        yf = _fox_attention(proj, qa, ka, proj_t)

        x4 = kcvc.reshape(2 * NSA_G, n_cmp_pad, CMP_STRIDE * HD)
        cmp_n, cmp_t = _compress(x4, w1cat, pe16, w2, l)
        glog_t = small_t[H_FOX:H_FOX + 3 * H_NSA].reshape(NSA_G, NSA_R * 3, s)
        yn = _nsa_attention(proj, proj_t, cmp_n, cmp_t, c2s_t, cstrip, t0, t1, glog_t)

        nxt = None
        if l + 1 < depth:
            nxt = (pre_norm[l + 1:l + 2], scale[l + 1], shift[l + 1])
        x2, h = _out_proj(proj, conv_w[l], yf, yn, w_out_b, l, x2, post_norm[l:l + 1], gate[l], nxt)
    return x2.reshape(bsz, s, d)
```

```python
import functools
import math

import numpy as np
import jax
import jax.numpy as jnp
from jax import lax
from jax.experimental import pallas as pl
from jax.experimental.pallas import tpu as pltpu

F32 = jnp.float32
BF16 = jnp.bfloat16

D_MODEL = 2048
HD = 128
D_CONV = 512
D_FOX = 768
D_NSA = 768
H_FOX = 6
H_NSA = 6
NSA_G = 2
NSA_R = 3
CONV_WIDTH = 3
CMP_LEN = 32
CMP_STRIDE = 16
CMP_HIDDEN = 256
SEL_LEN = 64
SEL_TOPK = 16
N_FORCED = 3
WINDOW = 512
REL_BUCKETS = 32
REL_MAX_DIST = 128
NORM_EPS = 1e-6
NEG_BIG = -1e30
LOG2E = math.log2(math.e)

TILE = 128
ONES_ROWS = 16
NSA_CHUNK = 512
FOX_T = 512
FOX_TK = 512
FOX_HEADS = 2
PROJ_TM = 1024
PROJ_TN = 1280
PROJ_TV = 1280
VMEM_LIMIT = 56 * 1024 * 1024

NT_DIMS = (((1,), (1,)), ((), ()))

BLK_QN, BLK_ZN, BLK_KS, BLK_KW = 16, 22, 28, 30
BLK_QF, BLK_KF, BLK_ZF = 32, 38, 44
N_A = 6400


def _cparams(sem, vmem=VMEM_LIMIT):
    return pltpu.CompilerParams(dimension_semantics=sem, vmem_limit_bytes=vmem)


def _silu(v):
    return v * jax.nn.sigmoid(v)


def _mod_kernel(c_ref, w_ref, b_ref, o_ref):
    ca = _silu(c_ref[...])
    o_ref[0] = jnp.sum(w_ref[0] * ca, axis=0, keepdims=True) + b_ref[0]


def _ada_mod(c_col, w_ada, b_ada):
    depth, d, n = w_ada.shape
    tn = 1536
    return pl.pallas_call(
        _mod_kernel,
        grid=(depth, n // tn),
        in_specs=[pl.BlockSpec((d, 1), lambda l, j: (0, 0)),
                  pl.BlockSpec((1, d, tn), lambda l, j: (l, 0, j)),
                  pl.BlockSpec((1, 1, tn), lambda l, j: (l, 0, j))],
        out_specs=pl.BlockSpec((1, 1, tn), lambda l, j: (l, 0, j)),
        out_shape=jax.ShapeDtypeStruct((depth, 1, n), F32),
        compiler_params=_cparams(("parallel", "parallel")),
        name="ada_mod",
    )(c_col, w_ada, b_ada.reshape(depth, 1, n))


def _modulated_norm(x, g, scale, shift):
    y = x * lax.rsqrt(jnp.mean(x * x, axis=-1, keepdims=True) + NORM_EPS) * g
    return y * (1.0 + scale) + shift


def _prenorm_kernel(x_ref, g_ref, sc_ref, sh_ref, h_ref):
    h_ref[...] = _modulated_norm(x_ref[...], g_ref[...], sc_ref[...], sh_ref[...]).astype(BF16)


def _prenorm(x2, g, scale, shift):
    s, d = x2.shape
    tm = 512
    vec = pl.BlockSpec((1, d), lambda i: (0, 0))
    return pl.pallas_call(
        _prenorm_kernel,
        grid=(s // tm,),
        in_specs=[pl.BlockSpec((tm, d), lambda i: (i, 0)), vec, vec, vec],
        out_specs=pl.BlockSpec((tm, d), lambda i: (i, 0)),
        out_shape=jax.ShapeDtypeStruct((s, d), BF16),
        compiler_params=_cparams(("parallel",)),
        name="prenorm",
    )(x2, g, scale, shift)


def _mm_kernel(h_ref, wt_ref, o_ref):
    o_ref[...] = lax.dot_general(h_ref[...], wt_ref[...], NT_DIMS,
                                 preferred_element_type=F32).astype(o_ref.dtype)


def _matmul(h, wt, l):
    s, d = h.shape
    n = wt.shape[1]
    tm, tn = PROJ_TM, PROJ_TN
    return pl.pallas_call(
        _mm_kernel,
        grid=(s // tm, n // tn),
        in_specs=[pl.BlockSpec((tm, d), lambda i, j: (i, 0)),
                  pl.BlockSpec((None, tn, d), lambda i, j: (l, j, 0))],
        out_specs=pl.BlockSpec((tm, tn), lambda i, j: (i, j)),
        out_shape=jax.ShapeDtypeStruct((s, n), BF16),
        compiler_params=_cparams(("parallel", "arbitrary")),
        name="proj_rows",
    )(h, wt)


def _mm_split_kernel(h_ref, wt_ref, o_ref):
    res = lax.dot_general(h_ref[...], wt_ref[...], NT_DIMS, preferred_element_type=F32)
    for c in range(o_ref.shape[0]):
        o_ref[c] = res[:, c * TILE:(c + 1) * TILE].astype(o_ref.dtype)


def _matmul_split(h, wt, l):
    s, d = h.shape
    n = wt.shape[1]
    tm = PROJ_TM
    return pl.pallas_call(
        _mm_split_kernel,
        grid=(s // tm,),
        in_specs=[pl.BlockSpec((tm, d), lambda i: (i, 0)),
                  pl.BlockSpec((None, n, d), lambda i: (l, 0, 0))],
        out_specs=pl.BlockSpec((n // TILE, tm, TILE), lambda i: (0, i, 0)),
        out_shape=jax.ShapeDtypeStruct((n // TILE, s, TILE), BF16),
        compiler_params=_cparams(("parallel",)),
        name="proj_rows_split",
    )(h, wt)


def _mm_t_tiled_kernel(h_ref, wt_ref, o_ref):
    res = lax.dot_general(wt_ref[...], h_ref[...], NT_DIMS, preferred_element_type=F32)
    for c in range(o_ref.shape[0]):
        o_ref[c] = res[:, c * TILE:(c + 1) * TILE].astype(o_ref.dtype)


def _matmul_t_tiled(h, wt, l):
    s, d = h.shape
    n = wt.shape[1]
    tm, tn = PROJ_TM, PROJ_TV
    return pl.pallas_call(
        _mm_t_tiled_kernel,
        grid=(s // tm, n // tn),
        in_specs=[pl.BlockSpec((tm, d), lambda i, j: (i, 0)),
                  pl.BlockSpec((None, tn, d), lambda i, j: (l, j, 0))],
        out_specs=pl.BlockSpec((tm // TILE, tn, TILE), lambda i, j: (i, j, 0)),
        out_shape=jax.ShapeDtypeStruct((s // TILE, n, TILE), BF16),
        compiler_params=_cparams(("parallel", "arbitrary")),
        name="proj_cols_tiled",
    )(h, wt)


def _fox_prep_kernel(h_ref, w_ref, b_ref, qa_ref, ka_ref, st_ref, carry_ref, *, tm):
    @pl.when(pl.program_id(0) == 0)
    def _():
        carry_ref[...] = jnp.zeros_like(carry_ref)

    small = lax.dot_general(h_ref[...], w_ref[...], NT_DIMS,
                            preferred_element_type=F32)
    st_ref[...] = small.T
    v = small + b_ref[...]
    c = jnp.minimum(v, 0.0) - jnp.log1p(jnp.exp(-jnp.abs(v)))
    row = lax.broadcasted_iota(jnp.int32, (tm, TILE), 0)
    sh = 1
    while sh < tm:
        c = c + jnp.where(row >= sh, pltpu.roll(c, sh, axis=0), 0.0)
        sh *= 2
    c = c + carry_ref[...]
    carry_ref[...] = c[tm - 1:tm, :]
    c = c * LOG2E

    lane = lax.broadcasted_iota(jnp.int32, (tm, TILE), 1)
    for h in range(H_FOX):
        ch = jnp.broadcast_to(c[:, h:h + 1], (tm, TILE))
        hi = ch.astype(BF16).astype(F32)
        r1 = ch - hi
        mid = r1.astype(BF16).astype(F32)
        lo = r1 - mid
        pieces = jnp.where((lane == 0) | (lane == 3), hi,
                           jnp.where((lane == 1) | (lane == 4), mid, lo))
        qa = jnp.where(lane < 3, pieces, jnp.where(lane < 6, 1.0, 0.0))
        ka = jnp.where(lane < 3, 1.0, jnp.where(lane < 6, -pieces, 0.0))
        qa_ref[:, h * TILE:(h + 1) * TILE] = qa.astype(BF16)
        ka_ref[:, h * TILE:(h + 1) * TILE] = ka.astype(BF16)


def _fox_prep(h, w_small, l, b_pad):
    s, d = h.shape
    tm = 512
    out = jax.ShapeDtypeStruct((s, H_FOX * TILE), BF16)
    return pl.pallas_call(
        functools.partial(_fox_prep_kernel, tm=tm),
        grid=(s // tm,),
        in_specs=[pl.BlockSpec((tm, d), lambda i: (i, 0)),
                  pl.BlockSpec((None, TILE, d), lambda i: (l, 0, 0)),
                  pl.BlockSpec((1, TILE), lambda i: (0, 0))],
        out_specs=[pl.BlockSpec((tm, H_FOX * TILE), lambda i: (i, 0)),
                   pl.BlockSpec((tm, H_FOX * TILE), lambda i: (i, 0)),
                   pl.BlockSpec((TILE, tm), lambda i: (0, i))],
        out_shape=[out, out, jax.ShapeDtypeStruct((TILE, s), F32)],
        scratch_shapes=[pltpu.VMEM((1, TILE), F32)],
        compiler_params=_cparams(("arbitrary",)),
        name="fox_prep",
    )(h, w_small, b_pad)


def _fox_kernel(q_ref, qa_ref, k_ref, ka_ref, vt_ref, z_ref, o_ref, s0_ref, s1_ref, s2_ref,
                acc_ref):
    i = pl.program_id(1)
    t = FOX_T
    tk = FOX_TK
    heads = range(FOX_HEADS)
    col = lambda h: slice(h * TILE, (h + 1) * TILE)
    qaug = [jnp.concatenate([q_ref[:, col(h)], qa_ref[:, col(h)]], axis=1) for h in heads]

    def scores(h, j, rows):
        r0 = pl.multiple_of(j * rows, rows)
        kaug = jnp.concatenate([k_ref[pl.ds(r0, rows), col(h)], ka_ref[pl.ds(r0, rows), col(h)]],
                               axis=1)
        return lax.dot_general(kaug, qaug[h], NT_DIMS, preferred_element_type=F32)

    def values_t(h, j, rows):
        n_sub = rows // TILE
        return jnp.concatenate(
            [jnp.concatenate([vt_ref[n_sub * j + c, col(h), :] for c in range(n_sub)], axis=1),
             jnp.ones((ONES_ROWS, rows), BF16)], axis=0)

    kio = lax.broadcasted_iota(jnp.int32, (t, t), 0)
    qio = lax.broadcasted_iota(jnp.int32, (t, t), 1)
    ms = []
    for h in heads:
        s = jnp.where(kio <= qio, scores(h, i, t), NEG_BIG)
        m = jnp.max(s, axis=0, keepdims=True)
        p = jnp.exp2(s - m)
        ms.append(m)
        acc_ref[h] = jnp.dot(values_t(h, i, t), p.astype(BF16), preferred_element_type=F32)

    def update(h, s, vt, m):
        m_new = jnp.maximum(m, jnp.max(s, axis=0, keepdims=True))
        alpha = jnp.exp2(m - m_new)
        p = jnp.exp2(s - m_new)
        acc_ref[h] = alpha * acc_ref[h] + jnp.dot(vt, p.astype(BF16), preferred_element_type=F32)
        return m_new

    bufs = (s0_ref, s1_ref, s2_ref)
    for h in heads:
        s0_ref[h] = scores(h, 0, tk)

    def triple(pp, ms):
        ms = list(ms)
        for stage in range(3):
            cur, nxt = bufs[stage], bufs[(stage + 1) % 3]
            for h in heads:
                nxt[h] = scores(h, 3 * pp + stage + 1, tk)
            for h in heads:
                ms[h] = update(h, cur[h], values_t(h, 3 * pp + stage, tk), ms[h])
        return tuple(ms)

    n_far = i * (t // tk)
    ms = lax.fori_loop(0, n_far // 3, triple, tuple(ms))
    done = (n_far // 3) * 3

    @pl.when(n_far - done == 1)
    def _():
        for h in heads:
            update(h, s0_ref[h], values_t(h, done, tk), ms[h])

    @pl.when(n_far - done == 2)
    def _():
        for h in heads:
            s1_ref[h] = scores(h, done + 1, tk)
        for h in heads:
            m_mid = update(h, s0_ref[h], values_t(h, done, tk), ms[h])
            update(h, s1_ref[h], values_t(h, done + 1, tk), m_mid)

    for h in heads:
        o = (acc_ref[h, 0:HD, :] * (1.0 / acc_ref[h, HD:HD + 1, :])).T
        o_ref[:, col(h)] = (o * _silu(z_ref[:, col(h)].astype(F32))).astype(BF16)


def _fox_attention(proj, qa, ka, proj_t):
    s = proj.shape[0]
    t = FOX_T
    nk = s // TILE
    hw = FOX_HEADS * TILE
    score_buf = pltpu.VMEM((FOX_HEADS, FOX_TK, t), F32)
    return pl.pallas_call(
        _fox_kernel,
        grid=(H_FOX // FOX_HEADS, s // t),
        scratch_shapes=[score_buf, score_buf, score_buf,
                        pltpu.VMEM((FOX_HEADS, HD + ONES_ROWS, t), F32)],
        in_specs=[pl.BlockSpec((t, hw), lambda h, i: (i, BLK_QF // FOX_HEADS + h)),
                  pl.BlockSpec((t, hw), lambda h, i: (i, h)),
                  pl.BlockSpec((s, hw), lambda h, i: (0, BLK_KF // FOX_HEADS + h)),
                  pl.BlockSpec((s, hw), lambda h, i: (0, h)),
                  pl.BlockSpec((nk, hw, TILE), lambda h, i: (0, h, 0)),
                  pl.BlockSpec((t, hw), lambda h, i: (i, BLK_ZF // FOX_HEADS + h))],
        out_specs=pl.BlockSpec((t, hw), lambda h, i: (i, h)),
        out_shape=jax.ShapeDtypeStruct((s, D_FOX), BF16),
        compiler_params=_cparams(("parallel", "parallel")),
        name="fox_attention",
    )(proj, qa, proj, ka, proj_t, proj)


def _compress_kernel(x_ref, w1_ref, pe_ref, w2_ref, o_ref, ot_ref):
    n = x_ref.shape[0]
    ab = jnp.dot(x_ref[...], w1_ref[0], preferred_element_type=F32)
    pe = jnp.dot(pe_ref[0], w1_ref[0], preferred_element_type=F32)
    a = ab[:, :CMP_HIDDEN]
    b_next = pltpu.roll(ab[:, CMP_HIDDEN:], n - 1, axis=0)
    hid = a + b_next + pe[0:1, :CMP_HIDDEN] + pe[8:9, CMP_HIDDEN:]
    out = jnp.dot(_silu(hid).astype(BF16), w2_ref[0], preferred_element_type=F32)
    o_ref[...] = out.astype(BF16)
    ot_ref[...] = out.T.astype(BF16)


def _compress(x4, w1cat, pe16, w2, l):
    _, n, k = x4.shape
    kind = lambda c: (l, c // NSA_G, 0, 0)
    return pl.pallas_call(
        _compress_kernel,
        grid=(2 * NSA_G,),
        in_specs=[pl.BlockSpec((None, n, k), lambda c: (c, 0, 0)),
                  pl.BlockSpec((None, 1, k, 2 * CMP_HIDDEN), kind),
                  pl.BlockSpec((None, 1, 16, k), kind),
                  pl.BlockSpec((None, 1, CMP_HIDDEN, HD), kind)],
        out_specs=[pl.BlockSpec((None, n, HD), lambda c: (c, 0, 0)),
                   pl.BlockSpec((None, HD, n), lambda c: (c, 0, 0))],
        out_shape=[jax.ShapeDtypeStruct((2 * NSA_G, n, HD), BF16),
                   jax.ShapeDtypeStruct((2 * NSA_G, HD, n), BF16)],
        compiler_params=_cparams(("parallel",)),
        name="nsa_compress",
    )(x4, w1cat, pe16, w2)


def _nsa_kernel(*refs, n_cmp):
    q_refs, z_refs = refs[0:H_NSA], refs[H_NSA:2 * H_NSA]
    (ks_ref, kw_ref, vst_ref, vwt_ref, kc_ref, vct_ref, c2s_ref, cstrip_ref, t0_ref, t1_ref,
     gl_ref, o_ref, sc_ref, sel_ref, m_s, acc_s, acc_w, ss_ref, sw_ref, sa_ref, sb_ref,
     sc3_ref) = refs[2 * H_NSA:]
    i = pl.program_id(0)
    w3 = NSA_R * TILE
    gw = NSA_G * TILE
    groups = range(NSA_G)
    col = lambda g: slice(g * TILE, (g + 1) * TILE)
    qs = [jnp.concatenate([q_refs[NSA_R * g + r][...] for r in range(NSA_R)], axis=0)
          for g in groups]
    b_io = lax.broadcasted_iota(jnp.int32, (TILE, w3), 0)
    a_io = lax.broadcasted_iota(jnp.int32, (TILE, w3), 1) & (TILE - 1)
    tpc = NSA_CHUNK // TILE
    bpc = NSA_CHUNK // SEL_LEN

    def key_scores(k_ref, g, tile0, n_tiles):
        r0 = pl.multiple_of(tile0 * TILE, TILE)
        return lax.dot_general(k_ref[pl.ds(r0, n_tiles * TILE), col(g)], qs[g], NT_DIMS,
                               preferred_element_type=F32)

    def values_t(vt_ref, g, tile0, n_tiles):
        return jnp.concatenate(
            [jnp.concatenate([vt_ref[tile0 + c, col(g), :] for c in range(n_tiles)], axis=1),
             jnp.ones((ONES_ROWS, n_tiles * TILE), BF16)], axis=0)

    def fix_near(s_ref, g, d):
        own = pl.ds(pl.multiple_of(d * TILE, TILE), TILE)
        prev = pl.ds(pl.multiple_of(jnp.maximum(d - 1, 0) * TILE, TILE), TILE)
        s_ref[g, prev, :] = s_ref[g, prev, :] + jnp.where(d >= 1, t1_ref[g], 0.0)
        s_ref[g, own, :] = jnp.where(b_io <= a_io, s_ref[g, own, :] + t0_ref[g], NEG_BIG)

    w0 = pl.multiple_of(8 * i, 8)
    n_io = lax.broadcasted_iota(jnp.int32, (n_cmp, w3), 0)
    qa_io = lax.broadcasted_iota(jnp.int32, (n_cmp, w3), 1) & (TILE - 1)
    valid = (CMP_STRIDE * n_io + (CMP_LEN - 1)) <= (TILE * i + qa_io)
    n_wt = WINDOW // TILE + 1
    w_tile0 = jnp.maximum(i - WINDOW // TILE, 0)
    d_win = i - w_tile0
    for g in groups:
        sc_ref[g, 0:16, :] = jnp.zeros((16, w3), F32)
        sc_ref[g, 16:16 + n_cmp, :] = lax.dot_general(kc_ref[g], qs[g], NT_DIMS,
                                                      preferred_element_type=F32)
        sw_ref[g] = key_scores(kw_ref, g, w_tile0, n_wt)
    oc_t, psums = [], []
    for g in groups:
        sc_ref[g, pl.ds(w0, 24), :] = sc_ref[g, pl.ds(w0, 24), :] + cstrip_ref[g]
        scm = jnp.where(valid, sc_ref[g, 16:16 + n_cmp, :], NEG_BIG)
        mc = jnp.max(scm, axis=0, keepdims=True)
        pc = jnp.exp2(scm - mc)
        lc = jnp.sum(pc, axis=0, keepdims=True)
        pcn = pc * jnp.where(mc > 0.5 * NEG_BIG, 1.0 / lc, 0.0)
        oc_t.append(jnp.dot(vct_ref[g], pcn.astype(BF16), preferred_element_type=F32))
        psums.append(pcn[:, 0:TILE] + pcn[:, TILE:2 * TILE] + pcn[:, 2 * TILE:3 * TILE])

        for c in range(1, n_wt):
            rows = slice(c * TILE, (c + 1) * TILE)
            sw_ref[g, rows, :] = jnp.where(d_win < c, NEG_BIG, sw_ref[g, rows, :])
        fix_near(sw_ref, g, d_win)
        sw_ref[g, 0:TILE, :] = jnp.where((a_io < b_io) | (i < WINDOW // TILE),
                                         sw_ref[g, 0:TILE, :], NEG_BIG)
        sw = sw_ref[g]
        p_w = jnp.exp2(sw - jnp.max(sw, axis=0, keepdims=True))
        acc_w[g] = jnp.dot(values_t(vwt_ref, g, w_tile0, n_wt), p_w.astype(BF16),
                           preferred_element_type=F32)

    psum = jnp.concatenate(psums, axis=1)
    p_hi = psum.astype(BF16)
    p_lo = (psum - p_hi.astype(F32)).astype(BF16)
    c2s = c2s_ref[...]
    imp = (jnp.dot(c2s, p_hi, preferred_element_type=F32)
           + jnp.dot(c2s, p_lo, preferred_element_type=F32))
    n_sel = imp.shape[0]
    m_io = lax.broadcasted_iota(jnp.int32, (n_sel, gw), 0)
    m_f = m_io.astype(F32)
    qpos = TILE * i + (lax.broadcasted_iota(jnp.int32, (n_sel, gw), 1) & (TILE - 1))
    own = qpos >> 6
    forced = (m_io == 0) | (m_io == own) | (m_io == own - 1)
    eligible = SEL_LEN * m_io <= qpos
    sel = jnp.where(eligible & forced, 1.0, 0.0)
    cur = jnp.where(eligible, jnp.where(forced, -2.0, imp), -1.0)
    for _ in range(min(SEL_TOPK, n_sel) - N_FORCED):
        mx = jnp.max(cur, axis=0, keepdims=True)
        idx = jnp.min(jnp.where(cur == mx, m_f, float(n_sel)), axis=0, keepdims=True)
        pick = m_f == idx
        sel = jnp.where(pick, jnp.maximum(jnp.where(mx >= 0.0, 1.0, 0.0), sel), sel)
        cur = jnp.where(pick, -2.0, cur)
    sel_ref[...] = sel

    def sel_mask(g, c, n_tiles):
        rows = sel_ref[pl.ds(pl.multiple_of(c * bpc, 8), bpc), col(g)]
        mk = jnp.concatenate([jnp.broadcast_to(rows[b:b + 1, :], (SEL_LEN, TILE))
                              for b in range(n_tiles * TILE // SEL_LEN)], axis=0)
        return jnp.concatenate([mk, mk, mk], axis=1) > 0.5

    def sel_update(g, c, s, n_tiles=tpc):
        s = jnp.where(sel_mask(g, c, n_tiles), s, NEG_BIG)
        m_old = m_s[g]
        m_new = jnp.maximum(m_old, jnp.max(s, axis=0, keepdims=True))
        alpha = jnp.exp2(m_old - m_new)
        p = jnp.exp2(s - m_new)
        acc_s[g] = alpha * acc_s[g] + jnp.dot(values_t(vst_ref, g, c * tpc, n_tiles),
                                              p.astype(BF16), preferred_element_type=F32)
        m_s[g] = m_new

    m_s[...] = jnp.full(m_s.shape, NEG_BIG, F32)
    acc_s[...] = jnp.zeros(acc_s.shape, F32)

    c_own = i // tpc
    c_prev = jnp.maximum(i - 1, 0) // tpc

    bufs = (sa_ref, sb_ref, sc3_ref)
    for g in groups:
        sa_ref[g] = key_scores(ks_ref, g, 0, tpc)

    def far_triple(pp, carry):
        for stage in range(3):
            cur_buf, nxt_buf = bufs[stage], bufs[(stage + 1) % 3]
            for g in groups:
                nxt_buf[g] = key_scores(ks_ref, g, (3 * pp + stage + 1) * tpc, tpc)
            for g in groups:
                sel_update(g, 3 * pp + stage, cur_buf[g])
        return carry

    lax.fori_loop(0, c_prev // 3, far_triple, 0)
    done = (c_prev // 3) * 3

    @pl.when(c_prev - done == 1)
    def _():
        for g in groups:
            sel_update(g, done, sa_ref[g])

    @pl.when(c_prev - done == 2)
    def _():
        for g in groups:
            sb_ref[g] = key_scores(ks_ref, g, (done + 1) * tpc, tpc)
        for g in groups:
            sel_update(g, done, sa_ref[g])
            sel_update(g, done + 1, sb_ref[g])

    @pl.when(c_prev != c_own)
    def _():
        last = slice((tpc - 1) * TILE, tpc * TILE)
        for g in groups:
            ss_ref[g] = key_scores(ks_ref, g, c_prev * tpc, tpc)
            ss_ref[g, last, :] = ss_ref[g, last, :] + t1_ref[g]
            sel_update(g, c_prev, ss_ref[g])

    for n_t in range(1, tpc + 1):
        @pl.when(i - c_own * tpc == n_t - 1)
        def _(n_t=n_t):
            own = slice((n_t - 1) * TILE, n_t * TILE)
            prev = slice((n_t - 2) * TILE, (n_t - 1) * TILE)
            for g in groups:
                ss_ref[g, 0:n_t * TILE, :] = key_scores(ks_ref, g, c_own * tpc, n_t)
            for g in groups:
                if n_t >= 2:
                    ss_ref[g, prev, :] = ss_ref[g, prev, :] + t1_ref[g]
                ss_ref[g, own, :] = jnp.where(b_io <= a_io, ss_ref[g, own, :] + t0_ref[g], NEG_BIG)
                sel_update(g, c_own, ss_ref[g, 0:n_t * TILE, :], n_t)

    for g in groups:
        gates = jax.nn.sigmoid(gl_ref[g])

        def gate_row(branch):
            return jnp.concatenate([gates[r * 3 + branch:r * 3 + branch + 1, :]
                                    for r in range(NSA_R)], axis=1)

        out_t = (oc_t[g] * gate_row(0)
                 + acc_s[g, 0:HD, :] * (gate_row(1) / acc_s[g, HD:HD + 1, :])
                 + acc_w[g, 0:HD, :] * (gate_row(2) / acc_w[g, HD:HD + 1, :]))
        for r in range(NSA_R):
            h = NSA_R * g + r
            o_r = out_t[:, r * TILE:(r + 1) * TILE].T
            o_ref[:, h * TILE:(h + 1) * TILE] = (
                o_r * _silu(z_refs[h][...].astype(F32))).astype(BF16)


def _nsa_attention(proj, proj_t, cmp_n, cmp_t, c2s_t, cstrip, t0, t1, glog_t):
    s = proj.shape[0]
    nk = s // TILE
    n_cmp = cmp_n.shape[1]
    n_sel = s // SEL_LEN
    w3 = NSA_R * TILE
    gw = NSA_G * TILE
    qspec = lambda h: pl.BlockSpec((TILE, TILE), lambda i, h=h: (i, BLK_QN + h))
    zspec = lambda h: pl.BlockSpec((TILE, TILE), lambda i, h=h: (i, BLK_ZN + h))
    whole = lambda shape: pl.BlockSpec(shape, lambda i: (0,) * len(shape))
    resident = dict(pipeline_mode=pl.Buffered(1))
    chunk_buf = pltpu.VMEM((NSA_G, NSA_CHUNK, w3), F32)
    acc_buf = pltpu.VMEM((NSA_G, HD + ONES_ROWS, w3), F32)
    return pl.pallas_call(
        functools.partial(_nsa_kernel, n_cmp=n_cmp),
        grid=(s // TILE,),
        in_specs=[qspec(h) for h in range(H_NSA)] + [zspec(h) for h in range(H_NSA)] + [
            pl.BlockSpec((s, gw), lambda i: (0, BLK_KS // NSA_G), **resident),
            pl.BlockSpec((s, gw), lambda i: (0, BLK_KW // NSA_G), **resident),
            pl.BlockSpec((nk, gw, TILE), lambda i: (0, H_FOX // NSA_G, 0), **resident),
            pl.BlockSpec((nk, gw, TILE), lambda i: (0, H_FOX // NSA_G + 1, 0), **resident),
            pl.BlockSpec((NSA_G, n_cmp, HD), lambda i: (0, 0, 0)),
            pl.BlockSpec((NSA_G, HD, n_cmp), lambda i: (1, 0, 0)),
            whole((n_sel, n_cmp)), whole((NSA_G, 24, w3)), whole((NSA_G, TILE, w3)),
            whole((NSA_G, TILE, w3)),
            pl.BlockSpec((NSA_G, NSA_R * 3, TILE), lambda i: (0, 0, i))],
        out_specs=pl.BlockSpec((TILE, D_NSA), lambda i: (i, 0)),
        out_shape=jax.ShapeDtypeStruct((s, D_NSA), BF16),
        scratch_shapes=[pltpu.VMEM((NSA_G, 16 + n_cmp + 8, w3), F32),
                        pltpu.VMEM((n_sel, gw), F32),
                        pltpu.VMEM((NSA_G, 1, w3), F32),
                        acc_buf, acc_buf,
                        chunk_buf,
                        pltpu.VMEM((NSA_G, WINDOW + TILE, w3), F32),
                        chunk_buf, chunk_buf, chunk_buf],
        compiler_params=_cparams(("parallel",)),
        name="nsa_attention",
    )(*([proj] * (2 * H_NSA)), proj, proj, proj_t, proj_t,
      cmp_n, cmp_t, c2s_t, cstrip, t0, t1, glog_t)


def _gated_conv(u_ref, gb_ref, gc_ref, z_ref, uh_ref, gch_ref, w_ref):
    tm = u_ref.shape[0]
    y = gc_ref[...].astype(F32) * u_ref[...].astype(F32)
    yh = gch_ref[...].astype(F32) * uh_ref[...].astype(F32)
    yh = jnp.where(pl.program_id(0) > 0, yh, 0.0)
    h1 = yh[15:16, :]
    h2 = yh[14:15, :]
    row = lax.broadcasted_iota(jnp.int32, (tm, D_CONV), 0)
    y1 = jnp.where(row == 0, h1, pltpu.roll(y, 1, axis=0))
    y2 = jnp.where(row == 0, h2, jnp.where(row == 1, h1, pltpu.roll(y, 2, axis=0)))
    w = w_ref[...]
    conv = w[0:1, :] * y2 + w[1:2, :] * y1 + w[2:3, :] * y
    return gb_ref[...].astype(F32) * conv * _silu(z_ref[...].astype(F32))


def _out_kernel(u_ref, gb_ref, gc_ref, z_ref, uh_ref, gch_ref, cw_ref, yf_ref, yn_ref, w_ref,
                x_ref, pg_ref, gate_ref, *rest, with_next):
    if with_next:
        ng_ref, nsc_ref, nsh_ref, xo_ref, h_ref, ya_ref = rest
    else:
        xo_ref, ya_ref = rest
    ya_ref[...] = _gated_conv(u_ref, gb_ref, gc_ref, z_ref, uh_ref, gch_ref, cw_ref).astype(BF16)
    tm = x_ref.shape[0]
    n_parts = 4
    for part in range(n_parts):
        rows = slice(part * tm // n_parts, (part + 1) * tm // n_parts)
        y = (jnp.dot(ya_ref[rows, :], w_ref[0:D_CONV, :], preferred_element_type=F32)
             + jnp.dot(yf_ref[rows, :], w_ref[D_CONV:D_CONV + D_FOX, :], preferred_element_type=F32)
             + jnp.dot(yn_ref[rows, :], w_ref[D_CONV + D_FOX:, :], preferred_element_type=F32))
        yn = y * lax.rsqrt(jnp.mean(y * y, axis=-1, keepdims=True) + NORM_EPS) * pg_ref[...]
        xn = x_ref[rows, :] + gate_ref[...] * yn
        if with_next:
            h_ref[rows, :] = _modulated_norm(xn, ng_ref[...], nsc_ref[...],
                                             nsh_ref[...]).astype(BF16)
        xo_ref[rows, :] = xn


def _out_proj(proj, conv_w, yf, yn, w_out, l, x2, post_g, gate, nxt):
    s, d = x2.shape
    tm = 512
    rows = lambda n: pl.BlockSpec((tm, n), lambda i: (i, 0))
    vec = pl.BlockSpec((1, d), lambda i: (0, 0))
    main = lambda c: pl.BlockSpec((tm, D_CONV), lambda i, c=c: (i, c))
    halo = lambda c: pl.BlockSpec((16, D_CONV), lambda i, c=c: (jnp.maximum(i * (tm // 16) - 1, 0), c))
    in_specs = [main(0), main(1), main(2), main(3), halo(0), halo(2),
                pl.BlockSpec((CONV_WIDTH, D_CONV), lambda i: (0, 0)),
                rows(D_FOX), rows(D_NSA),
                pl.BlockSpec((None, d, d), lambda i: (l, 0, 0), pipeline_mode=pl.Buffered(1)),
                rows(d), vec, vec]
    args = [proj, proj, proj, proj, proj, proj, conv_w, yf, yn, w_out, x2, post_g, gate]
    out_specs = [rows(d)]
    out_shape = [jax.ShapeDtypeStruct((s, d), F32)]
    if nxt is not None:
        in_specs += [vec, vec, vec]
        args += list(nxt)
        out_specs.append(rows(d))
        out_shape.append(jax.ShapeDtypeStruct((s, d), BF16))
    res = pl.pallas_call(
        functools.partial(_out_kernel, with_next=nxt is not None),
        grid=(s // tm,),
        in_specs=in_specs,
        out_specs=out_specs,
        out_shape=out_shape,
        scratch_shapes=[pltpu.VMEM((tm, D_CONV), BF16)],
        compiler_params=_cparams(("parallel",)),
        name="out_proj",
    )(*args)
    return res if nxt is not None else (res[0], None)


def _bucket_of_distance():
    max_exact = REL_BUCKETS // 2
    d = np.arange(REL_MAX_DIST)
    nf = np.maximum(d, max_exact).astype(np.float32)
    large = max_exact + (np.log(nf / np.float32(max_exact)) / np.float32(math.log(REL_MAX_DIST / max_exact))
                         * np.float32(REL_BUCKETS - max_exact)).astype(np.int32)
    return np.where(d < max_exact, d, np.minimum(large, REL_BUCKETS - 1))


def _distance_tables():
    b = np.arange(TILE)[:, None]
    a = np.arange(TILE)[None, :]
    none = REL_MAX_DIST
    d0 = np.where(a >= b, a - b, none)
    d1 = np.where(a < b, TILE + a - b, none)
    m = np.arange(24)[:, None]
    dc = a - CMP_STRIDE * m + (2 * TILE - (CMP_LEN - 1))
    dc = np.where((dc >= 0) & (dc < REL_MAX_DIST), dc, none)
    return d0, d1, dc


def _bias_tables(rel_bias):
    bucket = np.concatenate([_bucket_of_distance(), [REL_BUCKETS - 1]])
    rel = (rel_bias - rel_bias[REL_BUCKETS - 1:REL_BUCKETS, :]) * LOG2E
    rel = rel.reshape(REL_BUCKETS, NSA_G, NSA_R)

    def expand(idx):
        bk = bucket[idx]
        tab = jnp.zeros((NSA_G, idx.shape[0], NSA_R, TILE), F32)
        for b in range(REL_BUCKETS - 1):
            hit = jnp.asarray(bk == b)[None, :, None, :]
            tab = jnp.where(hit, rel[b][:, None, :, None], tab)
        return tab.reshape(NSA_G, idx.shape[0], NSA_R * TILE)

    d0, d1, dc = _distance_tables()
    return expand(d0), expand(d1), expand(dc)


def _cmp_to_sel_t(s, n_cmp_pad):
    n_cmp = (s - CMP_LEN) // CMP_STRIDE + 1
    n_sel = s // SEL_LEN
    c_lo = np.arange(n_cmp_pad)[None, :] * CMP_STRIDE
    sel_start = np.arange(n_sel)[:, None] * SEL_LEN
    overlap = (c_lo < sel_start + SEL_LEN) & (c_lo + CMP_LEN > sel_start)
    overlap &= np.arange(n_cmp_pad)[None, :] < n_cmp
    return jnp.asarray(overlap, dtype=BF16)


def _pack_w_in(w_in):
    sizes = ([D_CONV] * 4 + [D_FOX] * 3 + [H_FOX, D_FOX] + [D_NSA] + [2 * HD] * 6
             + [3 * H_NSA, D_NSA])
    offs = np.concatenate([[0], np.cumsum(sizes)])
    names = ["u", "gb", "gc", "za", "qf", "kf", "vf", "ff", "zf", "qn", "kc", "vc", "ks", "vs",
             "kw", "vw", "gn", "zn"]
    wt = jnp.swapaxes(w_in, 1, 2)
    col = {n: wt[:, int(offs[k]):int(offs[k + 1]), :] for k, n in enumerate(names)}
    scale = HD ** -0.5 * LOG2E
    w_a = jnp.concatenate([col["u"], col["gb"], col["gc"], col["za"],
                           col["qn"] * scale, col["zn"], col["ks"], col["kw"],
                           col["qf"] * scale, col["kf"], col["zf"]], axis=1).astype(BF16)
    w_kv = jnp.concatenate([col["kc"], col["vc"]], axis=1).astype(BF16)
    w_v = jnp.concatenate([col["vf"], col["vs"], col["vw"]], axis=1).astype(BF16)
    small = jnp.concatenate([col["ff"], col["gn"]], axis=1)
    small = jnp.pad(small, ((0, 0), (0, TILE - small.shape[1]), (0, 0))).astype(BF16)
    return w_a, w_kv, w_v, small


def kernel(x, c, w_ada, b_ada, pre_norm, post_norm, w_in, b_forget, conv_w,
           cmp_pe_k, cmp_w1_k, cmp_w2_k, cmp_pe_v, cmp_w1_v, cmp_w2_v, w_out, rel_bias):
    bsz, s, d = x.shape
    assert bsz == 1 and d == D_MODEL and s % PROJ_TM == 0 and s >= 2 * WINDOW
    depth = w_in.shape[0]
    x2 = x.reshape(s, d)
    n_cmp_pad = s // CMP_STRIDE

    mod = _ada_mod(c.reshape(d, 1), w_ada, b_ada)
    shift, scale, gate = mod[:, :, :d], mod[:, :, d:2 * d], mod[:, :, 2 * d:]

    w_a, w_kv, w_v, w_small = _pack_w_in(w_in)
    w_out_b = w_out.astype(BF16)
    half = CMP_LEN * HD // 2
    w1cat = jnp.stack([jnp.concatenate([w[:, :half], w[:, half:]], axis=-1)
                       for w in (cmp_w1_k, cmp_w1_v)], axis=1).astype(BF16)
    pe = jnp.stack([cmp_pe_k, cmp_pe_v], axis=1).reshape(depth, 2, 2, 1, half)
    pe16 = jnp.broadcast_to(pe, (depth, 2, 2, 8, half)).reshape(depth, 2, 16, half).astype(BF16)
    w2 = jnp.stack([cmp_w2_k, cmp_w2_v], axis=1).astype(BF16)
    b_pad = jnp.pad(b_forget, ((0, 0), (0, TILE - H_FOX))).reshape(depth, 1, TILE)
    t0, t1, cstrip = _bias_tables(rel_bias)
    c2s_t = _cmp_to_sel_t(s, n_cmp_pad)

    h = _prenorm(x2, pre_norm[0:1], scale[0], shift[0])
    for l in range(depth):
        proj = _matmul(h, w_a, l)
        kcvc = _matmul_split(h, w_kv, l)
        proj_t = _matmul_t_tiled(h, w_v, l)

        qa, ka, small_t = _fox_prep(h, w_small, l, b_pad[l])
        yf = _fox_attention(proj, qa, ka, proj_t)

        x4 = kcvc.reshape(2 * NSA_G, n_cmp_pad, CMP_STRIDE * HD)
        cmp_n, cmp_t = _compress(x4, w1cat, pe16, w2, l)
        glog_t = small_t[H_FOX:H_FOX + 3 * H_NSA].reshape(NSA_G, NSA_R * 3, s)
        yn = _nsa_attention(proj, proj_t, cmp_n, cmp_t, c2s_t, cstrip, t0, t1, glog_t)

        nxt = None
        if l + 1 < depth:
            nxt = (pre_norm[l + 1:l + 2], scale[l + 1], shift[l + 1])
        x2, h = _out_proj(proj, conv_w[l], yf, yn, w_out_b, l, x2, post_norm[l:l + 1], gate[l], nxt)
    return x2.reshape(bsz, s, d)
```

```python
import functools
import math

import numpy as np
import jax
import jax.numpy as jnp
from jax import lax
from jax.experimental import pallas as pl
from jax.experimental.pallas import tpu as pltpu

F32 = jnp.float32
BF16 = jnp.bfloat16

D_MODEL = 2048
HD = 128
D_CONV = 512
D_FOX = 768
D_NSA = 768
H_FOX = 6
H_NSA = 6
NSA_G = 2
NSA_R = 3
CONV_WIDTH = 3
CMP_LEN = 32
CMP_STRIDE = 16
CMP_HIDDEN = 256
SEL_LEN = 64
SEL_TOPK = 16
N_FORCED = 3
CMP_PAD = 16
CMP_STRIP = 24
WINDOW = 512
REL_BUCKETS = 32
REL_MAX_DIST = 128
NORM_EPS = 1e-6
NEG_BIG = -1e30
LOG2E = math.log2(math.e)

TILE = 128
ONES_ROWS = 16
NSA_CHUNK = 512
FOX_T = 512
FOX_TK = 512
FOX_HEADS = 2
PROJ_TM = 1024
PROJ_TN = 1280
PROJ_TV = 1280
VMEM_LIMIT = 56 * 1024 * 1024

NT_DIMS = (((1,), (1,)), ((), ()))

BLK_QN, BLK_ZN, BLK_KS, BLK_KW = 16, 22, 28, 30
BLK_QF, BLK_KF, BLK_ZF = 32, 38, 44


def _cparams(sem, vmem=VMEM_LIMIT):
    return pltpu.CompilerParams(dimension_semantics=sem, vmem_limit_bytes=vmem)


def _silu(v):
    return v * jax.nn.sigmoid(v)


def _mod_kernel(c_ref, w_ref, b_ref, o_ref):
    ca = _silu(c_ref[...])
    o_ref[0] = jnp.sum(w_ref[0] * ca, axis=0, keepdims=True) + b_ref[0]


def _ada_mod(c_col, w_ada, b_ada):
    depth, d, n = w_ada.shape
    tn = 1536
    return pl.pallas_call(
        _mod_kernel,
        grid=(depth, n // tn),
        in_specs=[pl.BlockSpec((d, 1), lambda l, j: (0, 0)),
                  pl.BlockSpec((1, d, tn), lambda l, j: (l, 0, j)),
                  pl.BlockSpec((1, 1, tn), lambda l, j: (l, 0, j))],
        out_specs=pl.BlockSpec((1, 1, tn), lambda l, j: (l, 0, j)),
        out_shape=jax.ShapeDtypeStruct((depth, 1, n), F32),
        compiler_params=_cparams(("parallel", "parallel")),
        name="ada_mod",
    )(c_col, w_ada, b_ada.reshape(depth, 1, n))


def _modulated_norm(x, g, scale, shift):
    y = x * lax.rsqrt(jnp.mean(x * x, axis=-1, keepdims=True) + NORM_EPS) * g
    return y * (1.0 + scale) + shift


def _prenorm_kernel(x_ref, g_ref, sc_ref, sh_ref, h_ref):
    h_ref[...] = _modulated_norm(x_ref[...], g_ref[...], sc_ref[...], sh_ref[...]).astype(BF16)


def _prenorm(x2, g, scale, shift):
    s, d = x2.shape
    tm = 512
    vec = pl.BlockSpec((1, d), lambda i: (0, 0))
    return pl.pallas_call(
        _prenorm_kernel,
        grid=(s // tm,),
        in_specs=[pl.BlockSpec((tm, d), lambda i: (i, 0)), vec, vec, vec],
        out_specs=pl.BlockSpec((tm, d), lambda i: (i, 0)),
        out_shape=jax.ShapeDtypeStruct((s, d), BF16),
        compiler_params=_cparams(("parallel",)),
        name="prenorm",
    )(x2, g, scale, shift)


def _mm_kernel(h_ref, wt_ref, o_ref):
    o_ref[...] = lax.dot_general(h_ref[...], wt_ref[...], NT_DIMS,
                                 preferred_element_type=F32).astype(o_ref.dtype)


def _matmul(h, wt, l):
    s, d = h.shape
    n = wt.shape[1]
    tm, tn = PROJ_TM, PROJ_TN
    return pl.pallas_call(
        _mm_kernel,
        grid=(s // tm, n // tn),
        in_specs=[pl.BlockSpec((tm, d), lambda i, j: (i, 0)),
                  pl.BlockSpec((None, tn, d), lambda i, j: (l, j, 0))],
        out_specs=pl.BlockSpec((tm, tn), lambda i, j: (i, j)),
        out_shape=jax.ShapeDtypeStruct((s, n), BF16),
        compiler_params=_cparams(("parallel", "arbitrary")),
        name="proj_rows",
    )(h, wt)


def _mm_split_kernel(h_ref, wt_ref, o_ref):
    res = lax.dot_general(h_ref[...], wt_ref[...], NT_DIMS, preferred_element_type=F32)
    for c in range(o_ref.shape[0]):
        o_ref[c] = res[:, c * TILE:(c + 1) * TILE].astype(o_ref.dtype)


def _matmul_split(h, wt, l):
    s, d = h.shape
    n = wt.shape[1]
    tm = PROJ_TM
    return pl.pallas_call(
        _mm_split_kernel,
        grid=(s // tm,),
        in_specs=[pl.BlockSpec((tm, d), lambda i: (i, 0)),
                  pl.BlockSpec((None, n, d), lambda i: (l, 0, 0))],
        out_specs=pl.BlockSpec((n // TILE, tm, TILE), lambda i: (0, i, 0)),
        out_shape=jax.ShapeDtypeStruct((n // TILE, s, TILE), BF16),
        compiler_params=_cparams(("parallel",)),
        name="proj_rows_split",
    )(h, wt)


def _mm_t_tiled_kernel(h_ref, wt_ref, o_ref):
    res = lax.dot_general(wt_ref[...], h_ref[...], NT_DIMS, preferred_element_type=F32)
    for c in range(o_ref.shape[0]):
        o_ref[c] = res[:, c * TILE:(c + 1) * TILE].astype(o_ref.dtype)


def _matmul_t_tiled(h, wt, l):
    s, d = h.shape
    n = wt.shape[1]
    tm, tn = PROJ_TM, PROJ_TV
    return pl.pallas_call(
        _mm_t_tiled_kernel,
        grid=(s // tm, n // tn),
        in_specs=[pl.BlockSpec((tm, d), lambda i, j: (i, 0)),
                  pl.BlockSpec((None, tn, d), lambda i, j: (l, j, 0))],
        out_specs=pl.BlockSpec((tm // TILE, tn, TILE), lambda i, j: (i, j, 0)),
        out_shape=jax.ShapeDtypeStruct((s // TILE, n, TILE), BF16),
        compiler_params=_cparams(("parallel", "arbitrary")),
        name="proj_cols_tiled",
    )(h, wt)


def _fox_prep_kernel(h_ref, w_ref, b_ref, qa_ref, ka_ref, st_ref, carry_ref, *, tm):
    @pl.when(pl.program_id(0) == 0)
    def _():
        carry_ref[...] = jnp.zeros_like(carry_ref)

    small = lax.dot_general(h_ref[...], w_ref[...], NT_DIMS,
                            preferred_element_type=F32)
    st_ref[...] = small.T
    v = small + b_ref[...]
    c = jnp.minimum(v, 0.0) - jnp.log1p(jnp.exp(-jnp.abs(v)))
    row = lax.broadcasted_iota(jnp.int32, (tm, TILE), 0)
    sh = 1
    while sh < tm:
        c = c + jnp.where(row >= sh, pltpu.roll(c, sh, axis=0), 0.0)
        sh *= 2
    c = c + carry_ref[...]
    carry_ref[...] = c[tm - 1:tm, :]
    c = c * LOG2E

    lane = lax.broadcasted_iota(jnp.int32, (tm, TILE), 1)
    for h in range(H_FOX):
        ch = jnp.broadcast_to(c[:, h:h + 1], (tm, TILE))
        hi = ch.astype(BF16).astype(F32)
        r1 = ch - hi
        mid = r1.astype(BF16).astype(F32)
        lo = r1 - mid
        pieces = jnp.where((lane == 0) | (lane == 3), hi,
                           jnp.where((lane == 1) | (lane == 4), mid, lo))
        qa = jnp.where(lane < 3, pieces, jnp.where(lane < 6, 1.0, 0.0))
        ka = jnp.where(lane < 3, 1.0, jnp.where(lane < 6, -pieces, 0.0))
        qa_ref[:, h * TILE:(h + 1) * TILE] = qa.astype(BF16)
        ka_ref[:, h * TILE:(h + 1) * TILE] = ka.astype(BF16)


def _fox_prep(h, w_small, l, b_pad):
    s, d = h.shape
    tm = 512
    out = jax.ShapeDtypeStruct((s, H_FOX * TILE), BF16)
    return pl.pallas_call(
        functools.partial(_fox_prep_kernel, tm=tm),
        grid=(s // tm,),
        in_specs=[pl.BlockSpec((tm, d), lambda i: (i, 0)),
                  pl.BlockSpec((None, TILE, d), lambda i: (l, 0, 0)),
                  pl.BlockSpec((1, TILE), lambda i: (0, 0))],
        out_specs=[pl.BlockSpec((tm, H_FOX * TILE), lambda i: (i, 0)),
                   pl.BlockSpec((tm, H_FOX * TILE), lambda i: (i, 0)),
                   pl.BlockSpec((TILE, tm), lambda i: (0, i))],
        out_shape=[out, out, jax.ShapeDtypeStruct((TILE, s), F32)],
        scratch_shapes=[pltpu.VMEM((1, TILE), F32)],
        compiler_params=_cparams(("arbitrary",)),
        name="fox_prep",
    )(h, w_small, b_pad)


def _fox_kernel(q_ref, qa_ref, k_ref, ka_ref, vt_ref, z_ref, o_ref, s0_ref, s1_ref, s2_ref,
                acc_ref):
    i = pl.program_id(1)
    t = FOX_T
    tk = FOX_TK
    heads = range(FOX_HEADS)
    col = lambda h: slice(h * TILE, (h + 1) * TILE)
    qaug = [jnp.concatenate([q_ref[:, col(h)], qa_ref[:, col(h)]], axis=1) for h in heads]

    def scores(h, j, rows):
        r0 = pl.multiple_of(j * rows, rows)
        kaug = jnp.concatenate([k_ref[pl.ds(r0, rows), col(h)], ka_ref[pl.ds(r0, rows), col(h)]],
                               axis=1)
        return lax.dot_general(kaug, qaug[h], NT_DIMS, preferred_element_type=F32)

    def values_t(h, j, rows):
        n_sub = rows // TILE
        return jnp.concatenate(
            [jnp.concatenate([vt_ref[n_sub * j + c, col(h), :] for c in range(n_sub)], axis=1),
             jnp.ones((ONES_ROWS, rows), BF16)], axis=0)

    kio = lax.broadcasted_iota(jnp.int32, (t, t), 0)
    qio = lax.broadcasted_iota(jnp.int32, (t, t), 1)
    ms = []
    for h in heads:
        s = jnp.where(kio <= qio, scores(h, i, t), NEG_BIG)
        m = jnp.max(s, axis=0, keepdims=True)
        p = jnp.exp2(s - m)
        ms.append(m)
        acc_ref[h] = jnp.dot(values_t(h, i, t), p.astype(BF16), preferred_element_type=F32)

    def update(h, s, vt, m):
        m_new = jnp.maximum(m, jnp.max(s, axis=0, keepdims=True))
        alpha = jnp.exp2(m - m_new)
        p = jnp.exp2(s - m_new)
        acc_ref[h] = alpha * acc_ref[h] + jnp.dot(vt, p.astype(BF16), preferred_element_type=F32)
        return m_new

    bufs = (s0_ref, s1_ref, s2_ref)
    for h in heads:
        s0_ref[h] = scores(h, 0, tk)

    def triple(pp, ms):
        ms = list(ms)
        for stage in range(3):
            cur, nxt = bufs[stage], bufs[(stage + 1) % 3]
            for h in heads:
                nxt[h] = scores(h, 3 * pp + stage + 1, tk)
            for h in heads:
                ms[h] = update(h, cur[h], values_t(h, 3 * pp + stage, tk), ms[h])
        return tuple(ms)

    n_far = i * (t // tk)
    ms = lax.fori_loop(0, n_far // 3, triple, tuple(ms))
    done = (n_far // 3) * 3

    @pl.when(n_far - done == 1)
    def _():
        for h in heads:
            update(h, s0_ref[h], values_t(h, done, tk), ms[h])

    @pl.when(n_far - done == 2)
    def _():
        for h in heads:
            s1_ref[h] = scores(h, done + 1, tk)
        for h in heads:
            m_mid = update(h, s0_ref[h], values_t(h, done, tk), ms[h])
            update(h, s1_ref[h], values_t(h, done + 1, tk), m_mid)

    for h in heads:
        o = (acc_ref[h, 0:HD, :] * (1.0 / acc_ref[h, HD:HD + 1, :])).T
        o_ref[:, col(h)] = (o * _silu(z_ref[:, col(h)].astype(F32))).astype(BF16)


def _fox_attention(proj, qa, ka, proj_t):
    s = proj.shape[0]
    t = FOX_T
    nk = s // TILE
    hw = FOX_HEADS * TILE
    score_buf = pltpu.VMEM((FOX_HEADS, FOX_TK, t), F32)
    return pl.pallas_call(
        _fox_kernel,
        grid=(H_FOX // FOX_HEADS, s // t),
        scratch_shapes=[score_buf, score_buf, score_buf,
                        pltpu.VMEM((FOX_HEADS, HD + ONES_ROWS, t), F32)],
        in_specs=[pl.BlockSpec((t, hw), lambda h, i: (i, BLK_QF // FOX_HEADS + h)),
                  pl.BlockSpec((t, hw), lambda h, i: (i, h)),
                  pl.BlockSpec((s, hw), lambda h, i: (0, BLK_KF // FOX_HEADS + h)),
                  pl.BlockSpec((s, hw), lambda h, i: (0, h)),
                  pl.BlockSpec((nk, hw, TILE), lambda h, i: (0, h, 0)),
                  pl.BlockSpec((t, hw), lambda h, i: (i, BLK_ZF // FOX_HEADS + h))],
        out_specs=pl.BlockSpec((t, hw), lambda h, i: (i, h)),
        out_shape=jax.ShapeDtypeStruct((s, D_FOX), BF16),
        compiler_params=_cparams(("parallel", "parallel")),
        name="fox_attention",
    )(proj, qa, proj, ka, proj_t, proj)


def _compress_kernel(x_ref, w1_ref, pe_ref, w2_ref, o_ref, ot_ref):
    n = x_ref.shape[0]
    ab = jnp.dot(x_ref[...], w1_ref[0], preferred_element_type=F32)
    pe = jnp.dot(pe_ref[0], w1_ref[0], preferred_element_type=F32)
    a = ab[:, :CMP_HIDDEN]
    b_next = pltpu.roll(ab[:, CMP_HIDDEN:], n - 1, axis=0)
    hid = a + b_next + pe[0:1, :CMP_HIDDEN] + pe[8:9, CMP_HIDDEN:]
    out = jnp.dot(_silu(hid).astype(BF16), w2_ref[0], preferred_element_type=F32)
    o_ref[...] = out.astype(BF16)
    ot_ref[...] = out.T.astype(BF16)


def _compress(x4, w1cat, pe16, w2, l):
    _, n, k = x4.shape
    kind = lambda c: (l, c // NSA_G, 0, 0)
    return pl.pallas_call(
        _compress_kernel,
        grid=(2 * NSA_G,),
        in_specs=[pl.BlockSpec((None, n, k), lambda c: (c, 0, 0)),
                  pl.BlockSpec((None, 1, k, 2 * CMP_HIDDEN), kind),
                  pl.BlockSpec((None, 1, 16, k), kind),
                  pl.BlockSpec((None, 1, CMP_HIDDEN, HD), kind)],
        out_specs=[pl.BlockSpec((None, n, HD), lambda c: (c, 0, 0)),
                   pl.BlockSpec((None, HD, n), lambda c: (c, 0, 0))],
        out_shape=[jax.ShapeDtypeStruct((2 * NSA_G, n, HD), BF16),
                   jax.ShapeDtypeStruct((2 * NSA_G, HD, n), BF16)],
        compiler_params=_cparams(("parallel",)),
        name="nsa_compress",
    )(x4, w1cat, pe16, w2)


def _nsa_kernel(*refs, n_cmp):
    q_refs, z_refs = refs[0:H_NSA], refs[H_NSA:2 * H_NSA]
    (ks_ref, kw_ref, vst_ref, vwt_ref, kc_ref, vct_ref, c2s_ref, cstrip_ref, t0_ref, t1_ref,
     gl_ref, o_ref, sc_ref, sel_ref, m_s, acc_s, acc_w, ss_ref, sw_ref, sa_ref, sb_ref,
     sc3_ref) = refs[2 * H_NSA:]
    i = pl.program_id(0)
    w3 = NSA_R * TILE
    gw = NSA_G * TILE
    groups = range(NSA_G)
    col = lambda g: slice(g * TILE, (g + 1) * TILE)
    qs = [jnp.concatenate([q_refs[NSA_R * g + r][...] for r in range(NSA_R)], axis=0)
          for g in groups]
    b_io = lax.broadcasted_iota(jnp.int32, (TILE, w3), 0)
    a_io = lax.broadcasted_iota(jnp.int32, (TILE, w3), 1) & (TILE - 1)
    tpc = NSA_CHUNK // TILE
    bpc = NSA_CHUNK // SEL_LEN

    def key_scores(k_ref, g, tile0, n_tiles):
        r0 = pl.multiple_of(tile0 * TILE, TILE)
        return lax.dot_general(k_ref[pl.ds(r0, n_tiles * TILE), col(g)], qs[g], NT_DIMS,
                               preferred_element_type=F32)

    def values_t(vt_ref, g, tile0, n_tiles):
        return jnp.concatenate(
            [jnp.concatenate([vt_ref[tile0 + c, col(g), :] for c in range(n_tiles)], axis=1),
             jnp.ones((ONES_ROWS, n_tiles * TILE), BF16)], axis=0)

    def fix_near(s_ref, g, d):
        own = pl.ds(pl.multiple_of(d * TILE, TILE), TILE)
        prev = pl.ds(pl.multiple_of(jnp.maximum(d - 1, 0) * TILE, TILE), TILE)
        s_ref[g, prev, :] = s_ref[g, prev, :] + jnp.where(d >= 1, t1_ref[g], 0.0)
        s_ref[g, own, :] = jnp.where(b_io <= a_io, s_ref[g, own, :] + t0_ref[g], NEG_BIG)

    w0 = pl.multiple_of((TILE // CMP_STRIDE) * i, TILE // CMP_STRIDE)
    n_io = lax.broadcasted_iota(jnp.int32, (n_cmp, w3), 0)
    qa_io = lax.broadcasted_iota(jnp.int32, (n_cmp, w3), 1) & (TILE - 1)
    valid = (CMP_STRIDE * n_io + (CMP_LEN - 1)) <= (TILE * i + qa_io)
    n_wt = WINDOW // TILE + 1
    w_tile0 = jnp.maximum(i - WINDOW // TILE, 0)
    d_win = i - w_tile0
    for g in groups:
        sc_ref[g, 0:CMP_PAD, :] = jnp.zeros((CMP_PAD, w3), F32)
        sc_ref[g, CMP_PAD:CMP_PAD + n_cmp, :] = lax.dot_general(kc_ref[g], qs[g], NT_DIMS,
                                                                preferred_element_type=F32)
        sw_ref[g] = key_scores(kw_ref, g, w_tile0, n_wt)
    oc_t, psums = [], []
    for g in groups:
        strip = pl.ds(w0, CMP_STRIP)
        sc_ref[g, strip, :] = sc_ref[g, strip, :] + cstrip_ref[g]
        scm = jnp.where(valid, sc_ref[g, CMP_PAD:CMP_PAD + n_cmp, :], NEG_BIG)
        mc = jnp.max(scm, axis=0, keepdims=True)
        pc = jnp.exp2(scm - mc)
        lc = jnp.sum(pc, axis=0, keepdims=True)
        pcn = pc * jnp.where(mc > 0.5 * NEG_BIG, 1.0 / lc, 0.0)
        oc_t.append(jnp.dot(vct_ref[g], pcn.astype(BF16), preferred_element_type=F32))
        psums.append(pcn[:, 0:TILE] + pcn[:, TILE:2 * TILE] + pcn[:, 2 * TILE:3 * TILE])

        for c in range(1, n_wt):
            rows = slice(c * TILE, (c + 1) * TILE)
            sw_ref[g, rows, :] = jnp.where(d_win < c, NEG_BIG, sw_ref[g, rows, :])
        fix_near(sw_ref, g, d_win)
        sw_ref[g, 0:TILE, :] = jnp.where((a_io < b_io) | (i < WINDOW // TILE),
                                         sw_ref[g, 0:TILE, :], NEG_BIG)
        sw = sw_ref[g]
        p_w = jnp.exp2(sw - jnp.max(sw, axis=0, keepdims=True))
        acc_w[g] = jnp.dot(values_t(vwt_ref, g, w_tile0, n_wt), p_w.astype(BF16),
                           preferred_element_type=F32)

    psum = jnp.concatenate(psums, axis=1)
    p_hi = psum.astype(BF16)
    p_lo = (psum - p_hi.astype(F32)).astype(BF16)
    c2s = c2s_ref[...]
    imp = (jnp.dot(c2s, p_hi, preferred_element_type=F32)
           + jnp.dot(c2s, p_lo, preferred_element_type=F32))
    n_sel = imp.shape[0]
    m_io = lax.broadcasted_iota(jnp.int32, (n_sel, gw), 0)
    m_f = m_io.astype(F32)
    qpos = TILE * i + (lax.broadcasted_iota(jnp.int32, (n_sel, gw), 1) & (TILE - 1))
    own = qpos >> (SEL_LEN.bit_length() - 1)
    forced = (m_io == 0) | (m_io == own) | (m_io == own - 1)
    eligible = SEL_LEN * m_io <= qpos
    sel = jnp.where(eligible & forced, 1.0, 0.0)
    cur = jnp.where(eligible, jnp.where(forced, -2.0, imp), -1.0)
    for _ in range(min(SEL_TOPK, n_sel) - N_FORCED):
        mx = jnp.max(cur, axis=0, keepdims=True)
        idx = jnp.min(jnp.where(cur == mx, m_f, float(n_sel)), axis=0, keepdims=True)
        pick = m_f == idx
        sel = jnp.where(pick, jnp.maximum(jnp.where(mx >= 0.0, 1.0, 0.0), sel), sel)
        cur = jnp.where(pick, -2.0, cur)
    sel_ref[...] = sel

    def sel_mask(g, c, n_tiles):
        rows = sel_ref[pl.ds(pl.multiple_of(c * bpc, 8), bpc), col(g)]
        mk = jnp.concatenate([jnp.broadcast_to(rows[b:b + 1, :], (SEL_LEN, TILE))
                              for b in range(n_tiles * TILE // SEL_LEN)], axis=0)
        return jnp.concatenate([mk, mk, mk], axis=1) > 0.5

    def sel_update(g, c, s, n_tiles=tpc):
        s = jnp.where(sel_mask(g, c, n_tiles), s, NEG_BIG)
        m_old = m_s[g]
        m_new = jnp.maximum(m_old, jnp.max(s, axis=0, keepdims=True))
        alpha = jnp.exp2(m_old - m_new)
        p = jnp.exp2(s - m_new)
        acc_s[g] = alpha * acc_s[g] + jnp.dot(values_t(vst_ref, g, c * tpc, n_tiles),
                                              p.astype(BF16), preferred_element_type=F32)
        m_s[g] = m_new

    m_s[...] = jnp.full(m_s.shape, NEG_BIG, F32)
    acc_s[...] = jnp.zeros(acc_s.shape, F32)

    c_own = i // tpc
    c_prev = jnp.maximum(i - 1, 0) // tpc

    bufs = (sa_ref, sb_ref, sc3_ref)
    for g in groups:
        sa_ref[g] = key_scores(ks_ref, g, 0, tpc)

    def far_triple(pp, carry):
        for stage in range(3):
            cur_buf, nxt_buf = bufs[stage], bufs[(stage + 1) % 3]
            for g in groups:
                nxt_buf[g] = key_scores(ks_ref, g, (3 * pp + stage + 1) * tpc, tpc)
            for g in groups:
                sel_update(g, 3 * pp + stage, cur_buf[g])
        return carry

    lax.fori_loop(0, c_prev // 3, far_triple, 0)
    done = (c_prev // 3) * 3

    @pl.when(c_prev - done == 1)
    def _():
        for g in groups:
            sel_update(g, done, sa_ref[g])

    @pl.when(c_prev - done == 2)
    def _():
        for g in groups:
            sb_ref[g] = key_scores(ks_ref, g, (done + 1) * tpc, tpc)
        for g in groups:
            sel_update(g, done, sa_ref[g])
            sel_update(g, done + 1, sb_ref[g])

    @pl.when(c_prev != c_own)
    def _():
        last = slice((tpc - 1) * TILE, tpc * TILE)
        for g in groups:
            ss_ref[g] = key_scores(ks_ref, g, c_prev * tpc, tpc)
            ss_ref[g, last, :] = ss_ref[g, last, :] + t1_ref[g]
            sel_update(g, c_prev, ss_ref[g])

    for n_t in range(1, tpc + 1):
        @pl.when(i - c_own * tpc == n_t - 1)
        def _(n_t=n_t):
            own = slice((n_t - 1) * TILE, n_t * TILE)
            prev = slice((n_t - 2) * TILE, (n_t - 1) * TILE)
            for g in groups:
                ss_ref[g, 0:n_t * TILE, :] = key_scores(ks_ref, g, c_own * tpc, n_t)
            for g in groups:
                if n_t >= 2:
                    ss_ref[g, prev, :] = ss_ref[g, prev, :] + t1_ref[g]
                ss_ref[g, own, :] = jnp.where(b_io <= a_io, ss_ref[g, own, :] + t0_ref[g], NEG_BIG)
                sel_update(g, c_own, ss_ref[g, 0:n_t * TILE, :], n_t)

    for g in groups:
        gates = jax.nn.sigmoid(gl_ref[g])

        def gate_row(branch):
            return jnp.concatenate([gates[r * 3 + branch:r * 3 + branch + 1, :]
                                    for r in range(NSA_R)], axis=1)

        out_t = (oc_t[g] * gate_row(0)
                 + acc_s[g, 0:HD, :] * (gate_row(1) / acc_s[g, HD:HD + 1, :])
                 + acc_w[g, 0:HD, :] * (gate_row(2) / acc_w[g, HD:HD + 1, :]))
        for r in range(NSA_R):
            h = NSA_R * g + r
            o_r = out_t[:, r * TILE:(r + 1) * TILE].T
            o_ref[:, h * TILE:(h + 1) * TILE] = (
                o_r * _silu(z_refs[h][...].astype(F32))).astype(BF16)


def _nsa_attention(proj, proj_t, cmp_n, cmp_t, c2s_t, cstrip, t0, t1, glog_t):
    s = proj.shape[0]
    nk = s // TILE
    n_cmp = cmp_n.shape[1]
    n_sel = s // SEL_LEN
    w3 = NSA_R * TILE
    gw = NSA_G * TILE
    qspec = lambda h: pl.BlockSpec((TILE, TILE), lambda i, h=h: (i, BLK_QN + h))
    zspec = lambda h: pl.BlockSpec((TILE, TILE), lambda i, h=h: (i, BLK_ZN + h))
    whole = lambda shape: pl.BlockSpec(shape, lambda i: (0,) * len(shape))
    resident = dict(pipeline_mode=pl.Buffered(1))
    chunk_buf = pltpu.VMEM((NSA_G, NSA_CHUNK, w3), F32)
    acc_buf = pltpu.VMEM((NSA_G, HD + ONES_ROWS, w3), F32)
    return pl.pallas_call(
        functools.partial(_nsa_kernel, n_cmp=n_cmp),
        grid=(s // TILE,),
        in_specs=[qspec(h) for h in range(H_NSA)] + [zspec(h) for h in range(H_NSA)] + [
            pl.BlockSpec((s, gw), lambda i: (0, BLK_KS // NSA_G), **resident),
            pl.BlockSpec((s, gw), lambda i: (0, BLK_KW // NSA_G), **resident),
            pl.BlockSpec((nk, gw, TILE), lambda i: (0, H_FOX // NSA_G, 0), **resident),
            pl.BlockSpec((nk, gw, TILE), lambda i: (0, H_FOX // NSA_G + 1, 0), **resident),
            pl.BlockSpec((NSA_G, n_cmp, HD), lambda i: (0, 0, 0)),
            pl.BlockSpec((NSA_G, HD, n_cmp), lambda i: (1, 0, 0)),
            whole((n_sel, n_cmp)), whole((NSA_G, CMP_STRIP, w3)), whole((NSA_G, TILE, w3)),
            whole((NSA_G, TILE, w3)),
            pl.BlockSpec((NSA_G, NSA_R * 3, TILE), lambda i: (0, 0, i))],
        out_specs=pl.BlockSpec((TILE, D_NSA), lambda i: (i, 0)),
        out_shape=jax.ShapeDtypeStruct((s, D_NSA), BF16),
        scratch_shapes=[pltpu.VMEM((NSA_G, CMP_PAD + n_cmp + 8, w3), F32),
                        pltpu.VMEM((n_sel, gw), F32),
                        pltpu.VMEM((NSA_G, 1, w3), F32),
                        acc_buf, acc_buf,
                        chunk_buf,
                        pltpu.VMEM((NSA_G, WINDOW + TILE, w3), F32),
                        chunk_buf, chunk_buf, chunk_buf],
        compiler_params=_cparams(("parallel",)),
        name="nsa_attention",
    )(*([proj] * (2 * H_NSA)), proj, proj, proj_t, proj_t,
      cmp_n, cmp_t, c2s_t, cstrip, t0, t1, glog_t)


def _gated_conv(u_ref, gb_ref, gc_ref, z_ref, uh_ref, gch_ref, w_ref):
    tm = u_ref.shape[0]
    y = gc_ref[...].astype(F32) * u_ref[...].astype(F32)
    yh = gch_ref[...].astype(F32) * uh_ref[...].astype(F32)
    yh = jnp.where(pl.program_id(0) > 0, yh, 0.0)
    h1 = yh[15:16, :]
    h2 = yh[14:15, :]
    row = lax.broadcasted_iota(jnp.int32, (tm, D_CONV), 0)
    y1 = jnp.where(row == 0, h1, pltpu.roll(y, 1, axis=0))
    y2 = jnp.where(row == 0, h2, jnp.where(row == 1, h1, pltpu.roll(y, 2, axis=0)))
    w = w_ref[...]
    conv = w[0:1, :] * y2 + w[1:2, :] * y1 + w[2:3, :] * y
    return gb_ref[...].astype(F32) * conv * _silu(z_ref[...].astype(F32))


def _out_kernel(u_ref, gb_ref, gc_ref, z_ref, uh_ref, gch_ref, cw_ref, yf_ref, yn_ref, w_ref,
                x_ref, pg_ref, gate_ref, *rest, with_next):
    if with_next:
        ng_ref, nsc_ref, nsh_ref, xo_ref, h_ref, ya_ref = rest
    else:
        xo_ref, ya_ref = rest
    ya_ref[...] = _gated_conv(u_ref, gb_ref, gc_ref, z_ref, uh_ref, gch_ref, cw_ref).astype(BF16)
    tm = x_ref.shape[0]
    n_parts = 4
    for part in range(n_parts):
        rows = slice(part * tm // n_parts, (part + 1) * tm // n_parts)
        y = (jnp.dot(ya_ref[rows, :], w_ref[0:D_CONV, :], preferred_element_type=F32)
             + jnp.dot(yf_ref[rows, :], w_ref[D_CONV:D_CONV + D_FOX, :], preferred_element_type=F32)
             + jnp.dot(yn_ref[rows, :], w_ref[D_CONV + D_FOX:, :], preferred_element_type=F32))
        yn = y * lax.rsqrt(jnp.mean(y * y, axis=-1, keepdims=True) + NORM_EPS) * pg_ref[...]
        xn = x_ref[rows, :] + gate_ref[...] * yn
        if with_next:
            h_ref[rows, :] = _modulated_norm(xn, ng_ref[...], nsc_ref[...],
                                             nsh_ref[...]).astype(BF16)
        xo_ref[rows, :] = xn


def _out_proj(proj, conv_w, yf, yn, w_out, l, x2, post_g, gate, nxt):
    s, d = x2.shape
    tm = 512
    rows = lambda n: pl.BlockSpec((tm, n), lambda i: (i, 0))
    vec = pl.BlockSpec((1, d), lambda i: (0, 0))
    main = lambda c: pl.BlockSpec((tm, D_CONV), lambda i, c=c: (i, c))
    halo = lambda c: pl.BlockSpec((16, D_CONV), lambda i, c=c: (jnp.maximum(i * (tm // 16) - 1, 0), c))
    in_specs = [main(0), main(1), main(2), main(3), halo(0), halo(2),
                pl.BlockSpec((CONV_WIDTH, D_CONV), lambda i: (0, 0)),
                rows(D_FOX), rows(D_NSA),
                pl.BlockSpec((None, d, d), lambda i: (l, 0, 0), pipeline_mode=pl.Buffered(1)),
                rows(d), vec, vec]
    args = [proj, proj, proj, proj, proj, proj, conv_w, yf, yn, w_out, x2, post_g, gate]
    out_specs = [rows(d)]
    out_shape = [jax.ShapeDtypeStruct((s, d), F32)]
    if nxt is not None:
        in_specs += [vec, vec, vec]
        args += list(nxt)
        out_specs.append(rows(d))
        out_shape.append(jax.ShapeDtypeStruct((s, d), BF16))
    res = pl.pallas_call(
        functools.partial(_out_kernel, with_next=nxt is not None),
        grid=(s // tm,),
        in_specs=in_specs,
        out_specs=out_specs,
        out_shape=out_shape,
        scratch_shapes=[pltpu.VMEM((tm, D_CONV), BF16)],
        compiler_params=_cparams(("parallel",)),
        name="out_proj",
    )(*args)
    return res if nxt is not None else (res[0], None)


def _bucket_of_distance():
    max_exact = REL_BUCKETS // 2
    d = np.arange(REL_MAX_DIST)
    nf = np.maximum(d, max_exact).astype(np.float32)
    large = max_exact + (np.log(nf / np.float32(max_exact)) / np.float32(math.log(REL_MAX_DIST / max_exact))
                         * np.float32(REL_BUCKETS - max_exact)).astype(np.int32)
    return np.where(d < max_exact, d, np.minimum(large, REL_BUCKETS - 1))


def _distance_tables():
    b = np.arange(TILE)[:, None]
    a = np.arange(TILE)[None, :]
    none = REL_MAX_DIST
    d0 = np.where(a >= b, a - b, none)
    d1 = np.where(a < b, TILE + a - b, none)
    m = np.arange(CMP_STRIP)[:, None]
    dc = a - CMP_STRIDE * m + (CMP_STRIDE * CMP_PAD - (CMP_LEN - 1))
    dc = np.where((dc >= 0) & (dc < REL_MAX_DIST), dc, none)
    return d0, d1, dc


def _bias_tables(rel_bias):
    bucket = np.concatenate([_bucket_of_distance(), [REL_BUCKETS - 1]])
    rel = (rel_bias - rel_bias[REL_BUCKETS - 1:REL_BUCKETS, :]) * LOG2E
    rel = rel.reshape(REL_BUCKETS, NSA_G, NSA_R)

    def expand(idx):
        bk = bucket[idx]
        tab = jnp.zeros((NSA_G, idx.shape[0], NSA_R, TILE), F32)
        for b in range(REL_BUCKETS - 1):
            hit = jnp.asarray(bk == b)[None, :, None, :]
            tab = jnp.where(hit, rel[b][:, None, :, None], tab)
        return tab.reshape(NSA_G, idx.shape[0], NSA_R * TILE)

    d0, d1, dc = _distance_tables()
    return expand(d0), expand(d1), expand(dc)


def _cmp_to_sel_t(s, n_cmp_pad):
    n_cmp = (s - CMP_LEN) // CMP_STRIDE + 1
    n_sel = s // SEL_LEN
    c_lo = np.arange(n_cmp_pad)[None, :] * CMP_STRIDE
    sel_start = np.arange(n_sel)[:, None] * SEL_LEN
    overlap = (c_lo < sel_start + SEL_LEN) & (c_lo + CMP_LEN > sel_start)
    overlap &= np.arange(n_cmp_pad)[None, :] < n_cmp
    return jnp.asarray(overlap, dtype=BF16)


def _pack_w_in(w_in):
    sizes = ([D_CONV] * 4 + [D_FOX] * 3 + [H_FOX, D_FOX] + [D_NSA] + [2 * HD] * 6
             + [3 * H_NSA, D_NSA])
    offs = np.concatenate([[0], np.cumsum(sizes)])
    names = ["u", "gb", "gc", "za", "qf", "kf", "vf", "ff", "zf", "qn", "kc", "vc", "ks", "vs",
             "kw", "vw", "gn", "zn"]
    wt = jnp.swapaxes(w_in, 1, 2)
    col = {n: wt[:, int(offs[k]):int(offs[k + 1]), :] for k, n in enumerate(names)}
    scale = HD ** -0.5 * LOG2E
    w_a = jnp.concatenate([col["u"], col["gb"], col["gc"], col["za"],
                           col["qn"] * scale, col["zn"], col["ks"], col["kw"],
                           col["qf"] * scale, col["kf"], col["zf"]], axis=1).astype(BF16)
    w_kv = jnp.concatenate([col["kc"], col["vc"]], axis=1).astype(BF16)
    w_v = jnp.concatenate([col["vf"], col["vs"], col["vw"]], axis=1).astype(BF16)
    small = jnp.concatenate([col["ff"], col["gn"]], axis=1)
    small = jnp.pad(small, ((0, 0), (0, TILE - small.shape[1]), (0, 0))).astype(BF16)
    return w_a, w_kv, w_v, small


def kernel(x, c, w_ada, b_ada, pre_norm, post_norm, w_in, b_forget, conv_w,
           cmp_pe_k, cmp_w1_k, cmp_w2_k, cmp_pe_v, cmp_w1_v, cmp_w2_v, w_out, rel_bias):
    bsz, s, d = x.shape
    assert bsz == 1 and d == D_MODEL and s % PROJ_TM == 0 and s >= 2 * WINDOW
    depth = w_in.shape[0]
    x2 = x.reshape(s, d)
    n_cmp_pad = s // CMP_STRIDE

    mod = _ada_mod(c.reshape(d, 1), w_ada, b_ada)
    shift, scale, gate = mod[:, :, :d], mod[:, :, d:2 * d], mod[:, :, 2 * d:]

    w_a, w_kv, w_v, w_small = _pack_w_in(w_in)
    w_out_b = w_out.astype(BF16)
    half = CMP_LEN * HD // 2
    w1cat = jnp.stack([jnp.concatenate([w[:, :half], w[:, half:]], axis=-1)
                       for w in (cmp_w1_k, cmp_w1_v)], axis=1).astype(BF16)
    pe = jnp.stack([cmp_pe_k, cmp_pe_v], axis=1).reshape(depth, 2, 2, 1, half)
    pe16 = jnp.broadcast_to(pe, (depth, 2, 2, 8, half)).reshape(depth, 2, 16, half).astype(BF16)
    w2 = jnp.stack([cmp_w2_k, cmp_w2_v], axis=1).astype(BF16)
    b_pad = jnp.pad(b_forget, ((0, 0), (0, TILE - H_FOX))).reshape(depth, 1, TILE)
    t0, t1, cstrip = _bias_tables(rel_bias)
    c2s_t = _cmp_to_sel_t(s, n_cmp_pad)

    h = _prenorm(x2, pre_norm[0:1], scale[0], shift[0])
    for l in range(depth):
        proj = _matmul(h, w_a, l)
        kcvc = _matmul_split(h, w_kv, l)
        proj_t = _matmul_t_tiled(h, w_v, l)

        qa, ka, small_t = _fox_prep(h, w_small, l, b_pad[l])
        yf = _fox_attention(proj, qa, ka, proj_t)

        x4 = kcvc.reshape(2 * NSA_G, n_cmp_pad, CMP_STRIDE * HD)
        cmp_n, cmp_t = _compress(x4, w1cat, pe16, w2, l)
        glog_t = small_t[H_FOX:H_FOX + 3 * H_NSA].reshape(NSA_G, NSA_R * 3, s)
        yn = _nsa_attention(proj, proj_t, cmp_n, cmp_t, c2s_t, cstrip, t0, t1, glog_t)

        nxt = None
        if l + 1 < depth:
            nxt = (pre_norm[l + 1:l + 2], scale[l + 1], shift[l + 1])
        x2, h = _out_proj(proj, conv_w[l], yf, yn, w_out_b, l, x2, post_norm[l:l + 1], gate[l], nxt)
    return x2.reshape(bsz, s, d)
```

```python
import functools
import math

import numpy as np
import jax
import jax.numpy as jnp
from jax import lax
from jax.experimental import pallas as pl
from jax.experimental.pallas import tpu as pltpu

F32 = jnp.float32
BF16 = jnp.bfloat16

D_MODEL = 2048
HD = 128
D_CONV = 512
D_FOX = 768
D_NSA = 768
H_FOX = 6
H_NSA = 6
NSA_G = 2
NSA_R = 3
CONV_WIDTH = 3
CMP_LEN = 32
CMP_STRIDE = 16
CMP_HIDDEN = 256
SEL_LEN = 64
SEL_TOPK = 16
N_FORCED = 3
CMP_PAD = 16
CMP_STRIP = 24
WINDOW = 512
REL_BUCKETS = 32
REL_MAX_DIST = 128
NORM_EPS = 1e-6
NEG_BIG = -1e30
LOG2E = math.log2(math.e)

TILE = 128
ONES_ROWS = 16
NSA_CHUNK = 512
FOX_T = 512
FOX_TK = 512
FOX_HEADS = 2
PROJ_TM = 1024
PROJ_TN = 1280
PROJ_TV = 1280
VMEM_LIMIT = 56 * 1024 * 1024

NT_DIMS = (((1,), (1,)), ((), ()))

BLK_QN, BLK_ZN, BLK_KS, BLK_KW = 16, 22, 28, 30
BLK_QF, BLK_KF, BLK_ZF = 32, 38, 44


def _cparams(sem, vmem=VMEM_LIMIT):
    return pltpu.CompilerParams(dimension_semantics=sem, vmem_limit_bytes=vmem)


def _silu(v):
    return v * jax.nn.sigmoid(v)


def _mod_kernel(c_ref, w_ref, b_ref, o_ref):
    ca = _silu(c_ref[...])
    o_ref[0] = jnp.sum(w_ref[0] * ca, axis=0, keepdims=True) + b_ref[0]


def _ada_mod(c_col, w_ada, b_ada):
    depth, d, n = w_ada.shape
    tn = 1536
    return pl.pallas_call(
        _mod_kernel,
        grid=(depth, n // tn),
        in_specs=[pl.BlockSpec((d, 1), lambda l, j: (0, 0)),
                  pl.BlockSpec((1, d, tn), lambda l, j: (l, 0, j)),
                  pl.BlockSpec((1, 1, tn), lambda l, j: (l, 0, j))],
        out_specs=pl.BlockSpec((1, 1, tn), lambda l, j: (l, 0, j)),
        out_shape=jax.ShapeDtypeStruct((depth, 1, n), F32),
        compiler_params=_cparams(("parallel", "parallel")),
        name="ada_mod",
    )(c_col, w_ada, b_ada.reshape(depth, 1, n))


def _modulated_norm(x, g, scale, shift):
    y = x * lax.rsqrt(jnp.mean(x * x, axis=-1, keepdims=True) + NORM_EPS) * g
    return y * (1.0 + scale) + shift


def _prenorm_kernel(x_ref, g_ref, sc_ref, sh_ref, h_ref):
    h_ref[...] = _modulated_norm(x_ref[...], g_ref[...], sc_ref[...], sh_ref[...]).astype(BF16)


def _prenorm(x2, g, scale, shift):
    s, d = x2.shape
    tm = 512
    vec = pl.BlockSpec((1, d), lambda i: (0, 0))
    return pl.pallas_call(
        _prenorm_kernel,
        grid=(s // tm,),
        in_specs=[pl.BlockSpec((tm, d), lambda i: (i, 0)), vec, vec, vec],
        out_specs=pl.BlockSpec((tm, d), lambda i: (i, 0)),
        out_shape=jax.ShapeDtypeStruct((s, d), BF16),
        compiler_params=_cparams(("parallel",)),
        name="prenorm",
    )(x2, g, scale, shift)


def _mm_kernel(h_ref, wt_ref, o_ref):
    o_ref[...] = lax.dot_general(h_ref[...], wt_ref[...], NT_DIMS,
                                 preferred_element_type=F32).astype(o_ref.dtype)


def _matmul(h, wt, l):
    s, d = h.shape
    n = wt.shape[1]
    tm, tn = PROJ_TM, PROJ_TN
    return pl.pallas_call(
        _mm_kernel,
        grid=(s // tm, n // tn),
        in_specs=[pl.BlockSpec((tm, d), lambda i, j: (i, 0)),
                  pl.BlockSpec((None, tn, d), lambda i, j: (l, j, 0))],
        out_specs=pl.BlockSpec((tm, tn), lambda i, j: (i, j)),
        out_shape=jax.ShapeDtypeStruct((s, n), BF16),
        compiler_params=_cparams(("parallel", "arbitrary")),
        name="proj_rows",
    )(h, wt)


def _mm_split_kernel(h_ref, wt_ref, o_ref):
    res = lax.dot_general(h_ref[...], wt_ref[...], NT_DIMS, preferred_element_type=F32)
    for c in range(o_ref.shape[0]):
        o_ref[c] = res[:, c * TILE:(c + 1) * TILE].astype(o_ref.dtype)


def _matmul_split(h, wt, l):
    s, d = h.shape
    n = wt.shape[1]
    tm = PROJ_TM
    return pl.pallas_call(
        _mm_split_kernel,
        grid=(s // tm,),
        in_specs=[pl.BlockSpec((tm, d), lambda i: (i, 0)),
                  pl.BlockSpec((None, n, d), lambda i: (l, 0, 0))],
        out_specs=pl.BlockSpec((n // TILE, tm, TILE), lambda i: (0, i, 0)),
        out_shape=jax.ShapeDtypeStruct((n // TILE, s, TILE), BF16),
        compiler_params=_cparams(("parallel",)),
        name="proj_rows_split",
    )(h, wt)


def _mm_t_tiled_kernel(h_ref, wt_ref, o_ref):
    res = lax.dot_general(wt_ref[...], h_ref[...], NT_DIMS, preferred_element_type=F32)
    for c in range(o_ref.shape[0]):
        o_ref[c] = res[:, c * TILE:(c + 1) * TILE].astype(o_ref.dtype)


def _matmul_t_tiled(h, wt, l):
    s, d = h.shape
    n = wt.shape[1]
    tm, tn = PROJ_TM, PROJ_TV
    return pl.pallas_call(
        _mm_t_tiled_kernel,
        grid=(s // tm, n // tn),
        in_specs=[pl.BlockSpec((tm, d), lambda i, j: (i, 0)),
                  pl.BlockSpec((None, tn, d), lambda i, j: (l, j, 0))],
        out_specs=pl.BlockSpec((tm // TILE, tn, TILE), lambda i, j: (i, j, 0)),
        out_shape=jax.ShapeDtypeStruct((s // TILE, n, TILE), BF16),
        compiler_params=_cparams(("parallel", "arbitrary")),
        name="proj_cols_tiled",
    )(h, wt)


def _fox_prep_kernel(h_ref, w_ref, b_ref, qa_ref, ka_ref, st_ref, carry_ref, *, tm):
    @pl.when(pl.program_id(0) == 0)
    def _():
        carry_ref[...] = jnp.zeros_like(carry_ref)

    small = lax.dot_general(h_ref[...], w_ref[...], NT_DIMS,
                            preferred_element_type=F32)
    st_ref[...] = small.T
    v = small + b_ref[...]
    c = jnp.minimum(v, 0.0) - jnp.log1p(jnp.exp(-jnp.abs(v)))
    row = lax.broadcasted_iota(jnp.int32, (tm, TILE), 0)
    sh = 1
    while sh < tm:
        c = c + jnp.where(row >= sh, pltpu.roll(c, sh, axis=0), 0.0)
        sh *= 2
    c = c + carry_ref[...]
    carry_ref[...] = c[tm - 1:tm, :]
    c = c * LOG2E

    lane = lax.broadcasted_iota(jnp.int32, (tm, TILE), 1)
    for h in range(H_FOX):
        ch = jnp.broadcast_to(c[:, h:h + 1], (tm, TILE))
        hi = ch.astype(BF16).astype(F32)
        r1 = ch - hi
        mid = r1.astype(BF16).astype(F32)
        lo = r1 - mid
        pieces = jnp.where((lane == 0) | (lane == 3), hi,
                           jnp.where((lane == 1) | (lane == 4), mid, lo))
        qa = jnp.where(lane < 3, pieces, jnp.where(lane < 6, 1.0, 0.0))
        ka = jnp.where(lane < 3, 1.0, jnp.where(lane < 6, -pieces, 0.0))
        qa_ref[:, h * TILE:(h + 1) * TILE] = qa.astype(BF16)
        ka_ref[:, h * TILE:(h + 1) * TILE] = ka.astype(BF16)


def _fox_prep(h, w_small, l, b_pad):
    s, d = h.shape
    tm = 512
    out = jax.ShapeDtypeStruct((s, H_FOX * TILE), BF16)
    return pl.pallas_call(
        functools.partial(_fox_prep_kernel, tm=tm),
        grid=(s // tm,),
        in_specs=[pl.BlockSpec((tm, d), lambda i: (i, 0)),
                  pl.BlockSpec((None, TILE, d), lambda i: (l, 0, 0)),
                  pl.BlockSpec((1, TILE), lambda i: (0, 0))],
        out_specs=[pl.BlockSpec((tm, H_FOX * TILE), lambda i: (i, 0)),
                   pl.BlockSpec((tm, H_FOX * TILE), lambda i: (i, 0)),
                   pl.BlockSpec((TILE, tm), lambda i: (0, i))],
        out_shape=[out, out, jax.ShapeDtypeStruct((TILE, s), F32)],
        scratch_shapes=[pltpu.VMEM((1, TILE), F32)],
        compiler_params=_cparams(("arbitrary",)),
        name="fox_prep",
    )(h, w_small, b_pad)


def _fox_kernel(q_ref, qa_ref, k_ref, ka_ref, vt_ref, z_ref, o_ref, s0_ref, s1_ref, s2_ref,
                acc_ref):
    i = pl.program_id(1)
    t = FOX_T
    tk = FOX_TK
    heads = range(FOX_HEADS)
    col = lambda h: slice(h * TILE, (h + 1) * TILE)
    qaug = [jnp.concatenate([q_ref[:, col(h)], qa_ref[:, col(h)]], axis=1) for h in heads]

    def scores(h, j, rows):
        r0 = pl.multiple_of(j * rows, rows)
        kaug = jnp.concatenate([k_ref[pl.ds(r0, rows), col(h)], ka_ref[pl.ds(r0, rows), col(h)]],
                               axis=1)
        return lax.dot_general(kaug, qaug[h], NT_DIMS, preferred_element_type=F32)

    def values_t(h, j, rows):
        n_sub = rows // TILE
        return jnp.concatenate(
            [jnp.concatenate([vt_ref[n_sub * j + c, col(h), :] for c in range(n_sub)], axis=1),
             jnp.ones((ONES_ROWS, rows), BF16)], axis=0)

    kio = lax.broadcasted_iota(jnp.int32, (t, t), 0)
    qio = lax.broadcasted_iota(jnp.int32, (t, t), 1)
    ms = []
    for h in heads:
        s = jnp.where(kio <= qio, scores(h, i, t), NEG_BIG)
        m = jnp.max(s, axis=0, keepdims=True)
        p = jnp.exp2(s - m)
        ms.append(m)
        acc_ref[h] = jnp.dot(values_t(h, i, t), p.astype(BF16), preferred_element_type=F32)

    def update(h, s, vt, m):
        m_new = jnp.maximum(m, jnp.max(s, axis=0, keepdims=True))
        alpha = jnp.exp2(m - m_new)
        p = jnp.exp2(s - m_new)
        acc_ref[h] = alpha * acc_ref[h] + jnp.dot(vt, p.astype(BF16), preferred_element_type=F32)
        return m_new

    bufs = (s0_ref, s1_ref, s2_ref)
    for h in heads:
        s0_ref[h] = scores(h, 0, tk)

    def triple(pp, ms):
        ms = list(ms)
        for stage in range(3):
            cur, nxt = bufs[stage], bufs[(stage + 1) % 3]
            for h in heads:
                nxt[h] = scores(h, 3 * pp + stage + 1, tk)
            for h in heads:
                ms[h] = update(h, cur[h], values_t(h, 3 * pp + stage, tk), ms[h])
        return tuple(ms)

    n_far = i * (t // tk)
    ms = lax.fori_loop(0, n_far // 3, triple, tuple(ms))
    done = (n_far // 3) * 3

    @pl.when(n_far - done == 1)
    def _():
        for h in heads:
            update(h, s0_ref[h], values_t(h, done, tk), ms[h])

    @pl.when(n_far - done == 2)
    def _():
        for h in heads:
            s1_ref[h] = scores(h, done + 1, tk)
        for h in heads:
            m_mid = update(h, s0_ref[h], values_t(h, done, tk), ms[h])
            update(h, s1_ref[h], values_t(h, done + 1, tk), m_mid)

    for h in heads:
        o = (acc_ref[h, 0:HD, :] * (1.0 / acc_ref[h, HD:HD + 1, :])).T
        o_ref[:, col(h)] = (o * _silu(z_ref[:, col(h)].astype(F32))).astype(BF16)


def _fox_attention(proj, qa, ka, proj_t):
    s = proj.shape[0]
    t = FOX_T
    nk = s // TILE
    hw = FOX_HEADS * TILE
    score_buf = pltpu.VMEM((FOX_HEADS, FOX_TK, t), F32)
    return pl.pallas_call(
        _fox_kernel,
        grid=(H_FOX // FOX_HEADS, s // t),
        scratch_shapes=[score_buf, score_buf, score_buf,
                        pltpu.VMEM((FOX_HEADS, HD + ONES_ROWS, t), F32)],
        in_specs=[pl.BlockSpec((t, hw), lambda h, i: (i, BLK_QF // FOX_HEADS + h)),
                  pl.BlockSpec((t, hw), lambda h, i: (i, h)),
                  pl.BlockSpec((s, hw), lambda h, i: (0, BLK_KF // FOX_HEADS + h)),
                  pl.BlockSpec((s, hw), lambda h, i: (0, h)),
                  pl.BlockSpec((nk, hw, TILE), lambda h, i: (0, h, 0)),
                  pl.BlockSpec((t, hw), lambda h, i: (i, BLK_ZF // FOX_HEADS + h))],
        out_specs=pl.BlockSpec((t, hw), lambda h, i: (i, h)),
        out_shape=jax.ShapeDtypeStruct((s, D_FOX), BF16),
        compiler_params=_cparams(("parallel", "parallel")),
        name="fox_attention",
    )(proj, qa, proj, ka, proj_t, proj)


def _compress_kernel(x_ref, w1_ref, pe_ref, w2_ref, o_ref, ot_ref):
    n = x_ref.shape[0]
    ab = jnp.dot(x_ref[...], w1_ref[0], preferred_element_type=F32)
    pe = jnp.dot(pe_ref[0], w1_ref[0], preferred_element_type=F32)
    a = ab[:, :CMP_HIDDEN]
    b_next = pltpu.roll(ab[:, CMP_HIDDEN:], n - 1, axis=0)
    hid = a + b_next + pe[0:1, :CMP_HIDDEN] + pe[8:9, CMP_HIDDEN:]
    out = jnp.dot(_silu(hid).astype(BF16), w2_ref[0], preferred_element_type=F32)
    o_ref[...] = out.astype(BF16)
    ot_ref[...] = out.T.astype(BF16)


def _compress(x4, w1cat, pe16, w2, l):
    _, n, k = x4.shape
    kind = lambda c: (l, c // NSA_G, 0, 0)
    return pl.pallas_call(
        _compress_kernel,
        grid=(2 * NSA_G,),
        in_specs=[pl.BlockSpec((None, n, k), lambda c: (c, 0, 0)),
                  pl.BlockSpec((None, 1, k, 2 * CMP_HIDDEN), kind),
                  pl.BlockSpec((None, 1, 16, k), kind),
                  pl.BlockSpec((None, 1, CMP_HIDDEN, HD), kind)],
        out_specs=[pl.BlockSpec((None, n, HD), lambda c: (c, 0, 0)),
                   pl.BlockSpec((None, HD, n), lambda c: (c, 0, 0))],
        out_shape=[jax.ShapeDtypeStruct((2 * NSA_G, n, HD), BF16),
                   jax.ShapeDtypeStruct((2 * NSA_G, HD, n), BF16)],
        compiler_params=_cparams(("parallel",)),
        name="nsa_compress",
    )(x4, w1cat, pe16, w2)


def _nsa_kernel(*refs, n_cmp):
    q_refs, z_refs = refs[0:H_NSA], refs[H_NSA:2 * H_NSA]
    (ks_ref, kw_ref, vst_ref, vwt_ref, kc_ref, vct_ref, c2s_ref, cstrip_ref, t0_ref, t1_ref,
     gl_ref, o_ref, sc_ref, sel_ref, m_s, acc_s, acc_w, ss_ref, sw_ref, sa_ref, sb_ref,
     sc3_ref) = refs[2 * H_NSA:]
    i = pl.program_id(0)
    w3 = NSA_R * TILE
    gw = NSA_G * TILE
    groups = range(NSA_G)
    col = lambda g: slice(g * TILE, (g + 1) * TILE)
    qs = [jnp.concatenate([q_refs[NSA_R * g + r][...] for r in range(NSA_R)], axis=0)
          for g in groups]
    b_io = lax.broadcasted_iota(jnp.int32, (TILE, w3), 0)
    a_io = lax.broadcasted_iota(jnp.int32, (TILE, w3), 1) & (TILE - 1)
    tpc = NSA_CHUNK // TILE
    bpc = NSA_CHUNK // SEL_LEN

    def key_scores(k_ref, g, tile0, n_tiles):
        r0 = pl.multiple_of(tile0 * TILE, TILE)
        return lax.dot_general(k_ref[pl.ds(r0, n_tiles * TILE), col(g)], qs[g], NT_DIMS,
                               preferred_element_type=F32)

    def values_t(vt_ref, g, tile0, n_tiles):
        return jnp.concatenate(
            [jnp.concatenate([vt_ref[tile0 + c, col(g), :] for c in range(n_tiles)], axis=1),
             jnp.ones((ONES_ROWS, n_tiles * TILE), BF16)], axis=0)

    def fix_near(s_ref, g, d):
        own = pl.ds(pl.multiple_of(d * TILE, TILE), TILE)
        prev = pl.ds(pl.multiple_of(jnp.maximum(d - 1, 0) * TILE, TILE), TILE)
        s_ref[g, prev, :] = s_ref[g, prev, :] + jnp.where(d >= 1, t1_ref[g], 0.0)
        s_ref[g, own, :] = jnp.where(b_io <= a_io, s_ref[g, own, :] + t0_ref[g], NEG_BIG)

    w0 = pl.multiple_of((TILE // CMP_STRIDE) * i, TILE // CMP_STRIDE)
    n_io = lax.broadcasted_iota(jnp.int32, (n_cmp, w3), 0)
    qa_io = lax.broadcasted_iota(jnp.int32, (n_cmp, w3), 1) & (TILE - 1)
    valid = (CMP_STRIDE * n_io + (CMP_LEN - 1)) <= (TILE * i + qa_io)
    n_wt = WINDOW // TILE + 1
    w_tile0 = jnp.maximum(i - WINDOW // TILE, 0)
    d_win = i - w_tile0
    for g in groups:
        sc_ref[g, 0:CMP_PAD, :] = jnp.zeros((CMP_PAD, w3), F32)
        sc_ref[g, CMP_PAD:CMP_PAD + n_cmp, :] = lax.dot_general(kc_ref[g], qs[g], NT_DIMS,
                                                                preferred_element_type=F32)
        sw_ref[g] = key_scores(kw_ref, g, w_tile0, n_wt)
    oc_t, psums = [], []
    for g in groups:
        strip = pl.ds(w0, CMP_STRIP)
        sc_ref[g, strip, :] = sc_ref[g, strip, :] + cstrip_ref[g]
        scm = jnp.where(valid, sc_ref[g, CMP_PAD:CMP_PAD + n_cmp, :], NEG_BIG)
        mc = jnp.max(scm, axis=0, keepdims=True)
        pc = jnp.exp2(scm - mc)
        lc = jnp.sum(pc, axis=0, keepdims=True)
        pcn = pc * jnp.where(mc > 0.5 * NEG_BIG, 1.0 / lc, 0.0)
        oc_t.append(jnp.dot(vct_ref[g], pcn.astype(BF16), preferred_element_type=F32))
        psums.append(pcn[:, 0:TILE] + pcn[:, TILE:2 * TILE] + pcn[:, 2 * TILE:3 * TILE])

        for c in range(1, n_wt):
            rows = slice(c * TILE, (c + 1) * TILE)
            sw_ref[g, rows, :] = jnp.where(d_win < c, NEG_BIG, sw_ref[g, rows, :])
        fix_near(sw_ref, g, d_win)
        sw_ref[g, 0:TILE, :] = jnp.where((a_io < b_io) | (i < WINDOW // TILE),
                                         sw_ref[g, 0:TILE, :], NEG_BIG)
        sw = sw_ref[g]
        p_w = jnp.exp2(sw - jnp.max(sw, axis=0, keepdims=True))
        acc_w[g] = jnp.dot(values_t(vwt_ref, g, w_tile0, n_wt), p_w.astype(BF16),
                           preferred_element_type=F32)

    psum = jnp.concatenate(psums, axis=1)
    p_hi = psum.astype(BF16)
    p_lo = (psum - p_hi.astype(F32)).astype(BF16)
    c2s = c2s_ref[...]
    imp = (jnp.dot(c2s, p_hi, preferred_element_type=F32)
           + jnp.dot(c2s, p_lo, preferred_element_type=F32))
    n_sel = imp.shape[0]
    m_io = lax.broadcasted_iota(jnp.int32, (n_sel, gw), 0)
    m_f = m_io.astype(F32)
    qpos = TILE * i + (lax.broadcasted_iota(jnp.int32, (n_sel, gw), 1) & (TILE - 1))
    own = qpos >> (SEL_LEN.bit_length() - 1)
    forced = (m_io == 0) | (m_io == own) | (m_io == own - 1)
    eligible = SEL_LEN * m_io <= qpos
    sel = jnp.where(eligible & forced, 1.0, 0.0)
    cur = jnp.where(eligible, jnp.where(forced, -2.0, imp), -1.0)
    for _ in range(min(SEL_TOPK, n_sel) - N_FORCED):
        mx = jnp.max(cur, axis=0, keepdims=True)
        idx = jnp.min(jnp.where(cur == mx, m_f, float(n_sel)), axis=0, keepdims=True)
        pick = m_f == idx
        sel = jnp.where(pick, jnp.maximum(jnp.where(mx >= 0.0, 1.0, 0.0), sel), sel)
        cur = jnp.where(pick, -2.0, cur)
    sel_ref[...] = sel

    def sel_mask(g, c, n_tiles):
        rows = sel_ref[pl.ds(pl.multiple_of(c * bpc, 8), bpc), col(g)]
        mk = jnp.concatenate([jnp.broadcast_to(rows[b:b + 1, :], (SEL_LEN, TILE))
                              for b in range(n_tiles * TILE // SEL_LEN)], axis=0)
        return jnp.concatenate([mk, mk, mk], axis=1) > 0.5

    def sel_update(g, c, s, n_tiles=tpc):
        s = jnp.where(sel_mask(g, c, n_tiles), s, NEG_BIG)
        m_old = m_s[g]
        m_new = jnp.maximum(m_old, jnp.max(s, axis=0, keepdims=True))
        alpha = jnp.exp2(m_old - m_new)
        p = jnp.exp2(s - m_new)
        acc_s[g] = alpha * acc_s[g] + jnp.dot(values_t(vst_ref, g, c * tpc, n_tiles),
                                              p.astype(BF16), preferred_element_type=F32)
        m_s[g] = m_new

    m_s[...] = jnp.full(m_s.shape, NEG_BIG, F32)
    acc_s[...] = jnp.zeros(acc_s.shape, F32)

    c_own = i // tpc
    c_prev = jnp.maximum(i - 1, 0) // tpc

    bufs = (sa_ref, sb_ref, sc3_ref)
    for g in groups:
        sa_ref[g] = key_scores(ks_ref, g, 0, tpc)

    def far_triple(pp, carry):
        for stage in range(3):
            cur_buf, nxt_buf = bufs[stage], bufs[(stage + 1) % 3]
            for g in groups:
                nxt_buf[g] = key_scores(ks_ref, g, (3 * pp + stage + 1) * tpc, tpc)
            for g in groups:
                sel_update(g, 3 * pp + stage, cur_buf[g])
        return carry

    lax.fori_loop(0, c_prev // 3, far_triple, 0)
    done = (c_prev // 3) * 3

    @pl.when(c_prev - done == 1)
    def _():
        for g in groups:
            sel_update(g, done, sa_ref[g])

    @pl.when(c_prev - done == 2)
    def _():
        for g in groups:
            sb_ref[g] = key_scores(ks_ref, g, (done + 1) * tpc, tpc)
        for g in groups:
            sel_update(g, done, sa_ref[g])
            sel_update(g, done + 1, sb_ref[g])

    @pl.when(c_prev != c_own)
    def _():
        last = slice((tpc - 1) * TILE, tpc * TILE)
        for g in groups:
            ss_ref[g] = key_scores(ks_ref, g, c_prev * tpc, tpc)
            ss_ref[g, last, :] = ss_ref[g, last, :] + t1_ref[g]
            sel_update(g, c_prev, ss_ref[g])

    for n_t in range(1, tpc + 1):
        @pl.when(i - c_own * tpc == n_t - 1)
        def _(n_t=n_t):
            own = slice((n_t - 1) * TILE, n_t * TILE)
            prev = slice((n_t - 2) * TILE, (n_t - 1) * TILE)
            for g in groups:
                ss_ref[g, 0:n_t * TILE, :] = key_scores(ks_ref, g, c_own * tpc, n_t)
            for g in groups:
                if n_t >= 2:
                    ss_ref[g, prev, :] = ss_ref[g, prev, :] + t1_ref[g]
                ss_ref[g, own, :] = jnp.where(b_io <= a_io, ss_ref[g, own, :] + t0_ref[g], NEG_BIG)
                sel_update(g, c_own, ss_ref[g, 0:n_t * TILE, :], n_t)

    for g in groups:
        gates = jax.nn.sigmoid(gl_ref[g])

        def gate_row(branch):
            return jnp.concatenate([gates[r * 3 + branch:r * 3 + branch + 1, :]
                                    for r in range(NSA_R)], axis=1)

        out_t = (oc_t[g] * gate_row(0)
                 + acc_s[g, 0:HD, :] * (gate_row(1) / acc_s[g, HD:HD + 1, :])
                 + acc_w[g, 0:HD, :] * (gate_row(2) / acc_w[g, HD:HD + 1, :]))
        for r in range(NSA_R):
            h = NSA_R * g + r
            o_r = out_t[:, r * TILE:(r + 1) * TILE].T
            o_ref[:, h * TILE:(h + 1) * TILE] = (
                o_r * _silu(z_refs[h][...].astype(F32))).astype(BF16)


def _nsa_attention(proj, proj_t, cmp_n, cmp_t, c2s_t, cstrip, t0, t1, glog_t):
    s = proj.shape[0]
    nk = s // TILE
    n_cmp = cmp_n.shape[1]
    n_sel = s // SEL_LEN
    w3 = NSA_R * TILE
    gw = NSA_G * TILE
    qspec = lambda h: pl.BlockSpec((TILE, TILE), lambda i, h=h: (i, BLK_QN + h))
    zspec = lambda h: pl.BlockSpec((TILE, TILE), lambda i, h=h: (i, BLK_ZN + h))
    whole = lambda shape: pl.BlockSpec(shape, lambda i: (0,) * len(shape))
    resident = dict(pipeline_mode=pl.Buffered(1))
    chunk_buf = pltpu.VMEM((NSA_G, NSA_CHUNK, w3), F32)
    acc_buf = pltpu.VMEM((NSA_G, HD + ONES_ROWS, w3), F32)
    return pl.pallas_call(
        functools.partial(_nsa_kernel, n_cmp=n_cmp),
        grid=(s // TILE,),
        in_specs=[qspec(h) for h in range(H_NSA)] + [zspec(h) for h in range(H_NSA)] + [
            pl.BlockSpec((s, gw), lambda i: (0, BLK_KS // NSA_G), **resident),
            pl.BlockSpec((s, gw), lambda i: (0, BLK_KW // NSA_G), **resident),
            pl.BlockSpec((nk, gw, TILE), lambda i: (0, H_FOX // NSA_G, 0), **resident),
            pl.BlockSpec((nk, gw, TILE), lambda i: (0, H_FOX // NSA_G + 1, 0), **resident),
            pl.BlockSpec((NSA_G, n_cmp, HD), lambda i: (0, 0, 0)),
            pl.BlockSpec((NSA_G, HD, n_cmp), lambda i: (1, 0, 0)),
            whole((n_sel, n_cmp)), whole((NSA_G, CMP_STRIP, w3)), whole((NSA_G, TILE, w3)),
            whole((NSA_G, TILE, w3)),
            pl.BlockSpec((NSA_G, NSA_R * 3, TILE), lambda i: (0, 0, i))],
        out_specs=pl.BlockSpec((TILE, D_NSA), lambda i: (i, 0)),
        out_shape=jax.ShapeDtypeStruct((s, D_NSA), BF16),
        scratch_shapes=[pltpu.VMEM((NSA_G, CMP_PAD + n_cmp + 8, w3), F32),
                        pltpu.VMEM((n_sel, gw), F32),
                        pltpu.VMEM((NSA_G, 1, w3), F32),
                        acc_buf, acc_buf,
                        chunk_buf,
                        pltpu.VMEM((NSA_G, WINDOW + TILE, w3), F32),
                        chunk_buf, chunk_buf, chunk_buf],
        compiler_params=_cparams(("parallel",)),
        name="nsa_attention",
    )(*([proj] * (2 * H_NSA)), proj, proj, proj_t, proj_t,
      cmp_n, cmp_t, c2s_t, cstrip, t0, t1, glog_t)


def _gated_conv(u_ref, gb_ref, gc_ref, z_ref, uh_ref, gch_ref, w_ref):
    tm = u_ref.shape[0]
    y = gc_ref[...].astype(F32) * u_ref[...].astype(F32)
    yh = gch_ref[...].astype(F32) * uh_ref[...].astype(F32)
    yh = jnp.where(pl.program_id(0) > 0, yh, 0.0)
    h1 = yh[15:16, :]
    h2 = yh[14:15, :]
    row = lax.broadcasted_iota(jnp.int32, (tm, D_CONV), 0)
    y1 = jnp.where(row == 0, h1, pltpu.roll(y, 1, axis=0))
    y2 = jnp.where(row == 0, h2, jnp.where(row == 1, h1, pltpu.roll(y, 2, axis=0)))
    w = w_ref[...]
    conv = w[0:1, :] * y2 + w[1:2, :] * y1 + w[2:3, :] * y
    return gb_ref[...].astype(F32) * conv * _silu(z_ref[...].astype(F32))


def _out_kernel(u_ref, gb_ref, gc_ref, z_ref, uh_ref, gch_ref, cw_ref, yf_ref, yn_ref, w_ref,
                x_ref, pg_ref, gate_ref, *rest, with_next):
    if with_next:
        ng_ref, nsc_ref, nsh_ref, xo_ref, h_ref, ya_ref = rest
    else:
        xo_ref, ya_ref = rest
    ya_ref[...] = _gated_conv(u_ref, gb_ref, gc_ref, z_ref, uh_ref, gch_ref, cw_ref).astype(BF16)
    tm = x_ref.shape[0]
    n_parts = 4
    for part in range(n_parts):
        rows = slice(part * tm // n_parts, (part + 1) * tm // n_parts)
        y = (jnp.dot(ya_ref[rows, :], w_ref[0:D_CONV, :], preferred_element_type=F32)
             + jnp.dot(yf_ref[rows, :], w_ref[D_CONV:D_CONV + D_FOX, :], preferred_element_type=F32)
             + jnp.dot(yn_ref[rows, :], w_ref[D_CONV + D_FOX:, :], preferred_element_type=F32))
        yn = y * lax.rsqrt(jnp.mean(y * y, axis=-1, keepdims=True) + NORM_EPS) * pg_ref[...]
        xn = x_ref[rows, :] + gate_ref[...] * yn
        if with_next:
            h_ref[rows, :] = _modulated_norm(xn, ng_ref[...], nsc_ref[...],
                                             nsh_ref[...]).astype(BF16)
        xo_ref[rows, :] = xn


def _out_proj(proj, conv_w, yf, yn, w_out, l, x2, post_g, gate, nxt):
    s, d = x2.shape
    tm = 512
    rows = lambda n: pl.BlockSpec((tm, n), lambda i: (i, 0))
    vec = pl.BlockSpec((1, d), lambda i: (0, 0))
    main = lambda c: pl.BlockSpec((tm, D_CONV), lambda i, c=c: (i, c))
    halo = lambda c: pl.BlockSpec((16, D_CONV), lambda i, c=c: (jnp.maximum(i * (tm // 16) - 1, 0), c))
    in_specs = [main(0), main(1), main(2), main(3), halo(0), halo(2),
                pl.BlockSpec((CONV_WIDTH, D_CONV), lambda i: (0, 0)),
                rows(D_FOX), rows(D_NSA),
                pl.BlockSpec((None, d, d), lambda i: (l, 0, 0), pipeline_mode=pl.Buffered(1)),
                rows(d), vec, vec]
    args = [proj, proj, proj, proj, proj, proj, conv_w, yf, yn, w_out, x2, post_g, gate]
    out_specs = [rows(d)]
    out_shape = [jax.ShapeDtypeStruct((s, d), F32)]
    if nxt is not None:
        in_specs += [vec, vec, vec]
        args += list(nxt)
        out_specs.append(rows(d))
        out_shape.append(jax.ShapeDtypeStruct((s, d), BF16))
    res = pl.pallas_call(
        functools.partial(_out_kernel, with_next=nxt is not None),
        grid=(s // tm,),
        in_specs=in_specs,
        out_specs=out_specs,
        out_shape=out_shape,
        scratch_shapes=[pltpu.VMEM((tm, D_CONV), BF16)],
        compiler_params=_cparams(("parallel",)),
        name="out_proj",
    )(*args)
    return res if nxt is not None else (res[0], None)


def _bucket_of_distance():
    max_exact = REL_BUCKETS // 2
    d = np.arange(REL_MAX_DIST)
    nf = np.maximum(d, max_exact).astype(np.float32)
    large = max_exact + (np.log(nf / np.float32(max_exact)) / np.float32(math.log(REL_MAX_DIST / max_exact))
                         * np.float32(REL_BUCKETS - max_exact)).astype(np.int32)
    return np.where(d < max_exact, d, np.minimum(large, REL_BUCKETS - 1))


def _distance_tables():
    b = np.arange(TILE)[:, None]
    a = np.arange(TILE)[None, :]
    none = REL_MAX_DIST
    d0 = np.where(a >= b, a - b, none)
    d1 = np.where(a < b, TILE + a - b, none)
    m = np.arange(CMP_STRIP)[:, None]
    dc = a - CMP_STRIDE * m + (CMP_STRIDE * CMP_PAD - (CMP_LEN - 1))
    dc = np.where((dc >= 0) & (dc < REL_MAX_DIST), dc, none)
    return d0, d1, dc


def _bias_tables(rel_bias):
    bucket = np.concatenate([_bucket_of_distance(), [REL_BUCKETS - 1]])
    rel = (rel_bias - rel_bias[REL_BUCKETS - 1:REL_BUCKETS, :]) * LOG2E
    rel = rel.reshape(REL_BUCKETS, NSA_G, NSA_R)

    def expand(idx):
        bk = bucket[idx]
        tab = jnp.zeros((NSA_G, idx.shape[0], NSA_R, TILE), F32)
        for b in range(REL_BUCKETS - 1):
            hit = jnp.asarray(bk == b)[None, :, None, :]
            tab = jnp.where(hit, rel[b][:, None, :, None], tab)
        return tab.reshape(NSA_G, idx.shape[0], NSA_R * TILE)

    d0, d1, dc = _distance_tables()
    return expand(d0), expand(d1), expand(dc)


def _cmp_to_sel_t(s, n_cmp_pad):
    n_cmp = (s - CMP_LEN) // CMP_STRIDE + 1
    n_sel = s // SEL_LEN
    c_lo = np.arange(n_cmp_pad)[None, :] * CMP_STRIDE
    sel_start = np.arange(n_sel)[:, None] * SEL_LEN
    overlap = (c_lo < sel_start + SEL_LEN) & (c_lo + CMP_LEN > sel_start)
    overlap &= np.arange(n_cmp_pad)[None, :] < n_cmp
    return jnp.asarray(overlap, dtype=BF16)


def _pack_w_in(w_in):
    sizes = ([D_CONV] * 4 + [D_FOX] * 3 + [H_FOX, D_FOX] + [D_NSA] + [2 * HD] * 6
             + [3 * H_NSA, D_NSA])
    offs = np.concatenate([[0], np.cumsum(sizes)])
    names = ["u", "gb", "gc", "za", "qf", "kf", "vf", "ff", "zf", "qn", "kc", "vc", "ks", "vs",
             "kw", "vw", "gn", "zn"]
    wt = jnp.swapaxes(w_in, 1, 2)
    col = {n: wt[:, int(offs[k]):int(offs[k + 1]), :] for k, n in enumerate(names)}
    scale = HD ** -0.5 * LOG2E
    order = ["u", "gb", "gc", "za", "qn", "zn", "ks", "kw", "qf", "kf", "zf"]
    row_scale = np.concatenate([np.full(col[n].shape[1], scale if n in ("qn", "qf") else 1.0,
                                        np.float32) for n in order])
    w_a = (jnp.concatenate([col[n] for n in order], axis=1)
           * row_scale[None, :, None]).astype(BF16)
    w_kv = jnp.concatenate([col["kc"], col["vc"]], axis=1).astype(BF16)
    w_v = jnp.concatenate([col["vf"], col["vs"], col["vw"]], axis=1).astype(BF16)
    small = jnp.concatenate([col["ff"], col["gn"]], axis=1)
    small = jnp.pad(small, ((0, 0), (0, TILE - small.shape[1]), (0, 0))).astype(BF16)
    return w_a, w_kv, w_v, small


def kernel(x, c, w_ada, b_ada, pre_norm, post_norm, w_in, b_forget, conv_w,
           cmp_pe_k, cmp_w1_k, cmp_w2_k, cmp_pe_v, cmp_w1_v, cmp_w2_v, w_out, rel_bias):
    bsz, s, d = x.shape
    assert bsz == 1 and d == D_MODEL and s % PROJ_TM == 0 and s >= 2 * WINDOW
    depth = w_in.shape[0]
    x2 = x.reshape(s, d)
    n_cmp_pad = s // CMP_STRIDE

    mod = _ada_mod(c.reshape(d, 1), w_ada, b_ada)
    shift, scale, gate = mod[:, :, :d], mod[:, :, d:2 * d], mod[:, :, 2 * d:]

    w_a, w_kv, w_v, w_small = _pack_w_in(w_in)
    w_out_b = w_out.astype(BF16)
    half = CMP_LEN * HD // 2
    w1cat = jnp.stack([jnp.concatenate([w[:, :half], w[:, half:]], axis=-1)
                       for w in (cmp_w1_k, cmp_w1_v)], axis=1).astype(BF16)
    pe = jnp.stack([cmp_pe_k, cmp_pe_v], axis=1).reshape(depth, 2, 2, 1, half)
    pe16 = jnp.broadcast_to(pe, (depth, 2, 2, 8, half)).reshape(depth, 2, 16, half).astype(BF16)
    w2 = jnp.stack([cmp_w2_k, cmp_w2_v], axis=1).astype(BF16)
    b_pad = jnp.pad(b_forget, ((0, 0), (0, TILE - H_FOX))).reshape(depth, 1, TILE)
    t0, t1, cstrip = _bias_tables(rel_bias)
    c2s_t = _cmp_to_sel_t(s, n_cmp_pad)

    h = _prenorm(x2, pre_norm[0:1], scale[0], shift[0])
    for l in range(depth):
        proj = _matmul(h, w_a, l)
        kcvc = _matmul_split(h, w_kv, l)
        proj_t = _matmul_t_tiled(h, w_v, l)

        qa, ka, small_t = _fox_prep(h, w_small, l, b_pad[l])
        yf = _fox_attention(proj, qa, ka, proj_t)

        x4 = kcvc.reshape(2 * NSA_G, n_cmp_pad, CMP_STRIDE * HD)
        cmp_n, cmp_t = _compress(x4, w1cat, pe16, w2, l)
        glog_t = small_t[H_FOX:H_FOX + 3 * H_NSA].reshape(NSA_G, NSA_R * 3, s)
        yn = _nsa_attention(proj, proj_t, cmp_n, cmp_t, c2s_t, cstrip, t0, t1, glog_t)

        nxt = None
        if l + 1 < depth:
            nxt = (pre_norm[l + 1:l + 2], scale[l + 1], shift[l + 1])
        x2, h = _out_proj(proj, conv_w[l], yf, yn, w_out_b, l, x2, post_norm[l:l + 1], gate[l], nxt)
    return x2.reshape(bsz, s, d)
```

```python
import functools
import math

import numpy as np
import jax
import jax.numpy as jnp
from jax import lax
from jax.experimental import pallas as pl
from jax.experimental.pallas import tpu as pltpu

F32 = jnp.float32
BF16 = jnp.bfloat16

D_MODEL = 2048
HD = 128
D_CONV = 512
D_FOX = 768
D_NSA = 768
H_FOX = 6
H_NSA = 6
NSA_G = 2
NSA_R = 3
CONV_WIDTH = 3
CMP_LEN = 32
CMP_STRIDE = 16
CMP_HIDDEN = 256
SEL_LEN = 64
SEL_TOPK = 16
N_FORCED = 3
CMP_PAD = 16
CMP_STRIP = 24
WINDOW = 512
REL_BUCKETS = 32
REL_MAX_DIST = 128
NORM_EPS = 1e-6
NEG_BIG = -1e30
LOG2E = math.log2(math.e)

TILE = 128
ONES_ROWS = 16
NSA_CHUNK = 512
FOX_T = 512
FOX_TK = 512
FOX_HEADS = 2
PROJ_TM = 1024
PROJ_TN = 1280
PROJ_TV = 1280
VMEM_LIMIT = 56 * 1024 * 1024

NT_DIMS = (((1,), (1,)), ((), ()))

BLK_QN, BLK_ZN, BLK_KS, BLK_KW = 16, 22, 28, 30
BLK_QF, BLK_KF, BLK_ZF = 32, 38, 44


def _cparams(sem, vmem=VMEM_LIMIT):
    return pltpu.CompilerParams(dimension_semantics=sem, vmem_limit_bytes=vmem)


def _silu(v):
    return v * jax.nn.sigmoid(v)


def _mod_kernel(c_ref, w_ref, b_ref, o_ref):
    ca = _silu(c_ref[...])
    o_ref[0] = jnp.sum(w_ref[0] * ca, axis=0, keepdims=True) + b_ref[0]


def _ada_mod(c_col, w_ada, b_ada):
    depth, d, n = w_ada.shape
    tn = 1536
    return pl.pallas_call(
        _mod_kernel,
        grid=(depth, n // tn),
        in_specs=[pl.BlockSpec((d, 1), lambda l, j: (0, 0)),
                  pl.BlockSpec((1, d, tn), lambda l, j: (l, 0, j)),
                  pl.BlockSpec((1, 1, tn), lambda l, j: (l, 0, j))],
        out_specs=pl.BlockSpec((1, 1, tn), lambda l, j: (l, 0, j)),
        out_shape=jax.ShapeDtypeStruct((depth, 1, n), F32),
        compiler_params=_cparams(("parallel", "parallel")),
        name="ada_mod",
    )(c_col, w_ada, b_ada.reshape(depth, 1, n))


def _modulated_norm(x, g, scale, shift):
    y = x * lax.rsqrt(jnp.mean(x * x, axis=-1, keepdims=True) + NORM_EPS) * g
    return y * (1.0 + scale) + shift


def _prenorm_kernel(x_ref, g_ref, sc_ref, sh_ref, h_ref):
    h_ref[...] = _modulated_norm(x_ref[...], g_ref[...], sc_ref[...], sh_ref[...]).astype(BF16)


def _prenorm(x2, g, scale, shift):
    s, d = x2.shape
    tm = 512
    vec = pl.BlockSpec((1, d), lambda i: (0, 0))
    return pl.pallas_call(
        _prenorm_kernel,
        grid=(s // tm,),
        in_specs=[pl.BlockSpec((tm, d), lambda i: (i, 0)), vec, vec, vec],
        out_specs=pl.BlockSpec((tm, d), lambda i: (i, 0)),
        out_shape=jax.ShapeDtypeStruct((s, d), BF16),
        compiler_params=_cparams(("parallel",)),
        name="prenorm",
    )(x2, g, scale, shift)


def _mm_kernel(h_ref, wt_ref, o_ref):
    o_ref[...] = lax.dot_general(h_ref[...], wt_ref[...], NT_DIMS,
                                 preferred_element_type=F32).astype(o_ref.dtype)


def _matmul(h, wt, l):
    s, d = h.shape
    n = wt.shape[1]
    tm, tn = PROJ_TM, PROJ_TN
    return pl.pallas_call(
        _mm_kernel,
        grid=(s // tm, n // tn),
        in_specs=[pl.BlockSpec((tm, d), lambda i, j: (i, 0)),
                  pl.BlockSpec((None, tn, d), lambda i, j: (l, j, 0))],
        out_specs=pl.BlockSpec((tm, tn), lambda i, j: (i, j)),
        out_shape=jax.ShapeDtypeStruct((s, n), BF16),
        compiler_params=_cparams(("parallel", "arbitrary")),
        name="proj_rows",
    )(h, wt)


def _mm_split_kernel(h_ref, wt_ref, o_ref):
    res = lax.dot_general(h_ref[...], wt_ref[...], NT_DIMS, preferred_element_type=F32)
    for c in range(o_ref.shape[0]):
        o_ref[c] = res[:, c * TILE:(c + 1) * TILE].astype(o_ref.dtype)


def _matmul_split(h, wt, l):
    s, d = h.shape
    n = wt.shape[1]
    tm = PROJ_TM
    return pl.pallas_call(
        _mm_split_kernel,
        grid=(s // tm,),
        in_specs=[pl.BlockSpec((tm, d), lambda i: (i, 0)),
                  pl.BlockSpec((None, n, d), lambda i: (l, 0, 0))],
        out_specs=pl.BlockSpec((n // TILE, tm, TILE), lambda i: (0, i, 0)),
        out_shape=jax.ShapeDtypeStruct((n // TILE, s, TILE), BF16),
        compiler_params=_cparams(("parallel",)),
        name="proj_rows_split",
    )(h, wt)


def _mm_t_tiled_kernel(h_ref, wt_ref, o_ref):
    res = lax.dot_general(wt_ref[...], h_ref[...], NT_DIMS, preferred_element_type=F32)
    for c in range(o_ref.shape[0]):
        o_ref[c] = res[:, c * TILE:(c + 1) * TILE].astype(o_ref.dtype)


def _matmul_t_tiled(h, wt, l):
    s, d = h.shape
    n = wt.shape[1]
    tm, tn = PROJ_TM, PROJ_TV
    return pl.pallas_call(
        _mm_t_tiled_kernel,
        grid=(s // tm, n // tn),
        in_specs=[pl.BlockSpec((tm, d), lambda i, j: (i, 0)),
                  pl.BlockSpec((None, tn, d), lambda i, j: (l, j, 0))],
        out_specs=pl.BlockSpec((tm // TILE, tn, TILE), lambda i, j: (i, j, 0)),
        out_shape=jax.ShapeDtypeStruct((s // TILE, n, TILE), BF16),
        compiler_params=_cparams(("parallel", "arbitrary")),
        name="proj_cols_tiled",
    )(h, wt)


def _fox_prep_kernel(h_ref, w_ref, b_ref, qa_ref, ka_ref, st_ref, carry_ref, *, tm):
    @pl.when(pl.program_id(0) == 0)
    def _():
        carry_ref[...] = jnp.zeros_like(carry_ref)

    small = lax.dot_general(h_ref[...], w_ref[...], NT_DIMS,
                            preferred_element_type=F32)
    st_ref[...] = small.T
    v = small + b_ref[...]
    c = jnp.minimum(v, 0.0) - jnp.log1p(jnp.exp(-jnp.abs(v)))
    row = lax.broadcasted_iota(jnp.int32, (tm, TILE), 0)
    sh = 1
    while sh < tm:
        c = c + jnp.where(row >= sh, pltpu.roll(c, sh, axis=0), 0.0)
        sh *= 2
    c = c + carry_ref[...]
    carry_ref[...] = c[tm - 1:tm, :]
    c = c * LOG2E

    lane = lax.broadcasted_iota(jnp.int32, (tm, TILE), 1)
    for h in range(H_FOX):
        ch = jnp.broadcast_to(c[:, h:h + 1], (tm, TILE))
        hi = ch.astype(BF16).astype(F32)
        r1 = ch - hi
        mid = r1.astype(BF16).astype(F32)
        lo = r1 - mid
        pieces = jnp.where((lane == 0) | (lane == 3), hi,
                           jnp.where((lane == 1) | (lane == 4), mid, lo))
        qa = jnp.where(lane < 3, pieces, jnp.where(lane < 6, 1.0, 0.0))
        ka = jnp.where(lane < 3, 1.0, jnp.where(lane < 6, -pieces, 0.0))
        qa_ref[:, h * TILE:(h + 1) * TILE] = qa.astype(BF16)
        ka_ref[:, h * TILE:(h + 1) * TILE] = ka.astype(BF16)


def _fox_prep(h, w_small, l, b_pad):
    s, d = h.shape
    tm = 512
    out = jax.ShapeDtypeStruct((s, H_FOX * TILE), BF16)
    return pl.pallas_call(
        functools.partial(_fox_prep_kernel, tm=tm),
        grid=(s // tm,),
        in_specs=[pl.BlockSpec((tm, d), lambda i: (i, 0)),
                  pl.BlockSpec((None, TILE, d), lambda i: (l, 0, 0)),
                  pl.BlockSpec((1, TILE), lambda i: (0, 0))],
        out_specs=[pl.BlockSpec((tm, H_FOX * TILE), lambda i: (i, 0)),
                   pl.BlockSpec((tm, H_FOX * TILE), lambda i: (i, 0)),
                   pl.BlockSpec((TILE, tm), lambda i: (0, i))],
        out_shape=[out, out, jax.ShapeDtypeStruct((TILE, s), F32)],
        scratch_shapes=[pltpu.VMEM((1, TILE), F32)],
        compiler_params=_cparams(("arbitrary",)),
        name="fox_prep",
    )(h, w_small, b_pad)


def _fox_kernel(q_ref, qa_ref, k_ref, ka_ref, vt_ref, z_ref, o_ref, s0_ref, s1_ref, s2_ref,
                acc_ref):
    i = pl.program_id(1)
    t = FOX_T
    tk = FOX_TK
    heads = range(FOX_HEADS)
    col = lambda h: slice(h * TILE, (h + 1) * TILE)
    qaug = [jnp.concatenate([q_ref[:, col(h)], qa_ref[:, col(h)]], axis=1) for h in heads]

    def scores(h, j, rows):
        r0 = pl.multiple_of(j * rows, rows)
        kaug = jnp.concatenate([k_ref[pl.ds(r0, rows), col(h)], ka_ref[pl.ds(r0, rows), col(h)]],
                               axis=1)
        return lax.dot_general(kaug, qaug[h], NT_DIMS, preferred_element_type=F32)

    def values_t(h, j, rows):
        n_sub = rows // TILE
        return jnp.concatenate(
            [jnp.concatenate([vt_ref[n_sub * j + c, col(h), :] for c in range(n_sub)], axis=1),
             jnp.ones((ONES_ROWS, rows), BF16)], axis=0)

    kio = lax.broadcasted_iota(jnp.int32, (t, t), 0)
    qio = lax.broadcasted_iota(jnp.int32, (t, t), 1)
    ms = []
    for h in heads:
        s = jnp.where(kio <= qio, scores(h, i, t), NEG_BIG)
        m = jnp.max(s, axis=0, keepdims=True)
        p = jnp.exp2(s - m)
        ms.append(m)
        acc_ref[h] = jnp.dot(values_t(h, i, t), p.astype(BF16), preferred_element_type=F32)

    def update(h, s_buf, vt, m):
        hk = tk // 2
        parts = (s_buf[h, 0:hk, :], s_buf[h, hk:tk, :])
        m_new = jnp.maximum(m, jnp.maximum(jnp.max(parts[0], axis=0, keepdims=True),
                                           jnp.max(parts[1], axis=0, keepdims=True)))
        acc = jnp.exp2(m - m_new) * acc_ref[h]
        for c, s in enumerate(parts):
            acc = acc + jnp.dot(vt[:, c * hk:(c + 1) * hk], jnp.exp2(s - m_new).astype(BF16),
                                preferred_element_type=F32)
        acc_ref[h] = acc
        return m_new

    bufs = (s0_ref, s1_ref, s2_ref)
    for h in heads:
        s0_ref[h] = scores(h, 0, tk)

    def triple(pp, ms):
        ms = list(ms)
        for stage in range(3):
            cur, nxt = bufs[stage], bufs[(stage + 1) % 3]
            for h in heads:
                nxt[h] = scores(h, 3 * pp + stage + 1, tk)
            for h in heads:
                ms[h] = update(h, cur, values_t(h, 3 * pp + stage, tk), ms[h])
        return tuple(ms)

    n_far = i * (t // tk)
    ms = lax.fori_loop(0, n_far // 3, triple, tuple(ms))
    done = (n_far // 3) * 3

    @pl.when(n_far - done == 1)
    def _():
        for h in heads:
            update(h, s0_ref, values_t(h, done, tk), ms[h])

    @pl.when(n_far - done == 2)
    def _():
        for h in heads:
            s1_ref[h] = scores(h, done + 1, tk)
        for h in heads:
            m_mid = update(h, s0_ref, values_t(h, done, tk), ms[h])
            update(h, s1_ref, values_t(h, done + 1, tk), m_mid)

    for h in heads:
        o = (acc_ref[h, 0:HD, :] * (1.0 / acc_ref[h, HD:HD + 1, :])).T
        o_ref[:, col(h)] = (o * _silu(z_ref[:, col(h)].astype(F32))).astype(BF16)


def _fox_attention(proj, qa, ka, proj_t):
    s = proj.shape[0]
    t = FOX_T
    nk = s // TILE
    hw = FOX_HEADS * TILE
    score_buf = pltpu.VMEM((FOX_HEADS, FOX_TK, t), F32)
    return pl.pallas_call(
        _fox_kernel,
        grid=(H_FOX // FOX_HEADS, s // t),
        scratch_shapes=[score_buf, score_buf, score_buf,
                        pltpu.VMEM((FOX_HEADS, HD + ONES_ROWS, t), F32)],
        in_specs=[pl.BlockSpec((t, hw), lambda h, i: (i, BLK_QF // FOX_HEADS + h)),
                  pl.BlockSpec((t, hw), lambda h, i: (i, h)),
                  pl.BlockSpec((s, hw), lambda h, i: (0, BLK_KF // FOX_HEADS + h)),
                  pl.BlockSpec((s, hw), lambda h, i: (0, h)),
                  pl.BlockSpec((nk, hw, TILE), lambda h, i: (0, h, 0)),
                  pl.BlockSpec((t, hw), lambda h, i: (i, BLK_ZF // FOX_HEADS + h))],
        out_specs=pl.BlockSpec((t, hw), lambda h, i: (i, h)),
        out_shape=jax.ShapeDtypeStruct((s, D_FOX), BF16),
        compiler_params=_cparams(("parallel", "parallel")),
        name="fox_attention",
    )(proj, qa, proj, ka, proj_t, proj)


def _compress_kernel(x_ref, w1_ref, pe_ref, w2_ref, o_ref, ot_ref):
    n = x_ref.shape[0]
    ab = jnp.dot(x_ref[...], w1_ref[0], preferred_element_type=F32)
    pe = jnp.dot(pe_ref[0], w1_ref[0], preferred_element_type=F32)
    a = ab[:, :CMP_HIDDEN]
    b_next = pltpu.roll(ab[:, CMP_HIDDEN:], n - 1, axis=0)
    hid = a + b_next + pe[0:1, :CMP_HIDDEN] + pe[8:9, CMP_HIDDEN:]
    out = jnp.dot(_silu(hid).astype(BF16), w2_ref[0], preferred_element_type=F32)
    o_ref[...] = out.astype(BF16)
    ot_ref[...] = out.T.astype(BF16)


def _compress(x4, w1cat, pe16, w2, l):
    _, n, k = x4.shape
    kind = lambda c: (l, c // NSA_G, 0, 0)
    return pl.pallas_call(
        _compress_kernel,
        grid=(2 * NSA_G,),
        in_specs=[pl.BlockSpec((None, n, k), lambda c: (c, 0, 0)),
                  pl.BlockSpec((None, 1, k, 2 * CMP_HIDDEN), kind),
                  pl.BlockSpec((None, 1, 16, k), kind),
                  pl.BlockSpec((None, 1, CMP_HIDDEN, HD), kind)],
        out_specs=[pl.BlockSpec((None, n, HD), lambda c: (c, 0, 0)),
                   pl.BlockSpec((None, HD, n), lambda c: (c, 0, 0))],
        out_shape=[jax.ShapeDtypeStruct((2 * NSA_G, n, HD), BF16),
                   jax.ShapeDtypeStruct((2 * NSA_G, HD, n), BF16)],
        compiler_params=_cparams(("parallel",)),
        name="nsa_compress",
    )(x4, w1cat, pe16, w2)


def _nsa_kernel(*refs, n_cmp):
    q_refs, z_refs = refs[0:H_NSA], refs[H_NSA:2 * H_NSA]
    (ks_ref, kw_ref, vst_ref, vwt_ref, kc_ref, vct_ref, c2s_ref, cstrip_ref, t0_ref, t1_ref,
     gl_ref, o_ref, sc_ref, sel_ref, m_s, acc_s, acc_w, ss_ref, sw_ref, sa_ref, sb_ref,
     sc3_ref) = refs[2 * H_NSA:]
    i = pl.program_id(0)
    w3 = NSA_R * TILE
    gw = NSA_G * TILE
    groups = range(NSA_G)
    col = lambda g: slice(g * TILE, (g + 1) * TILE)
    qs = [jnp.concatenate([q_refs[NSA_R * g + r][...] for r in range(NSA_R)], axis=0)
          for g in groups]
    b_io = lax.broadcasted_iota(jnp.int32, (TILE, w3), 0)
    a_io = lax.broadcasted_iota(jnp.int32, (TILE, w3), 1) & (TILE - 1)
    tpc = NSA_CHUNK // TILE
    bpc = NSA_CHUNK // SEL_LEN

    def key_scores(k_ref, g, tile0, n_tiles):
        r0 = pl.multiple_of(tile0 * TILE, TILE)
        return lax.dot_general(k_ref[pl.ds(r0, n_tiles * TILE), col(g)], qs[g], NT_DIMS,
                               preferred_element_type=F32)

    def values_t(vt_ref, g, tile0, n_tiles):
        return jnp.concatenate(
            [jnp.concatenate([vt_ref[tile0 + c, col(g), :] for c in range(n_tiles)], axis=1),
             jnp.ones((ONES_ROWS, n_tiles * TILE), BF16)], axis=0)

    def fix_near(s_ref, g, d):
        own = pl.ds(pl.multiple_of(d * TILE, TILE), TILE)
        prev = pl.ds(pl.multiple_of(jnp.maximum(d - 1, 0) * TILE, TILE), TILE)
        s_ref[g, prev, :] = s_ref[g, prev, :] + jnp.where(d >= 1, t1_ref[g], 0.0)
        s_ref[g, own, :] = jnp.where(b_io <= a_io, s_ref[g, own, :] + t0_ref[g], NEG_BIG)

    w0 = pl.multiple_of((TILE // CMP_STRIDE) * i, TILE // CMP_STRIDE)
    n_io = lax.broadcasted_iota(jnp.int32, (n_cmp, w3), 0)
    qa_io = lax.broadcasted_iota(jnp.int32, (n_cmp, w3), 1) & (TILE - 1)
    valid = (CMP_STRIDE * n_io + (CMP_LEN - 1)) <= (TILE * i + qa_io)
    n_wt = WINDOW // TILE + 1
    w_tile0 = jnp.maximum(i - WINDOW // TILE, 0)
    d_win = i - w_tile0
    for g in groups:
        sc_ref[g, 0:CMP_PAD, :] = jnp.zeros((CMP_PAD, w3), F32)
        sc_ref[g, CMP_PAD:CMP_PAD + n_cmp, :] = lax.dot_general(kc_ref[g], qs[g], NT_DIMS,
                                                                preferred_element_type=F32)
        sw_ref[g] = key_scores(kw_ref, g, w_tile0, n_wt)
    oc_t, psums = [], []
    for g in groups:
        strip = pl.ds(w0, CMP_STRIP)
        sc_ref[g, strip, :] = sc_ref[g, strip, :] + cstrip_ref[g]
        scm = jnp.where(valid, sc_ref[g, CMP_PAD:CMP_PAD + n_cmp, :], NEG_BIG)
        mc = jnp.max(scm, axis=0, keepdims=True)
        pc = jnp.exp2(scm - mc)
        lc = jnp.sum(pc, axis=0, keepdims=True)
        pcn = pc * jnp.where(mc > 0.5 * NEG_BIG, 1.0 / lc, 0.0)
        oc_t.append(jnp.dot(vct_ref[g], pcn.astype(BF16), preferred_element_type=F32))
        psums.append(pcn[:, 0:TILE] + pcn[:, TILE:2 * TILE] + pcn[:, 2 * TILE:3 * TILE])

        for c in range(1, n_wt):
            rows = slice(c * TILE, (c + 1) * TILE)
            sw_ref[g, rows, :] = jnp.where(d_win < c, NEG_BIG, sw_ref[g, rows, :])
        fix_near(sw_ref, g, d_win)
        sw_ref[g, 0:TILE, :] = jnp.where((a_io < b_io) | (i < WINDOW // TILE),
                                         sw_ref[g, 0:TILE, :], NEG_BIG)
        sw = sw_ref[g]
        p_w = jnp.exp2(sw - jnp.max(sw, axis=0, keepdims=True))
        acc_w[g] = jnp.dot(values_t(vwt_ref, g, w_tile0, n_wt), p_w.astype(BF16),
                           preferred_element_type=F32)

    psum = jnp.concatenate(psums, axis=1)
    p_hi = psum.astype(BF16)
    p_lo = (psum - p_hi.astype(F32)).astype(BF16)
    c2s = c2s_ref[...]
    imp = (jnp.dot(c2s, p_hi, preferred_element_type=F32)
           + jnp.dot(c2s, p_lo, preferred_element_type=F32))
    n_sel = imp.shape[0]
    m_io = lax.broadcasted_iota(jnp.int32, (n_sel, gw), 0)
    m_f = m_io.astype(F32)
    qpos = TILE * i + (lax.broadcasted_iota(jnp.int32, (n_sel, gw), 1) & (TILE - 1))
    own = qpos >> (SEL_LEN.bit_length() - 1)
    forced = (m_io == 0) | (m_io == own) | (m_io == own - 1)
    eligible = SEL_LEN * m_io <= qpos
    sel = jnp.where(eligible & forced, 1.0, 0.0)
    cur = jnp.where(eligible, jnp.where(forced, -2.0, imp), -1.0)
    for _ in range(min(SEL_TOPK, n_sel) - N_FORCED):
        mx = jnp.max(cur, axis=0, keepdims=True)
        idx = jnp.min(jnp.where(cur == mx, m_f, float(n_sel)), axis=0, keepdims=True)
        pick = m_f == idx
        sel = jnp.where(pick, jnp.maximum(jnp.where(mx >= 0.0, 1.0, 0.0), sel), sel)
        cur = jnp.where(pick, -2.0, cur)
    sel_ref[...] = sel

    def sel_mask(g, c, n_tiles):
        rows = sel_ref[pl.ds(pl.multiple_of(c * bpc, 8), bpc), col(g)]
        mk = jnp.concatenate([jnp.broadcast_to(rows[b:b + 1, :], (SEL_LEN, TILE))
                              for b in range(n_tiles * TILE // SEL_LEN)], axis=0)
        return jnp.concatenate([mk, mk, mk], axis=1) > 0.5

    def sel_update(g, c, s, n_tiles=tpc):
        s = jnp.where(sel_mask(g, c, n_tiles), s, NEG_BIG)
        m_old = m_s[g]
        m_new = jnp.maximum(m_old, jnp.max(s, axis=0, keepdims=True))
        alpha = jnp.exp2(m_old - m_new)
        p = jnp.exp2(s - m_new)
        acc_s[g] = alpha * acc_s[g] + jnp.dot(values_t(vst_ref, g, c * tpc, n_tiles),
                                              p.astype(BF16), preferred_element_type=F32)
        m_s[g] = m_new

    m_s[...] = jnp.full(m_s.shape, NEG_BIG, F32)
    acc_s[...] = jnp.zeros(acc_s.shape, F32)

    c_own = i // tpc
    c_prev = jnp.maximum(i - 1, 0) // tpc

    bufs = (sa_ref, sb_ref, sc3_ref)
    for g in groups:
        sa_ref[g] = key_scores(ks_ref, g, 0, tpc)

    def far_triple(pp, carry):
        for stage in range(3):
            cur_buf, nxt_buf = bufs[stage], bufs[(stage + 1) % 3]
            for g in groups:
                nxt_buf[g] = key_scores(ks_ref, g, (3 * pp + stage + 1) * tpc, tpc)
            for g in groups:
                sel_update(g, 3 * pp + stage, cur_buf[g])
        return carry

    lax.fori_loop(0, c_prev // 3, far_triple, 0)
    done = (c_prev // 3) * 3

    @pl.when(c_prev - done == 1)
    def _():
        for g in groups:
            sel_update(g, done, sa_ref[g])

    @pl.when(c_prev - done == 2)
    def _():
        for g in groups:
            sb_ref[g] = key_scores(ks_ref, g, (done + 1) * tpc, tpc)
        for g in groups:
            sel_update(g, done, sa_ref[g])
            sel_update(g, done + 1, sb_ref[g])

    @pl.when(c_prev != c_own)
    def _():
        last = slice((tpc - 1) * TILE, tpc * TILE)
        for g in groups:
            ss_ref[g] = key_scores(ks_ref, g, c_prev * tpc, tpc)
            ss_ref[g, last, :] = ss_ref[g, last, :] + t1_ref[g]
            sel_update(g, c_prev, ss_ref[g])

    for n_t in range(1, tpc + 1):
        @pl.when(i - c_own * tpc == n_t - 1)
        def _(n_t=n_t):
            own = slice((n_t - 1) * TILE, n_t * TILE)
            prev = slice((n_t - 2) * TILE, (n_t - 1) * TILE)
            for g in groups:
                ss_ref[g, 0:n_t * TILE, :] = key_scores(ks_ref, g, c_own * tpc, n_t)
            for g in groups:
                if n_t >= 2:
                    ss_ref[g, prev, :] = ss_ref[g, prev, :] + t1_ref[g]
                ss_ref[g, own, :] = jnp.where(b_io <= a_io, ss_ref[g, own, :] + t0_ref[g], NEG_BIG)
                sel_update(g, c_own, ss_ref[g, 0:n_t * TILE, :], n_t)

    for g in groups:
        gates = jax.nn.sigmoid(gl_ref[g])

        def gate_row(branch):
            return jnp.concatenate([gates[r * 3 + branch:r * 3 + branch + 1, :]
                                    for r in range(NSA_R)], axis=1)

        out_t = (oc_t[g] * gate_row(0)
                 + acc_s[g, 0:HD, :] * (gate_row(1) / acc_s[g, HD:HD + 1, :])
                 + acc_w[g, 0:HD, :] * (gate_row(2) / acc_w[g, HD:HD + 1, :]))
        for r in range(NSA_R):
            h = NSA_R * g + r
            o_r = out_t[:, r * TILE:(r + 1) * TILE].T
            o_ref[:, h * TILE:(h + 1) * TILE] = (
                o_r * _silu(z_refs[h][...].astype(F32))).astype(BF16)


def _nsa_attention(proj, proj_t, cmp_n, cmp_t, c2s_t, cstrip, t0, t1, glog_t):
    s = proj.shape[0]
    nk = s // TILE
    n_cmp = cmp_n.shape[1]
    n_sel = s // SEL_LEN
    w3 = NSA_R * TILE
    gw = NSA_G * TILE
    qspec = lambda h: pl.BlockSpec((TILE, TILE), lambda i, h=h: (i, BLK_QN + h))
    zspec = lambda h: pl.BlockSpec((TILE, TILE), lambda i, h=h: (i, BLK_ZN + h))
    whole = lambda shape: pl.BlockSpec(shape, lambda i: (0,) * len(shape))
    resident = dict(pipeline_mode=pl.Buffered(1))
    chunk_buf = pltpu.VMEM((NSA_G, NSA_CHUNK, w3), F32)
    acc_buf = pltpu.VMEM((NSA_G, HD + ONES_ROWS, w3), F32)
    return pl.pallas_call(
        functools.partial(_nsa_kernel, n_cmp=n_cmp),
        grid=(s // TILE,),
        in_specs=[qspec(h) for h in range(H_NSA)] + [zspec(h) for h in range(H_NSA)] + [
            pl.BlockSpec((s, gw), lambda i: (0, BLK_KS // NSA_G), **resident),
            pl.BlockSpec((s, gw), lambda i: (0, BLK_KW // NSA_G), **resident),
            pl.BlockSpec((nk, gw, TILE), lambda i: (0, H_FOX // NSA_G, 0), **resident),
            pl.BlockSpec((nk, gw, TILE), lambda i: (0, H_FOX // NSA_G + 1, 0), **resident),
            pl.BlockSpec((NSA_G, n_cmp, HD), lambda i: (0, 0, 0)),
            pl.BlockSpec((NSA_G, HD, n_cmp), lambda i: (1, 0, 0)),
            whole((n_sel, n_cmp)), whole((NSA_G, CMP_STRIP, w3)), whole((NSA_G, TILE, w3)),
            whole((NSA_G, TILE, w3)),
            pl.BlockSpec((NSA_G, NSA_R * 3, TILE), lambda i: (0, 0, i))],
        out_specs=pl.BlockSpec((TILE, D_NSA), lambda i: (i, 0)),
        out_shape=jax.ShapeDtypeStruct((s, D_NSA), BF16),
        scratch_shapes=[pltpu.VMEM((NSA_G, CMP_PAD + n_cmp + 8, w3), F32),
                        pltpu.VMEM((n_sel, gw), F32),
                        pltpu.VMEM((NSA_G, 1, w3), F32),
                        acc_buf, acc_buf,
                        chunk_buf,
                        pltpu.VMEM((NSA_G, WINDOW + TILE, w3), F32),
                        chunk_buf, chunk_buf, chunk_buf],
        compiler_params=_cparams(("parallel",)),
        name="nsa_attention",
    )(*([proj] * (2 * H_NSA)), proj, proj, proj_t, proj_t,
      cmp_n, cmp_t, c2s_t, cstrip, t0, t1, glog_t)


def _gated_conv(u_ref, gb_ref, gc_ref, z_ref, uh_ref, gch_ref, w_ref):
    tm = u_ref.shape[0]
    y = gc_ref[...].astype(F32) * u_ref[...].astype(F32)
    yh = gch_ref[...].astype(F32) * uh_ref[...].astype(F32)
    yh = jnp.where(pl.program_id(0) > 0, yh, 0.0)
    h1 = yh[15:16, :]
    h2 = yh[14:15, :]
    row = lax.broadcasted_iota(jnp.int32, (tm, D_CONV), 0)
    y1 = jnp.where(row == 0, h1, pltpu.roll(y, 1, axis=0))
    y2 = jnp.where(row == 0, h2, jnp.where(row == 1, h1, pltpu.roll(y, 2, axis=0)))
    w = w_ref[...]
    conv = w[0:1, :] * y2 + w[1:2, :] * y1 + w[2:3, :] * y
    return gb_ref[...].astype(F32) * conv * _silu(z_ref[...].astype(F32))


def _out_kernel(u_ref, gb_ref, gc_ref, z_ref, uh_ref, gch_ref, cw_ref, yf_ref, yn_ref, w_ref,
                x_ref, pg_ref, gate_ref, *rest, with_next):
    if with_next:
        ng_ref, nsc_ref, nsh_ref, xo_ref, h_ref, ya_ref = rest
    else:
        xo_ref, ya_ref = rest
    ya_ref[...] = _gated_conv(u_ref, gb_ref, gc_ref, z_ref, uh_ref, gch_ref, cw_ref).astype(BF16)
    tm = x_ref.shape[0]
    n_parts = 4
    for part in range(n_parts):
        rows = slice(part * tm // n_parts, (part + 1) * tm // n_parts)
        y = (jnp.dot(ya_ref[rows, :], w_ref[0:D_CONV, :], preferred_element_type=F32)
             + jnp.dot(yf_ref[rows, :], w_ref[D_CONV:D_CONV + D_FOX, :], preferred_element_type=F32)
             + jnp.dot(yn_ref[rows, :], w_ref[D_CONV + D_FOX:, :], preferred_element_type=F32))
        yn = y * lax.rsqrt(jnp.mean(y * y, axis=-1, keepdims=True) + NORM_EPS) * pg_ref[...]
        xn = x_ref[rows, :] + gate_ref[...] * yn
        if with_next:
            h_ref[rows, :] = _modulated_norm(xn, ng_ref[...], nsc_ref[...],
                                             nsh_ref[...]).astype(BF16)
        xo_ref[rows, :] = xn


def _out_proj(proj, conv_w, yf, yn, w_out, l, x2, post_g, gate, nxt):
    s, d = x2.shape
    tm = 512
    rows = lambda n: pl.BlockSpec((tm, n), lambda i: (i, 0))
    vec = pl.BlockSpec((1, d), lambda i: (0, 0))
    main = lambda c: pl.BlockSpec((tm, D_CONV), lambda i, c=c: (i, c))
    halo = lambda c: pl.BlockSpec((16, D_CONV), lambda i, c=c: (jnp.maximum(i * (tm // 16) - 1, 0), c))
    in_specs = [main(0), main(1), main(2), main(3), halo(0), halo(2),
                pl.BlockSpec((CONV_WIDTH, D_CONV), lambda i: (0, 0)),
                rows(D_FOX), rows(D_NSA),
                pl.BlockSpec((None, d, d), lambda i: (l, 0, 0), pipeline_mode=pl.Buffered(1)),
                rows(d), vec, vec]
    args = [proj, proj, proj, proj, proj, proj, conv_w, yf, yn, w_out, x2, post_g, gate]
    out_specs = [rows(d)]
    out_shape = [jax.ShapeDtypeStruct((s, d), F32)]
    if nxt is not None:
        in_specs += [vec, vec, vec]
        args += list(nxt)
        out_specs.append(rows(d))
        out_shape.append(jax.ShapeDtypeStruct((s, d), BF16))
    res = pl.pallas_call(
        functools.partial(_out_kernel, with_next=nxt is not None),
        grid=(s // tm,),
        in_specs=in_specs,
        out_specs=out_specs,
        out_shape=out_shape,
        scratch_shapes=[pltpu.VMEM((tm, D_CONV), BF16)],
        compiler_params=_cparams(("parallel",)),
        name="out_proj",
    )(*args)
    return res if nxt is not None else (res[0], None)


def _bucket_of_distance():
    max_exact = REL_BUCKETS // 2
    d = np.arange(REL_MAX_DIST)
    nf = np.maximum(d, max_exact).astype(np.float32)
    large = max_exact + (np.log(nf / np.float32(max_exact)) / np.float32(math.log(REL_MAX_DIST / max_exact))
                         * np.float32(REL_BUCKETS - max_exact)).astype(np.int32)
    return np.where(d < max_exact, d, np.minimum(large, REL_BUCKETS - 1))


def _distance_tables():
    b = np.arange(TILE)[:, None]
    a = np.arange(TILE)[None, :]
    none = REL_MAX_DIST
    d0 = np.where(a >= b, a - b, none)
    d1 = np.where(a < b, TILE + a - b, none)
    m = np.arange(CMP_STRIP)[:, None]
    dc = a - CMP_STRIDE * m + (CMP_STRIDE * CMP_PAD - (CMP_LEN - 1))
    dc = np.where((dc >= 0) & (dc < REL_MAX_DIST), dc, none)
    return d0, d1, dc


def _bias_tables(rel_bias):
    bucket = np.concatenate([_bucket_of_distance(), [REL_BUCKETS - 1]])
    rel = (rel_bias - rel_bias[REL_BUCKETS - 1:REL_BUCKETS, :]) * LOG2E
    rel = rel.reshape(REL_BUCKETS, NSA_G, NSA_R)

    def expand(idx):
        bk = bucket[idx]
        tab = jnp.zeros((NSA_G, idx.shape[0], NSA_R, TILE), F32)
        for b in range(REL_BUCKETS - 1):
            hit = jnp.asarray(bk == b)[None, :, None, :]
            tab = jnp.where(hit, rel[b][:, None, :, None], tab)
        return tab.reshape(NSA_G, idx.shape[0], NSA_R * TILE)

    d0, d1, dc = _distance_tables()
    return expand(d0), expand(d1), expand(dc)


def _cmp_to_sel_t(s, n_cmp_pad):
    n_cmp = (s - CMP_LEN) // CMP_STRIDE + 1
    n_sel = s // SEL_LEN
    c_lo = np.arange(n_cmp_pad)[None, :] * CMP_STRIDE
    sel_start = np.arange(n_sel)[:, None] * SEL_LEN
    overlap = (c_lo < sel_start + SEL_LEN) & (c_lo + CMP_LEN > sel_start)
    overlap &= np.arange(n_cmp_pad)[None, :] < n_cmp
    return jnp.asarray(overlap, dtype=BF16)


def _pack_w_in(w_in):
    sizes = ([D_CONV] * 4 + [D_FOX] * 3 + [H_FOX, D_FOX] + [D_NSA] + [2 * HD] * 6
             + [3 * H_NSA, D_NSA])
    offs = np.concatenate([[0], np.cumsum(sizes)])
    names = ["u", "gb", "gc", "za", "qf", "kf", "vf", "ff", "zf", "qn", "kc", "vc", "ks", "vs",
             "kw", "vw", "gn", "zn"]
    wt = jnp.swapaxes(w_in, 1, 2)
    col = {n: wt[:, int(offs[k]):int(offs[k + 1]), :] for k, n in enumerate(names)}
    scale = HD ** -0.5 * LOG2E
    w_a = jnp.concatenate([col["u"], col["gb"], col["gc"], col["za"],
                           col["qn"] * scale, col["zn"], col["ks"], col["kw"],
                           col["qf"] * scale, col["kf"], col["zf"]], axis=1).astype(BF16)
    w_kv = jnp.concatenate([col["kc"], col["vc"]], axis=1).astype(BF16)
    w_v = jnp.concatenate([col["vf"], col["vs"], col["vw"]], axis=1).astype(BF16)
    small = jnp.concatenate([col["ff"], col["gn"]], axis=1)
    small = jnp.pad(small, ((0, 0), (0, TILE - small.shape[1]), (0, 0))).astype(BF16)
    return w_a, w_kv, w_v, small


def kernel(x, c, w_ada, b_ada, pre_norm, post_norm, w_in, b_forget, conv_w,
           cmp_pe_k, cmp_w1_k, cmp_w2_k, cmp_pe_v, cmp_w1_v, cmp_w2_v, w_out, rel_bias):
    bsz, s, d = x.shape
    assert bsz == 1 and d == D_MODEL and s % PROJ_TM == 0 and s >= 2 * WINDOW
    depth = w_in.shape[0]
    x2 = x.reshape(s, d)
    n_cmp_pad = s // CMP_STRIDE

    mod = _ada_mod(c.reshape(d, 1), w_ada, b_ada)
    shift, scale, gate = mod[:, :, :d], mod[:, :, d:2 * d], mod[:, :, 2 * d:]

    w_a, w_kv, w_v, w_small = _pack_w_in(w_in)
    w_out_b = w_out.astype(BF16)
    half = CMP_LEN * HD // 2
    w1cat = jnp.stack([jnp.concatenate([w[:, :half], w[:, half:]], axis=-1)
                       for w in (cmp_w1_k, cmp_w1_v)], axis=1).astype(BF16)
    pe = jnp.stack([cmp_pe_k, cmp_pe_v], axis=1).reshape(depth, 2, 2, 1, half)
    pe16 = jnp.broadcast_to(pe, (depth, 2, 2, 8, half)).reshape(depth, 2, 16, half).astype(BF16)
    w2 = jnp.stack([cmp_w2_k, cmp_w2_v], axis=1).astype(BF16)
    b_pad = jnp.pad(b_forget, ((0, 0), (0, TILE - H_FOX))).reshape(depth, 1, TILE)
    t0, t1, cstrip = _bias_tables(rel_bias)
    c2s_t = _cmp_to_sel_t(s, n_cmp_pad)

    h = _prenorm(x2, pre_norm[0:1], scale[0], shift[0])
    for l in range(depth):
        proj = _matmul(h, w_a, l)
        kcvc = _matmul_split(h, w_kv, l)
        proj_t = _matmul_t_tiled(h, w_v, l)

        qa, ka, small_t = _fox_prep(h, w_small, l, b_pad[l])
        yf = _fox_attention(proj, qa, ka, proj_t)

        x4 = kcvc.reshape(2 * NSA_G, n_cmp_pad, CMP_STRIDE * HD)
        cmp_n, cmp_t = _compress(x4, w1cat, pe16, w2, l)
        glog_t = small_t[H_FOX:H_FOX + 3 * H_NSA].reshape(NSA_G, NSA_R * 3, s)
        yn = _nsa_attention(proj, proj_t, cmp_n, cmp_t, c2s_t, cstrip, t0, t1, glog_t)

        nxt = None
        if l + 1 < depth:
            nxt = (pre_norm[l + 1:l + 2], scale[l + 1], shift[l + 1])
        x2, h = _out_proj(proj, conv_w[l], yf, yn, w_out_b, l, x2, post_norm[l:l + 1], gate[l], nxt)
    return x2.reshape(bsz, s, d)
```

```python
import functools
import math

import numpy as np
import jax
import jax.numpy as jnp
from jax import lax
from jax.experimental import pallas as pl
from jax.experimental.pallas import tpu as pltpu

F32 = jnp.float32
BF16 = jnp.bfloat16

D_MODEL = 2048
HD = 128
D_CONV = 512
D_FOX = 768
D_NSA = 768
H_FOX = 6
H_NSA = 6
NSA_G = 2
NSA_R = 3
CONV_WIDTH = 3
CMP_LEN = 32
CMP_STRIDE = 16
CMP_HIDDEN = 256
SEL_LEN = 64
SEL_TOPK = 16
N_FORCED = 3
CMP_PAD = 16
CMP_STRIP = 24
WINDOW = 512
REL_BUCKETS = 32
REL_MAX_DIST = 128
NORM_EPS = 1e-6
NEG_BIG = -1e30
LOG2E = math.log2(math.e)

TILE = 128
ONES_ROWS = 16
NSA_CHUNK = 512
FOX_T = 512
FOX_TK = 512
FOX_HEADS = 2
PROJ_TM = 1024
PROJ_TN = 1280
PROJ_TV = 1280
VMEM_LIMIT = 56 * 1024 * 1024

NT_DIMS = (((1,), (1,)), ((), ()))

BLK_QN, BLK_ZN, BLK_KS, BLK_KW = 16, 22, 28, 30
BLK_QF, BLK_KF, BLK_ZF = 32, 38, 44


def _cparams(sem, vmem=VMEM_LIMIT):
    return pltpu.CompilerParams(dimension_semantics=sem, vmem_limit_bytes=vmem)


def _silu(v):
    return v * jax.nn.sigmoid(v)


def _mod_kernel(c_ref, w_ref, b_ref, o_ref):
    ca = _silu(c_ref[...])
    o_ref[0] = jnp.sum(w_ref[0] * ca, axis=0, keepdims=True) + b_ref[0]


def _ada_mod(c_col, w_ada, b_ada):
    depth, d, n = w_ada.shape
    tn = 1536
    return pl.pallas_call(
        _mod_kernel,
        grid=(depth, n // tn),
        in_specs=[pl.BlockSpec((d, 1), lambda l, j: (0, 0)),
                  pl.BlockSpec((1, d, tn), lambda l, j: (l, 0, j)),
                  pl.BlockSpec((1, 1, tn), lambda l, j: (l, 0, j))],
        out_specs=pl.BlockSpec((1, 1, tn), lambda l, j: (l, 0, j)),
        out_shape=jax.ShapeDtypeStruct((depth, 1, n), F32),
        compiler_params=_cparams(("parallel", "parallel")),
        name="ada_mod",
    )(c_col, w_ada, b_ada.reshape(depth, 1, n))


def _modulated_norm(x, g, scale, shift):
    y = x * lax.rsqrt(jnp.mean(x * x, axis=-1, keepdims=True) + NORM_EPS) * g
    return y * (1.0 + scale) + shift


def _prenorm_kernel(x_ref, g_ref, sc_ref, sh_ref, h_ref):
    h_ref[...] = _modulated_norm(x_ref[...], g_ref[...], sc_ref[...], sh_ref[...]).astype(BF16)


def _prenorm(x2, g, scale, shift):
    s, d = x2.shape
    tm = 512
    vec = pl.BlockSpec((1, d), lambda i: (0, 0))
    return pl.pallas_call(
        _prenorm_kernel,
        grid=(s // tm,),
        in_specs=[pl.BlockSpec((tm, d), lambda i: (i, 0)), vec, vec, vec],
        out_specs=pl.BlockSpec((tm, d), lambda i: (i, 0)),
        out_shape=jax.ShapeDtypeStruct((s, d), BF16),
        compiler_params=_cparams(("parallel",)),
        name="prenorm",
    )(x2, g, scale, shift)


def _mm_kernel(h_ref, wt_ref, o_ref):
    o_ref[...] = lax.dot_general(h_ref[...], wt_ref[...], NT_DIMS,
                                 preferred_element_type=F32).astype(o_ref.dtype)


def _matmul(h, wt, l):
    s, d = h.shape
    n = wt.shape[1]
    tm, tn = PROJ_TM, PROJ_TN
    return pl.pallas_call(
        _mm_kernel,
        grid=(s // tm, n // tn),
        in_specs=[pl.BlockSpec((tm, d), lambda i, j: (i, 0)),
                  pl.BlockSpec((None, tn, d), lambda i, j: (l, j, 0))],
        out_specs=pl.BlockSpec((tm, tn), lambda i, j: (i, j)),
        out_shape=jax.ShapeDtypeStruct((s, n), BF16),
        compiler_params=_cparams(("parallel", "arbitrary")),
        name="proj_rows",
    )(h, wt)


def _mm_split_kernel(h_ref, wt_ref, o_ref):
    res = lax.dot_general(h_ref[...], wt_ref[...], NT_DIMS, preferred_element_type=F32)
    for c in range(o_ref.shape[0]):
        o_ref[c] = res[:, c * TILE:(c + 1) * TILE].astype(o_ref.dtype)


def _matmul_split(h, wt, l):
    s, d = h.shape
    n = wt.shape[1]
    tm = PROJ_TM
    return pl.pallas_call(
        _mm_split_kernel,
        grid=(s // tm,),
        in_specs=[pl.BlockSpec((tm, d), lambda i: (i, 0)),
                  pl.BlockSpec((None, n, d), lambda i: (l, 0, 0))],
        out_specs=pl.BlockSpec((n // TILE, tm, TILE), lambda i: (0, i, 0)),
        out_shape=jax.ShapeDtypeStruct((n // TILE, s, TILE), BF16),
        compiler_params=_cparams(("parallel",)),
        name="proj_rows_split",
    )(h, wt)


def _mm_t_tiled_kernel(h_ref, wt_ref, o_ref):
    res = lax.dot_general(wt_ref[...], h_ref[...], NT_DIMS, preferred_element_type=F32)
    for c in range(o_ref.shape[0]):
        o_ref[c] = res[:, c * TILE:(c + 1) * TILE].astype(o_ref.dtype)


def _matmul_t_tiled(h, wt, l):
    s, d = h.shape
    n = wt.shape[1]
    tm, tn = PROJ_TM, PROJ_TV
    return pl.pallas_call(
        _mm_t_tiled_kernel,
        grid=(s // tm, n // tn),
        in_specs=[pl.BlockSpec((tm, d), lambda i, j: (i, 0)),
                  pl.BlockSpec((None, tn, d), lambda i, j: (l, j, 0))],
        out_specs=pl.BlockSpec((tm // TILE, tn, TILE), lambda i, j: (i, j, 0)),
        out_shape=jax.ShapeDtypeStruct((s // TILE, n, TILE), BF16),
        compiler_params=_cparams(("parallel", "arbitrary")),
        name="proj_cols_tiled",
    )(h, wt)


def _fox_prep_kernel(h_ref, w_ref, b_ref, qa_ref, ka_ref, st_ref, carry_ref, *, tm):
    @pl.when(pl.program_id(0) == 0)
    def _():
        carry_ref[...] = jnp.zeros_like(carry_ref)

    small = lax.dot_general(h_ref[...], w_ref[...], NT_DIMS,
                            preferred_element_type=F32)
    st_ref[...] = small.T
    v = small + b_ref[...]
    c = jnp.minimum(v, 0.0) - jnp.log1p(jnp.exp(-jnp.abs(v)))
    row = lax.broadcasted_iota(jnp.int32, (tm, TILE), 0)
    sh = 1
    while sh < tm:
        c = c + jnp.where(row >= sh, pltpu.roll(c, sh, axis=0), 0.0)
        sh *= 2
    c = c + carry_ref[...]
    carry_ref[...] = c[tm - 1:tm, :]
    c = c * LOG2E

    lane = lax.broadcasted_iota(jnp.int32, (tm, TILE), 1)
    for h in range(H_FOX):
        ch = jnp.broadcast_to(c[:, h:h + 1], (tm, TILE))
        hi = ch.astype(BF16).astype(F32)
        r1 = ch - hi
        mid = r1.astype(BF16).astype(F32)
        lo = r1 - mid
        pieces = jnp.where((lane == 0) | (lane == 3), hi,
                           jnp.where((lane == 1) | (lane == 4), mid, lo))
        qa = jnp.where(lane < 3, pieces, jnp.where(lane < 6, 1.0, 0.0))
        ka = jnp.where(lane < 3, 1.0, jnp.where(lane < 6, -pieces, 0.0))
        qa_ref[:, h * TILE:(h + 1) * TILE] = qa.astype(BF16)
        ka_ref[:, h * TILE:(h + 1) * TILE] = ka.astype(BF16)


def _fox_prep(h, w_small, l, b_pad):
    s, d = h.shape
    tm = 512
    out = jax.ShapeDtypeStruct((s, H_FOX * TILE), BF16)
    return pl.pallas_call(
        functools.partial(_fox_prep_kernel, tm=tm),
        grid=(s // tm,),
        in_specs=[pl.BlockSpec((tm, d), lambda i: (i, 0)),
                  pl.BlockSpec((None, TILE, d), lambda i: (l, 0, 0)),
                  pl.BlockSpec((1, TILE), lambda i: (0, 0))],
        out_specs=[pl.BlockSpec((tm, H_FOX * TILE), lambda i: (i, 0)),
                   pl.BlockSpec((tm, H_FOX * TILE), lambda i: (i, 0)),
                   pl.BlockSpec((TILE, tm), lambda i: (0, i))],
        out_shape=[out, out, jax.ShapeDtypeStruct((TILE, s), F32)],
        scratch_shapes=[pltpu.VMEM((1, TILE), F32)],
        compiler_params=_cparams(("arbitrary",)),
        name="fox_prep",
    )(h, w_small, b_pad)


def _fox_kernel(q_ref, qa_ref, k_ref, ka_ref, vt_ref, z_ref, o_ref, s0_ref, s1_ref, s2_ref,
                acc_ref):
    i = pl.program_id(1)
    t = FOX_T
    tk = FOX_TK
    heads = range(FOX_HEADS)
    col = lambda h: slice(h * TILE, (h + 1) * TILE)
    qaug = [jnp.concatenate([q_ref[:, col(h)], qa_ref[:, col(h)]], axis=1) for h in heads]

    def scores(h, j, rows):
        r0 = pl.multiple_of(j * rows, rows)
        kaug = jnp.concatenate([k_ref[pl.ds(r0, rows), col(h)], ka_ref[pl.ds(r0, rows), col(h)]],
                               axis=1)
        return lax.dot_general(kaug, qaug[h], NT_DIMS, preferred_element_type=F32)

    def values_t(h, j, rows):
        n_sub = rows // TILE
        return jnp.concatenate(
            [jnp.concatenate([vt_ref[n_sub * j + c, col(h), :] for c in range(n_sub)], axis=1),
             jnp.ones((ONES_ROWS, rows), BF16)], axis=0)

    kio = lax.broadcasted_iota(jnp.int32, (t, t), 0)
    qio = lax.broadcasted_iota(jnp.int32, (t, t), 1)
    ms = []
    for h in heads:
        s = jnp.where(kio <= qio, scores(h, i, t), NEG_BIG)
        m = jnp.max(s, axis=0, keepdims=True)
        p = jnp.exp2(s - m)
        ms.append(m)
        acc_ref[h] = jnp.dot(values_t(h, i, t), p.astype(BF16), preferred_element_type=F32)

    def update(h, s, vt, m):
        m_new = jnp.maximum(m, jnp.max(s, axis=0, keepdims=True))
        alpha = jnp.exp2(m - m_new)
        p = jnp.exp2(s - m_new)
        acc_ref[h] = alpha * acc_ref[h] + jnp.dot(vt, p.astype(BF16), preferred_element_type=F32)
        return m_new

    bufs = (s0_ref, s1_ref, s2_ref)
    for h in heads:
        s0_ref[h] = scores(h, 0, tk)

    def triple(pp, ms):
        ms = list(ms)
        for stage in range(3):
            cur, nxt = bufs[stage], bufs[(stage + 1) % 3]
            for h in heads:
                nxt[h] = scores(h, 3 * pp + stage + 1, tk)
            for h in heads:
                ms[h] = update(h, cur[h], values_t(h, 3 * pp + stage, tk), ms[h])
        return tuple(ms)

    n_far = i * (t // tk)
    ms = lax.fori_loop(0, n_far // 3, triple, tuple(ms))
    done = (n_far // 3) * 3

    @pl.when(n_far - done == 1)
    def _():
        for h in heads:
            update(h, s0_ref[h], values_t(h, done, tk), ms[h])

    @pl.when(n_far - done == 2)
    def _():
        for h in heads:
            s1_ref[h] = scores(h, done + 1, tk)
        for h in heads:
            m_mid = update(h, s0_ref[h], values_t(h, done, tk), ms[h])
            update(h, s1_ref[h], values_t(h, done + 1, tk), m_mid)

    for h in heads:
        o = (acc_ref[h, 0:HD, :] * (1.0 / acc_ref[h, HD:HD + 1, :])).T
        o_ref[:, col(h)] = (o * _silu(z_ref[:, col(h)].astype(F32))).astype(BF16)


def _fox_attention(proj, qa, ka, proj_t):
    s = proj.shape[0]
    t = FOX_T
    nk = s // TILE
    hw = FOX_HEADS * TILE
    score_buf = pltpu.VMEM((FOX_HEADS, FOX_TK, t), F32)
    return pl.pallas_call(
        _fox_kernel,
        grid=(H_FOX // FOX_HEADS, s // t),
        scratch_shapes=[score_buf, score_buf, score_buf,
                        pltpu.VMEM((FOX_HEADS, HD + ONES_ROWS, t), F32)],
        in_specs=[pl.BlockSpec((t, hw), lambda h, i: (i, BLK_QF // FOX_HEADS + h)),
                  pl.BlockSpec((t, hw), lambda h, i: (i, h)),
                  pl.BlockSpec((s, hw), lambda h, i: (0, BLK_KF // FOX_HEADS + h)),
                  pl.BlockSpec((s, hw), lambda h, i: (0, h)),
                  pl.BlockSpec((nk, hw, TILE), lambda h, i: (0, h, 0)),
                  pl.BlockSpec((t, hw), lambda h, i: (i, BLK_ZF // FOX_HEADS + h))],
        out_specs=pl.BlockSpec((t, hw), lambda h, i: (i, h)),
        out_shape=jax.ShapeDtypeStruct((s, D_FOX), BF16),
        compiler_params=_cparams(("parallel", "parallel")),
        name="fox_attention",
    )(proj, qa, proj, ka, proj_t, proj)


def _compress_kernel(x_ref, w1_ref, pe_ref, w2_ref, o_ref, ot_ref):
    n = x_ref.shape[0]
    ab = jnp.dot(x_ref[...], w1_ref[0], preferred_element_type=F32)
    pe = jnp.dot(pe_ref[0], w1_ref[0], preferred_element_type=F32)
    a = ab[:, :CMP_HIDDEN]
    b_next = pltpu.roll(ab[:, CMP_HIDDEN:], n - 1, axis=0)
    hid = a + b_next + pe[0:1, :CMP_HIDDEN] + pe[8:9, CMP_HIDDEN:]
    out = jnp.dot(_silu(hid).astype(BF16), w2_ref[0], preferred_element_type=F32)
    o_ref[...] = out.astype(BF16)
    ot_ref[...] = out.T.astype(BF16)


def _compress(x4, w1cat, pe16, w2, l):
    _, n, k = x4.shape
    kind = lambda c: (l, c // NSA_G, 0, 0)
    return pl.pallas_call(
        _compress_kernel,
        grid=(2 * NSA_G,),
        in_specs=[pl.BlockSpec((None, n, k), lambda c: (c, 0, 0)),
                  pl.BlockSpec((None, 1, k, 2 * CMP_HIDDEN), kind),
                  pl.BlockSpec((None, 1, 16, k), kind),
                  pl.BlockSpec((None, 1, CMP_HIDDEN, HD), kind)],
        out_specs=[pl.BlockSpec((None, n, HD), lambda c: (c, 0, 0)),
                   pl.BlockSpec((None, HD, n), lambda c: (c, 0, 0))],
        out_shape=[jax.ShapeDtypeStruct((2 * NSA_G, n, HD), BF16),
                   jax.ShapeDtypeStruct((2 * NSA_G, HD, n), BF16)],
        compiler_params=_cparams(("parallel",)),
        name="nsa_compress",
    )(x4, w1cat, pe16, w2)


def _nsa_kernel(*refs, n_cmp):
    q_refs, z_refs = refs[0:H_NSA], refs[H_NSA:2 * H_NSA]
    (ks_ref, kw_ref, vst_ref, vwt_ref, kc_ref, vct_ref, c2s_ref, cstrip_ref, t0_ref, t1_ref,
     gl_ref, o_ref, sc_ref, sel_ref, m_s, acc_s, acc_w, ss_ref, sw_ref, sa_ref, sb_ref,
     sc3_ref) = refs[2 * H_NSA:]
    i = pl.program_id(0)
    w3 = NSA_R * TILE
    gw = NSA_G * TILE
    groups = range(NSA_G)
    col = lambda g: slice(g * TILE, (g + 1) * TILE)
    qs = [jnp.concatenate([q_refs[NSA_R * g + r][...] for r in range(NSA_R)], axis=0)
          for g in groups]
    b_io = lax.broadcasted_iota(jnp.int32, (TILE, w3), 0)
    a_io = lax.broadcasted_iota(jnp.int32, (TILE, w3), 1) & (TILE - 1)
    tpc = NSA_CHUNK // TILE
    bpc = NSA_CHUNK // SEL_LEN

    def key_scores(k_ref, g, tile0, n_tiles):
        r0 = pl.multiple_of(tile0 * TILE, TILE)
        return lax.dot_general(k_ref[pl.ds(r0, n_tiles * TILE), col(g)], qs[g], NT_DIMS,
                               preferred_element_type=F32)

    def values_t(vt_ref, g, tile0, n_tiles):
        return jnp.concatenate(
            [jnp.concatenate([vt_ref[tile0 + c, col(g), :] for c in range(n_tiles)], axis=1),
             jnp.ones((ONES_ROWS, n_tiles * TILE), BF16)], axis=0)

    def fix_near(s_ref, g, d):
        own = pl.ds(pl.multiple_of(d * TILE, TILE), TILE)
        prev = pl.ds(pl.multiple_of(jnp.maximum(d - 1, 0) * TILE, TILE), TILE)
        s_ref[g, prev, :] = s_ref[g, prev, :] + jnp.where(d >= 1, t1_ref[g], 0.0)
        s_ref[g, own, :] = jnp.where(b_io <= a_io, s_ref[g, own, :] + t0_ref[g], NEG_BIG)

    w0 = pl.multiple_of((TILE // CMP_STRIDE) * i, TILE // CMP_STRIDE)
    n_io = lax.broadcasted_iota(jnp.int32, (n_cmp, w3), 0)
    qa_io = lax.broadcasted_iota(jnp.int32, (n_cmp, w3), 1) & (TILE - 1)
    valid = (CMP_STRIDE * n_io + (CMP_LEN - 1)) <= (TILE * i + qa_io)
    n_wt = WINDOW // TILE + 1
    w_tile0 = jnp.maximum(i - WINDOW // TILE, 0)
    d_win = i - w_tile0
    for g in groups:
        sc_ref[g, 0:CMP_PAD, :] = jnp.zeros((CMP_PAD, w3), F32)
        sc_ref[g, CMP_PAD:CMP_PAD + n_cmp, :] = lax.dot_general(kc_ref[g], qs[g], NT_DIMS,
                                                                preferred_element_type=F32)
        sw_ref[g] = key_scores(kw_ref, g, w_tile0, n_wt)
    oc_t, psums = [], []
    for g in groups:
        strip = pl.ds(w0, CMP_STRIP)
        sc_ref[g, strip, :] = sc_ref[g, strip, :] + cstrip_ref[g]
        scm = jnp.where(valid, sc_ref[g, CMP_PAD:CMP_PAD + n_cmp, :], NEG_BIG)
        mc = jnp.max(scm, axis=0, keepdims=True)
        pc = jnp.exp2(scm - mc)
        lc = jnp.sum(pc, axis=0, keepdims=True)
        pcn = pc * jnp.where(mc > 0.5 * NEG_BIG, 1.0 / lc, 0.0)
        oc_t.append(jnp.dot(vct_ref[g], pcn.astype(BF16), preferred_element_type=F32))
        psums.append(pcn[:, 0:TILE] + pcn[:, TILE:2 * TILE] + pcn[:, 2 * TILE:3 * TILE])

        for c in range(1, n_wt):
            rows = slice(c * TILE, (c + 1) * TILE)
            sw_ref[g, rows, :] = jnp.where(d_win < c, NEG_BIG, sw_ref[g, rows, :])
        fix_near(sw_ref, g, d_win)
        sw_ref[g, 0:TILE, :] = jnp.where((a_io < b_io) | (i < WINDOW // TILE),
                                         sw_ref[g, 0:TILE, :], NEG_BIG)
        sw = sw_ref[g]
        p_w = jnp.exp2(sw - jnp.max(sw, axis=0, keepdims=True))
        acc_w[g] = jnp.dot(values_t(vwt_ref, g, w_tile0, n_wt), p_w.astype(BF16),
                           preferred_element_type=F32)

    psum = jnp.concatenate(psums, axis=1)
    p_hi = psum.astype(BF16)
    p_lo = (psum - p_hi.astype(F32)).astype(BF16)
    c2s = c2s_ref[...]
    imp = (jnp.dot(c2s, p_hi, preferred_element_type=F32)
           + jnp.dot(c2s, p_lo, preferred_element_type=F32))
    n_sel = imp.shape[0]
    m_io = lax.broadcasted_iota(jnp.int32, (n_sel, gw), 0)
    m_f = m_io.astype(F32)
    qpos = TILE * i + (lax.broadcasted_iota(jnp.int32, (n_sel, gw), 1) & (TILE - 1))
    own = qpos >> (SEL_LEN.bit_length() - 1)
    forced = (m_io == 0) | (m_io == own) | (m_io == own - 1)
    eligible = SEL_LEN * m_io <= qpos
    sel = jnp.where(eligible & forced, 1.0, 0.0)
    cur = jnp.where(eligible, jnp.where(forced, -2.0, imp), -1.0)
    for _ in range(min(SEL_TOPK, n_sel) - N_FORCED):
        mx = jnp.max(cur, axis=0, keepdims=True)
        idx = jnp.min(jnp.where(cur == mx, m_f, float(n_sel)), axis=0, keepdims=True)
        pick = m_f == idx
        sel = jnp.where(pick, jnp.maximum(jnp.where(mx >= 0.0, 1.0, 0.0), sel), sel)
        cur = jnp.where(pick, -2.0, cur)
    sel_ref[...] = sel

    def sel_mask(g, c, n_tiles):
        rows = sel_ref[pl.ds(pl.multiple_of(c * bpc, 8), bpc), col(g)]
        mk = jnp.concatenate([jnp.broadcast_to(rows[b:b + 1, :], (SEL_LEN, TILE))
                              for b in range(n_tiles * TILE // SEL_LEN)], axis=0)
        return jnp.concatenate([mk, mk, mk], axis=1) > 0.5

    def sel_update(g, c, s, n_tiles=tpc):
        s = jnp.where(sel_mask(g, c, n_tiles), s, NEG_BIG)
        m_old = m_s[g]
        m_new = jnp.maximum(m_old, jnp.max(s, axis=0, keepdims=True))
        acc = jnp.exp2(m_old - m_new) * acc_s[g]
        vt = values_t(vst_ref, g, c * tpc, n_tiles)
        n_parts = 2 if n_tiles % 2 == 0 else 1
        hk = n_tiles * TILE // n_parts
        for part in range(n_parts):
            keys = slice(part * hk, (part + 1) * hk)
            acc = acc + jnp.dot(vt[:, keys], jnp.exp2(s[keys, :] - m_new).astype(BF16),
                                preferred_element_type=F32)
        acc_s[g] = acc
        m_s[g] = m_new

    m_s[...] = jnp.full(m_s.shape, NEG_BIG, F32)
    acc_s[...] = jnp.zeros(acc_s.shape, F32)

    c_own = i // tpc
    c_prev = jnp.maximum(i - 1, 0) // tpc

    bufs = (sa_ref, sb_ref, sc3_ref)
    for g in groups:
        sa_ref[g] = key_scores(ks_ref, g, 0, tpc)

    def far_triple(pp, carry):
        for stage in range(3):
            cur_buf, nxt_buf = bufs[stage], bufs[(stage + 1) % 3]
            for g in groups:
                nxt_buf[g] = key_scores(ks_ref, g, (3 * pp + stage + 1) * tpc, tpc)
            for g in groups:
                sel_update(g, 3 * pp + stage, cur_buf[g])
        return carry

    lax.fori_loop(0, c_prev // 3, far_triple, 0)
    done = (c_prev // 3) * 3

    @pl.when(c_prev - done == 1)
    def _():
        for g in groups:
            sel_update(g, done, sa_ref[g])

    @pl.when(c_prev - done == 2)
    def _():
        for g in groups:
            sb_ref[g] = key_scores(ks_ref, g, (done + 1) * tpc, tpc)
        for g in groups:
            sel_update(g, done, sa_ref[g])
            sel_update(g, done + 1, sb_ref[g])

    @pl.when(c_prev != c_own)
    def _():
        last = slice((tpc - 1) * TILE, tpc * TILE)
        for g in groups:
            ss_ref[g] = key_scores(ks_ref, g, c_prev * tpc, tpc)
            ss_ref[g, last, :] = ss_ref[g, last, :] + t1_ref[g]
            sel_update(g, c_prev, ss_ref[g])

    for n_t in range(1, tpc + 1):
        @pl.when(i - c_own * tpc == n_t - 1)
        def _(n_t=n_t):
            own = slice((n_t - 1) * TILE, n_t * TILE)
            prev = slice((n_t - 2) * TILE, (n_t - 1) * TILE)
            for g in groups:
                ss_ref[g, 0:n_t * TILE, :] = key_scores(ks_ref, g, c_own * tpc, n_t)
            for g in groups:
                if n_t >= 2:
                    ss_ref[g, prev, :] = ss_ref[g, prev, :] + t1_ref[g]
                ss_ref[g, own, :] = jnp.where(b_io <= a_io, ss_ref[g, own, :] + t0_ref[g], NEG_BIG)
                sel_update(g, c_own, ss_ref[g, 0:n_t * TILE, :], n_t)

    for g in groups:
        gates = jax.nn.sigmoid(gl_ref[g])

        def gate_row(branch):
            return jnp.concatenate([gates[r * 3 + branch:r * 3 + branch + 1, :]
                                    for r in range(NSA_R)], axis=1)

        out_t = (oc_t[g] * gate_row(0)
                 + acc_s[g, 0:HD, :] * (gate_row(1) / acc_s[g, HD:HD + 1, :])
                 + acc_w[g, 0:HD, :] * (gate_row(2) / acc_w[g, HD:HD + 1, :]))
        for r in range(NSA_R):
            h = NSA_R * g + r
            o_r = out_t[:, r * TILE:(r + 1) * TILE].T
            o_ref[:, h * TILE:(h + 1) * TILE] = (
                o_r * _silu(z_refs[h][...].astype(F32))).astype(BF16)


def _nsa_attention(proj, proj_t, cmp_n, cmp_t, c2s_t, cstrip, t0, t1, glog_t):
    s = proj.shape[0]
    nk = s // TILE
    n_cmp = cmp_n.shape[1]
    n_sel = s // SEL_LEN
    w3 = NSA_R * TILE
    gw = NSA_G * TILE
    qspec = lambda h: pl.BlockSpec((TILE, TILE), lambda i, h=h: (i, BLK_QN + h))
    zspec = lambda h: pl.BlockSpec((TILE, TILE), lambda i, h=h: (i, BLK_ZN + h))
    whole = lambda shape: pl.BlockSpec(shape, lambda i: (0,) * len(shape))
    resident = dict(pipeline_mode=pl.Buffered(1))
    chunk_buf = pltpu.VMEM((NSA_G, NSA_CHUNK, w3), F32)
    acc_buf = pltpu.VMEM((NSA_G, HD + ONES_ROWS, w3), F32)
    return pl.pallas_call(
        functools.partial(_nsa_kernel, n_cmp=n_cmp),
        grid=(s // TILE,),
        in_specs=[qspec(h) for h in range(H_NSA)] + [zspec(h) for h in range(H_NSA)] + [
            pl.BlockSpec((s, gw), lambda i: (0, BLK_KS // NSA_G), **resident),
            pl.BlockSpec((s, gw), lambda i: (0, BLK_KW // NSA_G), **resident),
            pl.BlockSpec((nk, gw, TILE), lambda i: (0, H_FOX // NSA_G, 0), **resident),
            pl.BlockSpec((nk, gw, TILE), lambda i: (0, H_FOX // NSA_G + 1, 0), **resident),
            pl.BlockSpec((NSA_G, n_cmp, HD), lambda i: (0, 0, 0)),
            pl.BlockSpec((NSA_G, HD, n_cmp), lambda i: (1, 0, 0)),
            whole((n_sel, n_cmp)), whole((NSA_G, CMP_STRIP, w3)), whole((NSA_G, TILE, w3)),
            whole((NSA_G, TILE, w3)),
            pl.BlockSpec((NSA_G, NSA_R * 3, TILE), lambda i: (0, 0, i))],
        out_specs=pl.BlockSpec((TILE, D_NSA), lambda i: (i, 0)),
        out_shape=jax.ShapeDtypeStruct((s, D_NSA), BF16),
        scratch_shapes=[pltpu.VMEM((NSA_G, CMP_PAD + n_cmp + 8, w3), F32),
                        pltpu.VMEM((n_sel, gw), F32),
                        pltpu.VMEM((NSA_G, 1, w3), F32),
                        acc_buf, acc_buf,
                        chunk_buf,
                        pltpu.VMEM((NSA_G, WINDOW + TILE, w3), F32),
                        chunk_buf, chunk_buf, chunk_buf],
        compiler_params=_cparams(("parallel",)),
        name="nsa_attention",
    )(*([proj] * (2 * H_NSA)), proj, proj, proj_t, proj_t,
      cmp_n, cmp_t, c2s_t, cstrip, t0, t1, glog_t)


def _gated_conv(u_ref, gb_ref, gc_ref, z_ref, uh_ref, gch_ref, w_ref):
    tm = u_ref.shape[0]
    y = gc_ref[...].astype(F32) * u_ref[...].astype(F32)
    yh = gch_ref[...].astype(F32) * uh_ref[...].astype(F32)
    yh = jnp.where(pl.program_id(0) > 0, yh, 0.0)
    h1 = yh[15:16, :]
    h2 = yh[14:15, :]
    row = lax.broadcasted_iota(jnp.int32, (tm, D_CONV), 0)
    y1 = jnp.where(row == 0, h1, pltpu.roll(y, 1, axis=0))
    y2 = jnp.where(row == 0, h2, jnp.where(row == 1, h1, pltpu.roll(y, 2, axis=0)))
    w = w_ref[...]
    conv = w[0:1, :] * y2 + w[1:2, :] * y1 + w[2:3, :] * y
    return gb_ref[...].astype(F32) * conv * _silu(z_ref[...].astype(F32))


def _out_kernel(u_ref, gb_ref, gc_ref, z_ref, uh_ref, gch_ref, cw_ref, yf_ref, yn_ref, w_ref,
                x_ref, pg_ref, gate_ref, *rest, with_next):
    if with_next:
        ng_ref, nsc_ref, nsh_ref, xo_ref, h_ref, ya_ref = rest
    else:
        xo_ref, ya_ref = rest
    ya_ref[...] = _gated_conv(u_ref, gb_ref, gc_ref, z_ref, uh_ref, gch_ref, cw_ref).astype(BF16)
    tm = x_ref.shape[0]
    n_parts = 4
    for part in range(n_parts):
        rows = slice(part * tm // n_parts, (part + 1) * tm // n_parts)
        y = (jnp.dot(ya_ref[rows, :], w_ref[0:D_CONV, :], preferred_element_type=F32)
             + jnp.dot(yf_ref[rows, :], w_ref[D_CONV:D_CONV + D_FOX, :], preferred_element_type=F32)
             + jnp.dot(yn_ref[rows, :], w_ref[D_CONV + D_FOX:, :], preferred_element_type=F32))
        yn = y * lax.rsqrt(jnp.mean(y * y, axis=-1, keepdims=True) + NORM_EPS) * pg_ref[...]
        xn = x_ref[rows, :] + gate_ref[...] * yn
        if with_next:
            h_ref[rows, :] = _modulated_norm(xn, ng_ref[...], nsc_ref[...],
                                             nsh_ref[...]).astype(BF16)
        xo_ref[rows, :] = xn


def _out_proj(proj, conv_w, yf, yn, w_out, l, x2, post_g, gate, nxt):
    s, d = x2.shape
    tm = 512
    rows = lambda n: pl.BlockSpec((tm, n), lambda i: (i, 0))
    vec = pl.BlockSpec((1, d), lambda i: (0, 0))
    main = lambda c: pl.BlockSpec((tm, D_CONV), lambda i, c=c: (i, c))
    halo = lambda c: pl.BlockSpec((16, D_CONV), lambda i, c=c: (jnp.maximum(i * (tm // 16) - 1, 0), c))
    in_specs = [main(0), main(1), main(2), main(3), halo(0), halo(2),
                pl.BlockSpec((CONV_WIDTH, D_CONV), lambda i: (0, 0)),
                rows(D_FOX), rows(D_NSA),
                pl.BlockSpec((None, d, d), lambda i: (l, 0, 0), pipeline_mode=pl.Buffered(1)),
                rows(d), vec, vec]
    args = [proj, proj, proj, proj, proj, proj, conv_w, yf, yn, w_out, x2, post_g, gate]
    out_specs = [rows(d)]
    out_shape = [jax.ShapeDtypeStruct((s, d), F32)]
    if nxt is not None:
        in_specs += [vec, vec, vec]
        args += list(nxt)
        out_specs.append(rows(d))
        out_shape.append(jax.ShapeDtypeStruct((s, d), BF16))
    res = pl.pallas_call(
        functools.partial(_out_kernel, with_next=nxt is not None),
        grid=(s // tm,),
        in_specs=in_specs,
        out_specs=out_specs,
        out_shape=out_shape,
        scratch_shapes=[pltpu.VMEM((tm, D_CONV), BF16)],
        compiler_params=_cparams(("parallel",)),
        name="out_proj",
    )(*args)
    return res if nxt is not None else (res[0], None)


def _bucket_of_distance():
    max_exact = REL_BUCKETS // 2
    d = np.arange(REL_MAX_DIST)
    nf = np.maximum(d, max_exact).astype(np.float32)
    large = max_exact + (np.log(nf / np.float32(max_exact)) / np.float32(math.log(REL_MAX_DIST / max_exact))
                         * np.float32(REL_BUCKETS - max_exact)).astype(np.int32)
    return np.where(d < max_exact, d, np.minimum(large, REL_BUCKETS - 1))


def _distance_tables():
    b = np.arange(TILE)[:, None]
    a = np.arange(TILE)[None, :]
    none = REL_MAX_DIST
    d0 = np.where(a >= b, a - b, none)
    d1 = np.where(a < b, TILE + a - b, none)
    m = np.arange(CMP_STRIP)[:, None]
    dc = a - CMP_STRIDE * m + (CMP_STRIDE * CMP_PAD - (CMP_LEN - 1))
    dc = np.where((dc >= 0) & (dc < REL_MAX_DIST), dc, none)
    return d0, d1, dc


def _bias_tables(rel_bias):
    bucket = np.concatenate([_bucket_of_distance(), [REL_BUCKETS - 1]])
    rel = (rel_bias - rel_bias[REL_BUCKETS - 1:REL_BUCKETS, :]) * LOG2E
    rel = rel.reshape(REL_BUCKETS, NSA_G, NSA_R)

    def expand(idx):
        bk = bucket[idx]
        tab = jnp.zeros((NSA_G, idx.shape[0], NSA_R, TILE), F32)
        for b in range(REL_BUCKETS - 1):
            hit = jnp.asarray(bk == b)[None, :, None, :]
            tab = jnp.where(hit, rel[b][:, None, :, None], tab)
        return tab.reshape(NSA_G, idx.shape[0], NSA_R * TILE)

    d0, d1, dc = _distance_tables()
    return expand(d0), expand(d1), expand(dc)


def _cmp_to_sel_t(s, n_cmp_pad):
    n_cmp = (s - CMP_LEN) // CMP_STRIDE + 1
    n_sel = s // SEL_LEN
    c_lo = np.arange(n_cmp_pad)[None, :] * CMP_STRIDE
    sel_start = np.arange(n_sel)[:, None] * SEL_LEN
    overlap = (c_lo < sel_start + SEL_LEN) & (c_lo + CMP_LEN > sel_start)
    overlap &= np.arange(n_cmp_pad)[None, :] < n_cmp
    return jnp.asarray(overlap, dtype=BF16)


def _pack_w_in(w_in):
    sizes = ([D_CONV] * 4 + [D_FOX] * 3 + [H_FOX, D_FOX] + [D_NSA] + [2 * HD] * 6
             + [3 * H_NSA, D_NSA])
    offs = np.concatenate([[0], np.cumsum(sizes)])
    names = ["u", "gb", "gc", "za", "qf", "kf", "vf", "ff", "zf", "qn", "kc", "vc", "ks", "vs",
             "kw", "vw", "gn", "zn"]
    wt = jnp.swapaxes(w_in, 1, 2)
    col = {n: wt[:, int(offs[k]):int(offs[k + 1]), :] for k, n in enumerate(names)}
    scale = HD ** -0.5 * LOG2E
    w_a = jnp.concatenate([col["u"], col["gb"], col["gc"], col["za"],
                           col["qn"] * scale, col["zn"], col["ks"], col["kw"],
                           col["qf"] * scale, col["kf"], col["zf"]], axis=1).astype(BF16)
    w_kv = jnp.concatenate([col["kc"], col["vc"]], axis=1).astype(BF16)
    w_v = jnp.concatenate([col["vf"], col["vs"], col["vw"]], axis=1).astype(BF16)
    small = jnp.concatenate([col["ff"], col["gn"]], axis=1)
    small = jnp.pad(small, ((0, 0), (0, TILE - small.shape[1]), (0, 0))).astype(BF16)
    return w_a, w_kv, w_v, small


def kernel(x, c, w_ada, b_ada, pre_norm, post_norm, w_in, b_forget, conv_w,
           cmp_pe_k, cmp_w1_k, cmp_w2_k, cmp_pe_v, cmp_w1_v, cmp_w2_v, w_out, rel_bias):
    bsz, s, d = x.shape
    assert bsz == 1 and d == D_MODEL and s % PROJ_TM == 0 and s >= 2 * WINDOW
    depth = w_in.shape[0]
    x2 = x.reshape(s, d)
    n_cmp_pad = s // CMP_STRIDE

    mod = _ada_mod(c.reshape(d, 1), w_ada, b_ada)
    shift, scale, gate = mod[:, :, :d], mod[:, :, d:2 * d], mod[:, :, 2 * d:]

    w_a, w_kv, w_v, w_small = _pack_w_in(w_in)
    w_out_b = w_out.astype(BF16)
    half = CMP_LEN * HD // 2
    w1cat = jnp.stack([jnp.concatenate([w[:, :half], w[:, half:]], axis=-1)
                       for w in (cmp_w1_k, cmp_w1_v)], axis=1).astype(BF16)
    pe = jnp.stack([cmp_pe_k, cmp_pe_v], axis=1).reshape(depth, 2, 2, 1, half)
    pe16 = jnp.broadcast_to(pe, (depth, 2, 2, 8, half)).reshape(depth, 2, 16, half).astype(BF16)
    w2 = jnp.stack([cmp_w2_k, cmp_w2_v], axis=1).astype(BF16)
    b_pad = jnp.pad(b_forget, ((0, 0), (0, TILE - H_FOX))).reshape(depth, 1, TILE)
    t0, t1, cstrip = _bias_tables(rel_bias)
    c2s_t = _cmp_to_sel_t(s, n_cmp_pad)

    h = _prenorm(x2, pre_norm[0:1], scale[0], shift[0])
    for l in range(depth):
        proj = _matmul(h, w_a, l)
        kcvc = _matmul_split(h, w_kv, l)
        proj_t = _matmul_t_tiled(h, w_v, l)

        qa, ka, small_t = _fox_prep(h, w_small, l, b_pad[l])
        yf = _fox_attention(proj, qa, ka, proj_t)

        x4 = kcvc.reshape(2 * NSA_G, n_cmp_pad, CMP_STRIDE * HD)
        cmp_n, cmp_t = _compress(x4, w1cat, pe16, w2, l)
        glog_t = small_t[H_FOX:H_FOX + 3 * H_NSA].reshape(NSA_G, NSA_R * 3, s)
        yn = _nsa_attention(proj, proj_t, cmp_n, cmp_t, c2s_t, cstrip, t0, t1, glog_t)

        nxt = None
        if l + 1 < depth:
            nxt = (pre_norm[l + 1:l + 2], scale[l + 1], shift[l + 1])
        x2, h = _out_proj(proj, conv_w[l], yf, yn, w_out_b, l, x2, post_norm[l:l + 1], gate[l], nxt)
    return x2.reshape(bsz, s, d)
```
